```python
import jax, jax.numpy as jnp
from jax import lax
import numpy as np

D_MODEL = 2048
BATCH = 4
SEQ = 2048
DEPTH = 1
DEC_BATCH = 8
DEC_SEQ = 16
PAST_LEN = 1024

CHUNK = 64
N_PREV_CHUNKS = 8
BAND_CHUNKS = N_PREV_CHUNKS + 1
ATT_REACH = N_PREV_CHUNKS * CHUNK
HEAD_DIM = 64
N_ATT_HEADS = 16
N_RWKV_HEADS = 16
D_ATT = N_ATT_HEADS * HEAD_DIM
D_RWKV = N_RWKV_HEADS * HEAD_DIM
D_MIX = D_ATT + D_RWKV
REL_CLIP = 128
N_REL = 2 * REL_CLIP + 1
RANK_W = 64
RANK_A = 64
RANK_G = 128
D_SHIFT = 3 * D_RWKV + RANK_W + RANK_A + RANK_G
D_IN = 3 * D_ATT + D_SHIFT
RWKV_SPLITS = (D_RWKV, 2 * D_RWKV, 3 * D_RWKV, 3 * D_RWKV + RANK_W, 3 * D_RWKV + RANK_W + RANK_A)
D_FF = 5632
CONV_W = 3
RMS_EPS = 1e-6
GN_EPS = 64e-5
ATT_SCALE = HEAD_DIM ** -0.5
NEG_INF = -1e30

kernel_name = 'chunk_band_attn_rwkv7_hybrid_step'


def rms_norm(x, g):
    xf = x.astype(jnp.float32)
    y = xf * lax.rsqrt(jnp.mean(xf * xf, axis=-1, keepdims=True) + RMS_EPS)
    return (y * g.astype(jnp.float32)).astype(x.dtype)


def rel_bias(table, n_q, n_k, q_offset):
    rel = q_offset + jnp.arange(n_q)[:, None] - jnp.arange(n_k)[None, :]
    idx = jnp.clip(rel, -REL_CLIP, REL_CLIP) + REL_CLIP
    return table.astype(jnp.float32)[:, idx]


def band_attend(qb, kb, vb, bias, mask=None):
    s = jnp.einsum('bnqhd,bnkhd->bnhqk', qb, kb).astype(jnp.float32) * ATT_SCALE + bias
    if mask is not None:
        s = jnp.where(mask, s, NEG_INF)
    p = jax.nn.softmax(s, axis=-1).astype(vb.dtype)
    return jnp.einsum('bnhqk,bnkhd->bnqhd', p, vb)


def chunk_band_attention(q, k, v, table):
    B, S, H, dh = q.shape
    NC = S // CHUNK
    pad = jnp.zeros((B, ATT_REACH, H, dh), k.dtype)
    kc = jnp.concatenate([pad, k], axis=1).reshape(B, NC + N_PREV_CHUNKS, CHUNK, H, dh)
    vc = jnp.concatenate([pad, v], axis=1).reshape(B, NC + N_PREV_CHUNKS, CHUNK, H, dh)
    band = jnp.arange(NC)[:, None] + jnp.arange(BAND_CHUNKS)[None, :]
    kb = kc[:, band].reshape(B, NC, BAND_CHUNKS * CHUNK, H, dh)
    vb = vc[:, band].reshape(B, NC, BAND_CHUNKS * CHUNK, H, dh)
    key_pos = (jnp.arange(NC)[:, None] - N_PREV_CHUNKS) * CHUNK + jnp.arange(BAND_CHUNKS * CHUNK)[None, :]
    mask = (key_pos >= 0)[None, :, None, None, :]
    bias = rel_bias(table, CHUNK, BAND_CHUNKS * CHUNK, ATT_REACH)
    out = band_attend(q.reshape(B, NC, CHUNK, H, dh), kb, vb, bias, mask)
    return out.reshape(B, S, H, dh)


def cached_chunk_attention(q, k, v, k_past, v_past, table):
    R = k_past.shape[1]
    T = q.shape[1]
    kb = jnp.concatenate([k_past.astype(k.dtype), k], axis=1)[:, None]
    vb = jnp.concatenate([v_past.astype(v.dtype), v], axis=1)[:, None]
    bias = rel_bias(table, T, R + T, R)
    return band_attend(q[:, None], kb, vb, bias)[:, 0]


def rwkv7_scan(S0, r, decay, k, v, kk, a):
    def step(S, inp):
        r_t, w_t, k_t, v_t, kk_t, a_t = inp
        sa = jnp.einsum('bhvk,bhk->bhv', S, -kk_t)
        S = (S * w_t[:, :, None, :] + sa[..., None] * (kk_t * a_t)[:, :, None, :]
             + v_t[..., None] * k_t[:, :, None, :])
        return S, jnp.einsum('bhvk,bhk->bhv', S, r_t)
    xs = tuple(jnp.swapaxes(t, 0, 1) for t in (r, decay, k, v, kk, a))
    S, ys = lax.scan(step, S0, xs)
    return jnp.swapaxes(ys, 0, 1), S


def rwkv7_time_mix(z, shift_prev, S0, p):
    B, T, _ = z.shape
    f32 = jnp.float32
    z_prev = jnp.concatenate([shift_prev.astype(z.dtype)[:, None], z[:, :-1]], axis=1)
    zs = z + (z_prev - z) * p['mu_shift']
    r, k, v, wd, ad, gd = jnp.split(zs, RWKV_SPLITS, axis=-1)
    w_log = -jax.nn.softplus(-(p['w0'] + jnp.tanh(wd) @ p['w2']).astype(f32)) - 0.5
    decay = jnp.exp(-jnp.exp(w_log))
    a = jax.nn.sigmoid((p['a0'] + ad @ p['a2']).astype(f32))
    g = jax.nn.sigmoid(gd) @ p['g2']
    hd = lambda t: t.astype(f32).reshape(B, T, N_RWKV_HEADS, HEAD_DIM)
    ph = lambda t: t.astype(f32).reshape(N_RWKV_HEADS, HEAD_DIM)
    r, k, v, decay, a = hd(r), hd(k), hd(v), hd(decay), hd(a)
    kk = k * ph(p['k_k'])
    kk = kk / jnp.maximum(jnp.sqrt(jnp.sum(kk * kk, axis=-1, keepdims=True)), 1e-12)
    k = k * (1.0 + (a - 1.0) * ph(p['k_a']))
    y, S = rwkv7_scan(S0, r, decay, k, v, kk, a)
    mu = jnp.mean(y, axis=-1, keepdims=True)
    var = jnp.mean(jnp.square(y - mu), axis=-1, keepdims=True)
    yn = ((y - mu) * lax.rsqrt(var + GN_EPS)).reshape(B, T, D_RWKV)
    yn = yn * p['ln_x_w'].astype(f32) + p['ln_x_b'].astype(f32)
    bonus = jnp.sum(r * k * p['r_k'].astype(f32), axis=-1, keepdims=True) * v
    out = (yn + bonus.reshape(B, T, D_RWKV)).astype(z.dtype) * g
    return out, S, z[:, -1]


def causal_dwconv(h, h_past, w, b):
    T = h.shape[1]
    hp = jnp.concatenate([h_past.astype(h.dtype), h], axis=1)
    out = b + hp[:, 0:T] * w[0]
    for j in range(1, CONV_W):
        out = out + hp[:, j:j + T] * w[j]
    return out, hp[:, -(CONV_W - 1):]


def layer(x, c, k_past, v_past, S0, shift_prev, conv_prev, p):
    B, T, _ = x.shape
    mod = jax.nn.silu(c) @ p['w_ada'] + p['b_ada']
    sh_a, sc_a, g_a, sh_f, sc_f, g_f = jnp.split(mod[:, None, :], 6, axis=-1)
    h = rms_norm(x, p['norm_att_g']) * (1 + sc_a) + sh_a
    z = h @ p['w_in']
    q = rms_norm(z[..., :D_ATT].reshape(B, T, N_ATT_HEADS, HEAD_DIM), p['q_norm_g'])
    k = rms_norm(z[..., D_ATT:2 * D_ATT].reshape(B, T, N_ATT_HEADS, HEAD_DIM), p['k_norm_g'])
    v = z[..., 2 * D_ATT:3 * D_ATT].reshape(B, T, N_ATT_HEADS, HEAD_DIM)
    if k_past is None:
        att = chunk_band_attention(q, k, v, p['rel_bias'])
    else:
        att = cached_chunk_attention(q, k, v, k_past, v_past, p['rel_bias'])
    rw, S, shift_last = rwkv7_time_mix(z[..., 3 * D_ATT:], shift_prev, S0, p)
    mix = jnp.concatenate([att.reshape(B, T, D_ATT), rw], axis=-1) @ p['w_out']
    x = x + g_a * mix
    h = rms_norm(x, p['norm_ffn_g']) * (1 + sc_f) + sh_f
    gate_pre, val = jnp.split(h @ p['w_up'], 2, axis=-1)
    gate_c, conv_last = causal_dwconv(gate_pre, conv_prev, p['dw_conv'], p['dw_bias'])
    x = x + g_f * ((jax.nn.gelu(gate_c, approximate=False) * val) @ p['w_down'])
    return x, k, v, S, shift_last, conv_last


def setup_inputs(seed: int = 0) -> dict:
    key = jax.random.key(seed)
    ks = iter(jax.random.split(key, 40))
    f32 = jnp.float32
    L = DEPTH
    R = min(ATT_REACH, PAST_LEN)

    def nrm(shape, scale=1.0):
        return scale * jax.random.normal(next(ks), shape, f32)

    return {
        'x_prompt': nrm((BATCH, SEQ, D_MODEL)),
        'x_sample': nrm((DEC_BATCH, DEC_SEQ, D_MODEL)),
        'c_prompt': nrm((BATCH, D_MODEL)),
        'c_sample': nrm((DEC_BATCH, D_MODEL)),
        'cache_att_k': nrm((L, DEC_BATCH, R, N_ATT_HEADS, HEAD_DIM)),
        'cache_att_v': nrm((L, DEC_BATCH, R, N_ATT_HEADS, HEAD_DIM)),
        'state_rwkv': nrm((L, DEC_BATCH, N_RWKV_HEADS, HEAD_DIM, HEAD_DIM)),
        'state_shift': nrm((L, DEC_BATCH, D_SHIFT)),
        'state_ffn_conv': nrm((L, DEC_BATCH, CONV_W - 1, D_FF)),
        'norm_att_g': 1.0 + nrm((L, D_MODEL), 0.1),
        'norm_ffn_g': 1.0 + nrm((L, D_MODEL), 0.1),
        'w_ada': nrm((L, D_MODEL, 6 * D_MODEL), 0.5 * D_MODEL ** -0.5),
        'b_ada': nrm((L, 6 * D_MODEL), 0.01),
        'w_in': nrm((L, D_MODEL, D_IN), D_MODEL ** -0.5),
        'q_norm_g': 1.0 + nrm((L, HEAD_DIM), 0.1),
        'k_norm_g': 1.0 + nrm((L, HEAD_DIM), 0.1),
        'rel_bias': nrm((L, N_ATT_HEADS, N_REL), 0.5),
        'mu_shift': jax.random.uniform(next(ks), (L, D_SHIFT), f32),
        'w0': jax.random.uniform(next(ks), (L, D_RWKV), f32, -5.0, 1.0),
        'w2': nrm((L, RANK_W, D_RWKV), 0.5 * RANK_W ** -0.5),
        'a0': nrm((L, D_RWKV), 0.5),
        'a2': nrm((L, RANK_A, D_RWKV), 0.5 * RANK_A ** -0.5),
        'g2': nrm((L, RANK_G, D_RWKV), RANK_G ** -0.5),
        'k_k': 1.0 + nrm((L, D_RWKV), 0.1),
        'k_a': 1.0 + nrm((L, D_RWKV), 0.1),
        'r_k': nrm((L, N_RWKV_HEADS, HEAD_DIM), 0.1),
        'ln_x_w': 1.0 + nrm((L, D_RWKV), 0.1),
        'ln_x_b': nrm((L, D_RWKV), 0.01),
        'w_out': nrm((L, D_MIX, D_MODEL), D_MIX ** -0.5),
        'w_up': nrm((L, D_MODEL, 2 * D_FF), D_MODEL ** -0.5),
        'dw_conv': nrm((L, CONV_W, D_FF), CONV_W ** -0.5),
        'dw_bias': nrm((L, D_FF), 0.01),
        'w_down': nrm((L, D_FF, D_MODEL), D_FF ** -0.5),
    }


def reference(x_prompt, x_sample, c_prompt, c_sample, cache_att_k, cache_att_v, state_rwkv,
              state_shift, state_ffn_conv, norm_att_g, norm_ffn_g, w_ada, b_ada, w_in,
              q_norm_g, k_norm_g, rel_bias, mu_shift, w0, w2, a0, a2, g2, k_k, k_a, r_k,
              ln_x_w, ln_x_b, w_out, w_up, dw_conv, dw_bias, w_down):
    hp, hs = x_prompt, x_sample
    Bp, Tp = hp.shape[0], hp.shape[1]
    keep = min(ATT_REACH, Tp)
    kp_l, vp_l, Sp_l, shp_l, cvp_l = [], [], [], [], []
    ks_l, vs_l, Ss_l, shs_l, cvs_l = [], [], [], [], []
    for l in range(DEPTH):
        p = dict(norm_att_g=norm_att_g[l], norm_ffn_g=norm_ffn_g[l], w_ada=w_ada[l], b_ada=b_ada[l],
                 w_in=w_in[l], q_norm_g=q_norm_g[l], k_norm_g=k_norm_g[l], rel_bias=rel_bias[l],
                 mu_shift=mu_shift[l], w0=w0[l], w2=w2[l], a0=a0[l], a2=a2[l], g2=g2[l],
                 k_k=k_k[l], k_a=k_a[l], r_k=r_k[l], ln_x_w=ln_x_w[l], ln_x_b=ln_x_b[l],
                 w_out=w_out[l], w_up=w_up[l], dw_conv=dw_conv[l], dw_bias=dw_bias[l], w_down=w_down[l])
        hp, kp, vp, Sp, shp, cvp = layer(
            hp, c_prompt, None, None,
            jnp.zeros((Bp, N_RWKV_HEADS, HEAD_DIM, HEAD_DIM), jnp.float32),
            jnp.zeros((Bp, D_SHIFT), hp.dtype),
            jnp.zeros((Bp, CONV_W - 1, D_FF), hp.dtype), p)
        kp_l.append(kp[:, Tp - keep:]); vp_l.append(vp[:, Tp - keep:])
        Sp_l.append(Sp.astype(hp.dtype)); shp_l.append(shp); cvp_l.append(cvp)
        hs, kn, vn, Sn, shn, cvn = layer(
            hs, c_sample, cache_att_k[l], cache_att_v[l],
            state_rwkv[l].astype(jnp.float32), state_shift[l], state_ffn_conv[l], p)
        ks_l.append(kn); vs_l.append(vn)
        Ss_l.append(Sn.astype(hs.dtype)); shs_l.append(shn); cvs_l.append(cvn)
    return (hp, hs,
            jnp.stack(kp_l), jnp.stack(vp_l), jnp.stack(Sp_l), jnp.stack(shp_l), jnp.stack(cvp_l),
            jnp.stack(ks_l), jnp.stack(vs_l), jnp.stack(Ss_l), jnp.stack(shs_l), jnp.stack(cvs_l))
```

```python
import functools

import jax
import jax.numpy as jnp
from jax import lax
from jax.experimental import pallas as pl
from jax.experimental.pallas import tpu as pltpu

F32 = jnp.float32
BF16 = jnp.bfloat16

CHUNK = 64
N_PREV_CHUNKS = 8
ATT_REACH = N_PREV_CHUNKS * CHUNK
HEAD_DIM = 64
N_ATT_HEADS = 16
N_RWKV_HEADS = 16
D_ATT = N_ATT_HEADS * HEAD_DIM
D_RWKV = N_RWKV_HEADS * HEAD_DIM
REL_CLIP = 128
RANK_W = 64
RANK_A = 64
RANK_G = 128
D_LORA = RANK_W + RANK_A + RANK_G
D_SHIFT = 3 * D_RWKV + D_LORA
D_IN = 3 * D_ATT + D_SHIFT
CONV_W = 3
RMS_EPS = 1e-6
GN_EPS = 64e-5
ATT_SCALE = HEAD_DIM ** -0.5
NEG_INF = -1e30

LANES = 128
PAIR = 2 * HEAD_DIM
VMEM_LIMIT = 56 * 1024 * 1024

ROW_TILE = 1024
ADA_COLS = 512
IN_COLS = 640
OUT_COLS = 512
UP_COLS = 512
DOWN_COLS = 256
ATT_QROWS = 256
ATT_WIN = ATT_QROWS + ATT_REACH
BIAS_LEN = 1024
RWKV_ROWS = 256


def _cparams(sem):
    return pltpu.CompilerParams(dimension_semantics=sem, vmem_limit_bytes=VMEM_LIMIT)


def _dot(a, b, dims=(((1,), (0,)), ((), ()))):
    return lax.dot_general(a.astype(BF16), b.astype(BF16), dims, preferred_element_type=F32)


def _split2(x):
    hi = x.astype(BF16)
    lo = (x - hi.astype(F32)).astype(BF16)
    return hi, lo


def _dot3(a, b, dims=(((1,), (0,)), ((), ()))):
    a1, a2 = _split2(a)
    b1, b2 = _split2(b)
    d = functools.partial(lax.dot_general, dimension_numbers=dims, preferred_element_type=F32)
    return d(a1, b1) + (d(a1, b2) + d(a2, b1))


def _dot_exact_rhs(a, b_bf16):
    a1 = a.astype(BF16)
    r1 = a - a1.astype(F32)
    a2 = r1.astype(BF16)
    a3 = (r1 - a2.astype(F32)).astype(BF16)
    d = functools.partial(jnp.dot, preferred_element_type=F32)
    return d(a1, b_bf16) + (d(a2, b_bf16) + d(a3, b_bf16))


def _dot_exact_lhs(a_bf16, b):
    b1 = b.astype(BF16)
    r1 = b - b1.astype(F32)
    b2 = r1.astype(BF16)
    b3 = (r1 - b2.astype(F32)).astype(BF16)
    d = functools.partial(jnp.dot, preferred_element_type=F32)
    return d(a_bf16, b1) + (d(a_bf16, b2) + d(a_bf16, b3))


NT = (((1,), (1,)), ((), ()))
TN = (((0,), (0,)), ((), ()))


def _iota(shape, dim):
    return lax.broadcasted_iota(jnp.int32, shape, dim)


def _blk(x, size):
    return jnp.right_shift(x, size.bit_length() - 1)


def _head_ones(n):
    r = _blk(_iota((n, n), 0), HEAD_DIM)
    c = _blk(_iota((n, n), 1), HEAD_DIM)
    return jnp.where(r == c, 1.0, 0.0).astype(BF16)


def _sigmoid(x):
    return 1.0 / (1.0 + jnp.exp(-x))


def _softplus(x):
    return jnp.maximum(x, 0.0) + jnp.log(1.0 + jnp.exp(-jnp.abs(x)))


def _ada_kernel(c_ref, w_ref, b_ref, o_ref):
    c = c_ref[...]
    s = c * _sigmoid(c)
    o_ref[...] = _dot3(s, w_ref[...]) + b_ref[...]


def _ada(c_all, w_ada, b_ada):
    rows, d = c_all.shape
    n = w_ada.shape[1]
    return pl.pallas_call(
        _ada_kernel,
        grid=(n // ADA_COLS,),
        in_specs=[
            pl.BlockSpec((rows, d), lambda j: (0, 0)),
            pl.BlockSpec((d, ADA_COLS), lambda j: (0, j)),
            pl.BlockSpec((1, ADA_COLS), lambda j: (0, j)),
        ],
        out_specs=pl.BlockSpec((rows, ADA_COLS), lambda j: (0, j)),
        out_shape=jax.ShapeDtypeStruct((rows, n), F32),
        compiler_params=_cparams(("arbitrary",)),
        name="ada_mod",
    )(c_all, w_ada, b_ada.reshape(1, n))


class _Mod:
    def __init__(self, arr, per_row, rows_per_batch=None, row_tile=None, batch0=0):
        self.arr = arr
        self.per_row = per_row
        self.rows_per_batch = rows_per_batch
        self.row_tile = row_tile
        self.batch0 = batch0

    def spec(self, idx, cols, col_of):
        if self.per_row:
            m = self.arr.shape[0]
            d = self.arr.shape[1] // 6
            nblk = d // cols
            return pl.BlockSpec((m, cols), lambda i, j: (0, idx * nblk + col_of(j)))
        tiles_per_batch = self.rows_per_batch // self.row_tile
        b0 = self.batch0
        return pl.BlockSpec((None, None, 1, cols),
                            lambda i, j: (b0 + i // tiles_per_batch, idx, 0, col_of(j)))


def _normed(x_ref, g_ref, sc_ref, sh_ref):
    x = x_ref[...]
    ms = jnp.mean(x * x, axis=-1, keepdims=True)
    xn = x * lax.rsqrt(ms + RMS_EPS) * g_ref[...]
    return (xn * (1.0 + sc_ref[...]) + sh_ref[...]).astype(BF16)


def _norm_proj_kernel(x_ref, g_ref, sc_ref, sh_ref, w_ref, o_ref, h_ref):
    @pl.when(pl.program_id(1) == 0)
    def _():
        h_ref[...] = _normed(x_ref, g_ref, sc_ref, sh_ref)

    o_ref[...] = jnp.dot(h_ref[...], w_ref[...].astype(BF16), preferred_element_type=F32)


def _norm_proj(x, gain, mod, sc_idx, sh_idx, w, cols, name):
    m, d = x.shape
    n = w.shape[1]
    tm = min(ROW_TILE, m)
    whole = lambda j: 0
    return pl.pallas_call(
        _norm_proj_kernel,
        grid=(m // tm, n // cols),
        in_specs=[
            pl.BlockSpec((tm, d), lambda i, j: (i, 0)),
            pl.BlockSpec((1, d), lambda i, j: (0, 0)),
            mod.spec(sc_idx, d, whole),
            mod.spec(sh_idx, d, whole),
            pl.BlockSpec((d, cols), lambda i, j: (0, j)),
        ],
        out_specs=pl.BlockSpec((tm, cols), lambda i, j: (i, j)),
        out_shape=jax.ShapeDtypeStruct((m, n), F32),
        scratch_shapes=[pltpu.VMEM((tm, d), BF16)],
        compiler_params=_cparams(("arbitrary", "arbitrary")),
        name=name,
    )(x, gain.reshape(1, d), mod.arr, mod.arr, w)


def _proj_resid_kernel(n_pairs, *refs):
    a_refs = refs[:n_pairs]
    w_refs = refs[n_pairs:2 * n_pairs]
    x_ref, g_ref, o_ref = refs[2 * n_pairs:]
    acc = jnp.dot(a_refs[0][...], w_refs[0][...].astype(BF16), preferred_element_type=F32)
    for a_ref, w_ref in zip(a_refs[1:], w_refs[1:]):
        acc = acc + jnp.dot(a_ref[...], w_ref[...].astype(BF16), preferred_element_type=F32)
    o_ref[...] = x_ref[...] + g_ref[...] * acc


def _proj_resid(a_list, w, x, mod, g_idx, cols, name):
    m, n = x.shape
    tm = min(ROW_TILE, m)
    in_specs, w_args = [], []
    row = 0
    for a in a_list:
        kdim = a.shape[1]
        in_specs.append(pl.BlockSpec((tm, kdim), lambda i, j: (i, 0)))
    for a in a_list:
        kdim = a.shape[1]
        in_specs.append(pl.BlockSpec((kdim, cols), lambda i, j, r=row // kdim: (r, j)))
        w_args.append(w)
        row += kdim
    in_specs.append(pl.BlockSpec((tm, cols), lambda i, j: (i, j)))
    in_specs.append(mod.spec(g_idx, cols, lambda j: j))
    return pl.pallas_call(
        functools.partial(_proj_resid_kernel, len(a_list)),
        grid=(m // tm, n // cols),
        in_specs=in_specs,
        out_specs=pl.BlockSpec((tm, cols), lambda i, j: (i, j)),
        out_shape=jax.ShapeDtypeStruct((m, n), F32),
        compiler_params=_cparams(("arbitrary", "arbitrary")),
        name=name,
    )(*a_list, *w_args, x, mod.arr)


def _gelu(x):
    return 0.5 * x * (1.0 + lax.erf(x * (2.0 ** -0.5)))


def _ffn_up_kernel(tiles_per_batch, x_ref, g_ref, sc_ref, sh_ref, wg_ref, wv_ref, hist_ref,
                   cw_ref, cb_ref, act_ref, last_ref, h_ref, carry_ref):
    i = pl.program_id(0)
    j = pl.program_id(1)

    @pl.when(j == 0)
    def _():
        h_ref[...] = _normed(x_ref, g_ref, sc_ref, sh_ref)

    h = h_ref[...]
    gate = jnp.dot(h, wg_ref[...].astype(BF16), preferred_element_type=F32)
    val = jnp.dot(h, wv_ref[...].astype(BF16), preferred_element_type=F32)
    tm = gate.shape[0]

    @pl.when((i % tiles_per_batch) == 0)
    def _():
        carry_ref[j] = hist_ref[...]

    prev = carry_ref[j]
    row = _iota(gate.shape, 0)
    g1 = pltpu.roll(gate, 1, 0)
    g2 = pltpu.roll(gate, 2, 0)
    g1 = jnp.where(row == 0, prev[1:2], g1)
    g2 = jnp.where(row == 0, prev[0:1], jnp.where(row == 1, prev[1:2], g2))
    cw = cw_ref[...]
    conv = cb_ref[...] + g2 * cw[0:1] + g1 * cw[1:2] + gate * cw[2:3]
    act_ref[...] = (_gelu(conv) * val).astype(BF16)
    tail = gate[tm - 2:tm]
    carry_ref[j] = tail
    last_ref[...] = tail


def _ffn_up_fused(x, gain, mod, w_up, hist, conv_w, conv_b, rows_per_batch):
    m, d = x.shape
    f = w_up.shape[1] // 2
    tm = min(ROW_TILE, rows_per_batch)
    cols = UP_COLS
    nj = f // cols
    tiles_per_batch = rows_per_batch // tm
    whole = lambda j: 0
    act, tile_tails = pl.pallas_call(
        functools.partial(_ffn_up_kernel, tiles_per_batch),
        grid=(m // tm, nj),
        in_specs=[
            pl.BlockSpec((tm, d), lambda i, j: (i, 0)),
            pl.BlockSpec((1, d), lambda i, j: (0, 0)),
            mod.spec(4, d, whole),
            mod.spec(3, d, whole),
            pl.BlockSpec((d, cols), lambda i, j: (0, j)),
            pl.BlockSpec((d, cols), lambda i, j: (0, nj + j)),
            pl.BlockSpec((None, CONV_W - 1, cols), lambda i, j: (i // tiles_per_batch, 0, j)),
            pl.BlockSpec((CONV_W, cols), lambda i, j: (0, j)),
            pl.BlockSpec((1, cols), lambda i, j: (0, j)),
        ],
        out_specs=[
            pl.BlockSpec((tm, cols), lambda i, j: (i, j)),
            pl.BlockSpec((None, CONV_W - 1, cols), lambda i, j: (i, 0, j)),
        ],
        out_shape=[
            jax.ShapeDtypeStruct((m, f), BF16),
            jax.ShapeDtypeStruct((m // tm, CONV_W - 1, f), F32),
        ],
        scratch_shapes=[pltpu.VMEM((tm, d), BF16), pltpu.VMEM((nj, CONV_W - 1, cols), F32)],
        compiler_params=_cparams(("arbitrary", "arbitrary")),
        name="ffn_up_prompt",
    )(x, gain.reshape(1, d), mod.arr, mod.arr, w_up, w_up, hist, conv_w, conv_b.reshape(1, f))
    return act, tile_tails[tiles_per_batch - 1::tiles_per_batch]


def _act_sample_kernel(gate_ref, val_ref, hist_ref, cw_ref, cb_ref, act_ref):
    gate = gate_ref[...]
    hist = hist_ref[...]
    t = _iota(gate.shape, 1)
    g1 = jnp.where(t == 0, hist[:, 1:2], pltpu.roll(gate, 1, 1))
    g2 = jnp.where(t == 0, hist[:, 0:1], jnp.where(t == 1, hist[:, 1:2], pltpu.roll(gate, 2, 1)))
    cw = cw_ref[...]
    conv = cb_ref[...] + g2 * cw[0:1] + g1 * cw[1:2] + gate * cw[2:3]
    act_ref[...] = (_gelu(conv) * val_ref[...]).astype(BF16)


def _act_sample(hu, hist, conv_w, conv_b, nb, t):
    f = hu.shape[1] // 2
    cols = UP_COLS
    nj = f // cols
    hu3 = hu.reshape(nb, t, 2 * f)
    act = pl.pallas_call(
        _act_sample_kernel,
        grid=(nj,),
        in_specs=[
            pl.BlockSpec((nb, t, cols), lambda j: (0, 0, j)),
            pl.BlockSpec((nb, t, cols), lambda j: (0, 0, nj + j)),
            pl.BlockSpec((nb, CONV_W - 1, cols), lambda j: (0, 0, j)),
            pl.BlockSpec((CONV_W, cols), lambda j: (0, j)),
            pl.BlockSpec((1, cols), lambda j: (0, j)),
        ],
        out_specs=pl.BlockSpec((nb, t, cols), lambda j: (0, 0, j)),
        out_shape=jax.ShapeDtypeStruct((nb, t, f), BF16),
        compiler_params=_cparams(("arbitrary",)),
        name="ffn_act_sample",
    )(hu3, hu3, hist, conv_w, conv_b.reshape(1, f))
    return act.reshape(nb * t, f)


def _pair_rms(x, gain, ones):
    ms = _dot_exact_rhs(x * x, ones) * (1.0 / HEAD_DIM)
    return x * lax.rsqrt(ms + RMS_EPS) * gain


def _bias_rows(table):
    h = table.shape[0]
    far = jnp.broadcast_to(table[:, 2 * REL_CLIP:], (h, ATT_REACH - REL_CLIP))
    mid = table[:, ::-1]
    near_len = BIAS_LEN - ATT_QROWS - (ATT_REACH - REL_CLIP) - (2 * REL_CLIP + 1)
    near = jnp.broadcast_to(table[:, 0:1], (h, near_len))
    wrap = jnp.broadcast_to(table[:, 2 * REL_CLIP:], (h, ATT_QROWS))
    return jnp.concatenate([far, mid, near, wrap], axis=1)


def _toeplitz(u_row, rows):
    return pltpu.roll(jnp.broadcast_to(u_row, (rows, BIAS_LEN)), 0, 1, stride=1, stride_axis=0)


def _attn_prompt_kernel(q_ref, k0_ref, k1_ref, k2_ref, v0_ref, v1_ref, v2_ref, qg_ref, kg_ref,
                        u_ref, o_ref, kn_ref, bias_ref):
    qb = pl.program_id(2)

    @pl.when(qb == 0)
    def _():
        for h in range(2):
            bias_ref[h] = _toeplitz(u_ref[h:h + 1, :], ATT_QROWS)[:, :ATT_WIN]

    ones = _head_ones(PAIR)
    qn = _pair_rms(q_ref[...], qg_ref[...], ones)
    kwin = jnp.concatenate([k0_ref[...], k1_ref[...], k2_ref[...]], axis=0)
    kn = _pair_rms(kwin, kg_ref[...], ones)
    kn_ref[...] = kn[ATT_REACH:ATT_WIN]
    kb = kn.astype(BF16)
    vb = jnp.concatenate([v0_ref[...], v1_ref[...], v2_ref[...]], axis=0).astype(BF16)

    shape = (ATT_QROWS, ATT_WIN)
    r = _iota(shape, 0)
    w = _iota(shape, 1)
    chunk_lo = _blk(r, CHUNK) * CHUNK
    lo = jnp.maximum(chunk_lo, ATT_REACH - qb * ATT_QROWS)
    valid = (w >= lo) & (w < chunk_lo + (ATT_REACH + CHUNK))
    lane = _iota((ATT_QROWS, PAIR), 1)
    outs = []
    for h in range(2):
        in_head = _blk(lane, HEAD_DIM) == h
        qh = jnp.where(in_head, qn, 0.0)
        s = _dot(qh, kb, NT) * ATT_SCALE + bias_ref[h]
        s = jnp.where(valid, s, NEG_INF)
        m = jnp.max(s, axis=-1, keepdims=True)
        p = jnp.exp(s - m)
        l = jnp.sum(p, axis=-1, keepdims=True)
        outs.append(_dot(p, vb) / l)
    o_ref[...] = jnp.where(lane < HEAD_DIM, outs[0], outs[1]).astype(BF16)


def _attn_prompt(z3, q_gain, k_gain, u):
    nb, t, _ = z3.shape
    npairs = N_ATT_HEADS // 2
    nq = t // ATT_QROWS
    kcol = D_ATT // PAIR
    vcol = 2 * D_ATT // PAIR
    keep_blocks = ATT_REACH // ATT_QROWS
    blk = (None, ATT_QROWS, PAIR)

    def kv_spec(col0, back):
        return pl.BlockSpec(blk, lambda b, p, q: (b, jnp.maximum(q - back, 0), col0 + p))

    att, kn = pl.pallas_call(
        _attn_prompt_kernel,
        grid=(nb, npairs, nq),
        in_specs=[
            pl.BlockSpec(blk, lambda b, p, q: (b, q, p)),
            kv_spec(kcol, 2), kv_spec(kcol, 1), kv_spec(kcol, 0),
            kv_spec(vcol, 2), kv_spec(vcol, 1), kv_spec(vcol, 0),
            pl.BlockSpec((1, PAIR), lambda b, p, q: (0, 0)),
            pl.BlockSpec((1, PAIR), lambda b, p, q: (0, 0)),
            pl.BlockSpec((None, 2, BIAS_LEN), lambda b, p, q: (p, 0, 0)),
        ],
        out_specs=[
            pl.BlockSpec(blk, lambda b, p, q: (b, q, p)),
            pl.BlockSpec(blk, lambda b, p, q: (b, jnp.maximum(q - (nq - keep_blocks), 0), p)),
        ],
        out_shape=[
            jax.ShapeDtypeStruct((nb, t, D_ATT), BF16),
            jax.ShapeDtypeStruct((nb, ATT_REACH, D_ATT), F32),
        ],
        scratch_shapes=[pltpu.VMEM((2, ATT_QROWS, ATT_WIN), F32)],
        compiler_params=_cparams(("arbitrary", "arbitrary", "arbitrary")),
        name="attn_prompt",
    )(z3, z3, z3, z3, z3, z3, z3, jnp.tile(q_gain, 2).reshape(1, PAIR),
      jnp.tile(k_gain, 2).reshape(1, PAIR), u.reshape(npairs, 2, BIAS_LEN))
    return att, kn


def _attn_sample_kernel(q_ref, k_ref, v_ref, kp_ref, vp_ref, qg_ref, kg_ref, u_ref, o_ref, kn_ref):
    t = q_ref.shape[0]
    reach = kp_ref.shape[0]
    ones = _head_ones(PAIR)
    lane = _iota((t, PAIR), 1)
    for p in range(N_ATT_HEADS // 2):
        cols = slice(p * PAIR, (p + 1) * PAIR)
        qn = _pair_rms(q_ref[:, cols], qg_ref[...], ones)
        kn = _pair_rms(k_ref[:, cols], kg_ref[...], ones)
        kn_ref[:, cols] = kn
        kpast = kp_ref[:, cols].astype(BF16)
        vpast = vp_ref[:, cols].astype(BF16)
        vnew = v_ref[:, cols].astype(BF16)
        outs = []
        for h in range(2):
            in_head = _blk(lane, HEAD_DIM) == h
            qh = jnp.where(in_head, qn, 0.0)
            bias = _toeplitz(u_ref[p, h:h + 1, :], t)
            s_past = _dot(qh, kpast, NT) * ATT_SCALE + bias[:, :reach]
            s_new = _dot(qh, kn, NT) * ATT_SCALE + bias[:, reach:reach + t]
            m = jnp.maximum(jnp.max(s_past, axis=-1, keepdims=True),
                            jnp.max(s_new, axis=-1, keepdims=True))
            p_past = jnp.exp(s_past - m)
            p_new = jnp.exp(s_new - m)
            l = jnp.sum(p_past, axis=-1, keepdims=True) + jnp.sum(p_new, axis=-1, keepdims=True)
            outs.append((_dot(p_past, vpast) + _dot(p_new, vnew)) / l)
        o_ref[:, cols] = jnp.where(lane < HEAD_DIM, outs[0], outs[1]).astype(BF16)


def _attn_sample(z3, k_past, v_past, q_gain, k_gain, u):
    nb, t, _ = z3.shape
    reach = k_past.shape[1]
    npairs = N_ATT_HEADS // 2
    att, kn = pl.pallas_call(
        _attn_sample_kernel,
        grid=(nb,),
        in_specs=[
            pl.BlockSpec((None, t, D_ATT), lambda b: (b, 0, 0)),
            pl.BlockSpec((None, t, D_ATT), lambda b: (b, 0, 1)),
            pl.BlockSpec((None, t, D_ATT), lambda b: (b, 0, 2)),
            pl.BlockSpec((None, reach, D_ATT), lambda b: (b, 0, 0)),
            pl.BlockSpec((None, reach, D_ATT), lambda b: (b, 0, 0)),
            pl.BlockSpec((1, PAIR), lambda b: (0, 0)),
            pl.BlockSpec((1, PAIR), lambda b: (0, 0)),
            pl.BlockSpec((npairs, 2, BIAS_LEN), lambda b: (0, 0, 0)),
        ],
        out_specs=[
            pl.BlockSpec((None, t, D_ATT), lambda b: (b, 0, 0)),
            pl.BlockSpec((None, t, D_ATT), lambda b: (b, 0, 0)),
        ],
        out_shape=[
            jax.ShapeDtypeStruct((nb, t, D_ATT), BF16),
            jax.ShapeDtypeStruct((nb, t, D_ATT), F32),
        ],
        compiler_params=_cparams(("arbitrary",)),
        name="attn_sample",
    )(z3, z3, z3, k_past.reshape(nb, reach, D_ATT), v_past.reshape(nb, reach, D_ATT),
      jnp.tile(q_gain, 2).reshape(1, PAIR), jnp.tile(k_gain, 2).reshape(1, PAIR),
      u.reshape(npairs, 2, BIAS_LEN))
    return att, kn


def _tri_inverse(n_mat, c):
    eye = jnp.where(_iota((c, c), 0) == _iota((c, c), 1), 1.0, 0.0).astype(F32)
    t = eye + n_mat
    pw = n_mat
    steps = max(c.bit_length() - 2, 0)
    for _ in range(steps):
        pw = _dot3(pw, pw)
        t = t + _dot3(t, pw)
    return t


def _rwkv_kernel(c, r_ref, k_ref, v_ref, lo_ref, sr_ref, sk_ref, sv_ref, slo_ref, s0_ref,
                 mur_ref, muk_ref, muv_ref, mulo_ref, w0_ref, a0_ref, kkg_ref, ka_ref, rk_ref,
                 lnw_ref, lnb_ref, w2_ref, a2_ref, g2_ref,
                 o_ref, sT_ref, s_ref, cr_ref, ck_ref, cv_ref, clo_ref):
    tb = pl.program_id(2)
    rows = r_ref.shape[0]
    nchunks = rows // c
    h0 = _iota((c, PAIR), 1) < HEAD_DIM
    bd = _blk(_iota((PAIR, PAIR), 0), HEAD_DIM) == _blk(_iota((PAIR, PAIR), 1), HEAD_DIM)

    @pl.when(tb == 0)
    def _():
        s_ref[...] = jnp.zeros((PAIR, PAIR), F32)
        s_ref[0:HEAD_DIM, 0:HEAD_DIM] = s0_ref[0]
        s_ref[HEAD_DIM:PAIR, HEAD_DIM:PAIR] = s0_ref[1]
        cr_ref[...] = sr_ref[...]
        ck_ref[...] = sk_ref[...]
        cv_ref[...] = sv_ref[...]
        clo_ref[...] = slo_ref[...]

    def shifted(x_ref, carry_ref, mu_ref):
        x = x_ref[...]
        prev = jnp.where(_iota(x.shape, 0) == 0, carry_ref[...], pltpu.roll(x, 1, 0))
        carry_ref[...] = x[rows - 1:rows]
        return x + (prev - x) * mu_ref[...]

    r = shifted(r_ref, cr_ref, mur_ref)
    k = shifted(k_ref, ck_ref, muk_ref)
    v = shifted(v_ref, cv_ref, muv_ref)
    lo = shifted(lo_ref, clo_ref, mulo_ref)

    zeros_w = jnp.zeros((RANK_W, PAIR), F32)
    w2p = jnp.concatenate([w2_ref[...], zeros_w], axis=0)
    a2p = jnp.concatenate([zeros_w, a2_ref[...]], axis=0)
    lo_wa = lo[:, 0:RANK_W + RANK_A]
    u = w0_ref[...] + _dot3(jnp.tanh(lo_wa), w2p)
    lw = -jnp.exp(-_softplus(-u) - 0.5)
    a = _sigmoid(a0_ref[...] + _dot3(lo_wa, a2p))
    g = _dot(_sigmoid(lo[:, RANK_W + RANK_A:]), g2_ref[...])

    ones = _head_ones(PAIR)
    kk = k * kkg_ref[...]
    kk = kk / jnp.maximum(jnp.sqrt(_dot_exact_rhs(kk * kk, ones)), 1e-12)
    k = k * (1.0 + (a - 1.0) * ka_ref[...])
    b = kk * a
    bonus = _dot_exact_rhs(r * k * rk_ref[...], ones) * v

    tr = _iota((rows, rows), 0)
    tc = _iota((rows, rows), 1)
    same_chunk = _blk(tr, c) == _blk(tc, c)
    tril_ones = jnp.where(same_chunk & (tr >= tc), 1.0, 0.0).astype(BF16)
    lp = _dot_exact_lhs(tril_ones, lw)

    alpha = kk * jnp.exp(lp - lw)
    inv_p = jnp.exp(-lp)
    beta = b * inv_p
    kappa = k * inv_p
    rho = r * jnp.exp(lp)

    cr = _iota((c, c), 0)
    cc = _iota((c, c), 1)
    strict = cr > cc
    incl = cr >= cc
    eye_p = (_iota((PAIR, PAIR), 0) == _iota((PAIR, PAIR), 1))

    s_cur = s_ref[...]
    ys = []
    for ci in range(nchunks):
        sl = slice(ci * c, (ci + 1) * c)
        al, be, ka, rh, vv = alpha[sl], beta[sl], kappa[sl], rho[sl], v[sl]
        lp_end = lp[(ci + 1) * c - 1:(ci + 1) * c]
        scale_end = jnp.exp(lp_end - lp[sl])
        be_e = b[sl] * scale_end
        ka_e = k[sl] * scale_end
        w1s, w2s, mqb, mqk = [], [], [], []
        for h in range(2):
            hm = h0 if h == 0 else ~h0
            al_h = jnp.where(hm, al, 0.0)
            rh_h = jnp.where(hm, rh, 0.0)
            a_ab = _dot3(al_h, be, NT)
            a_ak = _dot3(al_h, ka, NT)
            a_qb = _dot3(rh_h, be, NT)
            a_qk = _dot3(rh_h, ka, NT)
            t_inv = _tri_inverse(jnp.where(strict, -a_ab, 0.0), c)
            x = _dot3(jnp.where(strict, a_ak, 0.0), vv)
            w1s.append(_dot3(t_inv, al))
            w2s.append(_dot3(t_inv, x))
            mqb.append(jnp.where(incl, a_qb, 0.0))
            mqk.append(jnp.where(incl, a_qk, 0.0))
        w1 = jnp.where(h0, w1s[0], w1s[1])
        w2 = jnp.where(h0, w2s[0], w2s[1])
        rp = rh - jnp.where(h0, _dot3(mqb[0], w1), _dot3(mqb[1], w1))
        y0 = jnp.where(h0, _dot3(mqk[0], vv) - _dot3(mqb[0], w2),
                       _dot3(mqk[1], vv) - _dot3(mqb[1], w2))
        gmat = jnp.where(eye_p, jnp.broadcast_to(jnp.exp(lp_end), (PAIR, PAIR)), 0.0) \
            - jnp.where(bd, _dot3(w1, be_e, TN), 0.0)
        hmat = jnp.where(bd, _dot3(vv, ka_e, TN) - _dot3(w2, be_e, TN), 0.0)
        ys.append(_dot3(rp, s_cur, NT) + y0)
        s_cur = _dot3(s_cur, gmat) + hmat
    s_ref[...] = s_cur
    y = ys[0] if nchunks == 1 else jnp.concatenate(ys, axis=0)

    mu = _dot_exact_rhs(y, ones) * (1.0 / HEAD_DIM)
    d = y - mu
    var = _dot_exact_rhs(d * d, ones) * (1.0 / HEAD_DIM)
    yn = d * lax.rsqrt(var + GN_EPS) * lnw_ref[...] + lnb_ref[...]
    o_ref[...] = ((yn + bonus) * g).astype(BF16)

    @pl.when(tb == pl.num_programs(2) - 1)
    def _():
        sT_ref[0] = s_cur[0:HEAD_DIM, 0:HEAD_DIM]
        sT_ref[1] = s_cur[HEAD_DIM:PAIR, HEAD_DIM:PAIR]


def _rwkv(z3, shift_prev, s0, p, rows, c):
    nb, t, _ = z3.shape
    npairs = N_RWKV_HEADS // 2
    col0 = 3 * D_ATT // PAIR
    lo_blk = (3 * D_ATT + 3 * D_RWKV) // D_LORA
    sp = shift_prev.reshape(nb, 1, D_SHIFT)
    rp = D_RWKV // PAIR

    def zspec(off):
        return pl.BlockSpec((None, rows, PAIR), lambda b, q, s: (b, s, col0 + off * rp + q))

    def sspec(off):
        return pl.BlockSpec((None, 1, PAIR), lambda b, q, s: (b, 0, off * rp + q))

    def vec(off=0):
        return pl.BlockSpec((1, PAIR), lambda b, q, s: (0, off * rp + q))

    def row2(x):
        return x.reshape(1, -1)

    out, s_fin = pl.pallas_call(
        functools.partial(_rwkv_kernel, c),
        grid=(nb, npairs, t // rows),
        in_specs=[
            zspec(0), zspec(1), zspec(2),
            pl.BlockSpec((None, rows, D_LORA), lambda b, q, s: (b, s, lo_blk)),
            sspec(0), sspec(1), sspec(2),
            pl.BlockSpec((None, 1, D_LORA), lambda b, q, s: (b, 0, 3 * D_RWKV // D_LORA)),
            pl.BlockSpec((None, 2, HEAD_DIM, HEAD_DIM), lambda b, q, s: (b, q, 0, 0)),
            vec(0), vec(1), vec(2),
            pl.BlockSpec((1, D_LORA), lambda b, q, s: (0, 3 * D_RWKV // D_LORA)),
            vec(), vec(), vec(), vec(), vec(), vec(), vec(),
            pl.BlockSpec((RANK_W, PAIR), lambda b, q, s: (0, q)),
            pl.BlockSpec((RANK_A, PAIR), lambda b, q, s: (0, q)),
            pl.BlockSpec((RANK_G, PAIR), lambda b, q, s: (0, q)),
        ],
        out_specs=[
            pl.BlockSpec((None, rows, PAIR), lambda b, q, s: (b, s, q)),
            pl.BlockSpec((None, 2, HEAD_DIM, HEAD_DIM), lambda b, q, s: (b, q, 0, 0)),
        ],
        out_shape=[
            jax.ShapeDtypeStruct((nb, t, D_RWKV), BF16),
            jax.ShapeDtypeStruct((nb, N_RWKV_HEADS, HEAD_DIM, HEAD_DIM), F32),
        ],
        scratch_shapes=[
            pltpu.VMEM((PAIR, PAIR), F32),
            pltpu.VMEM((1, PAIR), F32), pltpu.VMEM((1, PAIR), F32), pltpu.VMEM((1, PAIR), F32),
            pltpu.VMEM((1, D_LORA), F32),
        ],
        compiler_params=_cparams(("arbitrary", "arbitrary", "arbitrary")),
        name="rwkv7_mix",
    )(z3, z3, z3, z3, sp, sp, sp, sp, s0,
      row2(p['mu_shift']), row2(p['mu_shift']), row2(p['mu_shift']), row2(p['mu_shift']),
      row2(p['w0']), row2(p['a0']), row2(p['k_k']), row2(p['k_a']), row2(p['r_k']),
      row2(p['ln_x_w']), row2(p['ln_x_b']), p['w2'], p['a2'], p['g2'])
    return out, s_fin


def _layer(x3, mod, p, u, k_past, v_past, s0, shift_prev, conv_prev):
    nb, t, d = x3.shape
    m = nb * t
    x = x3.reshape(m, d)
    z = _norm_proj(x, p['norm_att_g'], mod, 1, 0, p['w_in'], IN_COLS, "in_proj")
    z3 = z.reshape(nb, t, D_IN)
    if k_past is None:
        att, k_keep = _attn_prompt(z3, p['q_norm_g'], p['k_norm_g'], u)
        keep = min(ATT_REACH, t)
        v_keep = z3[:, t - keep:, 2 * D_ATT:3 * D_ATT]
        rw, s_fin = _rwkv(z3, shift_prev, s0, p, RWKV_ROWS, CHUNK)
    else:
        att, k_keep = _attn_sample(z3, k_past, v_past, p['q_norm_g'], p['k_norm_g'], u)
        v_keep = z3[:, :, 2 * D_ATT:3 * D_ATT]
        rw, s_fin = _rwkv(z3, shift_prev, s0, p, t, t)
    shift_last = z3[:, t - 1, 3 * D_ATT:]
    x1 = _proj_resid([att.reshape(m, D_ATT), rw.reshape(m, D_RWKV)], p['w_out'], x, mod, 2,
                     OUT_COLS, "out_proj")
    if k_past is None:
        act, conv_last = _ffn_up_fused(x1, p['norm_ffn_g'], mod, p['w_up'], conv_prev,
                                       p['dw_conv'], p['dw_bias'], t)
    else:
        hu = _norm_proj(x1, p['norm_ffn_g'], mod, 4, 3, p['w_up'], UP_COLS, "ffn_up_sample")
        f = hu.shape[1] // 2
        act = _act_sample(hu, conv_prev, p['dw_conv'], p['dw_bias'], nb, t)
        conv_last = hu.reshape(nb, t, 2 * f)[:, t - (CONV_W - 1):, :f]
    x2 = _proj_resid([act], p['w_down'], x1, mod, 5, DOWN_COLS, "ffn_down")
    heads = lambda a: a.reshape(nb, a.shape[1], N_ATT_HEADS, HEAD_DIM)
    return x2.reshape(nb, t, d), heads(k_keep), heads(v_keep), s_fin, shift_last, conv_last


def kernel(x_prompt, x_sample, c_prompt, c_sample, cache_att_k, cache_att_v, state_rwkv, state_shift, state_ffn_conv, norm_att_g, norm_ffn_g, w_ada, b_ada, w_in, q_norm_g, k_norm_g, rel_bias, mu_shift, w0, w2, a0, a2, g2, k_k, k_a, r_k, ln_x_w, ln_x_b, w_out, w_up, dw_conv, dw_bias, w_down):
    depth = w_in.shape[0]
    bp, tp, d = x_prompt.shape
    bs, ts, _ = x_sample.shape
    d_ff = w_down.shape[1]
    hp, hs = x_prompt, x_sample
    outs_p = [[] for _ in range(5)]
    outs_s = [[] for _ in range(5)]
    for l in range(depth):
        p = dict(norm_att_g=norm_att_g[l], norm_ffn_g=norm_ffn_g[l], w_in=w_in[l], q_norm_g=q_norm_g[l],
                 k_norm_g=k_norm_g[l], mu_shift=mu_shift[l], w0=w0[l], w2=w2[l], a0=a0[l], a2=a2[l],
                 g2=g2[l], k_k=k_k[l], k_a=k_a[l], r_k=r_k[l], ln_x_w=ln_x_w[l], ln_x_b=ln_x_b[l],
                 w_out=w_out[l], w_up=w_up[l], dw_conv=dw_conv[l], dw_bias=dw_bias[l], w_down=w_down[l])
        n_c = bp + bs
        pad = (-n_c) % 8
        c_all = jnp.concatenate([c_prompt, c_sample, jnp.zeros((pad, d), F32)], axis=0)
        mod = _ada(c_all, w_ada[l], b_ada[l])
        mod_p = _Mod(mod.reshape(n_c + pad, 6, 1, d), False, rows_per_batch=tp,
                     row_tile=min(ROW_TILE, tp))
        mod_s = _Mod(jnp.repeat(mod[bp:bp + bs], ts, axis=0), True)
        u = _bias_rows(rel_bias[l])

        res = _layer(hp, mod_p, p, u, None, None,
                     jnp.zeros((bp, N_RWKV_HEADS, HEAD_DIM, HEAD_DIM), F32),
                     jnp.zeros((bp, D_SHIFT), F32),
                     jnp.zeros((bp, CONV_W - 1, d_ff), F32))
        hp = res[0]
        for lst, val in zip(outs_p, res[1:]):
            lst.append(val)
        res = _layer(hs, mod_s, p, u, cache_att_k[l], cache_att_v[l], state_rwkv[l],
                     state_shift[l], state_ffn_conv[l])
        hs = res[0]
        for lst, val in zip(outs_s, res[1:]):
            lst.append(val)
    st = lambda lst: jnp.stack(lst)
    return (hp, hs, *[st(x) for x in outs_p], *[st(x) for x in outs_s])
```

```python
import functools

import jax
import jax.numpy as jnp
from jax import lax
from jax.experimental import pallas as pl
from jax.experimental.pallas import tpu as pltpu

F32 = jnp.float32
BF16 = jnp.bfloat16

CHUNK = 64
N_PREV_CHUNKS = 8
ATT_REACH = N_PREV_CHUNKS * CHUNK
HEAD_DIM = 64
N_ATT_HEADS = 16
N_RWKV_HEADS = 16
D_ATT = N_ATT_HEADS * HEAD_DIM
D_RWKV = N_RWKV_HEADS * HEAD_DIM
REL_CLIP = 128
RANK_W = 64
RANK_A = 64
RANK_G = 128
D_LORA = RANK_W + RANK_A + RANK_G
D_SHIFT = 3 * D_RWKV + D_LORA
D_IN = 3 * D_ATT + D_SHIFT
CONV_W = 3
RMS_EPS = 1e-6
GN_EPS = 64e-5
ATT_SCALE = HEAD_DIM ** -0.5
NEG_INF = -1e30

LANES = 128
PAIR = 2 * HEAD_DIM
VMEM_LIMIT = 56 * 1024 * 1024

ROW_TILE = 1024
ADA_COLS = 512
IN_COLS = 640
OUT_COLS = 512
UP_COLS = 512
DOWN_COLS = 256
ATT_QROWS = 256
ATT_WIN = ATT_QROWS + ATT_REACH
BIAS_LEN = 1024
RWKV_ROWS = 256


def _cparams(sem):
    return pltpu.CompilerParams(dimension_semantics=sem, vmem_limit_bytes=VMEM_LIMIT)


def _dot(a, b, dims=(((1,), (0,)), ((), ()))):
    return lax.dot_general(a.astype(BF16), b.astype(BF16), dims, preferred_element_type=F32)


def _split2(x):
    hi = x.astype(BF16)
    lo = (x - hi.astype(F32)).astype(BF16)
    return hi, lo


def _dot3(a, b, dims=(((1,), (0,)), ((), ()))):
    a1, a2 = _split2(a)
    b1, b2 = _split2(b)
    d = functools.partial(lax.dot_general, dimension_numbers=dims, preferred_element_type=F32)
    return d(a1, b1) + (d(a1, b2) + d(a2, b1))


def _dot_exact_rhs(a, b_bf16):
    a1 = a.astype(BF16)
    r1 = a - a1.astype(F32)
    a2 = r1.astype(BF16)
    a3 = (r1 - a2.astype(F32)).astype(BF16)
    d = functools.partial(jnp.dot, preferred_element_type=F32)
    return d(a1, b_bf16) + (d(a2, b_bf16) + d(a3, b_bf16))


def _dot_exact_lhs(a_bf16, b):
    b1 = b.astype(BF16)
    r1 = b - b1.astype(F32)
    b2 = r1.astype(BF16)
    b3 = (r1 - b2.astype(F32)).astype(BF16)
    d = functools.partial(jnp.dot, preferred_element_type=F32)
    return d(a_bf16, b1) + (d(a_bf16, b2) + d(a_bf16, b3))


NT = (((1,), (1,)), ((), ()))
TN = (((0,), (0,)), ((), ()))


def _iota(shape, dim):
    return lax.broadcasted_iota(jnp.int32, shape, dim)


def _blk(x, size):
    return jnp.right_shift(x, size.bit_length() - 1)


def _head_ones(n):
    r = _blk(_iota((n, n), 0), HEAD_DIM)
    c = _blk(_iota((n, n), 1), HEAD_DIM)
    return jnp.where(r == c, 1.0, 0.0).astype(BF16)


def _sigmoid(x):
    return 1.0 / (1.0 + jnp.exp(-x))


def _softplus(x):
    return jnp.maximum(x, 0.0) + jnp.log(1.0 + jnp.exp(-jnp.abs(x)))


def _ada_kernel(c_ref, w_ref, b_ref, o_ref):
    c = c_ref[...]
    s = c * _sigmoid(c)
    o_ref[...] = _dot3(s, w_ref[...]) + b_ref[...]


def _ada(c_all, w_ada, b_ada):
    rows, d = c_all.shape
    n = w_ada.shape[1]
    return pl.pallas_call(
        _ada_kernel,
        grid=(n // ADA_COLS,),
        in_specs=[
            pl.BlockSpec((rows, d), lambda j: (0, 0)),
            pl.BlockSpec((d, ADA_COLS), lambda j: (0, j)),
            pl.BlockSpec((1, ADA_COLS), lambda j: (0, j)),
        ],
        out_specs=pl.BlockSpec((rows, ADA_COLS), lambda j: (0, j)),
        out_shape=jax.ShapeDtypeStruct((rows, n), F32),
        compiler_params=_cparams(("arbitrary",)),
        name="ada_mod",
    )(c_all, w_ada, b_ada.reshape(1, n))


class _Mod:
    def __init__(self, arr, per_row, rows_per_batch=None, row_tile=None, batch0=0):
        self.arr = arr
        self.per_row = per_row
        self.rows_per_batch = rows_per_batch
        self.row_tile = row_tile
        self.batch0 = batch0

    def spec(self, idx, cols, col_of):
        if self.per_row:
            m = self.arr.shape[0]
            d = self.arr.shape[1] // 6
            nblk = d // cols
            return pl.BlockSpec((m, cols), lambda i, j: (0, idx * nblk + col_of(j)))
        tiles_per_batch = self.rows_per_batch // self.row_tile
        b0 = self.batch0
        return pl.BlockSpec((None, None, 1, cols),
                            lambda i, j: (b0 + i // tiles_per_batch, idx, 0, col_of(j)))


def _normed(x_ref, g_ref, sc_ref, sh_ref):
    x = x_ref[...]
    ms = jnp.mean(x * x, axis=-1, keepdims=True)
    xn = x * lax.rsqrt(ms + RMS_EPS) * g_ref[...]
    return (xn * (1.0 + sc_ref[...]) + sh_ref[...]).astype(BF16)


def _norm_proj_kernel(x_ref, g_ref, sc_ref, sh_ref, w_ref, o_ref, h_ref):
    @pl.when(pl.program_id(1) == 0)
    def _():
        h_ref[...] = _normed(x_ref, g_ref, sc_ref, sh_ref)

    o_ref[...] = jnp.dot(h_ref[...], w_ref[...].astype(BF16), preferred_element_type=F32)


def _norm_proj(x, gain, mod, sc_idx, sh_idx, w, cols, name):
    m, d = x.shape
    n = w.shape[1]
    tm = min(ROW_TILE, m)
    whole = lambda j: 0
    return pl.pallas_call(
        _norm_proj_kernel,
        grid=(m // tm, n // cols),
        in_specs=[
            pl.BlockSpec((tm, d), lambda i, j: (i, 0)),
            pl.BlockSpec((1, d), lambda i, j: (0, 0)),
            mod.spec(sc_idx, d, whole),
            mod.spec(sh_idx, d, whole),
            pl.BlockSpec((d, cols), lambda i, j: (0, j)),
        ],
        out_specs=pl.BlockSpec((tm, cols), lambda i, j: (i, j)),
        out_shape=jax.ShapeDtypeStruct((m, n), F32),
        scratch_shapes=[pltpu.VMEM((tm, d), BF16)],
        compiler_params=_cparams(("arbitrary", "arbitrary")),
        name=name,
    )(x, gain.reshape(1, d), mod.arr, mod.arr, w)


def _proj_resid_kernel(n_pairs, *refs):
    a_refs = refs[:n_pairs]
    w_refs = refs[n_pairs:2 * n_pairs]
    x_ref, g_ref, o_ref = refs[2 * n_pairs:]
    acc = jnp.dot(a_refs[0][...], w_refs[0][...].astype(BF16), preferred_element_type=F32)
    for a_ref, w_ref in zip(a_refs[1:], w_refs[1:]):
        acc = acc + jnp.dot(a_ref[...], w_ref[...].astype(BF16), preferred_element_type=F32)
    o_ref[...] = x_ref[...] + g_ref[...] * acc


def _proj_resid(a_list, w, x, mod, g_idx, cols, name):
    m, n = x.shape
    tm = min(ROW_TILE, m)
    in_specs, w_args = [], []
    row = 0
    for a in a_list:
        kdim = a.shape[1]
        in_specs.append(pl.BlockSpec((tm, kdim), lambda i, j: (i, 0)))
    for a in a_list:
        kdim = a.shape[1]
        in_specs.append(pl.BlockSpec((kdim, cols), lambda i, j, r=row // kdim: (r, j)))
        w_args.append(w)
        row += kdim
    in_specs.append(pl.BlockSpec((tm, cols), lambda i, j: (i, j)))
    in_specs.append(mod.spec(g_idx, cols, lambda j: j))
    return pl.pallas_call(
        functools.partial(_proj_resid_kernel, len(a_list)),
        grid=(m // tm, n // cols),
        in_specs=in_specs,
        out_specs=pl.BlockSpec((tm, cols), lambda i, j: (i, j)),
        out_shape=jax.ShapeDtypeStruct((m, n), F32),
        compiler_params=_cparams(("arbitrary", "arbitrary")),
        name=name,
    )(*a_list, *w_args, x, mod.arr)


def _gelu(x):
    return 0.5 * x * (1.0 + lax.erf(x * (2.0 ** -0.5)))


def _ffn_up_kernel(tiles_per_batch, x_ref, g_ref, sc_ref, sh_ref, wg_ref, wv_ref, hist_ref,
                   cw_ref, cb_ref, act_ref, last_ref, h_ref, carry_ref):
    i = pl.program_id(0)
    j = pl.program_id(1)

    @pl.when(j == 0)
    def _():
        h_ref[...] = _normed(x_ref, g_ref, sc_ref, sh_ref)

    h = h_ref[...]
    gate = jnp.dot(h, wg_ref[...].astype(BF16), preferred_element_type=F32)
    val = jnp.dot(h, wv_ref[...].astype(BF16), preferred_element_type=F32)
    tm = gate.shape[0]

    @pl.when((i % tiles_per_batch) == 0)
    def _():
        carry_ref[j] = hist_ref[...]

    prev = carry_ref[j]
    row = _iota(gate.shape, 0)
    g1 = pltpu.roll(gate, 1, 0)
    g2 = pltpu.roll(gate, 2, 0)
    g1 = jnp.where(row == 0, prev[1:2], g1)
    g2 = jnp.where(row == 0, prev[0:1], jnp.where(row == 1, prev[1:2], g2))
    cw = cw_ref[...]
    conv = cb_ref[...] + g2 * cw[0:1] + g1 * cw[1:2] + gate * cw[2:3]
    act_ref[...] = (_gelu(conv) * val).astype(BF16)
    tail = gate[tm - 2:tm]
    carry_ref[j] = tail
    last_ref[...] = tail


def _ffn_up_fused(x, gain, mod, w_up, hist, conv_w, conv_b, rows_per_batch):
    m, d = x.shape
    f = w_up.shape[1] // 2
    tm = min(ROW_TILE, rows_per_batch)
    cols = UP_COLS
    nj = f // cols
    tiles_per_batch = rows_per_batch // tm
    whole = lambda j: 0
    act, tile_tails = pl.pallas_call(
        functools.partial(_ffn_up_kernel, tiles_per_batch),
        grid=(m // tm, nj),
        in_specs=[
            pl.BlockSpec((tm, d), lambda i, j: (i, 0)),
            pl.BlockSpec((1, d), lambda i, j: (0, 0)),
            mod.spec(4, d, whole),
            mod.spec(3, d, whole),
            pl.BlockSpec((d, cols), lambda i, j: (0, j)),
            pl.BlockSpec((d, cols), lambda i, j: (0, nj + j)),
            pl.BlockSpec((None, CONV_W - 1, cols), lambda i, j: (i // tiles_per_batch, 0, j)),
            pl.BlockSpec((CONV_W, cols), lambda i, j: (0, j)),
            pl.BlockSpec((1, cols), lambda i, j: (0, j)),
        ],
        out_specs=[
            pl.BlockSpec((tm, cols), lambda i, j: (i, j)),
            pl.BlockSpec((None, CONV_W - 1, cols), lambda i, j: (i, 0, j)),
        ],
        out_shape=[
            jax.ShapeDtypeStruct((m, f), BF16),
            jax.ShapeDtypeStruct((m // tm, CONV_W - 1, f), F32),
        ],
        scratch_shapes=[pltpu.VMEM((tm, d), BF16), pltpu.VMEM((nj, CONV_W - 1, cols), F32)],
        compiler_params=_cparams(("arbitrary", "arbitrary")),
        name="ffn_up_prompt",
    )(x, gain.reshape(1, d), mod.arr, mod.arr, w_up, w_up, hist, conv_w, conv_b.reshape(1, f))
    return act, tile_tails[tiles_per_batch - 1::tiles_per_batch]


def _act_sample_kernel(gate_ref, val_ref, hist_ref, cw_ref, cb_ref, act_ref):
    gate = gate_ref[...]
    hist = hist_ref[...]
    t = _iota(gate.shape, 1)
    g1 = jnp.where(t == 0, hist[:, 1:2], pltpu.roll(gate, 1, 1))
    g2 = jnp.where(t == 0, hist[:, 0:1], jnp.where(t == 1, hist[:, 1:2], pltpu.roll(gate, 2, 1)))
    cw = cw_ref[...]
    conv = cb_ref[...] + g2 * cw[0:1] + g1 * cw[1:2] + gate * cw[2:3]
    act_ref[...] = (_gelu(conv) * val_ref[...]).astype(BF16)


def _act_sample(hu, hist, conv_w, conv_b, nb, t):
    f = hu.shape[1] // 2
    cols = UP_COLS
    nj = f // cols
    hu3 = hu.reshape(nb, t, 2 * f)
    act = pl.pallas_call(
        _act_sample_kernel,
        grid=(nj,),
        in_specs=[
            pl.BlockSpec((nb, t, cols), lambda j: (0, 0, j)),
            pl.BlockSpec((nb, t, cols), lambda j: (0, 0, nj + j)),
            pl.BlockSpec((nb, CONV_W - 1, cols), lambda j: (0, 0, j)),
            pl.BlockSpec((CONV_W, cols), lambda j: (0, j)),
            pl.BlockSpec((1, cols), lambda j: (0, j)),
        ],
        out_specs=pl.BlockSpec((nb, t, cols), lambda j: (0, 0, j)),
        out_shape=jax.ShapeDtypeStruct((nb, t, f), BF16),
        compiler_params=_cparams(("arbitrary",)),
        name="ffn_act_sample",
    )(hu3, hu3, hist, conv_w, conv_b.reshape(1, f))
    return act.reshape(nb * t, f)


def _pair_rms(x, gain, ones):
    ms = _dot_exact_rhs(x * x, ones) * (1.0 / HEAD_DIM)
    return x * lax.rsqrt(ms + RMS_EPS) * gain


def _bias_rows(table):
    h = table.shape[0]
    far = jnp.broadcast_to(table[:, 2 * REL_CLIP:], (h, ATT_REACH - REL_CLIP))
    mid = table[:, ::-1]
    near_len = BIAS_LEN - ATT_QROWS - (ATT_REACH - REL_CLIP) - (2 * REL_CLIP + 1)
    near = jnp.broadcast_to(table[:, 0:1], (h, near_len))
    wrap = jnp.broadcast_to(table[:, 2 * REL_CLIP:], (h, ATT_QROWS))
    return jnp.concatenate([far, mid, near, wrap], axis=1)


def _toeplitz(u_row, rows):
    return pltpu.roll(jnp.broadcast_to(u_row, (rows, BIAS_LEN)), 0, 1, stride=1, stride_axis=0)


def _attn_prompt_kernel(q_ref, k0_ref, k1_ref, k2_ref, v0_ref, v1_ref, v2_ref, qg_ref, kg_ref,
                        u_ref, o_ref, kn_ref, bias_ref):
    qb = pl.program_id(2)

    @pl.when(qb == 0)
    def _():
        for h in range(2):
            bias_ref[h] = _toeplitz(u_ref[h:h + 1, :], ATT_QROWS)[:, :ATT_WIN]

    ones = _head_ones(PAIR)
    qn = _pair_rms(q_ref[...], qg_ref[...], ones)
    kwin = jnp.concatenate([k0_ref[...], k1_ref[...], k2_ref[...]], axis=0)
    kn = _pair_rms(kwin, kg_ref[...], ones)
    kn_ref[...] = kn[ATT_REACH:ATT_WIN]
    kb = kn.astype(BF16)
    vb = jnp.concatenate([v0_ref[...], v1_ref[...], v2_ref[...]], axis=0).astype(BF16)

    shape = (ATT_QROWS, ATT_WIN)
    r = _iota(shape, 0)
    w = _iota(shape, 1)
    chunk_lo = _blk(r, CHUNK) * CHUNK
    lo = jnp.maximum(chunk_lo, ATT_REACH - qb * ATT_QROWS)
    valid = (w >= lo) & (w < chunk_lo + (ATT_REACH + CHUNK))
    lane = _iota((ATT_QROWS, PAIR), 1)
    outs = []
    for h in range(2):
        in_head = _blk(lane, HEAD_DIM) == h
        qh = jnp.where(in_head, qn, 0.0)
        s = _dot(qh, kb, NT) * ATT_SCALE + bias_ref[h]
        s = jnp.where(valid, s, NEG_INF)
        m = jnp.max(s, axis=-1, keepdims=True)
        p = jnp.exp(s - m)
        l = jnp.sum(p, axis=-1, keepdims=True)
        outs.append(_dot(p, vb) / l)
    o_ref[...] = jnp.where(lane < HEAD_DIM, outs[0], outs[1]).astype(BF16)


def _attn_prompt(z3, q_gain, k_gain, u):
    nb, t, _ = z3.shape
    npairs = N_ATT_HEADS // 2
    nq = t // ATT_QROWS
    kcol = D_ATT // PAIR
    vcol = 2 * D_ATT // PAIR
    keep_blocks = ATT_REACH // ATT_QROWS
    blk = (None, ATT_QROWS, PAIR)

    def kv_spec(col0, back):
        return pl.BlockSpec(blk, lambda b, p, q: (b, jnp.maximum(q - back, 0), col0 + p))

    att, kn = pl.pallas_call(
        _attn_prompt_kernel,
        grid=(nb, npairs, nq),
        in_specs=[
            pl.BlockSpec(blk, lambda b, p, q: (b, q, p)),
            kv_spec(kcol, 2), kv_spec(kcol, 1), kv_spec(kcol, 0),
            kv_spec(vcol, 2), kv_spec(vcol, 1), kv_spec(vcol, 0),
            pl.BlockSpec((1, PAIR), lambda b, p, q: (0, 0)),
            pl.BlockSpec((1, PAIR), lambda b, p, q: (0, 0)),
            pl.BlockSpec((None, 2, BIAS_LEN), lambda b, p, q: (p, 0, 0)),
        ],
        out_specs=[
            pl.BlockSpec(blk, lambda b, p, q: (b, q, p)),
            pl.BlockSpec(blk, lambda b, p, q: (b, jnp.maximum(q - (nq - keep_blocks), 0), p)),
        ],
        out_shape=[
            jax.ShapeDtypeStruct((nb, t, D_ATT), BF16),
            jax.ShapeDtypeStruct((nb, ATT_REACH, D_ATT), F32),
        ],
        scratch_shapes=[pltpu.VMEM((2, ATT_QROWS, ATT_WIN), F32)],
        compiler_params=_cparams(("arbitrary", "arbitrary", "arbitrary")),
        name="attn_prompt",
    )(z3, z3, z3, z3, z3, z3, z3, jnp.tile(q_gain, 2).reshape(1, PAIR),
      jnp.tile(k_gain, 2).reshape(1, PAIR), u.reshape(npairs, 2, BIAS_LEN))
    return att, kn


def _attn_sample_kernel(q_ref, k_ref, v_ref, kp_ref, vp_ref, qg_ref, kg_ref, u_ref, o_ref, kn_ref):
    t = q_ref.shape[0]
    reach = kp_ref.shape[0]
    ones = _head_ones(PAIR)
    lane = _iota((t, PAIR), 1)
    for p in range(N_ATT_HEADS // 2):
        cols = slice(p * PAIR, (p + 1) * PAIR)
        qn = _pair_rms(q_ref[:, cols], qg_ref[...], ones)
        kn = _pair_rms(k_ref[:, cols], kg_ref[...], ones)
        kn_ref[:, cols] = kn
        kpast = kp_ref[:, cols].astype(BF16)
        vpast = vp_ref[:, cols].astype(BF16)
        vnew = v_ref[:, cols].astype(BF16)
        outs = []
        for h in range(2):
            in_head = _blk(lane, HEAD_DIM) == h
            qh = jnp.where(in_head, qn, 0.0)
            bias = _toeplitz(u_ref[p, h:h + 1, :], t)
            s_past = _dot(qh, kpast, NT) * ATT_SCALE + bias[:, :reach]
            s_new = _dot(qh, kn, NT) * ATT_SCALE + bias[:, reach:reach + t]
            m = jnp.maximum(jnp.max(s_past, axis=-1, keepdims=True),
                            jnp.max(s_new, axis=-1, keepdims=True))
            p_past = jnp.exp(s_past - m)
            p_new = jnp.exp(s_new - m)
            l = jnp.sum(p_past, axis=-1, keepdims=True) + jnp.sum(p_new, axis=-1, keepdims=True)
            outs.append((_dot(p_past, vpast) + _dot(p_new, vnew)) / l)
        o_ref[:, cols] = jnp.where(lane < HEAD_DIM, outs[0], outs[1]).astype(BF16)


def _attn_sample(z3, k_past, v_past, q_gain, k_gain, u):
    nb, t, _ = z3.shape
    reach = k_past.shape[1]
    npairs = N_ATT_HEADS // 2
    att, kn = pl.pallas_call(
        _attn_sample_kernel,
        grid=(nb,),
        in_specs=[
            pl.BlockSpec((None, t, D_ATT), lambda b: (b, 0, 0)),
            pl.BlockSpec((None, t, D_ATT), lambda b: (b, 0, 1)),
            pl.BlockSpec((None, t, D_ATT), lambda b: (b, 0, 2)),
            pl.BlockSpec((None, reach, D_ATT), lambda b: (b, 0, 0)),
            pl.BlockSpec((None, reach, D_ATT), lambda b: (b, 0, 0)),
            pl.BlockSpec((1, PAIR), lambda b: (0, 0)),
            pl.BlockSpec((1, PAIR), lambda b: (0, 0)),
            pl.BlockSpec((npairs, 2, BIAS_LEN), lambda b: (0, 0, 0)),
        ],
        out_specs=[
            pl.BlockSpec((None, t, D_ATT), lambda b: (b, 0, 0)),
            pl.BlockSpec((None, t, D_ATT), lambda b: (b, 0, 0)),
        ],
        out_shape=[
            jax.ShapeDtypeStruct((nb, t, D_ATT), BF16),
            jax.ShapeDtypeStruct((nb, t, D_ATT), F32),
        ],
        compiler_params=_cparams(("arbitrary",)),
        name="attn_sample",
    )(z3, z3, z3, k_past.reshape(nb, reach, D_ATT), v_past.reshape(nb, reach, D_ATT),
      jnp.tile(q_gain, 2).reshape(1, PAIR), jnp.tile(k_gain, 2).reshape(1, PAIR),
      u.reshape(npairs, 2, BIAS_LEN))
    return att, kn


def _tri_inverse(l_mat, c):
    n = l_mat.shape[0]
    eye = jnp.where(_iota((n, n), 0) == _iota((n, n), 1), 1.0, 0.0).astype(F32)
    a = eye + l_mat
    t = eye - l_mat
    for _ in range(c.bit_length() - 3):
        t = t + _dot(t, eye - _dot(a, t))
    return t + _dot(t, eye - _dot3(a, t))


def _rwkv_kernel(c, r_ref, k_ref, v_ref, lo_ref, sr_ref, sk_ref, sv_ref, slo_ref, s0_ref,
                 mur_ref, muk_ref, muv_ref, mulo_ref, w0_ref, a0_ref, kkg_ref, ka_ref, rk_ref,
                 lnw_ref, lnb_ref, w2_ref, a2_ref, g2_ref,
                 o_ref, sT_ref, s_ref, cr_ref, ck_ref, cv_ref, clo_ref):
    tb = pl.program_id(2)
    rows = r_ref.shape[0]
    nchunks = rows // c
    h0 = _iota((rows, PAIR), 1) < HEAD_DIM
    bd = _blk(_iota((PAIR, PAIR), 0), HEAD_DIM) == _blk(_iota((PAIR, PAIR), 1), HEAD_DIM)

    @pl.when(tb == 0)
    def _():
        s_ref[...] = jnp.zeros((PAIR, PAIR), F32)
        s_ref[0:HEAD_DIM, 0:HEAD_DIM] = s0_ref[0]
        s_ref[HEAD_DIM:PAIR, HEAD_DIM:PAIR] = s0_ref[1]
        cr_ref[...] = sr_ref[...]
        ck_ref[...] = sk_ref[...]
        cv_ref[...] = sv_ref[...]
        clo_ref[...] = slo_ref[...]

    def shifted(x_ref, carry_ref, mu_ref):
        x = x_ref[...]
        prev = jnp.where(_iota(x.shape, 0) == 0, carry_ref[...], pltpu.roll(x, 1, 0))
        carry_ref[...] = x[rows - 1:rows]
        return x + (prev - x) * mu_ref[...]

    r = shifted(r_ref, cr_ref, mur_ref)
    k = shifted(k_ref, ck_ref, muk_ref)
    v = shifted(v_ref, cv_ref, muv_ref)
    lo = shifted(lo_ref, clo_ref, mulo_ref)

    zeros_w = jnp.zeros((RANK_W, PAIR), F32)
    w2p = jnp.concatenate([w2_ref[...], zeros_w], axis=0)
    a2p = jnp.concatenate([zeros_w, a2_ref[...]], axis=0)
    lo_wa = lo[:, 0:RANK_W + RANK_A]
    u = w0_ref[...] + _dot3(jnp.tanh(lo_wa), w2p)
    lw = -jnp.exp(-_softplus(-u) - 0.5)
    a = _sigmoid(a0_ref[...] + _dot3(lo_wa, a2p))
    g = _dot(_sigmoid(lo[:, RANK_W + RANK_A:]), g2_ref[...])

    ones = _head_ones(PAIR)
    kk = k * kkg_ref[...]
    kk = kk / jnp.maximum(jnp.sqrt(_dot_exact_rhs(kk * kk, ones)), 1e-12)
    k = k * (1.0 + (a - 1.0) * ka_ref[...])
    b = kk * a
    bonus = _dot_exact_rhs(r * k * rk_ref[...], ones) * v

    tr = _iota((rows, rows), 0)
    tc = _iota((rows, rows), 1)
    same_chunk = _blk(tr, c) == _blk(tc, c)
    strict = same_chunk & (tr > tc)
    incl = same_chunk & (tr >= tc)
    lp = _dot_exact_lhs(jnp.where(incl, 1.0, 0.0).astype(BF16), lw)
    lp_end = _dot_exact_lhs(jnp.where(same_chunk, 1.0, 0.0).astype(BF16), lw)

    alpha = kk * jnp.exp(lp - lw)
    inv_p = jnp.exp(-lp)
    beta = b * inv_p
    kappa = k * inv_p
    rho = r * jnp.exp(lp)
    to_end = jnp.exp(lp_end - lp)
    beta_e = b * to_end
    kappa_e = k * to_end

    bk = jnp.concatenate([beta, kappa], axis=0).astype(BF16)
    ws, mqb, mqk = [], [], []
    for h in range(2):
        hm = h0 if h == 0 else jnp.logical_not(h0)
        ar = jnp.concatenate([jnp.where(hm, alpha, 0.0), jnp.where(hm, rho, 0.0)], axis=0)
        prod = _dot(ar, bk, NT)
        t_inv = _tri_inverse(jnp.where(strict, prod[:rows, :rows], 0.0), c)
        x = _dot(jnp.where(strict, prod[:rows, rows:], 0.0), v)
        ws.append(_dot(t_inv, jnp.concatenate([alpha, x], axis=1)))
        mqb.append(jnp.where(incl, prod[rows:, :rows], 0.0))
        mqk.append(jnp.where(incl, prod[rows:, rows:], 0.0))
    w1 = jnp.where(h0, ws[0][:, :PAIR], ws[1][:, :PAIR])
    w2 = jnp.where(h0, ws[0][:, PAIR:], ws[1][:, PAIR:])
    w12 = jnp.concatenate([w1, w2], axis=1)
    q0 = _dot(mqb[0], w12)
    q1 = _dot(mqb[1], w12)
    rp = rho - jnp.where(h0, q0[:, :PAIR], q1[:, :PAIR])
    y0 = jnp.where(h0, _dot(mqk[0], v) - q0[:, PAIR:], _dot(mqk[1], v) - q1[:, PAIR:])

    wide = (rows, nchunks * PAIR)
    col_chunk = _blk(_iota(wide, 1), PAIR) == _blk(_iota(wide, 0), c)
    spread = lambda m: jnp.where(col_chunk, jnp.tile(m, (1, nchunks)), 0.0)
    wtb = _dot(w12, spread(beta_e), TN)
    vtk = _dot(v, spread(kappa_e), TN)
    eye_p = _iota((PAIR, PAIR), 0) == _iota((PAIR, PAIR), 1)

    s_cur = s_ref[...]
    ys = []
    for ci in range(nchunks):
        sl = slice(ci * c, (ci + 1) * c)
        cols = slice(ci * PAIR, (ci + 1) * PAIR)
        decay_end = jnp.exp(lp_end[ci * c:ci * c + 1])
        gmat = jnp.where(eye_p, jnp.broadcast_to(decay_end, (PAIR, PAIR)), 0.0) \
            - jnp.where(bd, wtb[:PAIR, cols], 0.0)
        hmat = jnp.where(bd, vtk[:, cols] - wtb[PAIR:, cols], 0.0)
        ys.append(_dot(rp[sl], s_cur, NT) + y0[sl])
        s_cur = _dot(s_cur, gmat) + hmat
    s_ref[...] = s_cur
    y = ys[0] if nchunks == 1 else jnp.concatenate(ys, axis=0)

    mu = _dot_exact_rhs(y, ones) * (1.0 / HEAD_DIM)
    d = y - mu
    var = _dot_exact_rhs(d * d, ones) * (1.0 / HEAD_DIM)
    yn = d * lax.rsqrt(var + GN_EPS) * lnw_ref[...] + lnb_ref[...]
    o_ref[...] = ((yn + bonus) * g).astype(BF16)

    @pl.when(tb == pl.num_programs(2) - 1)
    def _():
        sT_ref[0] = s_cur[0:HEAD_DIM, 0:HEAD_DIM]
        sT_ref[1] = s_cur[HEAD_DIM:PAIR, HEAD_DIM:PAIR]


def _rwkv(z3, shift_prev, s0, p, rows, c):
    nb, t, _ = z3.shape
    npairs = N_RWKV_HEADS // 2
    col0 = 3 * D_ATT // PAIR
    lo_blk = (3 * D_ATT + 3 * D_RWKV) // D_LORA
    sp = shift_prev.reshape(nb, 1, D_SHIFT)
    rp = D_RWKV // PAIR

    def zspec(off):
        return pl.BlockSpec((None, rows, PAIR), lambda b, q, s: (b, s, col0 + off * rp + q))

    def sspec(off):
        return pl.BlockSpec((None, 1, PAIR), lambda b, q, s: (b, 0, off * rp + q))

    def vec(off=0):
        return pl.BlockSpec((1, PAIR), lambda b, q, s: (0, off * rp + q))

    def row2(x):
        return x.reshape(1, -1)

    out, s_fin = pl.pallas_call(
        functools.partial(_rwkv_kernel, c),
        grid=(nb, npairs, t // rows),
        in_specs=[
            zspec(0), zspec(1), zspec(2),
            pl.BlockSpec((None, rows, D_LORA), lambda b, q, s: (b, s, lo_blk)),
            sspec(0), sspec(1), sspec(2),
            pl.BlockSpec((None, 1, D_LORA), lambda b, q, s: (b, 0, 3 * D_RWKV // D_LORA)),
            pl.BlockSpec((None, 2, HEAD_DIM, HEAD_DIM), lambda b, q, s: (b, q, 0, 0)),
            vec(0), vec(1), vec(2),
            pl.BlockSpec((1, D_LORA), lambda b, q, s: (0, 3 * D_RWKV // D_LORA)),
            vec(), vec(), vec(), vec(), vec(), vec(), vec(),
            pl.BlockSpec((RANK_W, PAIR), lambda b, q, s: (0, q)),
            pl.BlockSpec((RANK_A, PAIR), lambda b, q, s: (0, q)),
            pl.BlockSpec((RANK_G, PAIR), lambda b, q, s: (0, q)),
        ],
        out_specs=[
            pl.BlockSpec((None, rows, PAIR), lambda b, q, s: (b, s, q)),
            pl.BlockSpec((None, 2, HEAD_DIM, HEAD_DIM), lambda b, q, s: (b, q, 0, 0)),
        ],
        out_shape=[
            jax.ShapeDtypeStruct((nb, t, D_RWKV), BF16),
            jax.ShapeDtypeStruct((nb, N_RWKV_HEADS, HEAD_DIM, HEAD_DIM), F32),
        ],
        scratch_shapes=[
            pltpu.VMEM((PAIR, PAIR), F32),
            pltpu.VMEM((1, PAIR), F32), pltpu.VMEM((1, PAIR), F32), pltpu.VMEM((1, PAIR), F32),
            pltpu.VMEM((1, D_LORA), F32),
        ],
        compiler_params=_cparams(("arbitrary", "arbitrary", "arbitrary")),
        name="rwkv7_mix",
    )(z3, z3, z3, z3, sp, sp, sp, sp, s0,
      row2(p['mu_shift']), row2(p['mu_shift']), row2(p['mu_shift']), row2(p['mu_shift']),
      row2(p['w0']), row2(p['a0']), row2(p['k_k']), row2(p['k_a']), row2(p['r_k']),
      row2(p['ln_x_w']), row2(p['ln_x_b']), p['w2'], p['a2'], p['g2'])
    return out, s_fin


def _layer(x3, mod, p, u, k_past, v_past, s0, shift_prev, conv_prev):
    nb, t, d = x3.shape
    m = nb * t
    x = x3.reshape(m, d)
    z = _norm_proj(x, p['norm_att_g'], mod, 1, 0, p['w_in'], IN_COLS, "in_proj")
    z3 = z.reshape(nb, t, D_IN)
    if k_past is None:
        att, k_keep = _attn_prompt(z3, p['q_norm_g'], p['k_norm_g'], u)
        keep = min(ATT_REACH, t)
        v_keep = z3[:, t - keep:, 2 * D_ATT:3 * D_ATT]
        rw, s_fin = _rwkv(z3, shift_prev, s0, p, RWKV_ROWS, CHUNK)
    else:
        att, k_keep = _attn_sample(z3, k_past, v_past, p['q_norm_g'], p['k_norm_g'], u)
        v_keep = z3[:, :, 2 * D_ATT:3 * D_ATT]
        rw, s_fin = _rwkv(z3, shift_prev, s0, p, t, t)
    shift_last = z3[:, t - 1, 3 * D_ATT:]
    x1 = _proj_resid([att.reshape(m, D_ATT), rw.reshape(m, D_RWKV)], p['w_out'], x, mod, 2,
                     OUT_COLS, "out_proj")
    if k_past is None:
        act, conv_last = _ffn_up_fused(x1, p['norm_ffn_g'], mod, p['w_up'], conv_prev,
                                       p['dw_conv'], p['dw_bias'], t)
    else:
        hu = _norm_proj(x1, p['norm_ffn_g'], mod, 4, 3, p['w_up'], UP_COLS, "ffn_up_sample")
        f = hu.shape[1] // 2
        act = _act_sample(hu, conv_prev, p['dw_conv'], p['dw_bias'], nb, t)
        conv_last = hu.reshape(nb, t, 2 * f)[:, t - (CONV_W - 1):, :f]
    x2 = _proj_resid([act], p['w_down'], x1, mod, 5, DOWN_COLS, "ffn_down")
    heads = lambda a: a.reshape(nb, a.shape[1], N_ATT_HEADS, HEAD_DIM)
    return x2.reshape(nb, t, d), heads(k_keep), heads(v_keep), s_fin, shift_last, conv_last


def kernel(x_prompt, x_sample, c_prompt, c_sample, cache_att_k, cache_att_v, state_rwkv, state_shift, state_ffn_conv, norm_att_g, norm_ffn_g, w_ada, b_ada, w_in, q_norm_g, k_norm_g, rel_bias, mu_shift, w0, w2, a0, a2, g2, k_k, k_a, r_k, ln_x_w, ln_x_b, w_out, w_up, dw_conv, dw_bias, w_down):
    depth = w_in.shape[0]
    bp, tp, d = x_prompt.shape
    bs, ts, _ = x_sample.shape
    d_ff = w_down.shape[1]
    hp, hs = x_prompt, x_sample
    outs_p = [[] for _ in range(5)]
    outs_s = [[] for _ in range(5)]
    for l in range(depth):
        p = dict(norm_att_g=norm_att_g[l], norm_ffn_g=norm_ffn_g[l], w_in=w_in[l], q_norm_g=q_norm_g[l],
                 k_norm_g=k_norm_g[l], mu_shift=mu_shift[l], w0=w0[l], w2=w2[l], a0=a0[l], a2=a2[l],
                 g2=g2[l], k_k=k_k[l], k_a=k_a[l], r_k=r_k[l], ln_x_w=ln_x_w[l], ln_x_b=ln_x_b[l],
                 w_out=w_out[l], w_up=w_up[l], dw_conv=dw_conv[l], dw_bias=dw_bias[l], w_down=w_down[l])
        n_c = bp + bs
        pad = (-n_c) % 8
        c_all = jnp.concatenate([c_prompt, c_sample, jnp.zeros((pad, d), F32)], axis=0)
        mod = _ada(c_all, w_ada[l], b_ada[l])
        mod_p = _Mod(mod.reshape(n_c + pad, 6, 1, d), False, rows_per_batch=tp,
                     row_tile=min(ROW_TILE, tp))
        mod_s = _Mod(jnp.repeat(mod[bp:bp + bs], ts, axis=0), True)
        u = _bias_rows(rel_bias[l])

        res = _layer(hp, mod_p, p, u, None, None,
                     jnp.zeros((bp, N_RWKV_HEADS, HEAD_DIM, HEAD_DIM), F32),
                     jnp.zeros((bp, D_SHIFT), F32),
                     jnp.zeros((bp, CONV_W - 1, d_ff), F32))
        hp = res[0]
        for lst, val in zip(outs_p, res[1:]):
            lst.append(val)
        res = _layer(hs, mod_s, p, u, cache_att_k[l], cache_att_v[l], state_rwkv[l],
                     state_shift[l], state_ffn_conv[l])
        hs = res[0]
        for lst, val in zip(outs_s, res[1:]):
            lst.append(val)
    st = lambda lst: jnp.stack(lst)
    return (hp, hs, *[st(x) for x in outs_p], *[st(x) for x in outs_s])
```

```python
import functools

import jax
import jax.numpy as jnp
from jax import lax
from jax.experimental import pallas as pl
from jax.experimental.pallas import tpu as pltpu

F32 = jnp.float32
BF16 = jnp.bfloat16

CHUNK = 64
N_PREV_CHUNKS = 8
ATT_REACH = N_PREV_CHUNKS * CHUNK
HEAD_DIM = 64
N_ATT_HEADS = 16
N_RWKV_HEADS = 16
D_ATT = N_ATT_HEADS * HEAD_DIM
D_RWKV = N_RWKV_HEADS * HEAD_DIM
REL_CLIP = 128
RANK_W = 64
RANK_A = 64
RANK_G = 128
D_LORA = RANK_W + RANK_A + RANK_G
D_SHIFT = 3 * D_RWKV + D_LORA
D_IN = 3 * D_ATT + D_SHIFT
CONV_W = 3
RMS_EPS = 1e-6
GN_EPS = 64e-5
ATT_SCALE = HEAD_DIM ** -0.5
NEG_INF = -1e30

LANES = 128
PAIR = 2 * HEAD_DIM
MXU_DIM = 256
VMEM_LIMIT = 56 * 1024 * 1024

ROW_TILE = 1024
ADA_COLS = 512
IN_COLS = 640
OUT_COLS = 512
UP_COLS = 512
DOWN_COLS = 256
ATT_QROWS = 256
ATT_WIN = ATT_QROWS + ATT_REACH
BIAS_LEN = 1024
RWKV_ROWS = 256
RWKV_PAIRS_PROMPT = 4
RWKV_PAIRS_SAMPLE = 8


def _cparams(sem):
    return pltpu.CompilerParams(dimension_semantics=sem, vmem_limit_bytes=VMEM_LIMIT)


def _dot(a, b, dims=(((1,), (0,)), ((), ()))):
    return lax.dot_general(a.astype(BF16), b.astype(BF16), dims, preferred_element_type=F32)


def _split2(x):
    hi = x.astype(BF16)
    lo = (x - hi.astype(F32)).astype(BF16)
    return hi, lo


def _dot3(a, b, dims=(((1,), (0,)), ((), ()))):
    a1, a2 = _split2(a)
    b1, b2 = _split2(b)
    d = functools.partial(lax.dot_general, dimension_numbers=dims, preferred_element_type=F32)
    return d(a1, b1) + (d(a1, b2) + d(a2, b1))


def _dot_exact_rhs(a, b_bf16):
    a1 = a.astype(BF16)
    r1 = a - a1.astype(F32)
    a2 = r1.astype(BF16)
    a3 = (r1 - a2.astype(F32)).astype(BF16)
    d = functools.partial(jnp.dot, preferred_element_type=F32)
    return d(a1, b_bf16) + (d(a2, b_bf16) + d(a3, b_bf16))


def _dot_exact_lhs(a_bf16, b):
    b1 = b.astype(BF16)
    r1 = b - b1.astype(F32)
    b2 = r1.astype(BF16)
    b3 = (r1 - b2.astype(F32)).astype(BF16)
    d = functools.partial(jnp.dot, preferred_element_type=F32)
    return d(a_bf16, b1) + (d(a_bf16, b2) + d(a_bf16, b3))


NT = (((1,), (1,)), ((), ()))
TN = (((0,), (0,)), ((), ()))


def _iota(shape, dim):
    return lax.broadcasted_iota(jnp.int32, shape, dim)


def _blk(x, size):
    return jnp.right_shift(x, size.bit_length() - 1)


def _head_ones(n):
    r = _blk(_iota((n, n), 0), HEAD_DIM)
    c = _blk(_iota((n, n), 1), HEAD_DIM)
    return jnp.where(r == c, 1.0, 0.0).astype(BF16)


def _head_sums(x):
    lanes = x.shape[1]
    group = min(lanes, MXU_DIM)
    ones = _head_ones(group)
    parts = [_dot(x[:, i:i + group], ones) for i in range(0, lanes, group)]
    return parts[0] if len(parts) == 1 else jnp.concatenate(parts, axis=1)


def _sigmoid(x):
    return 1.0 / (1.0 + jnp.exp(-x))


def _softplus(x):
    return jnp.maximum(x, 0.0) + jnp.log(1.0 + jnp.exp(-jnp.abs(x)))


def _ada_kernel(c_ref, w_ref, b_ref, o_ref):
    c = c_ref[...]
    s = c * _sigmoid(c)
    o_ref[...] = _dot3(s, w_ref[...]) + b_ref[...]


def _ada(c_all, w_ada, b_ada):
    rows, d = c_all.shape
    n = w_ada.shape[1]
    return pl.pallas_call(
        _ada_kernel,
        grid=(n // ADA_COLS,),
        in_specs=[
            pl.BlockSpec((rows, d), lambda j: (0, 0)),
            pl.BlockSpec((d, ADA_COLS), lambda j: (0, j)),
            pl.BlockSpec((1, ADA_COLS), lambda j: (0, j)),
        ],
        out_specs=pl.BlockSpec((rows, ADA_COLS), lambda j: (0, j)),
        out_shape=jax.ShapeDtypeStruct((rows, n), F32),
        compiler_params=_cparams(("arbitrary",)),
        name="ada_mod",
    )(c_all, w_ada, b_ada.reshape(1, n))


class _Mod:
    def __init__(self, arr, per_row, rows_per_batch=None, row_tile=None, batch0=0):
        self.arr = arr
        self.per_row = per_row
        self.rows_per_batch = rows_per_batch
        self.row_tile = row_tile
        self.batch0 = batch0

    def spec(self, idx, cols, col_of):
        if self.per_row:
            m = self.arr.shape[0]
            d = self.arr.shape[1] // 6
            nblk = d // cols
            return pl.BlockSpec((m, cols), lambda i, j: (0, idx * nblk + col_of(j)))
        tiles_per_batch = self.rows_per_batch // self.row_tile
        b0 = self.batch0
        return pl.BlockSpec((None, None, 1, cols),
                            lambda i, j: (b0 + i // tiles_per_batch, idx, 0, col_of(j)))


def _normed(x_ref, g_ref, sc_ref, sh_ref):
    x = x_ref[...]
    ms = jnp.mean(x * x, axis=-1, keepdims=True)
    xn = x * lax.rsqrt(ms + RMS_EPS) * g_ref[...]
    return (xn * (1.0 + sc_ref[...]) + sh_ref[...]).astype(BF16)


def _norm_proj_kernel(x_ref, g_ref, sc_ref, sh_ref, w_ref, o_ref, h_ref):
    @pl.when(pl.program_id(1) == 0)
    def _():
        h_ref[...] = _normed(x_ref, g_ref, sc_ref, sh_ref)

    o_ref[...] = jnp.dot(h_ref[...], w_ref[...].astype(BF16), preferred_element_type=F32)


def _norm_proj(x, gain, mod, sc_idx, sh_idx, w, cols, name):
    m, d = x.shape
    n = w.shape[1]
    tm = min(ROW_TILE, m)
    whole = lambda j: 0
    return pl.pallas_call(
        _norm_proj_kernel,
        grid=(m // tm, n // cols),
        in_specs=[
            pl.BlockSpec((tm, d), lambda i, j: (i, 0)),
            pl.BlockSpec((1, d), lambda i, j: (0, 0)),
            mod.spec(sc_idx, d, whole),
            mod.spec(sh_idx, d, whole),
            pl.BlockSpec((d, cols), lambda i, j: (0, j)),
        ],
        out_specs=pl.BlockSpec((tm, cols), lambda i, j: (i, j)),
        out_shape=jax.ShapeDtypeStruct((m, n), F32),
        scratch_shapes=[pltpu.VMEM((tm, d), BF16)],
        compiler_params=_cparams(("arbitrary", "arbitrary")),
        name=name,
    )(x, gain.reshape(1, d), mod.arr, mod.arr, w)


def _proj_resid_kernel(n_pairs, *refs):
    a_refs = refs[:n_pairs]
    w_refs = refs[n_pairs:2 * n_pairs]
    x_ref, g_ref, o_ref = refs[2 * n_pairs:]
    acc = jnp.dot(a_refs[0][...], w_refs[0][...].astype(BF16), preferred_element_type=F32)
    for a_ref, w_ref in zip(a_refs[1:], w_refs[1:]):
        acc = acc + jnp.dot(a_ref[...], w_ref[...].astype(BF16), preferred_element_type=F32)
    o_ref[...] = x_ref[...] + g_ref[...] * acc


def _proj_resid(a_list, w, x, mod, g_idx, cols, name):
    m, n = x.shape
    tm = min(ROW_TILE, m)
    in_specs, w_args = [], []
    row = 0
    for a in a_list:
        kdim = a.shape[1]
        in_specs.append(pl.BlockSpec((tm, kdim), lambda i, j: (i, 0)))
    for a in a_list:
        kdim = a.shape[1]
        in_specs.append(pl.BlockSpec((kdim, cols), lambda i, j, r=row // kdim: (r, j)))
        w_args.append(w)
        row += kdim
    in_specs.append(pl.BlockSpec((tm, cols), lambda i, j: (i, j)))
    in_specs.append(mod.spec(g_idx, cols, lambda j: j))
    return pl.pallas_call(
        functools.partial(_proj_resid_kernel, len(a_list)),
        grid=(m // tm, n // cols),
        in_specs=in_specs,
        out_specs=pl.BlockSpec((tm, cols), lambda i, j: (i, j)),
        out_shape=jax.ShapeDtypeStruct((m, n), F32),
        compiler_params=_cparams(("arbitrary", "arbitrary")),
        name=name,
    )(*a_list, *w_args, x, mod.arr)


def _gelu(x):
    return 0.5 * x * (1.0 + lax.erf(x * (2.0 ** -0.5)))


def _ffn_up_kernel(tiles_per_batch, x_ref, g_ref, sc_ref, sh_ref, wg_ref, wv_ref, hist_ref,
                   cw_ref, cb_ref, act_ref, last_ref, h_ref, carry_ref):
    i = pl.program_id(0)
    j = pl.program_id(1)

    @pl.when(j == 0)
    def _():
        h_ref[...] = _normed(x_ref, g_ref, sc_ref, sh_ref)

    h = h_ref[...]
    gate = jnp.dot(h, wg_ref[...].astype(BF16), preferred_element_type=F32)
    val = jnp.dot(h, wv_ref[...].astype(BF16), preferred_element_type=F32)
    tm = gate.shape[0]

    @pl.when((i % tiles_per_batch) == 0)
    def _():
        carry_ref[j] = hist_ref[...]

    prev = carry_ref[j]
    row = _iota(gate.shape, 0)
    g1 = pltpu.roll(gate, 1, 0)
    g2 = pltpu.roll(gate, 2, 0)
    g1 = jnp.where(row == 0, prev[1:2], g1)
    g2 = jnp.where(row == 0, prev[0:1], jnp.where(row == 1, prev[1:2], g2))
    cw = cw_ref[...]
    conv = cb_ref[...] + g2 * cw[0:1] + g1 * cw[1:2] + gate * cw[2:3]
    act_ref[...] = (_gelu(conv) * val).astype(BF16)
    tail = gate[tm - 2:tm]
    carry_ref[j] = tail
    last_ref[...] = tail


def _ffn_up_fused(x, gain, mod, w_up, hist, conv_w, conv_b, rows_per_batch):
    m, d = x.shape
    f = w_up.shape[1] // 2
    tm = min(ROW_TILE, rows_per_batch)
    cols = UP_COLS
    nj = f // cols
    tiles_per_batch = rows_per_batch // tm
    whole = lambda j: 0
    act, tile_tails = pl.pallas_call(
        functools.partial(_ffn_up_kernel, tiles_per_batch),
        grid=(m // tm, nj),
        in_specs=[
            pl.BlockSpec((tm, d), lambda i, j: (i, 0)),
            pl.BlockSpec((1, d), lambda i, j: (0, 0)),
            mod.spec(4, d, whole),
            mod.spec(3, d, whole),
            pl.BlockSpec((d, cols), lambda i, j: (0, j)),
            pl.BlockSpec((d, cols), lambda i, j: (0, nj + j)),
            pl.BlockSpec((None, CONV_W - 1, cols), lambda i, j: (i // tiles_per_batch, 0, j)),
            pl.BlockSpec((CONV_W, cols), lambda i, j: (0, j)),
            pl.BlockSpec((1, cols), lambda i, j: (0, j)),
        ],
        out_specs=[
            pl.BlockSpec((tm, cols), lambda i, j: (i, j)),
            pl.BlockSpec((None, CONV_W - 1, cols), lambda i, j: (i, 0, j)),
        ],
        out_shape=[
            jax.ShapeDtypeStruct((m, f), BF16),
            jax.ShapeDtypeStruct((m // tm, CONV_W - 1, f), F32),
        ],
        scratch_shapes=[pltpu.VMEM((tm, d), BF16), pltpu.VMEM((nj, CONV_W - 1, cols), F32)],
        compiler_params=_cparams(("arbitrary", "arbitrary")),
        name="ffn_up_prompt",
    )(x, gain.reshape(1, d), mod.arr, mod.arr, w_up, w_up, hist, conv_w, conv_b.reshape(1, f))
    return act, tile_tails[tiles_per_batch - 1::tiles_per_batch]


def _act_sample_kernel(gate_ref, val_ref, hist_ref, cw_ref, cb_ref, act_ref):
    gate = gate_ref[...]
    hist = hist_ref[...]
    t = _iota(gate.shape, 1)
    g1 = jnp.where(t == 0, hist[:, 1:2], pltpu.roll(gate, 1, 1))
    g2 = jnp.where(t == 0, hist[:, 0:1], jnp.where(t == 1, hist[:, 1:2], pltpu.roll(gate, 2, 1)))
    cw = cw_ref[...]
    conv = cb_ref[...] + g2 * cw[0:1] + g1 * cw[1:2] + gate * cw[2:3]
    act_ref[...] = (_gelu(conv) * val_ref[...]).astype(BF16)


def _act_sample(hu, hist, conv_w, conv_b, nb, t):
    f = hu.shape[1] // 2
    cols = UP_COLS
    nj = f // cols
    hu3 = hu.reshape(nb, t, 2 * f)
    act = pl.pallas_call(
        _act_sample_kernel,
        grid=(nj,),
        in_specs=[
            pl.BlockSpec((nb, t, cols), lambda j: (0, 0, j)),
            pl.BlockSpec((nb, t, cols), lambda j: (0, 0, nj + j)),
            pl.BlockSpec((nb, CONV_W - 1, cols), lambda j: (0, 0, j)),
            pl.BlockSpec((CONV_W, cols), lambda j: (0, j)),
            pl.BlockSpec((1, cols), lambda j: (0, j)),
        ],
        out_specs=pl.BlockSpec((nb, t, cols), lambda j: (0, 0, j)),
        out_shape=jax.ShapeDtypeStruct((nb, t, f), BF16),
        compiler_params=_cparams(("arbitrary",)),
        name="ffn_act_sample",
    )(hu3, hu3, hist, conv_w, conv_b.reshape(1, f))
    return act.reshape(nb * t, f)


def _pair_rms(x, gain):
    x2 = x * x
    first = _iota(x.shape, 1) < HEAD_DIM
    s0 = jnp.sum(jnp.where(first, x2, 0.0), axis=-1, keepdims=True)
    s1 = jnp.sum(jnp.where(first, 0.0, x2), axis=-1, keepdims=True)
    ms = jnp.where(first, s0, s1) * (1.0 / HEAD_DIM)
    return x * lax.rsqrt(ms + RMS_EPS) * gain


def _bias_rows(table):
    h = table.shape[0]
    far = jnp.broadcast_to(table[:, 2 * REL_CLIP:], (h, ATT_REACH - REL_CLIP))
    mid = table[:, ::-1]
    near_len = BIAS_LEN - ATT_QROWS - (ATT_REACH - REL_CLIP) - (2 * REL_CLIP + 1)
    near = jnp.broadcast_to(table[:, 0:1], (h, near_len))
    wrap = jnp.broadcast_to(table[:, 2 * REL_CLIP:], (h, ATT_QROWS))
    return jnp.concatenate([far, mid, near, wrap], axis=1)


def _toeplitz(u_row, rows):
    return pltpu.roll(jnp.broadcast_to(u_row, (rows, BIAS_LEN)), 0, 1, stride=1, stride_axis=0)


def _attn_prompt_kernel(q_ref, k_ref, v_ref, qg_ref, kg_ref, u_ref, o_ref, kn_ref,
                        bias_ref, kwin_ref, vwin_ref):
    qb = pl.program_id(2)
    shape = (ATT_QROWS, ATT_WIN)

    @pl.when(qb == 0)
    def _():
        r = _iota(shape, 0)
        w = _iota(shape, 1)
        chunk_lo = _blk(r, CHUNK) * CHUNK
        in_band = (w >= chunk_lo) & (w < chunk_lo + (ATT_REACH + CHUNK))
        for h in range(2):
            bias = _toeplitz(u_ref[h:h + 1, :], ATT_QROWS)[:, :ATT_WIN]
            bias_ref[h] = jnp.where(in_band, bias, -jnp.inf)
        kwin_ref[0:ATT_REACH] = jnp.zeros((ATT_REACH, PAIR), BF16)
        vwin_ref[0:ATT_REACH] = jnp.zeros((ATT_REACH, PAIR), BF16)

    @pl.when(qb > 0)
    def _():
        kwin_ref[0:ATT_REACH] = kwin_ref[ATT_QROWS:ATT_WIN]
        vwin_ref[0:ATT_REACH] = vwin_ref[ATT_QROWS:ATT_WIN]

    kn = _pair_rms(k_ref[...], kg_ref[...])
    kn_ref[...] = kn
    kwin_ref[ATT_REACH:ATT_WIN] = kn.astype(BF16)
    vwin_ref[ATT_REACH:ATT_WIN] = v_ref[...].astype(BF16)
    qn = _pair_rms(q_ref[...], qg_ref[...]) * ATT_SCALE
    kb = kwin_ref[...]
    vb = vwin_ref[...]

    first = _iota((ATT_QROWS, PAIR), 1) < HEAD_DIM
    started = _iota(shape, 1) >= ATT_REACH - qb * ATT_QROWS
    heads = range(2)
    qh = [jnp.where(first, qn, 0.0), jnp.where(first, 0.0, qn)]
    s = [_dot(qh[h], kb, NT) + bias_ref[h] for h in heads]
    s = [jnp.maximum(jnp.where(started, s[h], -jnp.inf), NEG_INF) for h in heads]
    m = [jnp.max(s[h], axis=-1, keepdims=True) for h in heads]
    p = [jnp.exp(s[h] - m[h]) for h in heads]
    l = [jnp.sum(p[h], axis=-1, keepdims=True) for h in heads]
    o = [_dot(p[h], vb) for h in heads]
    o_ref[...] = jnp.where(first, o[0] / l[0], o[1] / l[1]).astype(BF16)


def _attn_prompt(z3, q_gain, k_gain, u):
    nb, t, _ = z3.shape
    npairs = N_ATT_HEADS // 2
    nq = t // ATT_QROWS
    kcol = D_ATT // PAIR
    vcol = 2 * D_ATT // PAIR
    keep_blocks = ATT_REACH // ATT_QROWS
    blk = (None, ATT_QROWS, PAIR)

    att, kn = pl.pallas_call(
        _attn_prompt_kernel,
        grid=(nb, npairs, nq),
        in_specs=[
            pl.BlockSpec(blk, lambda b, p, q: (b, q, p)),
            pl.BlockSpec(blk, lambda b, p, q: (b, q, kcol + p)),
            pl.BlockSpec(blk, lambda b, p, q: (b, q, vcol + p)),
            pl.BlockSpec((1, PAIR), lambda b, p, q: (0, 0)),
            pl.BlockSpec((1, PAIR), lambda b, p, q: (0, 0)),
            pl.BlockSpec((None, 2, BIAS_LEN), lambda b, p, q: (p, 0, 0)),
        ],
        out_specs=[
            pl.BlockSpec(blk, lambda b, p, q: (b, q, p)),
            pl.BlockSpec(blk, lambda b, p, q: (b, jnp.maximum(q - (nq - keep_blocks), 0), p)),
        ],
        out_shape=[
            jax.ShapeDtypeStruct((nb, t, D_ATT), BF16),
            jax.ShapeDtypeStruct((nb, ATT_REACH, D_ATT), F32),
        ],
        scratch_shapes=[pltpu.VMEM((2, ATT_QROWS, ATT_WIN), F32),
                        pltpu.VMEM((ATT_WIN, PAIR), BF16), pltpu.VMEM((ATT_WIN, PAIR), BF16)],
        compiler_params=_cparams(("arbitrary", "arbitrary", "arbitrary")),
        name="attn_prompt",
    )(z3, z3, z3, jnp.tile(q_gain, 2).reshape(1, PAIR),
      jnp.tile(k_gain, 2).reshape(1, PAIR), u.reshape(npairs, 2, BIAS_LEN))
    return att, kn


def _attn_sample_kernel(q_ref, k_ref, v_ref, kp_ref, vp_ref, qg_ref, kg_ref, u_ref, o_ref, kn_ref):
    t = q_ref.shape[0]
    reach = kp_ref.shape[0]
    first = _iota((t, PAIR), 1) < HEAD_DIM
    pairs = range(N_ATT_HEADS // 2)
    cols = [slice(p * PAIR, (p + 1) * PAIR) for p in pairs]
    chains = [(p, h) for p in pairs for h in range(2)]
    qn = [_pair_rms(q_ref[:, c], qg_ref[...]) * ATT_SCALE for c in cols]
    kn = [_pair_rms(k_ref[:, c], kg_ref[...]) for c in cols]
    for p in pairs:
        kn_ref[:, cols[p]] = kn[p]
    kpast = [kp_ref[:, c].astype(BF16) for c in cols]
    vpast = [vp_ref[:, c].astype(BF16) for c in cols]
    vnew = [v_ref[:, c].astype(BF16) for c in cols]
    qh = [jnp.where(first, qn[p], 0.0) if h == 0 else jnp.where(first, 0.0, qn[p]) for p, h in chains]
    bias = [_toeplitz(u_ref[p, h:h + 1, :], t) for p, h in chains]
    s_past = [_dot(qh[i], kpast[p], NT) + bias[i][:, :reach] for i, (p, h) in enumerate(chains)]
    s_new = [_dot(qh[i], kn[p], NT) + bias[i][:, reach:reach + t] for i, (p, h) in enumerate(chains)]
    m = [jnp.maximum(jnp.max(a, axis=-1, keepdims=True), jnp.max(b, axis=-1, keepdims=True))
         for a, b in zip(s_past, s_new)]
    p_past = [jnp.exp(a - mm) for a, mm in zip(s_past, m)]
    p_new = [jnp.exp(b - mm) for b, mm in zip(s_new, m)]
    l = [jnp.sum(a, axis=-1, keepdims=True) + jnp.sum(b, axis=-1, keepdims=True)
         for a, b in zip(p_past, p_new)]
    o = [(_dot(p_past[i], vpast[p]) + _dot(p_new[i], vnew[p])) / l[i] for i, (p, h) in enumerate(chains)]
    for p in pairs:
        o_ref[:, cols[p]] = jnp.where(first, o[2 * p], o[2 * p + 1]).astype(BF16)


def _attn_sample(z3, k_past, v_past, q_gain, k_gain, u):
    nb, t, _ = z3.shape
    reach = k_past.shape[1]
    npairs = N_ATT_HEADS // 2
    att, kn = pl.pallas_call(
        _attn_sample_kernel,
        grid=(nb,),
        in_specs=[
            pl.BlockSpec((None, t, D_ATT), lambda b: (b, 0, 0)),
            pl.BlockSpec((None, t, D_ATT), lambda b: (b, 0, 1)),
            pl.BlockSpec((None, t, D_ATT), lambda b: (b, 0, 2)),
            pl.BlockSpec((None, reach, D_ATT), lambda b: (b, 0, 0)),
            pl.BlockSpec((None, reach, D_ATT), lambda b: (b, 0, 0)),
            pl.BlockSpec((1, PAIR), lambda b: (0, 0)),
            pl.BlockSpec((1, PAIR), lambda b: (0, 0)),
            pl.BlockSpec((npairs, 2, BIAS_LEN), lambda b: (0, 0, 0)),
        ],
        out_specs=[
            pl.BlockSpec((None, t, D_ATT), lambda b: (b, 0, 0)),
            pl.BlockSpec((None, t, D_ATT), lambda b: (b, 0, 0)),
        ],
        out_shape=[
            jax.ShapeDtypeStruct((nb, t, D_ATT), BF16),
            jax.ShapeDtypeStruct((nb, t, D_ATT), F32),
        ],
        compiler_params=_cparams(("arbitrary",)),
        name="attn_sample",
    )(z3, z3, z3, k_past.reshape(nb, reach, D_ATT), v_past.reshape(nb, reach, D_ATT),
      jnp.tile(q_gain, 2).reshape(1, PAIR), jnp.tile(k_gain, 2).reshape(1, PAIR),
      u.reshape(npairs, 2, BIAS_LEN))
    return att, kn


def _tri_inverse(l_mats, c):
    n = l_mats[0].shape[0]
    eye = jnp.where(_iota((n, n), 0) == _iota((n, n), 1), 1.0, 0.0).astype(F32)
    a_s = [eye + l for l in l_mats]
    t_s = [eye - l for l in l_mats]
    for _ in range(c.bit_length() - 3):
        r_s = [eye - _dot(a, t) for a, t in zip(a_s, t_s)]
        t_s = [t + _dot(t, r) for t, r in zip(t_s, r_s)]
    r_s = [eye - _dot3(a, t) for a, t in zip(a_s, t_s)]
    return [t + _dot(t, r) for t, r in zip(t_s, r_s)]


def _rwkv_kernel(c, r_ref, k_ref, v_ref, lo_ref, sr_ref, sk_ref, sv_ref, slo_ref, s0_ref,
                 mur_ref, muk_ref, muv_ref, mulo_ref, w0_ref, a0_ref, kkg_ref, ka_ref, rk_ref,
                 lnw_ref, lnb_ref, w2_ref, a2_ref, g2_ref,
                 o_ref, sT_ref, s_ref, cr_ref, ck_ref, cv_ref, clo_ref):
    tb = pl.program_id(2)
    rows, width = r_ref.shape
    npp = width // PAIR
    nchunks = rows // c
    h0 = _iota((rows, PAIR), 1) < HEAD_DIM
    bd = _blk(_iota((PAIR, PAIR), 0), HEAD_DIM) == _blk(_iota((PAIR, PAIR), 1), HEAD_DIM)

    @pl.when(tb == 0)
    def _():
        s_ref[...] = jnp.zeros(s_ref.shape, F32)
        for pp in range(npp):
            s_ref[pp, 0:HEAD_DIM, 0:HEAD_DIM] = s0_ref[2 * pp]
            s_ref[pp, HEAD_DIM:PAIR, HEAD_DIM:PAIR] = s0_ref[2 * pp + 1]
        cr_ref[...] = sr_ref[...]
        ck_ref[...] = sk_ref[...]
        cv_ref[...] = sv_ref[...]
        clo_ref[...] = slo_ref[...]

    def shifted(x_ref, carry_ref, mu_ref):
        x = x_ref[...]
        prev = jnp.where(_iota(x.shape, 0) == 0, carry_ref[...], pltpu.roll(x, 1, 0))
        carry_ref[...] = x[rows - 1:rows]
        return x + (prev - x) * mu_ref[...]

    r = shifted(r_ref, cr_ref, mur_ref)
    k = shifted(k_ref, ck_ref, muk_ref)
    v = shifted(v_ref, cv_ref, muv_ref)
    lo = shifted(lo_ref, clo_ref, mulo_ref)

    zeros_w = jnp.zeros((RANK_W, width), F32)
    w2p = jnp.concatenate([w2_ref[...], zeros_w], axis=0)
    a2p = jnp.concatenate([zeros_w, a2_ref[...]], axis=0)
    lo_wa = lo[:, 0:RANK_W + RANK_A]
    u = w0_ref[...] + _dot3(jnp.tanh(lo_wa), w2p)
    lw = -jnp.exp(-_softplus(-u) - 0.5)
    a = _sigmoid(a0_ref[...] + _dot(lo_wa, a2p))
    g = _dot(_sigmoid(lo[:, RANK_W + RANK_A:]), g2_ref[...])

    kk = k * kkg_ref[...]
    kk = kk / jnp.maximum(jnp.sqrt(_head_sums(kk * kk)), 1e-12)
    k = k * (1.0 + (a - 1.0) * ka_ref[...])
    b = kk * a
    bonus = _head_sums(r * k * rk_ref[...]) * v

    tr = _iota((rows, rows), 0)
    tc = _iota((rows, rows), 1)
    same_chunk = _blk(tr, c) == _blk(tc, c)
    strict = same_chunk & (tr > tc)
    incl = same_chunk & (tr >= tc)
    lw_hi, lw_lo = _split2(lw)
    tril_ones = jnp.where(incl, 1.0, 0.0).astype(BF16)
    lp = jnp.dot(tril_ones, lw_hi, preferred_element_type=F32) + \
        jnp.dot(tril_ones, lw_lo, preferred_element_type=F32)
    lp_end = jnp.concatenate(
        [jnp.broadcast_to(lp[(ci + 1) * c - 1:(ci + 1) * c], (c, width)) for ci in range(nchunks)], axis=0)

    alpha_w = kk * jnp.exp(lp - lw)
    inv_p = jnp.exp(-lp)
    beta_w = b * inv_p
    kappa_w = k * inv_p
    rho_w = r * jnp.exp(lp)
    to_end = jnp.exp(lp_end - lp)
    beta_ew = b * to_end
    kappa_ew = k * to_end
    decay_end_w = jnp.exp(lp_end)

    wide = (rows, nchunks * PAIR)
    col_chunk = _blk(_iota(wide, 1), PAIR) == _blk(_iota(wide, 0), c)
    spread = lambda m: jnp.where(col_chunk, jnp.tile(m, (1, nchunks)), 0.0)
    eye_p = _iota((PAIR, PAIR), 0) == _iota((PAIR, PAIR), 1)

    pairs = range(npp)
    lanes = [slice(pp * PAIR, (pp + 1) * PAIR) for pp in pairs]
    alpha = [alpha_w[:, l] for l in lanes]
    rho = [rho_w[:, l] for l in lanes]
    vv = [v[:, l] for l in lanes]
    head_mask = [h0, jnp.logical_not(h0)]
    bk = [jnp.concatenate([beta_w[:, l], kappa_w[:, l]], axis=0).astype(BF16) for l in lanes]
    prod = [[_dot(jnp.concatenate([jnp.where(hm, alpha[pp], 0.0), jnp.where(hm, rho[pp], 0.0)], axis=0),
                  bk[pp], NT) for hm in head_mask] for pp in pairs]
    t_inv = _tri_inverse([jnp.where(strict, prod[pp][h][:rows, :rows], 0.0) for pp in pairs for h in range(2)], c)
    x = [[_dot(jnp.where(strict, prod[pp][h][:rows, rows:], 0.0), vv[pp]) for h in range(2)] for pp in pairs]
    ws = [[_dot(t_inv[2 * pp + h], jnp.concatenate([alpha[pp], x[pp][h]], axis=1)) for h in range(2)]
          for pp in pairs]
    w12 = [jnp.concatenate([jnp.where(h0, ws[pp][0][:, :PAIR], ws[pp][1][:, :PAIR]),
                            jnp.where(h0, ws[pp][0][:, PAIR:], ws[pp][1][:, PAIR:])], axis=1) for pp in pairs]
    q = [[_dot(jnp.where(incl, prod[pp][h][rows:, :rows], 0.0), w12[pp]) for h in range(2)] for pp in pairs]
    qk = [[_dot(jnp.where(incl, prod[pp][h][rows:, rows:], 0.0), vv[pp]) for h in range(2)] for pp in pairs]
    rp = [rho[pp] - jnp.where(h0, q[pp][0][:, :PAIR], q[pp][1][:, :PAIR]) for pp in pairs]
    y0 = [jnp.where(h0, qk[pp][0] - q[pp][0][:, PAIR:], qk[pp][1] - q[pp][1][:, PAIR:]) for pp in pairs]
    wtb = [_dot(w12[pp], spread(beta_ew[:, lanes[pp]]), TN) for pp in pairs]
    vtk = [_dot(vv[pp], spread(kappa_ew[:, lanes[pp]]), TN) for pp in pairs]

    s_cur = [s_ref[pp] for pp in pairs]
    ys = [[] for _ in pairs]
    for ci in range(nchunks):
        sl = slice(ci * c, (ci + 1) * c)
        cols = slice(ci * PAIR, (ci + 1) * PAIR)
        for pp in pairs:
            decay_end = decay_end_w[ci * c:ci * c + 1, lanes[pp]]
            gmat = jnp.where(eye_p, jnp.broadcast_to(decay_end, (PAIR, PAIR)), 0.0) \
                - jnp.where(bd, wtb[pp][:PAIR, cols], 0.0)
            hmat = jnp.where(bd, vtk[pp][:, cols] - wtb[pp][PAIR:, cols], 0.0)
            ys[pp].append(_dot(rp[pp][sl], s_cur[pp], NT) + y0[pp][sl])
            s_cur[pp] = _dot(s_cur[pp], gmat) + hmat
    for pp in pairs:
        s_ref[pp] = s_cur[pp]
    y_pairs = [ys[pp][0] if nchunks == 1 else jnp.concatenate(ys[pp], axis=0) for pp in pairs]

    @pl.when(tb == pl.num_programs(2) - 1)
    def _():
        for pp in range(npp):
            sT_ref[2 * pp] = s_ref[pp, 0:HEAD_DIM, 0:HEAD_DIM]
            sT_ref[2 * pp + 1] = s_ref[pp, HEAD_DIM:PAIR, HEAD_DIM:PAIR]

    y = y_pairs[0] if npp == 1 else jnp.concatenate(y_pairs, axis=1)
    mu = _head_sums(y) * (1.0 / HEAD_DIM)
    d = y - mu
    var = _head_sums(d * d) * (1.0 / HEAD_DIM)
    yn = d * lax.rsqrt(var + GN_EPS) * lnw_ref[...] + lnb_ref[...]
    o_ref[...] = ((yn + bonus) * g).astype(BF16)


def _rwkv(z3, shift_prev, s0, p, rows, c, npp):
    nb, t, _ = z3.shape
    width = npp * PAIR
    ngroups = D_RWKV // width
    col0 = 3 * D_ATT // width
    lo_blk = (3 * D_ATT + 3 * D_RWKV) // D_LORA
    sp = shift_prev.reshape(nb, 1, D_SHIFT)

    def zspec(off):
        return pl.BlockSpec((None, rows, width), lambda b, q, s: (b, s, col0 + off * ngroups + q))

    def sspec(off):
        return pl.BlockSpec((None, 1, width), lambda b, q, s: (b, 0, off * ngroups + q))

    def vec(off=0):
        return pl.BlockSpec((1, width), lambda b, q, s: (0, off * ngroups + q))

    def row2(x):
        return x.reshape(1, -1)

    out, s_fin = pl.pallas_call(
        functools.partial(_rwkv_kernel, c),
        grid=(nb, ngroups, t // rows),
        in_specs=[
            zspec(0), zspec(1), zspec(2),
            pl.BlockSpec((None, rows, D_LORA), lambda b, q, s: (b, s, lo_blk)),
            sspec(0), sspec(1), sspec(2),
            pl.BlockSpec((None, 1, D_LORA), lambda b, q, s: (b, 0, 3 * D_RWKV // D_LORA)),
            pl.BlockSpec((None, 2 * npp, HEAD_DIM, HEAD_DIM), lambda b, q, s: (b, q, 0, 0)),
            vec(0), vec(1), vec(2),
            pl.BlockSpec((1, D_LORA), lambda b, q, s: (0, 3 * D_RWKV // D_LORA)),
            vec(), vec(), vec(), vec(), vec(), vec(), vec(),
            pl.BlockSpec((RANK_W, width), lambda b, q, s: (0, q)),
            pl.BlockSpec((RANK_A, width), lambda b, q, s: (0, q)),
            pl.BlockSpec((RANK_G, width), lambda b, q, s: (0, q)),
        ],
        out_specs=[
            pl.BlockSpec((None, rows, width), lambda b, q, s: (b, s, q)),
            pl.BlockSpec((None, 2 * npp, HEAD_DIM, HEAD_DIM), lambda b, q, s: (b, q, 0, 0)),
        ],
        out_shape=[
            jax.ShapeDtypeStruct((nb, t, D_RWKV), BF16),
            jax.ShapeDtypeStruct((nb, N_RWKV_HEADS, HEAD_DIM, HEAD_DIM), F32),
        ],
        scratch_shapes=[
            pltpu.VMEM((npp, PAIR, PAIR), F32),
            pltpu.VMEM((1, width), F32), pltpu.VMEM((1, width), F32), pltpu.VMEM((1, width), F32),
            pltpu.VMEM((1, D_LORA), F32),
        ],
        compiler_params=_cparams(("arbitrary", "arbitrary", "arbitrary")),
        name="rwkv7_mix",
    )(z3, z3, z3, z3, sp, sp, sp, sp, s0,
      row2(p['mu_shift']), row2(p['mu_shift']), row2(p['mu_shift']), row2(p['mu_shift']),
      row2(p['w0']), row2(p['a0']), row2(p['k_k']), row2(p['k_a']), row2(p['r_k']),
      row2(p['ln_x_w']), row2(p['ln_x_b']), p['w2'], p['a2'], p['g2'])
    return out, s_fin


def _layer(x3, mod, p, u, k_past, v_past, s0, shift_prev, conv_prev):
    nb, t, d = x3.shape
    m = nb * t
    x = x3.reshape(m, d)
    z = _norm_proj(x, p['norm_att_g'], mod, 1, 0, p['w_in'], IN_COLS, "in_proj")
    z3 = z.reshape(nb, t, D_IN)
    if k_past is None:
        att, k_keep = _attn_prompt(z3, p['q_norm_g'], p['k_norm_g'], u)
        keep = min(ATT_REACH, t)
        v_keep = z3[:, t - keep:, 2 * D_ATT:3 * D_ATT]
        rw, s_fin = _rwkv(z3, shift_prev, s0, p, RWKV_ROWS, CHUNK, RWKV_PAIRS_PROMPT)
    else:
        att, k_keep = _attn_sample(z3, k_past, v_past, p['q_norm_g'], p['k_norm_g'], u)
        v_keep = z3[:, :, 2 * D_ATT:3 * D_ATT]
        rw, s_fin = _rwkv(z3, shift_prev, s0, p, t, t, RWKV_PAIRS_SAMPLE)
    shift_last = z3[:, t - 1, 3 * D_ATT:]
    x1 = _proj_resid([att.reshape(m, D_ATT), rw.reshape(m, D_RWKV)], p['w_out'], x, mod, 2,
                     OUT_COLS, "out_proj")
    if k_past is None:
        act, conv_last = _ffn_up_fused(x1, p['norm_ffn_g'], mod, p['w_up'], conv_prev,
                                       p['dw_conv'], p['dw_bias'], t)
    else:
        hu = _norm_proj(x1, p['norm_ffn_g'], mod, 4, 3, p['w_up'], UP_COLS, "ffn_up_sample")
        f = hu.shape[1] // 2
        act = _act_sample(hu, conv_prev, p['dw_conv'], p['dw_bias'], nb, t)
        conv_last = hu.reshape(nb, t, 2 * f)[:, t - (CONV_W - 1):, :f]
    x2 = _proj_resid([act], p['w_down'], x1, mod, 5, DOWN_COLS, "ffn_down")
    heads = lambda a: a.reshape(nb, a.shape[1], N_ATT_HEADS, HEAD_DIM)
    return x2.reshape(nb, t, d), heads(k_keep), heads(v_keep), s_fin, shift_last, conv_last


def kernel(x_prompt, x_sample, c_prompt, c_sample, cache_att_k, cache_att_v, state_rwkv, state_shift, state_ffn_conv, norm_att_g, norm_ffn_g, w_ada, b_ada, w_in, q_norm_g, k_norm_g, rel_bias, mu_shift, w0, w2, a0, a2, g2, k_k, k_a, r_k, ln_x_w, ln_x_b, w_out, w_up, dw_conv, dw_bias, w_down):
    depth = w_in.shape[0]
    bp, tp, d = x_prompt.shape
    bs, ts, _ = x_sample.shape
    d_ff = w_down.shape[1]
    hp, hs = x_prompt, x_sample
    outs_p = [[] for _ in range(5)]
    outs_s = [[] for _ in range(5)]
    for l in range(depth):
        p = dict(norm_att_g=norm_att_g[l], norm_ffn_g=norm_ffn_g[l], w_in=w_in[l], q_norm_g=q_norm_g[l],
                 k_norm_g=k_norm_g[l], mu_shift=mu_shift[l], w0=w0[l], w2=w2[l], a0=a0[l], a2=a2[l],
                 g2=g2[l], k_k=k_k[l], k_a=k_a[l], r_k=r_k[l], ln_x_w=ln_x_w[l], ln_x_b=ln_x_b[l],
                 w_out=w_out[l], w_up=w_up[l], dw_conv=dw_conv[l], dw_bias=dw_bias[l], w_down=w_down[l])
        n_c = bp + bs
        pad = (-n_c) % 8
        c_all = jnp.concatenate([c_prompt, c_sample, jnp.zeros((pad, d), F32)], axis=0)
        mod = _ada(c_all, w_ada[l], b_ada[l])
        mod_p = _Mod(mod.reshape(n_c + pad, 6, 1, d), False, rows_per_batch=tp,
                     row_tile=min(ROW_TILE, tp))
        mod_s = _Mod(jnp.repeat(mod[bp:bp + bs], ts, axis=0), True)
        u = _bias_rows(rel_bias[l])

        res = _layer(hp, mod_p, p, u, None, None,
                     jnp.zeros((bp, N_RWKV_HEADS, HEAD_DIM, HEAD_DIM), F32),
                     jnp.zeros((bp, D_SHIFT), F32),
                     jnp.zeros((bp, CONV_W - 1, d_ff), F32))
        hp = res[0]
        for lst, val in zip(outs_p, res[1:]):
            lst.append(val)
        res = _layer(hs, mod_s, p, u, cache_att_k[l], cache_att_v[l], state_rwkv[l],
                     state_shift[l], state_ffn_conv[l])
        hs = res[0]
        for lst, val in zip(outs_s, res[1:]):
            lst.append(val)
    st = lambda lst: jnp.stack(lst)
    return (hp, hs, *[st(x) for x in outs_p], *[st(x) for x in outs_s])
```

```python
import functools

import jax
import jax.numpy as jnp
from jax import lax
from jax.experimental import pallas as pl
from jax.experimental.pallas import tpu as pltpu

F32 = jnp.float32
BF16 = jnp.bfloat16

CHUNK = 64
N_PREV_CHUNKS = 8
ATT_REACH = N_PREV_CHUNKS * CHUNK
HEAD_DIM = 64
N_ATT_HEADS = 16
N_RWKV_HEADS = 16
D_ATT = N_ATT_HEADS * HEAD_DIM
D_RWKV = N_RWKV_HEADS * HEAD_DIM
REL_CLIP = 128
RANK_W = 64
RANK_A = 64
RANK_G = 128
D_LORA = RANK_W + RANK_A + RANK_G
D_SHIFT = 3 * D_RWKV + D_LORA
D_IN = 3 * D_ATT + D_SHIFT
CONV_W = 3
RMS_EPS = 1e-6
GN_EPS = 64e-5
ATT_SCALE = HEAD_DIM ** -0.5
NEG_INF = -1e30

LANES = 128
PAIR = 2 * HEAD_DIM
MXU_DIM = 256
VMEM_LIMIT = 60 * 1024 * 1024

ROW_TILE = 1024
IN_ROW_TILE = 2048
ADA_COLS = 512
IN_COLS = 640
OUT_COLS = 512
UP_COLS = 512
DOWN_COLS = 256
ATT_QROWS = 256
ATT_WIN = ATT_QROWS + ATT_REACH
ATT_PAIRS = 2
BIAS_LEN = 1024
RWKV_ROWS = 256
RWKV_PAIRS_PROMPT = 4
RWKV_PAIRS_SAMPLE = 8


def _cparams(sem):
    return pltpu.CompilerParams(dimension_semantics=sem, vmem_limit_bytes=VMEM_LIMIT)


def _dot(a, b, dims=(((1,), (0,)), ((), ()))):
    return lax.dot_general(a.astype(BF16), b.astype(BF16), dims, preferred_element_type=F32)


def _split2(x):
    hi = x.astype(BF16)
    lo = (x - hi.astype(F32)).astype(BF16)
    return hi, lo


def _dot3(a, b, dims=(((1,), (0,)), ((), ()))):
    a1, a2 = _split2(a)
    b1, b2 = _split2(b)
    d = functools.partial(lax.dot_general, dimension_numbers=dims, preferred_element_type=F32)
    return d(a1, b1) + (d(a1, b2) + d(a2, b1))


def _dot_exact_rhs(a, b_bf16):
    a1 = a.astype(BF16)
    r1 = a - a1.astype(F32)
    a2 = r1.astype(BF16)
    a3 = (r1 - a2.astype(F32)).astype(BF16)
    d = functools.partial(jnp.dot, preferred_element_type=F32)
    return d(a1, b_bf16) + (d(a2, b_bf16) + d(a3, b_bf16))


def _dot_exact_lhs(a_bf16, b):
    b1 = b.astype(BF16)
    r1 = b - b1.astype(F32)
    b2 = r1.astype(BF16)
    b3 = (r1 - b2.astype(F32)).astype(BF16)
    d = functools.partial(jnp.dot, preferred_element_type=F32)
    return d(a_bf16, b1) + (d(a_bf16, b2) + d(a_bf16, b3))


NT = (((1,), (1,)), ((), ()))
TN = (((0,), (0,)), ((), ()))


def _iota(shape, dim):
    return lax.broadcasted_iota(jnp.int32, shape, dim)


def _blk(x, size):
    return jnp.right_shift(x, size.bit_length() - 1)


def _head_ones(n):
    r = _blk(_iota((n, n), 0), HEAD_DIM)
    c = _blk(_iota((n, n), 1), HEAD_DIM)
    return jnp.where(r == c, 1.0, 0.0).astype(BF16)


def _head_sums(x):
    lanes = x.shape[1]
    group = min(lanes, MXU_DIM)
    ones = _head_ones(group)
    parts = [_dot(x[:, i:i + group], ones) for i in range(0, lanes, group)]
    return parts[0] if len(parts) == 1 else jnp.concatenate(parts, axis=1)


def _sigmoid(x):
    return 1.0 / (1.0 + jnp.exp(-x))


def _softplus(x):
    return jnp.maximum(x, 0.0) + jnp.log(1.0 + jnp.exp(-jnp.abs(x)))


def _ada_kernel(c_ref, w_ref, b_ref, o_ref):
    c = c_ref[...]
    s = c * _sigmoid(c)
    o_ref[...] = _dot3(s, w_ref[...]) + b_ref[...]


def _ada(c_all, w_ada, b_ada):
    rows, d = c_all.shape
    n = w_ada.shape[1]
    return pl.pallas_call(
        _ada_kernel,
        grid=(n // ADA_COLS,),
        in_specs=[
            pl.BlockSpec((rows, d), lambda j: (0, 0)),
            pl.BlockSpec((d, ADA_COLS), lambda j: (0, j)),
            pl.BlockSpec((1, ADA_COLS), lambda j: (0, j)),
        ],
        out_specs=pl.BlockSpec((rows, ADA_COLS), lambda j: (0, j)),
        out_shape=jax.ShapeDtypeStruct((rows, n), F32),
        compiler_params=_cparams(("arbitrary",)),
        name="ada_mod",
    )(c_all, w_ada, b_ada.reshape(1, n))


class _Mod:
    def __init__(self, arr, per_row, rows_per_batch=None):
        self.arr = arr
        self.per_row = per_row
        self.rows_per_batch = rows_per_batch

    def spec(self, idx, cols, col_of, row_tile):
        if self.per_row:
            m = self.arr.shape[0]
            d = self.arr.shape[1] // 6
            nblk = d // cols
            return pl.BlockSpec((m, cols), lambda i, j: (0, idx * nblk + col_of(j)))
        tiles_per_batch = self.rows_per_batch // row_tile
        return pl.BlockSpec((None, None, 1, cols),
                            lambda i, j: (i // tiles_per_batch, idx, 0, col_of(j)))


NORM_ROWS = 128


def _store_normed(h_ref, x_ref, g_ref, sc_ref, sh_ref):
    rows = x_ref.shape[0]
    step = min(NORM_ROWS, rows)
    per_row = sc_ref.shape[0] == rows

    def body(r, carry):
        sl = pl.ds(pl.multiple_of(r * step, step), step)
        x = x_ref[sl, :]
        ms = jnp.mean(x * x, axis=-1, keepdims=True)
        xn = x * lax.rsqrt(ms + RMS_EPS) * g_ref[...]
        sc = sc_ref[sl, :] if per_row else sc_ref[...]
        sh = sh_ref[sl, :] if per_row else sh_ref[...]
        h_ref[sl, :] = (xn * (1.0 + sc) + sh).astype(BF16)
        return carry

    lax.fori_loop(0, rows // step, body, 0)


def _norm_proj_kernel(x_ref, g_ref, sc_ref, sh_ref, w_ref, o_ref, h_ref):
    @pl.when(pl.program_id(1) == 0)
    def _():
        _store_normed(h_ref, x_ref, g_ref, sc_ref, sh_ref)

    o_ref[...] = jnp.dot(h_ref[...], w_ref[...].astype(BF16), preferred_element_type=F32)


def _norm_proj(x, gain, mod, sc_idx, sh_idx, w, cols, name, row_tile=ROW_TILE):
    m, d = x.shape
    n = w.shape[1]
    tm = min(row_tile, m)
    whole = lambda j: 0
    x_mode = dict(pipeline_mode=pl.Buffered(1)) if tm > ROW_TILE else {}
    return pl.pallas_call(
        _norm_proj_kernel,
        grid=(m // tm, n // cols),
        in_specs=[
            pl.BlockSpec((tm, d), lambda i, j: (i, 0), **x_mode),
            pl.BlockSpec((1, d), lambda i, j: (0, 0)),
            mod.spec(sc_idx, d, whole, tm),
            mod.spec(sh_idx, d, whole, tm),
            pl.BlockSpec((d, cols), lambda i, j: (0, j)),
        ],
        out_specs=pl.BlockSpec((tm, cols), lambda i, j: (i, j)),
        out_shape=jax.ShapeDtypeStruct((m, n), F32),
        scratch_shapes=[pltpu.VMEM((tm, d), BF16)],
        compiler_params=_cparams(("arbitrary", "arbitrary")),
        name=name,
    )(x, gain.reshape(1, d), mod.arr, mod.arr, w)


def _proj_resid_kernel(n_pairs, *refs):
    a_refs = refs[:n_pairs]
    w_refs = refs[n_pairs:2 * n_pairs]
    x_ref, g_ref, o_ref = refs[2 * n_pairs:]
    acc = jnp.dot(a_refs[0][...], w_refs[0][...].astype(BF16), preferred_element_type=F32)
    for a_ref, w_ref in zip(a_refs[1:], w_refs[1:]):
        acc = acc + jnp.dot(a_ref[...], w_ref[...].astype(BF16), preferred_element_type=F32)
    o_ref[...] = x_ref[...] + g_ref[...] * acc


def _proj_resid(a_list, w, x, mod, g_idx, cols, name):
    m, n = x.shape
    tm = min(ROW_TILE, m)
    in_specs, w_args = [], []
    row = 0
    for a in a_list:
        kdim = a.shape[1]
        in_specs.append(pl.BlockSpec((tm, kdim), lambda i, j: (i, 0)))
    for a in a_list:
        kdim = a.shape[1]
        in_specs.append(pl.BlockSpec((kdim, cols), lambda i, j, r=row // kdim: (r, j)))
        w_args.append(w)
        row += kdim
    in_specs.append(pl.BlockSpec((tm, cols), lambda i, j: (i, j)))
    in_specs.append(mod.spec(g_idx, cols, lambda j: j, tm))
    return pl.pallas_call(
        functools.partial(_proj_resid_kernel, len(a_list)),
        grid=(m // tm, n // cols),
        in_specs=in_specs,
        out_specs=pl.BlockSpec((tm, cols), lambda i, j: (i, j)),
        out_shape=jax.ShapeDtypeStruct((m, n), F32),
        compiler_params=_cparams(("arbitrary", "arbitrary")),
        name=name,
    )(*a_list, *w_args, x, mod.arr)


def _gelu(x):
    return 0.5 * x * (1.0 + lax.erf(x * (2.0 ** -0.5)))


def _ffn_up_kernel(tiles_per_batch, x_ref, g_ref, sc_ref, sh_ref, wg_ref, wv_ref, hist_ref,
                   cw_ref, cb_ref, act_ref, last_ref, h_ref, carry_ref):
    i = pl.program_id(0)
    j = pl.program_id(1)

    @pl.when(j == 0)
    def _():
        _store_normed(h_ref, x_ref, g_ref, sc_ref, sh_ref)

    @pl.when((i % tiles_per_batch) == 0)
    def _():
        carry_ref[j] = hist_ref[...]

    h = h_ref[...]
    gate = jnp.dot(h, wg_ref[...].astype(BF16), preferred_element_type=F32)
    val = jnp.dot(h, wv_ref[...].astype(BF16), preferred_element_type=F32)
    tm = gate.shape[0]
    prev = carry_ref[j]
    row = _iota(gate.shape, 0)
    g1 = pltpu.roll(gate, 1, 0)
    g2 = pltpu.roll(gate, 2, 0)
    g1 = jnp.where(row == 0, prev[1:2], g1)
    g2 = jnp.where(row == 0, prev[0:1], jnp.where(row == 1, prev[1:2], g2))
    cw = cw_ref[...]
    conv = cb_ref[...] + g2 * cw[0:1] + g1 * cw[1:2] + gate * cw[2:3]
    act_ref[...] = (_gelu(conv) * val).astype(BF16)
    tail = gate[tm - 2:tm]
    carry_ref[j] = tail
    last_ref[...] = tail


def _ffn_up_fused(x, gain, mod, w_up, hist, conv_w, conv_b, rows_per_batch):
    m, d = x.shape
    f = w_up.shape[1] // 2
    tm = min(ROW_TILE, rows_per_batch)
    cols = UP_COLS
    nj = f // cols
    tiles_per_batch = rows_per_batch // tm
    whole = lambda j: 0
    act, tile_tails = pl.pallas_call(
        functools.partial(_ffn_up_kernel, tiles_per_batch),
        grid=(m // tm, nj),
        in_specs=[
            pl.BlockSpec((tm, d), lambda i, j: (i, 0)),
            pl.BlockSpec((1, d), lambda i, j: (0, 0)),
            mod.spec(4, d, whole, tm),
            mod.spec(3, d, whole, tm),
            pl.BlockSpec((d, cols), lambda i, j: (0, j)),
            pl.BlockSpec((d, cols), lambda i, j: (0, nj + j)),
            pl.BlockSpec((None, CONV_W - 1, cols), lambda i, j: (i // tiles_per_batch, 0, j)),
            pl.BlockSpec((CONV_W, cols), lambda i, j: (0, j)),
            pl.BlockSpec((1, cols), lambda i, j: (0, j)),
        ],
        out_specs=[
            pl.BlockSpec((tm, cols), lambda i, j: (i, j)),
            pl.BlockSpec((None, CONV_W - 1, cols), lambda i, j: (i, 0, j)),
        ],
        out_shape=[
            jax.ShapeDtypeStruct((m, f), BF16),
            jax.ShapeDtypeStruct((m // tm, CONV_W - 1, f), F32),
        ],
        scratch_shapes=[pltpu.VMEM((tm, d), BF16), pltpu.VMEM((nj, CONV_W - 1, cols), F32)],
        compiler_params=_cparams(("arbitrary", "arbitrary")),
        name="ffn_up_prompt",
    )(x, gain.reshape(1, d), mod.arr, mod.arr, w_up, w_up, hist, conv_w, conv_b.reshape(1, f))
    return act, tile_tails[tiles_per_batch - 1::tiles_per_batch]


def _act_sample_kernel(gate_ref, val_ref, hist_ref, cw_ref, cb_ref, act_ref):
    gate = gate_ref[...]
    hist = hist_ref[...]
    t = _iota(gate.shape, 1)
    g1 = jnp.where(t == 0, hist[:, 1:2], pltpu.roll(gate, 1, 1))
    g2 = jnp.where(t == 0, hist[:, 0:1], jnp.where(t == 1, hist[:, 1:2], pltpu.roll(gate, 2, 1)))
    cw = cw_ref[...]
    conv = cb_ref[...] + g2 * cw[0:1] + g1 * cw[1:2] + gate * cw[2:3]
    act_ref[...] = (_gelu(conv) * val_ref[...]).astype(BF16)


def _act_sample(hu, hist, conv_w, conv_b, nb, t):
    f = hu.shape[1] // 2
    cols = UP_COLS
    nj = f // cols
    hu3 = hu.reshape(nb, t, 2 * f)
    act = pl.pallas_call(
        _act_sample_kernel,
        grid=(nj,),
        in_specs=[
            pl.BlockSpec((nb, t, cols), lambda j: (0, 0, j)),
            pl.BlockSpec((nb, t, cols), lambda j: (0, 0, nj + j)),
            pl.BlockSpec((nb, CONV_W - 1, cols), lambda j: (0, 0, j)),
            pl.BlockSpec((CONV_W, cols), lambda j: (0, j)),
            pl.BlockSpec((1, cols), lambda j: (0, j)),
        ],
        out_specs=pl.BlockSpec((nb, t, cols), lambda j: (0, 0, j)),
        out_shape=jax.ShapeDtypeStruct((nb, t, f), BF16),
        compiler_params=_cparams(("arbitrary",)),
        name="ffn_act_sample",
    )(hu3, hu3, hist, conv_w, conv_b.reshape(1, f))
    return act.reshape(nb * t, f)


def _pair_rms(x, gain):
    x2 = x * x
    first = _iota(x.shape, 1) < HEAD_DIM
    s0 = jnp.sum(jnp.where(first, x2, 0.0), axis=-1, keepdims=True)
    s1 = jnp.sum(jnp.where(first, 0.0, x2), axis=-1, keepdims=True)
    ms = jnp.where(first, s0, s1) * (1.0 / HEAD_DIM)
    return x * lax.rsqrt(ms + RMS_EPS) * gain


def _bias_rows(table):
    h = table.shape[0]
    far = jnp.broadcast_to(table[:, 2 * REL_CLIP:], (h, ATT_REACH - REL_CLIP))
    mid = table[:, ::-1]
    near_len = BIAS_LEN - ATT_QROWS - (ATT_REACH - REL_CLIP) - (2 * REL_CLIP + 1)
    near = jnp.broadcast_to(table[:, 0:1], (h, near_len))
    wrap = jnp.broadcast_to(table[:, 2 * REL_CLIP:], (h, ATT_QROWS))
    return jnp.concatenate([far, mid, near, wrap], axis=1)


def _toeplitz(u_row, rows):
    return pltpu.roll(jnp.broadcast_to(u_row, (rows, BIAS_LEN)), 0, 1, stride=1, stride_axis=0)


def _attn_prompt_kernel(q_ref, k_ref, v_ref, qg_ref, kg_ref, u_ref, o_ref, kn_ref,
                        bias_ref, kwin_ref, vwin_ref):
    b = pl.program_id(1)
    qb = pl.program_id(2)
    shape = (ATT_QROWS, ATT_WIN)
    pairs = range(q_ref.shape[1] // PAIR)
    cols = [slice(p * PAIR, (p + 1) * PAIR) for p in pairs]
    chains = [(p, h) for p in pairs for h in range(2)]

    @pl.when((b == 0) & (qb == 0))
    def _():
        r = _iota(shape, 0)
        w = _iota(shape, 1)
        chunk_lo = _blk(r, CHUNK) * CHUNK
        in_band = (w >= chunk_lo) & (w < chunk_lo + (ATT_REACH + CHUNK))
        for i, (p, h) in enumerate(chains):
            bias = _toeplitz(u_ref[p, h:h + 1, :], ATT_QROWS)[:, :ATT_WIN]
            bias_ref[i] = jnp.where(in_band, bias, -jnp.inf)

    @pl.when(qb == 0)
    def _():
        kwin_ref[0:ATT_REACH] = jnp.zeros((ATT_REACH, kwin_ref.shape[1]), BF16)
        vwin_ref[0:ATT_REACH] = jnp.zeros((ATT_REACH, vwin_ref.shape[1]), BF16)

    @pl.when(qb > 0)
    def _():
        kwin_ref[0:ATT_REACH] = kwin_ref[ATT_QROWS:ATT_WIN]
        vwin_ref[0:ATT_REACH] = vwin_ref[ATT_QROWS:ATT_WIN]

    kn = [_pair_rms(k_ref[:, c], kg_ref[...]) for c in cols]
    for p in pairs:
        kn_ref[:, cols[p]] = kn[p]
        kwin_ref[ATT_REACH:ATT_WIN, cols[p]] = kn[p].astype(BF16)
    vwin_ref[ATT_REACH:ATT_WIN] = v_ref[...].astype(BF16)
    qn = [_pair_rms(q_ref[:, c], qg_ref[...]) * ATT_SCALE for c in cols]
    kb = [kwin_ref[:, c] for c in cols]
    vb = [vwin_ref[:, c] for c in cols]

    first = _iota((ATT_QROWS, PAIR), 1) < HEAD_DIM
    started = _iota(shape, 1) >= ATT_REACH - qb * ATT_QROWS
    qh = [jnp.where(first, qn[p], 0.0) if h == 0 else jnp.where(first, 0.0, qn[p]) for p, h in chains]
    s = [_dot(qh[i], kb[p], NT) + bias_ref[i] for i, (p, h) in enumerate(chains)]
    s = [jnp.maximum(jnp.where(started, x, -jnp.inf), NEG_INF) for x in s]
    m = [jnp.max(x, axis=-1, keepdims=True) for x in s]
    pr = [jnp.exp(x - mm) for x, mm in zip(s, m)]
    l = [jnp.sum(x, axis=-1, keepdims=True) for x in pr]
    o = [_dot(pr[i], vb[p]) / l[i] for i, (p, h) in enumerate(chains)]
    for p in pairs:
        o_ref[:, cols[p]] = jnp.where(first, o[2 * p], o[2 * p + 1]).astype(BF16)


def _attn_prompt(z3, q_gain, k_gain, u):
    nb, t, _ = z3.shape
    npairs = N_ATT_HEADS // 2
    npp = ATT_PAIRS
    width = npp * PAIR
    ngroups = npairs // npp
    nq = t // ATT_QROWS
    kcol = D_ATT // width
    vcol = 2 * D_ATT // width
    keep_blocks = ATT_REACH // ATT_QROWS
    blk = (None, ATT_QROWS, width)

    att, kn = pl.pallas_call(
        _attn_prompt_kernel,
        grid=(ngroups, nb, nq),
        in_specs=[
            pl.BlockSpec(blk, lambda g, b, q: (b, q, g)),
            pl.BlockSpec(blk, lambda g, b, q: (b, q, kcol + g)),
            pl.BlockSpec(blk, lambda g, b, q: (b, q, vcol + g)),
            pl.BlockSpec((1, PAIR), lambda g, b, q: (0, 0)),
            pl.BlockSpec((1, PAIR), lambda g, b, q: (0, 0)),
            pl.BlockSpec((npp, 2, BIAS_LEN), lambda g, b, q: (g, 0, 0)),
        ],
        out_specs=[
            pl.BlockSpec(blk, lambda g, b, q: (b, q, g)),
            pl.BlockSpec(blk, lambda g, b, q: (b, jnp.maximum(q - (nq - keep_blocks), 0), g)),
        ],
        out_shape=[
            jax.ShapeDtypeStruct((nb, t, D_ATT), BF16),
            jax.ShapeDtypeStruct((nb, ATT_REACH, D_ATT), F32),
        ],
        scratch_shapes=[pltpu.VMEM((2 * npp, ATT_QROWS, ATT_WIN), F32),
                        pltpu.VMEM((ATT_WIN, width), BF16), pltpu.VMEM((ATT_WIN, width), BF16)],
        compiler_params=_cparams(("arbitrary", "arbitrary", "arbitrary")),
        name="attn_prompt",
    )(z3, z3, z3, jnp.tile(q_gain, 2).reshape(1, PAIR),
      jnp.tile(k_gain, 2).reshape(1, PAIR), u.reshape(npairs, 2, BIAS_LEN))
    return att, kn


def _attn_sample_kernel(q_ref, k_ref, v_ref, kp_ref, vp_ref, qg_ref, kg_ref, u_ref, o_ref, kn_ref):
    t = q_ref.shape[0]
    reach = kp_ref.shape[0]
    first = _iota((t, PAIR), 1) < HEAD_DIM
    pairs = range(N_ATT_HEADS // 2)
    cols = [slice(p * PAIR, (p + 1) * PAIR) for p in pairs]
    chains = [(p, h) for p in pairs for h in range(2)]
    qn = [_pair_rms(q_ref[:, c], qg_ref[...]) * ATT_SCALE for c in cols]
    kn = [_pair_rms(k_ref[:, c], kg_ref[...]) for c in cols]
    for p in pairs:
        kn_ref[:, cols[p]] = kn[p]
    kpast = [kp_ref[:, c].astype(BF16) for c in cols]
    vpast = [vp_ref[:, c].astype(BF16) for c in cols]
    vnew = [v_ref[:, c].astype(BF16) for c in cols]
    qh = [jnp.where(first, qn[p], 0.0) if h == 0 else jnp.where(first, 0.0, qn[p]) for p, h in chains]
    bias = [_toeplitz(u_ref[p, h:h + 1, :], t) for p, h in chains]
    s_past = [_dot(qh[i], kpast[p], NT) + bias[i][:, :reach] for i, (p, h) in enumerate(chains)]
    s_new = [_dot(qh[i], kn[p], NT) + bias[i][:, reach:reach + t] for i, (p, h) in enumerate(chains)]
    m = [jnp.maximum(jnp.max(a, axis=-1, keepdims=True), jnp.max(b, axis=-1, keepdims=True))
         for a, b in zip(s_past, s_new)]
    p_past = [jnp.exp(a - mm) for a, mm in zip(s_past, m)]
    p_new = [jnp.exp(b - mm) for b, mm in zip(s_new, m)]
    l = [jnp.sum(a, axis=-1, keepdims=True) + jnp.sum(b, axis=-1, keepdims=True)
         for a, b in zip(p_past, p_new)]
    o = [(_dot(p_past[i], vpast[p]) + _dot(p_new[i], vnew[p])) / l[i] for i, (p, h) in enumerate(chains)]
    for p in pairs:
        o_ref[:, cols[p]] = jnp.where(first, o[2 * p], o[2 * p + 1]).astype(BF16)


def _attn_sample(z3, k_past, v_past, q_gain, k_gain, u):
    nb, t, _ = z3.shape
    reach = k_past.shape[1]
    npairs = N_ATT_HEADS // 2
    att, kn = pl.pallas_call(
        _attn_sample_kernel,
        grid=(nb,),
        in_specs=[
            pl.BlockSpec((None, t, D_ATT), lambda b: (b, 0, 0)),
            pl.BlockSpec((None, t, D_ATT), lambda b: (b, 0, 1)),
            pl.BlockSpec((None, t, D_ATT), lambda b: (b, 0, 2)),
            pl.BlockSpec((None, reach, D_ATT), lambda b: (b, 0, 0)),
            pl.BlockSpec((None, reach, D_ATT), lambda b: (b, 0, 0)),
            pl.BlockSpec((1, PAIR), lambda b: (0, 0)),
            pl.BlockSpec((1, PAIR), lambda b: (0, 0)),
            pl.BlockSpec((npairs, 2, BIAS_LEN), lambda b: (0, 0, 0)),
        ],
        out_specs=[
            pl.BlockSpec((None, t, D_ATT), lambda b: (b, 0, 0)),
            pl.BlockSpec((None, t, D_ATT), lambda b: (b, 0, 0)),
        ],
        out_shape=[
            jax.ShapeDtypeStruct((nb, t, D_ATT), BF16),
            jax.ShapeDtypeStruct((nb, t, D_ATT), F32),
        ],
        compiler_params=_cparams(("arbitrary",)),
        name="attn_sample",
    )(z3, z3, z3, k_past.reshape(nb, reach, D_ATT), v_past.reshape(nb, reach, D_ATT),
      jnp.tile(q_gain, 2).reshape(1, PAIR), jnp.tile(k_gain, 2).reshape(1, PAIR),
      u.reshape(npairs, 2, BIAS_LEN))
    return att, kn


def _tri_inverse(l_mats, c):
    n = l_mats[0].shape[0]
    eye = jnp.where(_iota((n, n), 0) == _iota((n, n), 1), 1.0, 0.0).astype(F32)
    a_s = [eye + l for l in l_mats]
    t_s = [eye - l for l in l_mats]
    for _ in range(c.bit_length() - 3):
        r_s = [eye - _dot(a, t) for a, t in zip(a_s, t_s)]
        t_s = [t + _dot(t, r) for t, r in zip(t_s, r_s)]
    r_s = [eye - _dot3(a, t) for a, t in zip(a_s, t_s)]
    return [t + _dot(t, r) for t, r in zip(t_s, r_s)]


def _rwkv_kernel(c, r_ref, k_ref, v_ref, lo_ref, sr_ref, sk_ref, sv_ref, slo_ref, s0_ref,
                 mur_ref, muk_ref, muv_ref, mulo_ref, w0_ref, a0_ref, kkg_ref, ka_ref, rk_ref,
                 lnw_ref, lnb_ref, w2_ref, a2_ref, g2_ref,
                 o_ref, sT_ref, s_ref, cr_ref, ck_ref, cv_ref, clo_ref):
    tb = pl.program_id(2)
    rows, width = r_ref.shape
    npp = width // PAIR
    nchunks = rows // c
    h0 = _iota((rows, PAIR), 1) < HEAD_DIM
    bd = _blk(_iota((PAIR, PAIR), 0), HEAD_DIM) == _blk(_iota((PAIR, PAIR), 1), HEAD_DIM)

    @pl.when(tb == 0)
    def _():
        s_ref[...] = jnp.zeros(s_ref.shape, F32)
        for pp in range(npp):
            s_ref[pp, 0:HEAD_DIM, 0:HEAD_DIM] = s0_ref[2 * pp]
            s_ref[pp, HEAD_DIM:PAIR, HEAD_DIM:PAIR] = s0_ref[2 * pp + 1]
        cr_ref[...] = sr_ref[...]
        ck_ref[...] = sk_ref[...]
        cv_ref[...] = sv_ref[...]
        clo_ref[...] = slo_ref[...]

    def shifted(x_ref, carry_ref, mu_ref):
        x = x_ref[...]
        prev = jnp.where(_iota(x.shape, 0) == 0, carry_ref[...], pltpu.roll(x, 1, 0))
        carry_ref[...] = x[rows - 1:rows]
        return x + (prev - x) * mu_ref[...]

    r = shifted(r_ref, cr_ref, mur_ref)
    k = shifted(k_ref, ck_ref, muk_ref)
    v = shifted(v_ref, cv_ref, muv_ref)
    lo = shifted(lo_ref, clo_ref, mulo_ref)

    zeros_w = jnp.zeros((RANK_W, width), F32)
    w2p = jnp.concatenate([w2_ref[...], zeros_w], axis=0)
    a2p = jnp.concatenate([zeros_w, a2_ref[...]], axis=0)
    lo_wa = lo[:, 0:RANK_W + RANK_A]
    u = w0_ref[...] + _dot3(jnp.tanh(lo_wa), w2p)
    lw = -jnp.exp(-_softplus(-u) - 0.5)
    a = _sigmoid(a0_ref[...] + _dot(lo_wa, a2p))
    g = _dot(_sigmoid(lo[:, RANK_W + RANK_A:]), g2_ref[...])

    kk = k * kkg_ref[...]
    kk = kk / jnp.maximum(jnp.sqrt(_head_sums(kk * kk)), 1e-12)
    k = k * (1.0 + (a - 1.0) * ka_ref[...])
    b = kk * a
    bonus = _head_sums(r * k * rk_ref[...]) * v

    tr = _iota((rows, rows), 0)
    tc = _iota((rows, rows), 1)
    same_chunk = _blk(tr, c) == _blk(tc, c)
    strict = same_chunk & (tr > tc)
    incl = same_chunk & (tr >= tc)
    lw_hi, lw_lo = _split2(lw)
    tril_ones = jnp.where(incl, 1.0, 0.0).astype(BF16)
    lp = jnp.dot(tril_ones, lw_hi, preferred_element_type=F32) + \
        jnp.dot(tril_ones, lw_lo, preferred_element_type=F32)
    lp_end = jnp.concatenate(
        [jnp.broadcast_to(lp[(ci + 1) * c - 1:(ci + 1) * c], (c, width)) for ci in range(nchunks)], axis=0)

    alpha_w = kk * jnp.exp(lp - lw)
    inv_p = jnp.exp(-lp)
    beta_w = b * inv_p
    kappa_w = k * inv_p
    rho_w = r * jnp.exp(lp)
    to_end = jnp.exp(lp_end - lp)
    beta_ew = b * to_end
    kappa_ew = k * to_end
    decay_end_w = jnp.exp(lp_end)

    wide = (rows, nchunks * PAIR)
    col_chunk = _blk(_iota(wide, 1), PAIR) == _blk(_iota(wide, 0), c)
    spread = lambda m: jnp.where(col_chunk, jnp.tile(m, (1, nchunks)), 0.0)
    eye_p = _iota((PAIR, PAIR), 0) == _iota((PAIR, PAIR), 1)

    pairs = range(npp)
    lanes = [slice(pp * PAIR, (pp + 1) * PAIR) for pp in pairs]
    alpha = [alpha_w[:, l] for l in lanes]
    rho = [rho_w[:, l] for l in lanes]
    vv = [v[:, l] for l in lanes]
    head_mask = [h0, jnp.logical_not(h0)]
    bk = [jnp.concatenate([beta_w[:, l], kappa_w[:, l]], axis=0).astype(BF16) for l in lanes]
    prod = [[_dot(jnp.concatenate([jnp.where(hm, alpha[pp], 0.0), jnp.where(hm, rho[pp], 0.0)], axis=0),
                  bk[pp], NT) for hm in head_mask] for pp in pairs]
    t_inv = _tri_inverse([jnp.where(strict, prod[pp][h][:rows, :rows], 0.0) for pp in pairs for h in range(2)], c)
    x = [[_dot(jnp.where(strict, prod[pp][h][:rows, rows:], 0.0), vv[pp]) for h in range(2)] for pp in pairs]
    ws = [[_dot(t_inv[2 * pp + h], jnp.concatenate([alpha[pp], x[pp][h]], axis=1)) for h in range(2)]
          for pp in pairs]
    w12 = [jnp.concatenate([jnp.where(h0, ws[pp][0][:, :PAIR], ws[pp][1][:, :PAIR]),
                            jnp.where(h0, ws[pp][0][:, PAIR:], ws[pp][1][:, PAIR:])], axis=1) for pp in pairs]
    q = [[_dot(jnp.where(incl, prod[pp][h][rows:, :rows], 0.0), w12[pp]) for h in range(2)] for pp in pairs]
    qk = [[_dot(jnp.where(incl, prod[pp][h][rows:, rows:], 0.0), vv[pp]) for h in range(2)] for pp in pairs]
    rp = [rho[pp] - jnp.where(h0, q[pp][0][:, :PAIR], q[pp][1][:, :PAIR]) for pp in pairs]
    y0 = [jnp.where(h0, qk[pp][0] - q[pp][0][:, PAIR:], qk[pp][1] - q[pp][1][:, PAIR:]) for pp in pairs]
    wtb = [_dot(w12[pp], spread(beta_ew[:, lanes[pp]]), TN) for pp in pairs]
    vtk = [_dot(vv[pp], spread(kappa_ew[:, lanes[pp]]), TN) for pp in pairs]

    s_cur = [s_ref[pp] for pp in pairs]
    ys = [[] for _ in pairs]
    for ci in range(nchunks):
        sl = slice(ci * c, (ci + 1) * c)
        cols = slice(ci * PAIR, (ci + 1) * PAIR)
        for pp in pairs:
            decay_end = decay_end_w[ci * c:ci * c + 1, lanes[pp]]
            gmat = jnp.where(eye_p, jnp.broadcast_to(decay_end, (PAIR, PAIR)), 0.0) \
                - jnp.where(bd, wtb[pp][:PAIR, cols], 0.0)
            hmat = jnp.where(bd, vtk[pp][:, cols] - wtb[pp][PAIR:, cols], 0.0)
            ys[pp].append(_dot(rp[pp][sl], s_cur[pp], NT) + y0[pp][sl])
            s_cur[pp] = _dot(s_cur[pp], gmat) + hmat
    for pp in pairs:
        s_ref[pp] = s_cur[pp]
    y_pairs = [ys[pp][0] if nchunks == 1 else jnp.concatenate(ys[pp], axis=0) for pp in pairs]

    @pl.when(tb == pl.num_programs(2) - 1)
    def _():
        for pp in range(npp):
            sT_ref[2 * pp] = s_ref[pp, 0:HEAD_DIM, 0:HEAD_DIM]
            sT_ref[2 * pp + 1] = s_ref[pp, HEAD_DIM:PAIR, HEAD_DIM:PAIR]

    y = y_pairs[0] if npp == 1 else jnp.concatenate(y_pairs, axis=1)
    mu = _head_sums(y) * (1.0 / HEAD_DIM)
    d = y - mu
    var = _head_sums(d * d) * (1.0 / HEAD_DIM)
    yn = d * lax.rsqrt(var + GN_EPS) * lnw_ref[...] + lnb_ref[...]
    o_ref[...] = ((yn + bonus) * g).astype(BF16)


def _rwkv(z3, shift_prev, s0, p, rows, c, npp):
    nb, t, _ = z3.shape
    width = npp * PAIR
    ngroups = D_RWKV // width
    col0 = 3 * D_ATT // width
    lo_blk = (3 * D_ATT + 3 * D_RWKV) // D_LORA
    sp = shift_prev.reshape(nb, 1, D_SHIFT)

    def zspec(off):
        return pl.BlockSpec((None, rows, width), lambda b, q, s: (b, s, col0 + off * ngroups + q))

    def sspec(off):
        return pl.BlockSpec((None, 1, width), lambda b, q, s: (b, 0, off * ngroups + q))

    def vec(off=0):
        return pl.BlockSpec((1, width), lambda b, q, s: (0, off * ngroups + q))

    def row2(x):
        return x.reshape(1, -1)

    out, s_fin = pl.pallas_call(
        functools.partial(_rwkv_kernel, c),
        grid=(nb, ngroups, t // rows),
        in_specs=[
            zspec(0), zspec(1), zspec(2),
            pl.BlockSpec((None, rows, D_LORA), lambda b, q, s: (b, s, lo_blk)),
            sspec(0), sspec(1), sspec(2),
            pl.BlockSpec((None, 1, D_LORA), lambda b, q, s: (b, 0, 3 * D_RWKV // D_LORA)),
            pl.BlockSpec((None, 2 * npp, HEAD_DIM, HEAD_DIM), lambda b, q, s: (b, q, 0, 0)),
            vec(0), vec(1), vec(2),
            pl.BlockSpec((1, D_LORA), lambda b, q, s: (0, 3 * D_RWKV // D_LORA)),
            vec(), vec(), vec(), vec(), vec(), vec(), vec(),
            pl.BlockSpec((RANK_W, width), lambda b, q, s: (0, q)),
            pl.BlockSpec((RANK_A, width), lambda b, q, s: (0, q)),
            pl.BlockSpec((RANK_G, width), lambda b, q, s: (0, q)),
        ],
        out_specs=[
            pl.BlockSpec((None, rows, width), lambda b, q, s: (b, s, q)),
            pl.BlockSpec((None, 2 * npp, HEAD_DIM, HEAD_DIM), lambda b, q, s: (b, q, 0, 0)),
        ],
        out_shape=[
            jax.ShapeDtypeStruct((nb, t, D_RWKV), BF16),
            jax.ShapeDtypeStruct((nb, N_RWKV_HEADS, HEAD_DIM, HEAD_DIM), F32),
        ],
        scratch_shapes=[
            pltpu.VMEM((npp, PAIR, PAIR), F32),
            pltpu.VMEM((1, width), F32), pltpu.VMEM((1, width), F32), pltpu.VMEM((1, width), F32),
            pltpu.VMEM((1, D_LORA), F32),
        ],
        compiler_params=_cparams(("arbitrary", "arbitrary", "arbitrary")),
        name="rwkv7_mix",
    )(z3, z3, z3, z3, sp, sp, sp, sp, s0,
      row2(p['mu_shift']), row2(p['mu_shift']), row2(p['mu_shift']), row2(p['mu_shift']),
      row2(p['w0']), row2(p['a0']), row2(p['k_k']), row2(p['k_a']), row2(p['r_k']),
      row2(p['ln_x_w']), row2(p['ln_x_b']), p['w2'], p['a2'], p['g2'])
    return out, s_fin


def _layer(x3, mod, p, u, k_past, v_past, s0, shift_prev, conv_prev):
    nb, t, d = x3.shape
    m = nb * t
    x = x3.reshape(m, d)
    z = _norm_proj(x, p['norm_att_g'], mod, 1, 0, p['w_in'], IN_COLS, "in_proj",
                   min(IN_ROW_TILE, t) if k_past is None else m)
    z3 = z.reshape(nb, t, D_IN)
    if k_past is None:
        att, k_keep = _attn_prompt(z3, p['q_norm_g'], p['k_norm_g'], u)
        keep = min(ATT_REACH, t)
        v_keep = z3[:, t - keep:, 2 * D_ATT:3 * D_ATT]
        rw, s_fin = _rwkv(z3, shift_prev, s0, p, RWKV_ROWS, CHUNK, RWKV_PAIRS_PROMPT)
    else:
        att, k_keep = _attn_sample(z3, k_past, v_past, p['q_norm_g'], p['k_norm_g'], u)
        v_keep = z3[:, :, 2 * D_ATT:3 * D_ATT]
        rw, s_fin = _rwkv(z3, shift_prev, s0, p, t, t, RWKV_PAIRS_SAMPLE)
    shift_last = z3[:, t - 1, 3 * D_ATT:]
    x1 = _proj_resid([att.reshape(m, D_ATT), rw.reshape(m, D_RWKV)], p['w_out'], x, mod, 2,
                     OUT_COLS, "out_proj")
    if k_past is None:
        act, conv_last = _ffn_up_fused(x1, p['norm_ffn_g'], mod, p['w_up'], conv_prev,
                                       p['dw_conv'], p['dw_bias'], t)
    else:
        hu = _norm_proj(x1, p['norm_ffn_g'], mod, 4, 3, p['w_up'], UP_COLS, "ffn_up_sample")
        f = hu.shape[1] // 2
        act = _act_sample(hu, conv_prev, p['dw_conv'], p['dw_bias'], nb, t)
        conv_last = hu.reshape(nb, t, 2 * f)[:, t - (CONV_W - 1):, :f]
    x2 = _proj_resid([act], p['w_down'], x1, mod, 5, DOWN_COLS, "ffn_down")
    heads = lambda a: a.reshape(nb, a.shape[1], N_ATT_HEADS, HEAD_DIM)
    return x2.reshape(nb, t, d), heads(k_keep), heads(v_keep), s_fin, shift_last, conv_last


def kernel(x_prompt, x_sample, c_prompt, c_sample, cache_att_k, cache_att_v, state_rwkv, state_shift, state_ffn_conv, norm_att_g, norm_ffn_g, w_ada, b_ada, w_in, q_norm_g, k_norm_g, rel_bias, mu_shift, w0, w2, a0, a2, g2, k_k, k_a, r_k, ln_x_w, ln_x_b, w_out, w_up, dw_conv, dw_bias, w_down):
    depth = w_in.shape[0]
    bp, tp, d = x_prompt.shape
    bs, ts, _ = x_sample.shape
    d_ff = w_down.shape[1]
    hp, hs = x_prompt, x_sample
    outs_p = [[] for _ in range(5)]
    outs_s = [[] for _ in range(5)]
    for l in range(depth):
        p = dict(norm_att_g=norm_att_g[l], norm_ffn_g=norm_ffn_g[l], w_in=w_in[l], q_norm_g=q_norm_g[l],
                 k_norm_g=k_norm_g[l], mu_shift=mu_shift[l], w0=w0[l], w2=w2[l], a0=a0[l], a2=a2[l],
                 g2=g2[l], k_k=k_k[l], k_a=k_a[l], r_k=r_k[l], ln_x_w=ln_x_w[l], ln_x_b=ln_x_b[l],
                 w_out=w_out[l], w_up=w_up[l], dw_conv=dw_conv[l], dw_bias=dw_bias[l], w_down=w_down[l])
        n_c = bp + bs
        pad = (-n_c) % 8
        c_all = jnp.concatenate([c_prompt, c_sample, jnp.zeros((pad, d), F32)], axis=0)
        mod = _ada(c_all, w_ada[l], b_ada[l])
        mod_p = _Mod(mod.reshape(n_c + pad, 6, 1, d), False, rows_per_batch=tp)
        mod_s = _Mod(jnp.repeat(mod[bp:bp + bs], ts, axis=0), True)
        u = _bias_rows(rel_bias[l])

        res = _layer(hp, mod_p, p, u, None, None,
                     jnp.zeros((bp, N_RWKV_HEADS, HEAD_DIM, HEAD_DIM), F32),
                     jnp.zeros((bp, D_SHIFT), F32),
                     jnp.zeros((bp, CONV_W - 1, d_ff), F32))
        hp = res[0]
        for lst, val in zip(outs_p, res[1:]):
            lst.append(val)
        res = _layer(hs, mod_s, p, u, cache_att_k[l], cache_att_v[l], state_rwkv[l],
                     state_shift[l], state_ffn_conv[l])
        hs = res[0]
        for lst, val in zip(outs_s, res[1:]):
            lst.append(val)
    st = lambda lst: jnp.stack(lst)
    return (hp, hs, *[st(x) for x in outs_p], *[st(x) for x in outs_s])
```

```python
import functools

import jax
import jax.numpy as jnp
from jax import lax
from jax.experimental import pallas as pl
from jax.experimental.pallas import tpu as pltpu

F32 = jnp.float32
BF16 = jnp.bfloat16

CHUNK = 64
N_PREV_CHUNKS = 8
ATT_REACH = N_PREV_CHUNKS * CHUNK
HEAD_DIM = 64
N_ATT_HEADS = 16
N_RWKV_HEADS = 16
D_ATT = N_ATT_HEADS * HEAD_DIM
D_RWKV = N_RWKV_HEADS * HEAD_DIM
REL_CLIP = 128
RANK_W = 64
RANK_A = 64
RANK_G = 128
D_LORA = RANK_W + RANK_A + RANK_G
D_SHIFT = 3 * D_RWKV + D_LORA
D_IN = 3 * D_ATT + D_SHIFT
CONV_W = 3
RMS_EPS = 1e-6
GN_EPS = 64e-5
ATT_SCALE = HEAD_DIM ** -0.5
NEG_INF = -1e30

LANES = 128
PAIR = 2 * HEAD_DIM
MXU_DIM = 256
VMEM_LIMIT = 60 * 1024 * 1024

ROW_TILE = 1024
IN_ROW_TILE = 2048
ADA_COLS = 512
IN_COLS = 256
OUT_COLS = 512
UP_COLS = 512
DOWN_COLS = 256
ATT_QROWS = 256
ATT_WIN = ATT_QROWS + ATT_REACH
ATT_PAIRS = 2
BIAS_LEN = 1024
RWKV_ROWS = 256
RWKV_PAIRS_PROMPT = 8
RWKV_PAIRS_SAMPLE = 8


def _cparams(sem):
    return pltpu.CompilerParams(dimension_semantics=sem, vmem_limit_bytes=VMEM_LIMIT)


def _dot(a, b, dims=(((1,), (0,)), ((), ()))):
    return lax.dot_general(a.astype(BF16), b.astype(BF16), dims, preferred_element_type=F32)


def _split2(x):
    hi = x.astype(BF16)
    lo = (x - hi.astype(F32)).astype(BF16)
    return hi, lo


NT = (((1,), (1,)), ((), ()))
TN = (((0,), (0,)), ((), ()))


def _iota(shape, dim):
    return lax.broadcasted_iota(jnp.int32, shape, dim)


def _blk(x, size):
    return jnp.right_shift(x, size.bit_length() - 1)


def _head_ones(n):
    r = _blk(_iota((n, n), 0), HEAD_DIM)
    c = _blk(_iota((n, n), 1), HEAD_DIM)
    return jnp.where(r == c, 1.0, 0.0).astype(BF16)


def _head_sums(x):
    lanes = x.shape[1]
    group = min(lanes, MXU_DIM)
    ones = _head_ones(group)
    parts = [_dot(x[:, i:i + group], ones) for i in range(0, lanes, group)]
    return parts[0] if len(parts) == 1 else jnp.concatenate(parts, axis=1)


def _sigmoid(x):
    return 1.0 / (1.0 + jnp.exp(-x))


def _softplus(x):
    return jnp.maximum(x, 0.0) + jnp.log(1.0 + jnp.exp(-jnp.abs(x)))


def _ada_kernel(c_ref, w_ref, b_ref, o_ref):
    c = c_ref[...]
    s = c * _sigmoid(c)
    o_ref[...] = _dot(s, w_ref[...]) + b_ref[...]


def _ada(c_all, w_ada, b_ada):
    rows, d = c_all.shape
    n = w_ada.shape[1]
    return pl.pallas_call(
        _ada_kernel,
        grid=(n // ADA_COLS,),
        in_specs=[
            pl.BlockSpec((rows, d), lambda j: (0, 0)),
            pl.BlockSpec((d, ADA_COLS), lambda j: (0, j)),
            pl.BlockSpec((1, ADA_COLS), lambda j: (0, j)),
        ],
        out_specs=pl.BlockSpec((rows, ADA_COLS), lambda j: (0, j)),
        out_shape=jax.ShapeDtypeStruct((rows, n), F32),
        compiler_params=_cparams(("arbitrary",)),
        name="ada_mod",
    )(c_all, w_ada, b_ada.reshape(1, n))


class _Mod:
    def __init__(self, arr, per_row, rows_per_batch=None):
        self.arr = arr
        self.per_row = per_row
        self.rows_per_batch = rows_per_batch

    def spec(self, idx, cols, col_of, row_tile):
        if self.per_row:
            m = self.arr.shape[0]
            d = self.arr.shape[1] // 6
            nblk = d // cols
            return pl.BlockSpec((m, cols), lambda i, j: (0, idx * nblk + col_of(j)))
        tiles_per_batch = self.rows_per_batch // row_tile
        return pl.BlockSpec((None, None, 1, cols),
                            lambda i, j: (i // tiles_per_batch, idx, 0, col_of(j)))


NORM_ROWS = 128


def _store_normed(h_ref, x_ref, g_ref, sc_ref, sh_ref):
    rows = x_ref.shape[0]
    step = min(NORM_ROWS, rows)
    per_row = sc_ref.shape[0] == rows

    def body(r, carry):
        sl = pl.ds(pl.multiple_of(r * step, step), step)
        x = x_ref[sl, :]
        ms = jnp.mean(x * x, axis=-1, keepdims=True)
        xn = x * lax.rsqrt(ms + RMS_EPS) * g_ref[...]
        sc = sc_ref[sl, :] if per_row else sc_ref[...]
        sh = sh_ref[sl, :] if per_row else sh_ref[...]
        h_ref[sl, :] = (xn * (1.0 + sc) + sh).astype(BF16)
        return carry

    lax.fori_loop(0, rows // step, body, 0)


def _norm_proj_kernel(x_ref, g_ref, sc_ref, sh_ref, w_ref, o_ref, h_ref):
    @pl.when(pl.program_id(1) == 0)
    def _():
        _store_normed(h_ref, x_ref, g_ref, sc_ref, sh_ref)

    o_ref[...] = jnp.dot(h_ref[...], w_ref[...].astype(BF16), preferred_element_type=F32)


def _norm_proj(x, gain, mod, sc_idx, sh_idx, w, cols, name, row_tile=ROW_TILE):
    m, d = x.shape
    n = w.shape[1]
    tm = min(row_tile, m)
    whole = lambda j: 0
    x_mode = dict(pipeline_mode=pl.Buffered(1)) if tm > ROW_TILE else {}
    return pl.pallas_call(
        _norm_proj_kernel,
        grid=(m // tm, n // cols),
        in_specs=[
            pl.BlockSpec((tm, d), lambda i, j: (i, 0), **x_mode),
            pl.BlockSpec((1, d), lambda i, j: (0, 0)),
            mod.spec(sc_idx, d, whole, tm),
            mod.spec(sh_idx, d, whole, tm),
            pl.BlockSpec((d, cols), lambda i, j: (0, j)),
        ],
        out_specs=pl.BlockSpec((tm, cols), lambda i, j: (i, j)),
        out_shape=jax.ShapeDtypeStruct((m, n), F32),
        scratch_shapes=[pltpu.VMEM((tm, d), BF16)],
        compiler_params=_cparams(("arbitrary", "arbitrary")),
        name=name,
    )(x, gain.reshape(1, d), mod.arr, mod.arr, w)


def _proj_resid_kernel(n_pairs, *refs):
    a_refs = refs[:n_pairs]
    w_refs = refs[n_pairs:2 * n_pairs]
    x_ref, g_ref, o_ref = refs[2 * n_pairs:]
    acc = jnp.dot(a_refs[0][...], w_refs[0][...].astype(BF16), preferred_element_type=F32)
    for a_ref, w_ref in zip(a_refs[1:], w_refs[1:]):
        acc = acc + jnp.dot(a_ref[...], w_ref[...].astype(BF16), preferred_element_type=F32)
    o_ref[...] = x_ref[...] + g_ref[...] * acc


def _proj_resid(a_list, w, x, mod, g_idx, cols, name):
    m, n = x.shape
    tm = min(ROW_TILE, m)
    in_specs, w_args = [], []
    row = 0
    for a in a_list:
        kdim = a.shape[1]
        in_specs.append(pl.BlockSpec((tm, kdim), lambda i, j: (i, 0)))
    for a in a_list:
        kdim = a.shape[1]
        in_specs.append(pl.BlockSpec((kdim, cols), lambda i, j, r=row // kdim: (r, j)))
        w_args.append(w)
        row += kdim
    in_specs.append(pl.BlockSpec((tm, cols), lambda i, j: (i, j)))
    in_specs.append(mod.spec(g_idx, cols, lambda j: j, tm))
    return pl.pallas_call(
        functools.partial(_proj_resid_kernel, len(a_list)),
        grid=(m // tm, n // cols),
        in_specs=in_specs,
        out_specs=pl.BlockSpec((tm, cols), lambda i, j: (i, j)),
        out_shape=jax.ShapeDtypeStruct((m, n), F32),
        compiler_params=_cparams(("arbitrary", "arbitrary")),
        name=name,
    )(*a_list, *w_args, x, mod.arr)


def _gelu(x):
    return 0.5 * x * (1.0 + lax.erf(x * (2.0 ** -0.5)))


def _ffn_up_kernel(tiles_per_batch, x_ref, g_ref, sc_ref, sh_ref, wg_ref, wv_ref, hist_ref,
                   cw_ref, cb_ref, act_ref, last_ref, h_ref, carry_ref):
    i = pl.program_id(0)
    j = pl.program_id(1)

    @pl.when(j == 0)
    def _():
        _store_normed(h_ref, x_ref, g_ref, sc_ref, sh_ref)

    @pl.when((i % tiles_per_batch) == 0)
    def _():
        carry_ref[j] = hist_ref[...]

    h = h_ref[...]
    gate = jnp.dot(h, wg_ref[...].astype(BF16), preferred_element_type=F32)
    val = jnp.dot(h, wv_ref[...].astype(BF16), preferred_element_type=F32)
    tm = gate.shape[0]
    prev = carry_ref[j]
    row = _iota(gate.shape, 0)
    g1 = pltpu.roll(gate, 1, 0)
    g2 = pltpu.roll(gate, 2, 0)
    g1 = jnp.where(row == 0, prev[1:2], g1)
    g2 = jnp.where(row == 0, prev[0:1], jnp.where(row == 1, prev[1:2], g2))
    cw = cw_ref[...]
    conv = cb_ref[...] + g2 * cw[0:1] + g1 * cw[1:2] + gate * cw[2:3]
    act_ref[...] = (_gelu(conv) * val).astype(BF16)
    tail = gate[tm - 2:tm]
    carry_ref[j] = tail
    last_ref[...] = tail


def _ffn_up_fused(x, gain, mod, w_up, hist, conv_w, conv_b, rows_per_batch):
    m, d = x.shape
    f = w_up.shape[1] // 2
    tm = min(ROW_TILE, rows_per_batch)
    cols = UP_COLS
    nj = f // cols
    tiles_per_batch = rows_per_batch // tm
    whole = lambda j: 0
    act, tile_tails = pl.pallas_call(
        functools.partial(_ffn_up_kernel, tiles_per_batch),
        grid=(m // tm, nj),
        in_specs=[
            pl.BlockSpec((tm, d), lambda i, j: (i, 0)),
            pl.BlockSpec((1, d), lambda i, j: (0, 0)),
            mod.spec(4, d, whole, tm),
            mod.spec(3, d, whole, tm),
            pl.BlockSpec((d, cols), lambda i, j: (0, j)),
            pl.BlockSpec((d, cols), lambda i, j: (0, nj + j)),
            pl.BlockSpec((None, CONV_W - 1, cols), lambda i, j: (i // tiles_per_batch, 0, j)),
            pl.BlockSpec((CONV_W, cols), lambda i, j: (0, j)),
            pl.BlockSpec((1, cols), lambda i, j: (0, j)),
        ],
        out_specs=[
            pl.BlockSpec((tm, cols), lambda i, j: (i, j)),
            pl.BlockSpec((None, CONV_W - 1, cols), lambda i, j: (i, 0, j)),
        ],
        out_shape=[
            jax.ShapeDtypeStruct((m, f), BF16),
            jax.ShapeDtypeStruct((m // tm, CONV_W - 1, f), F32),
        ],
        scratch_shapes=[pltpu.VMEM((tm, d), BF16), pltpu.VMEM((nj, CONV_W - 1, cols), F32)],
        compiler_params=_cparams(("arbitrary", "arbitrary")),
        name="ffn_up_prompt",
    )(x, gain.reshape(1, d), mod.arr, mod.arr, w_up, w_up, hist, conv_w, conv_b.reshape(1, f))
    return act, tile_tails[tiles_per_batch - 1::tiles_per_batch]


def _act_sample_kernel(gate_ref, val_ref, hist_ref, cw_ref, cb_ref, act_ref):
    gate = gate_ref[...]
    hist = hist_ref[...]
    t = _iota(gate.shape, 1)
    g1 = jnp.where(t == 0, hist[:, 1:2], pltpu.roll(gate, 1, 1))
    g2 = jnp.where(t == 0, hist[:, 0:1], jnp.where(t == 1, hist[:, 1:2], pltpu.roll(gate, 2, 1)))
    cw = cw_ref[...]
    conv = cb_ref[...] + g2 * cw[0:1] + g1 * cw[1:2] + gate * cw[2:3]
    act_ref[...] = (_gelu(conv) * val_ref[...]).astype(BF16)


def _act_sample(hu, hist, conv_w, conv_b, nb, t):
    f = hu.shape[1] // 2
    cols = UP_COLS
    nj = f // cols
    hu3 = hu.reshape(nb, t, 2 * f)
    act = pl.pallas_call(
        _act_sample_kernel,
        grid=(nj,),
        in_specs=[
            pl.BlockSpec((nb, t, cols), lambda j: (0, 0, j)),
            pl.BlockSpec((nb, t, cols), lambda j: (0, 0, nj + j)),
            pl.BlockSpec((nb, CONV_W - 1, cols), lambda j: (0, 0, j)),
            pl.BlockSpec((CONV_W, cols), lambda j: (0, j)),
            pl.BlockSpec((1, cols), lambda j: (0, j)),
        ],
        out_specs=pl.BlockSpec((nb, t, cols), lambda j: (0, 0, j)),
        out_shape=jax.ShapeDtypeStruct((nb, t, f), BF16),
        compiler_params=_cparams(("arbitrary",)),
        name="ffn_act_sample",
    )(hu3, hu3, hist, conv_w, conv_b.reshape(1, f))
    return act.reshape(nb * t, f)


def _pair_rms(x, gain):
    x2 = x * x
    first = _iota(x.shape, 1) < HEAD_DIM
    s0 = jnp.sum(jnp.where(first, x2, 0.0), axis=-1, keepdims=True)
    s1 = jnp.sum(jnp.where(first, 0.0, x2), axis=-1, keepdims=True)
    ms = jnp.where(first, s0, s1) * (1.0 / HEAD_DIM)
    return x * lax.rsqrt(ms + RMS_EPS) * gain


def _bias_rows(table):
    h = table.shape[0]
    far = jnp.broadcast_to(table[:, 2 * REL_CLIP:], (h, ATT_REACH - REL_CLIP))
    mid = table[:, ::-1]
    near_len = BIAS_LEN - ATT_QROWS - (ATT_REACH - REL_CLIP) - (2 * REL_CLIP + 1)
    near = jnp.broadcast_to(table[:, 0:1], (h, near_len))
    wrap = jnp.broadcast_to(table[:, 2 * REL_CLIP:], (h, ATT_QROWS))
    return jnp.concatenate([far, mid, near, wrap], axis=1)


def _toeplitz(u_row, rows):
    return pltpu.roll(jnp.broadcast_to(u_row, (rows, BIAS_LEN)), 0, 1, stride=1, stride_axis=0)


def _attn_prompt_kernel(q_ref, k_ref, v_ref, qg_ref, kg_ref, u_ref, o_ref, kn_ref,
                        bias_ref, kwin_ref, vwin_ref):
    b = pl.program_id(1)
    qb = pl.program_id(2)
    shape = (ATT_QROWS, ATT_WIN)
    pairs = range(q_ref.shape[1] // PAIR)
    cols = [slice(p * PAIR, (p + 1) * PAIR) for p in pairs]
    chains = [(p, h) for p in pairs for h in range(2)]

    @pl.when((b == 0) & (qb == 0))
    def _():
        r = _iota(shape, 0)
        w = _iota(shape, 1)
        chunk_lo = _blk(r, CHUNK) * CHUNK
        in_band = (w >= chunk_lo) & (w < chunk_lo + (ATT_REACH + CHUNK))
        for i, (p, h) in enumerate(chains):
            bias = _toeplitz(u_ref[p, h:h + 1, :], ATT_QROWS)[:, :ATT_WIN]
            bias_ref[i] = jnp.where(in_band, bias, -jnp.inf)

    @pl.when(qb == 0)
    def _():
        kwin_ref[0:ATT_REACH] = jnp.zeros((ATT_REACH, kwin_ref.shape[1]), BF16)
        vwin_ref[0:ATT_REACH] = jnp.zeros((ATT_REACH, vwin_ref.shape[1]), BF16)

    @pl.when(qb > 0)
    def _():
        kwin_ref[0:ATT_REACH] = kwin_ref[ATT_QROWS:ATT_WIN]
        vwin_ref[0:ATT_REACH] = vwin_ref[ATT_QROWS:ATT_WIN]

    kn = [_pair_rms(k_ref[:, c], kg_ref[...]) for c in cols]
    for p in pairs:
        kn_ref[:, cols[p]] = kn[p]
        kwin_ref[ATT_REACH:ATT_WIN, cols[p]] = kn[p].astype(BF16)
    vwin_ref[ATT_REACH:ATT_WIN] = v_ref[...].astype(BF16)
    qn = [_pair_rms(q_ref[:, c], qg_ref[...]) * ATT_SCALE for c in cols]
    kb = [kwin_ref[:, c] for c in cols]
    vb = [vwin_ref[:, c] for c in cols]

    first = _iota((ATT_QROWS, PAIR), 1) < HEAD_DIM
    started = _iota(shape, 1) >= ATT_REACH - qb * ATT_QROWS
    qh = [jnp.where(first, qn[p], 0.0) if h == 0 else jnp.where(first, 0.0, qn[p]) for p, h in chains]
    s = [_dot(qh[i], kb[p], NT) + bias_ref[i] for i, (p, h) in enumerate(chains)]
    s = [jnp.maximum(jnp.where(started, x, -jnp.inf), NEG_INF) for x in s]
    m = [jnp.max(x, axis=-1, keepdims=True) for x in s]
    pr = [jnp.exp(x - mm) for x, mm in zip(s, m)]
    l = [jnp.sum(x, axis=-1, keepdims=True) for x in pr]
    o = [_dot(pr[i], vb[p]) / l[i] for i, (p, h) in enumerate(chains)]
    for p in pairs:
        o_ref[:, cols[p]] = jnp.where(first, o[2 * p], o[2 * p + 1]).astype(BF16)


def _attn_prompt(z3, q_gain, k_gain, u):
    nb, t, _ = z3.shape
    npairs = N_ATT_HEADS // 2
    npp = ATT_PAIRS
    width = npp * PAIR
    ngroups = npairs // npp
    nq = t // ATT_QROWS
    kcol = D_ATT // width
    vcol = 2 * D_ATT // width
    keep_blocks = ATT_REACH // ATT_QROWS
    blk = (None, ATT_QROWS, width)

    att, kn = pl.pallas_call(
        _attn_prompt_kernel,
        grid=(ngroups, nb, nq),
        in_specs=[
            pl.BlockSpec(blk, lambda g, b, q: (b, q, g)),
            pl.BlockSpec(blk, lambda g, b, q: (b, q, kcol + g)),
            pl.BlockSpec(blk, lambda g, b, q: (b, q, vcol + g)),
            pl.BlockSpec((1, PAIR), lambda g, b, q: (0, 0)),
            pl.BlockSpec((1, PAIR), lambda g, b, q: (0, 0)),
            pl.BlockSpec((npp, 2, BIAS_LEN), lambda g, b, q: (g, 0, 0)),
        ],
        out_specs=[
            pl.BlockSpec(blk, lambda g, b, q: (b, q, g)),
            pl.BlockSpec(blk, lambda g, b, q: (b, jnp.maximum(q - (nq - keep_blocks), 0), g)),
        ],
        out_shape=[
            jax.ShapeDtypeStruct((nb, t, D_ATT), BF16),
            jax.ShapeDtypeStruct((nb, ATT_REACH, D_ATT), F32),
        ],
        scratch_shapes=[pltpu.VMEM((2 * npp, ATT_QROWS, ATT_WIN), F32),
                        pltpu.VMEM((ATT_WIN, width), BF16), pltpu.VMEM((ATT_WIN, width), BF16)],
        compiler_params=_cparams(("arbitrary", "arbitrary", "arbitrary")),
        name="attn_prompt",
    )(z3, z3, z3, jnp.tile(q_gain, 2).reshape(1, PAIR),
      jnp.tile(k_gain, 2).reshape(1, PAIR), u.reshape(npairs, 2, BIAS_LEN))
    return att, kn


def _attn_sample_kernel(q_ref, k_ref, v_ref, kp_ref, vp_ref, qg_ref, kg_ref, u_ref, o_ref, kn_ref):
    t = q_ref.shape[0]
    reach = kp_ref.shape[0]
    first = _iota((t, PAIR), 1) < HEAD_DIM
    pairs = range(N_ATT_HEADS // 2)
    cols = [slice(p * PAIR, (p + 1) * PAIR) for p in pairs]
    chains = [(p, h) for p in pairs for h in range(2)]
    qn = [_pair_rms(q_ref[:, c], qg_ref[...]) * ATT_SCALE for c in cols]
    kn = [_pair_rms(k_ref[:, c], kg_ref[...]) for c in cols]
    for p in pairs:
        kn_ref[:, cols[p]] = kn[p]
    kpast = [kp_ref[:, c].astype(BF16) for c in cols]
    vpast = [vp_ref[:, c].astype(BF16) for c in cols]
    vnew = [v_ref[:, c].astype(BF16) for c in cols]
    qh = [jnp.where(first, qn[p], 0.0) if h == 0 else jnp.where(first, 0.0, qn[p]) for p, h in chains]
    bias = [_toeplitz(u_ref[p, h:h + 1, :], t) for p, h in chains]
    s_past = [_dot(qh[i], kpast[p], NT) + bias[i][:, :reach] for i, (p, h) in enumerate(chains)]
    s_new = [_dot(qh[i], kn[p], NT) + bias[i][:, reach:reach + t] for i, (p, h) in enumerate(chains)]
    m = [jnp.maximum(jnp.max(a, axis=-1, keepdims=True), jnp.max(b, axis=-1, keepdims=True))
         for a, b in zip(s_past, s_new)]
    p_past = [jnp.exp(a - mm) for a, mm in zip(s_past, m)]
    p_new = [jnp.exp(b - mm) for b, mm in zip(s_new, m)]
    l = [jnp.sum(a, axis=-1, keepdims=True) + jnp.sum(b, axis=-1, keepdims=True)
         for a, b in zip(p_past, p_new)]
    o = [(_dot(p_past[i], vpast[p]) + _dot(p_new[i], vnew[p])) / l[i] for i, (p, h) in enumerate(chains)]
    for p in pairs:
        o_ref[:, cols[p]] = jnp.where(first, o[2 * p], o[2 * p + 1]).astype(BF16)


def _attn_sample(z3, k_past, v_past, q_gain, k_gain, u):
    nb, t, _ = z3.shape
    reach = k_past.shape[1]
    npairs = N_ATT_HEADS // 2
    att, kn = pl.pallas_call(
        _attn_sample_kernel,
        grid=(nb,),
        in_specs=[
            pl.BlockSpec((None, t, D_ATT), lambda b: (b, 0, 0)),
            pl.BlockSpec((None, t, D_ATT), lambda b: (b, 0, 1)),
            pl.BlockSpec((None, t, D_ATT), lambda b: (b, 0, 2)),
            pl.BlockSpec((None, reach, D_ATT), lambda b: (b, 0, 0)),
            pl.BlockSpec((None, reach, D_ATT), lambda b: (b, 0, 0)),
            pl.BlockSpec((1, PAIR), lambda b: (0, 0)),
            pl.BlockSpec((1, PAIR), lambda b: (0, 0)),
            pl.BlockSpec((npairs, 2, BIAS_LEN), lambda b: (0, 0, 0)),
        ],
        out_specs=[
            pl.BlockSpec((None, t, D_ATT), lambda b: (b, 0, 0)),
            pl.BlockSpec((None, t, D_ATT), lambda b: (b, 0, 0)),
        ],
        out_shape=[
            jax.ShapeDtypeStruct((nb, t, D_ATT), BF16),
            jax.ShapeDtypeStruct((nb, t, D_ATT), F32),
        ],
        compiler_params=_cparams(("arbitrary",)),
        name="attn_sample",
    )(z3, z3, z3, k_past.reshape(nb, reach, D_ATT), v_past.reshape(nb, reach, D_ATT),
      jnp.tile(q_gain, 2).reshape(1, PAIR), jnp.tile(k_gain, 2).reshape(1, PAIR),
      u.reshape(npairs, 2, BIAS_LEN))
    return att, kn


def _tri_inverse(l_mats, c):
    n = l_mats[0].shape[0]
    eye = jnp.where(_iota((n, n), 0) == _iota((n, n), 1), 1.0, 0.0).astype(F32)
    a_s = [(eye + l).astype(BF16) for l in l_mats]
    t_s = [eye - l for l in l_mats]
    for _ in range(c.bit_length() - 2):
        r_s = [eye - _dot(a, t) for a, t in zip(a_s, t_s)]
        t_s = [t + _dot(t, r) for t, r in zip(t_s, r_s)]
    return t_s


def _rwkv_kernel(c, r_ref, k_ref, v_ref, lo_ref, sr_ref, sk_ref, sv_ref, slo_ref, s0_ref,
                 mur_ref, muk_ref, muv_ref, mulo_ref, w0_ref, a0_ref, kkg_ref, ka_ref, rk_ref,
                 lnw_ref, lnb_ref, w2_ref, a2_ref, g2_ref,
                 o_ref, sT_ref, s_ref, cr_ref, ck_ref, cv_ref, clo_ref):
    tb = pl.program_id(2)
    rows, width = r_ref.shape
    npp = width // PAIR
    nchunks = rows // c
    h0 = _iota((rows, PAIR), 1) < HEAD_DIM
    bd = _blk(_iota((PAIR, PAIR), 0), HEAD_DIM) == _blk(_iota((PAIR, PAIR), 1), HEAD_DIM)

    @pl.when(tb == 0)
    def _():
        s_ref[...] = jnp.zeros(s_ref.shape, F32)
        for pp in range(npp):
            s_ref[pp, 0:HEAD_DIM, 0:HEAD_DIM] = s0_ref[2 * pp]
            s_ref[pp, HEAD_DIM:PAIR, HEAD_DIM:PAIR] = s0_ref[2 * pp + 1]
        cr_ref[...] = sr_ref[...]
        ck_ref[...] = sk_ref[...]
        cv_ref[...] = sv_ref[...]
        clo_ref[...] = slo_ref[...]

    def shifted(x_ref, carry_ref, mu_ref):
        x = x_ref[...]
        prev = jnp.where(_iota(x.shape, 0) == 0, carry_ref[...], pltpu.roll(x, 1, 0))
        carry_ref[...] = x[rows - 1:rows]
        return x + (prev - x) * mu_ref[...]

    r = shifted(r_ref, cr_ref, mur_ref)
    k = shifted(k_ref, ck_ref, muk_ref)
    v = shifted(v_ref, cv_ref, muv_ref)
    lo = shifted(lo_ref, clo_ref, mulo_ref)

    zeros_w = jnp.zeros((RANK_W, width), F32)
    w2p = jnp.concatenate([w2_ref[...], zeros_w], axis=0)
    a2p = jnp.concatenate([zeros_w, a2_ref[...]], axis=0)
    lo_wa = lo[:, 0:RANK_W + RANK_A]
    u = w0_ref[...] + _dot(jnp.tanh(lo_wa), w2p)
    lw = -jnp.exp(-_softplus(-u) - 0.5)
    a = _sigmoid(a0_ref[...] + _dot(lo_wa, a2p))
    g = _dot(_sigmoid(lo[:, RANK_W + RANK_A:]), g2_ref[...])

    kk = k * kkg_ref[...]
    kk = kk / jnp.maximum(jnp.sqrt(_head_sums(kk * kk)), 1e-12)
    k = k * (1.0 + (a - 1.0) * ka_ref[...])
    b = kk * a
    bonus = _head_sums(r * k * rk_ref[...]) * v

    tr = _iota((rows, rows), 0)
    tc = _iota((rows, rows), 1)
    same_chunk = _blk(tr, c) == _blk(tc, c)
    strict = same_chunk & (tr > tc)
    incl = same_chunk & (tr >= tc)
    lw_hi, lw_lo = _split2(lw)
    tril_ones = jnp.where(incl, 1.0, 0.0).astype(BF16)
    lp = jnp.dot(tril_ones, lw_hi, preferred_element_type=F32) + \
        jnp.dot(tril_ones, lw_lo, preferred_element_type=F32)
    lp_end = jnp.concatenate(
        [jnp.broadcast_to(lp[(ci + 1) * c - 1:(ci + 1) * c], (c, width)) for ci in range(nchunks)], axis=0)

    alpha_w = kk * jnp.exp(lp - lw)
    inv_p = jnp.exp(-lp)
    beta_w = b * inv_p
    kappa_w = k * inv_p
    rho_w = r * jnp.exp(lp)
    to_end = jnp.exp(lp_end - lp)
    beta_ew = b * to_end
    kappa_ew = k * to_end
    decay_end_w = jnp.exp(lp_end)

    wide = (rows, nchunks * PAIR)
    col_chunk = _blk(_iota(wide, 1), PAIR) == _blk(_iota(wide, 0), c)
    spread = lambda m: jnp.where(col_chunk, jnp.tile(m, (1, nchunks)), 0.0)
    eye_p = _iota((PAIR, PAIR), 0) == _iota((PAIR, PAIR), 1)

    pairs = range(npp)
    lanes = [slice(pp * PAIR, (pp + 1) * PAIR) for pp in pairs]
    alpha = [alpha_w[:, l] for l in lanes]
    rho = [rho_w[:, l] for l in lanes]
    vv = [v[:, l] for l in lanes]
    head_mask = [h0, jnp.logical_not(h0)]
    bk = [jnp.concatenate([beta_w[:, l], kappa_w[:, l]], axis=0).astype(BF16) for l in lanes]
    prod = [[_dot(jnp.concatenate([jnp.where(hm, alpha[pp], 0.0), jnp.where(hm, rho[pp], 0.0)], axis=0),
                  bk[pp], NT) for hm in head_mask] for pp in pairs]
    t_inv = _tri_inverse([jnp.where(strict, prod[pp][h][:rows, :rows], 0.0) for pp in pairs for h in range(2)], c)
    x = [[_dot(jnp.where(strict, prod[pp][h][:rows, rows:], 0.0), vv[pp]) for h in range(2)] for pp in pairs]
    ws = [[_dot(t_inv[2 * pp + h], jnp.concatenate([alpha[pp], x[pp][h]], axis=1)) for h in range(2)]
          for pp in pairs]
    w12 = [jnp.concatenate([jnp.where(h0, ws[pp][0][:, :PAIR], ws[pp][1][:, :PAIR]),
                            jnp.where(h0, ws[pp][0][:, PAIR:], ws[pp][1][:, PAIR:])], axis=1) for pp in pairs]
    q = [[_dot(jnp.where(incl, prod[pp][h][rows:, :rows], 0.0), w12[pp]) for h in range(2)] for pp in pairs]
    qk = [[_dot(jnp.where(incl, prod[pp][h][rows:, rows:], 0.0), vv[pp]) for h in range(2)] for pp in pairs]
    rp = [rho[pp] - jnp.where(h0, q[pp][0][:, :PAIR], q[pp][1][:, :PAIR]) for pp in pairs]
    y0 = [jnp.where(h0, qk[pp][0] - q[pp][0][:, PAIR:], qk[pp][1] - q[pp][1][:, PAIR:]) for pp in pairs]
    wtb = [_dot(w12[pp], spread(beta_ew[:, lanes[pp]]), TN) for pp in pairs]
    vtk = [_dot(vv[pp], spread(kappa_ew[:, lanes[pp]]), TN) for pp in pairs]

    s_cur = [s_ref[pp] for pp in pairs]
    ys = [[] for _ in pairs]
    for ci in range(nchunks):
        sl = slice(ci * c, (ci + 1) * c)
        cols = slice(ci * PAIR, (ci + 1) * PAIR)
        for pp in pairs:
            decay_end = decay_end_w[ci * c:ci * c + 1, lanes[pp]]
            gmat = jnp.where(eye_p, jnp.broadcast_to(decay_end, (PAIR, PAIR)), 0.0) \
                - jnp.where(bd, wtb[pp][:PAIR, cols], 0.0)
            hmat = jnp.where(bd, vtk[pp][:, cols] - wtb[pp][PAIR:, cols], 0.0)
            ys[pp].append(_dot(rp[pp][sl], s_cur[pp], NT) + y0[pp][sl])
            s_cur[pp] = _dot(s_cur[pp], gmat) + hmat
    for pp in pairs:
        s_ref[pp] = s_cur[pp]
    y_pairs = [ys[pp][0] if nchunks == 1 else jnp.concatenate(ys[pp], axis=0) for pp in pairs]

    @pl.when(tb == pl.num_programs(2) - 1)
    def _():
        for pp in range(npp):
            sT_ref[2 * pp] = s_ref[pp, 0:HEAD_DIM, 0:HEAD_DIM]
            sT_ref[2 * pp + 1] = s_ref[pp, HEAD_DIM:PAIR, HEAD_DIM:PAIR]

    y = y_pairs[0] if npp == 1 else jnp.concatenate(y_pairs, axis=1)
    mu = _head_sums(y) * (1.0 / HEAD_DIM)
    d = y - mu
    var = _head_sums(d * d) * (1.0 / HEAD_DIM)
    yn = d * lax.rsqrt(var + GN_EPS) * lnw_ref[...] + lnb_ref[...]
    o_ref[...] = ((yn + bonus) * g).astype(BF16)


def _rwkv(z3, shift_prev, s0, p, rows, c, npp):
    nb, t, _ = z3.shape
    width = npp * PAIR
    ngroups = D_RWKV // width
    col0 = 3 * D_ATT // width
    lo_blk = (3 * D_ATT + 3 * D_RWKV) // D_LORA
    sp = shift_prev.reshape(nb, 1, D_SHIFT)

    def zspec(off):
        return pl.BlockSpec((None, rows, width), lambda b, q, s: (b, s, col0 + off * ngroups + q))

    def sspec(off):
        return pl.BlockSpec((None, 1, width), lambda b, q, s: (b, 0, off * ngroups + q))

    def vec(off=0):
        return pl.BlockSpec((1, width), lambda b, q, s: (0, off * ngroups + q))

    def row2(x):
        return x.reshape(1, -1)

    out, s_fin = pl.pallas_call(
        functools.partial(_rwkv_kernel, c),
        grid=(nb, ngroups, t // rows),
        in_specs=[
            zspec(0), zspec(1), zspec(2),
            pl.BlockSpec((None, rows, D_LORA), lambda b, q, s: (b, s, lo_blk)),
            sspec(0), sspec(1), sspec(2),
            pl.BlockSpec((None, 1, D_LORA), lambda b, q, s: (b, 0, 3 * D_RWKV // D_LORA)),
            pl.BlockSpec((None, 2 * npp, HEAD_DIM, HEAD_DIM), lambda b, q, s: (b, q, 0, 0)),
            vec(0), vec(1), vec(2),
            pl.BlockSpec((1, D_LORA), lambda b, q, s: (0, 3 * D_RWKV // D_LORA)),
            vec(), vec(), vec(), vec(), vec(), vec(), vec(),
            pl.BlockSpec((RANK_W, width), lambda b, q, s: (0, q)),
            pl.BlockSpec((RANK_A, width), lambda b, q, s: (0, q)),
            pl.BlockSpec((RANK_G, width), lambda b, q, s: (0, q)),
        ],
        out_specs=[
            pl.BlockSpec((None, rows, width), lambda b, q, s: (b, s, q)),
            pl.BlockSpec((None, 2 * npp, HEAD_DIM, HEAD_DIM), lambda b, q, s: (b, q, 0, 0)),
        ],
        out_shape=[
            jax.ShapeDtypeStruct((nb, t, D_RWKV), BF16),
            jax.ShapeDtypeStruct((nb, N_RWKV_HEADS, HEAD_DIM, HEAD_DIM), F32),
        ],
        scratch_shapes=[
            pltpu.VMEM((npp, PAIR, PAIR), F32),
            pltpu.VMEM((1, width), F32), pltpu.VMEM((1, width), F32), pltpu.VMEM((1, width), F32),
            pltpu.VMEM((1, D_LORA), F32),
        ],
        compiler_params=_cparams(("arbitrary", "arbitrary", "arbitrary")),
        name="rwkv7_mix",
    )(z3, z3, z3, z3, sp, sp, sp, sp, s0,
      row2(p['mu_shift']), row2(p['mu_shift']), row2(p['mu_shift']), row2(p['mu_shift']),
      row2(p['w0']), row2(p['a0']), row2(p['k_k']), row2(p['k_a']), row2(p['r_k']),
      row2(p['ln_x_w']), row2(p['ln_x_b']), p['w2'], p['a2'], p['g2'])
    return out, s_fin


def _layer(x3, mod, p, u, k_past, v_past, s0, shift_prev, conv_prev):
    nb, t, d = x3.shape
    m = nb * t
    x = x3.reshape(m, d)
    z = _norm_proj(x, p['norm_att_g'], mod, 1, 0, p['w_in'], IN_COLS, "in_proj",
                   min(IN_ROW_TILE, t) if k_past is None else m)
    z3 = z.reshape(nb, t, D_IN)
    if k_past is None:
        att, k_keep = _attn_prompt(z3, p['q_norm_g'], p['k_norm_g'], u)
        keep = min(ATT_REACH, t)
        v_keep = z3[:, t - keep:, 2 * D_ATT:3 * D_ATT]
        rw, s_fin = _rwkv(z3, shift_prev, s0, p, RWKV_ROWS, CHUNK, RWKV_PAIRS_PROMPT)
    else:
        att, k_keep = _attn_sample(z3, k_past, v_past, p['q_norm_g'], p['k_norm_g'], u)
        v_keep = z3[:, :, 2 * D_ATT:3 * D_ATT]
        rw, s_fin = _rwkv(z3, shift_prev, s0, p, t, t, RWKV_PAIRS_SAMPLE)
    shift_last = z3[:, t - 1, 3 * D_ATT:]
    x1 = _proj_resid([att.reshape(m, D_ATT), rw.reshape(m, D_RWKV)], p['w_out'], x, mod, 2,
                     OUT_COLS, "out_proj")
    if k_past is None:
        act, conv_last = _ffn_up_fused(x1, p['norm_ffn_g'], mod, p['w_up'], conv_prev,
                                       p['dw_conv'], p['dw_bias'], t)
    else:
        hu = _norm_proj(x1, p['norm_ffn_g'], mod, 4, 3, p['w_up'], UP_COLS, "ffn_up_sample")
        f = hu.shape[1] // 2
        act = _act_sample(hu, conv_prev, p['dw_conv'], p['dw_bias'], nb, t)
        conv_last = hu.reshape(nb, t, 2 * f)[:, t - (CONV_W - 1):, :f]
    x2 = _proj_resid([act], p['w_down'], x1, mod, 5, DOWN_COLS, "ffn_down")
    heads = lambda a: a.reshape(nb, a.shape[1], N_ATT_HEADS, HEAD_DIM)
    return x2.reshape(nb, t, d), heads(k_keep), heads(v_keep), s_fin, shift_last, conv_last


def kernel(x_prompt, x_sample, c_prompt, c_sample, cache_att_k, cache_att_v, state_rwkv, state_shift, state_ffn_conv, norm_att_g, norm_ffn_g, w_ada, b_ada, w_in, q_norm_g, k_norm_g, rel_bias, mu_shift, w0, w2, a0, a2, g2, k_k, k_a, r_k, ln_x_w, ln_x_b, w_out, w_up, dw_conv, dw_bias, w_down):
    depth = w_in.shape[0]
    bp, tp, d = x_prompt.shape
    bs, ts, _ = x_sample.shape
    d_ff = w_down.shape[1]
    hp, hs = x_prompt, x_sample
    outs_p = [[] for _ in range(5)]
    outs_s = [[] for _ in range(5)]
    for l in range(depth):
        p = dict(norm_att_g=norm_att_g[l], norm_ffn_g=norm_ffn_g[l], w_in=w_in[l], q_norm_g=q_norm_g[l],
                 k_norm_g=k_norm_g[l], mu_shift=mu_shift[l], w0=w0[l], w2=w2[l], a0=a0[l], a2=a2[l],
                 g2=g2[l], k_k=k_k[l], k_a=k_a[l], r_k=r_k[l], ln_x_w=ln_x_w[l], ln_x_b=ln_x_b[l],
                 w_out=w_out[l], w_up=w_up[l], dw_conv=dw_conv[l], dw_bias=dw_bias[l], w_down=w_down[l])
        n_c = bp + bs
        pad = (-n_c) % 8
        c_all = jnp.concatenate([c_prompt, c_sample, jnp.zeros((pad, d), F32)], axis=0)
        mod = _ada(c_all, w_ada[l], b_ada[l])
        mod_p = _Mod(mod.reshape(n_c + pad, 6, 1, d), False, rows_per_batch=tp)
        mod_s = _Mod(jnp.repeat(mod[bp:bp + bs], ts, axis=0), True)
        u = _bias_rows(rel_bias[l])

        res = _layer(hp, mod_p, p, u, None, None,
                     jnp.zeros((bp, N_RWKV_HEADS, HEAD_DIM, HEAD_DIM), F32),
                     jnp.zeros((bp, D_SHIFT), F32),
                     jnp.zeros((bp, CONV_W - 1, d_ff), F32))
        hp = res[0]
        for lst, val in zip(outs_p, res[1:]):
            lst.append(val)
        res = _layer(hs, mod_s, p, u, cache_att_k[l], cache_att_v[l], state_rwkv[l],
                     state_shift[l], state_ffn_conv[l])
        hs = res[0]
        for lst, val in zip(outs_s, res[1:]):
            lst.append(val)
    st = lambda lst: jnp.stack(lst)
    return (hp, hs, *[st(x) for x in outs_p], *[st(x) for x in outs_s])
```

```python
import functools

import jax
import jax.numpy as jnp
from jax import lax
from jax.experimental import pallas as pl
from jax.experimental.pallas import tpu as pltpu

F32 = jnp.float32
BF16 = jnp.bfloat16

CHUNK = 64
N_PREV_CHUNKS = 8
ATT_REACH = N_PREV_CHUNKS * CHUNK
HEAD_DIM = 64
N_ATT_HEADS = 16
N_RWKV_HEADS = 16
D_ATT = N_ATT_HEADS * HEAD_DIM
D_RWKV = N_RWKV_HEADS * HEAD_DIM
REL_CLIP = 128
RANK_W = 64
RANK_A = 64
RANK_G = 128
D_LORA = RANK_W + RANK_A + RANK_G
D_SHIFT = 3 * D_RWKV + D_LORA
D_IN = 3 * D_ATT + D_SHIFT
CONV_W = 3
RMS_EPS = 1e-6
GN_EPS = 64e-5
ATT_SCALE = HEAD_DIM ** -0.5
LOG2E = 1.4426950408889634

LANES = 128
PAIR = 2 * HEAD_DIM
MXU_DIM = 256
VMEM_LIMIT = 60 * 1024 * 1024

ROW_TILE = 1024
IN_ROW_TILE = 2048
ADA_COLS = 512
IN_COLS = 256
IN_COLS_SAMPLE = 640
OUT_COLS = 512
UP_COLS = 512
DOWN_COLS = 256
ATT_QROWS = 256
ATT_WIN = ATT_QROWS + ATT_REACH
ATT_PAIRS = 2
BIAS_LEN = 1024
RWKV_ROWS = 256
RWKV_PAIRS_PROMPT = 8
RWKV_PAIRS_SAMPLE = 8


def _cparams(sem):
    return pltpu.CompilerParams(dimension_semantics=sem, vmem_limit_bytes=VMEM_LIMIT)


def _dot(a, b, dims=(((1,), (0,)), ((), ()))):
    return lax.dot_general(a.astype(BF16), b.astype(BF16), dims, preferred_element_type=F32)


def _split2(x):
    hi = x.astype(BF16)
    lo = (x - hi.astype(F32)).astype(BF16)
    return hi, lo


NT = (((1,), (1,)), ((), ()))
TN = (((0,), (0,)), ((), ()))


def _iota(shape, dim):
    return lax.broadcasted_iota(jnp.int32, shape, dim)


def _blk(x, size):
    return jnp.right_shift(x, size.bit_length() - 1)


def _head_ones(n):
    r = _blk(_iota((n, n), 0), HEAD_DIM)
    c = _blk(_iota((n, n), 1), HEAD_DIM)
    return jnp.where(r == c, 1.0, 0.0).astype(BF16)


def _head_sums(x):
    lanes = x.shape[1]
    group = min(lanes, MXU_DIM)
    ones = _head_ones(group)
    parts = [_dot(x[:, i:i + group], ones) for i in range(0, lanes, group)]
    return parts[0] if len(parts) == 1 else jnp.concatenate(parts, axis=1)


def _sigmoid(x):
    return 1.0 / (1.0 + jnp.exp(-x))


def _softplus(x):
    return jnp.maximum(x, 0.0) + jnp.log(1.0 + jnp.exp(-jnp.abs(x)))


def _ada_kernel(c_ref, w_ref, b_ref, o_ref):
    c = c_ref[...]
    s = c * _sigmoid(c)
    o_ref[...] = _dot(s, w_ref[...]) + b_ref[...]


def _ada(c_all, w_ada, b_ada):
    rows, d = c_all.shape
    n = w_ada.shape[1]
    return pl.pallas_call(
        _ada_kernel,
        grid=(n // ADA_COLS,),
        in_specs=[
            pl.BlockSpec((rows, d), lambda j: (0, 0)),
            pl.BlockSpec((d, ADA_COLS), lambda j: (0, j)),
            pl.BlockSpec((1, ADA_COLS), lambda j: (0, j)),
        ],
        out_specs=pl.BlockSpec((rows, ADA_COLS), lambda j: (0, j)),
        out_shape=jax.ShapeDtypeStruct((rows, n), F32),
        compiler_params=_cparams(("arbitrary",)),
        name="ada_mod",
    )(c_all, w_ada, b_ada.reshape(1, n))


class _Mod:
    def __init__(self, arr, per_row, rows_per_batch=None):
        self.arr = arr
        self.per_row = per_row
        self.rows_per_batch = rows_per_batch

    def spec(self, idx, cols, col_of, row_tile):
        if self.per_row:
            m = self.arr.shape[0]
            d = self.arr.shape[1] // 6
            nblk = d // cols
            return pl.BlockSpec((m, cols), lambda i, j: (0, idx * nblk + col_of(j)))
        tiles_per_batch = self.rows_per_batch // row_tile
        return pl.BlockSpec((None, None, 1, cols),
                            lambda i, j: (i // tiles_per_batch, idx, 0, col_of(j)))


NORM_ROWS = 128


def _store_normed(h_ref, x_ref, g_ref, sc_ref, sh_ref):
    rows = x_ref.shape[0]
    step = min(NORM_ROWS, rows)
    per_row = sc_ref.shape[0] == rows

    def body(r, carry):
        sl = pl.ds(pl.multiple_of(r * step, step), step)
        x = x_ref[sl, :]
        ms = jnp.mean(x * x, axis=-1, keepdims=True)
        xn = x * lax.rsqrt(ms + RMS_EPS) * g_ref[...]
        sc = sc_ref[sl, :] if per_row else sc_ref[...]
        sh = sh_ref[sl, :] if per_row else sh_ref[...]
        h_ref[sl, :] = (xn * (1.0 + sc) + sh).astype(BF16)
        return carry

    lax.fori_loop(0, rows // step, body, 0)


def _norm_proj_kernel(x_ref, g_ref, sc_ref, sh_ref, w_ref, o_ref, h_ref):
    @pl.when(pl.program_id(1) == 0)
    def _():
        _store_normed(h_ref, x_ref, g_ref, sc_ref, sh_ref)

    o_ref[...] = jnp.dot(h_ref[...], w_ref[...].astype(BF16), preferred_element_type=F32)


def _norm_proj(x, gain, mod, sc_idx, sh_idx, w, cols, name, row_tile=ROW_TILE):
    m, d = x.shape
    n = w.shape[1]
    tm = min(row_tile, m)
    whole = lambda j: 0
    x_mode = dict(pipeline_mode=pl.Buffered(1)) if tm > ROW_TILE else {}
    return pl.pallas_call(
        _norm_proj_kernel,
        grid=(m // tm, n // cols),
        in_specs=[
            pl.BlockSpec((tm, d), lambda i, j: (i, 0), **x_mode),
            pl.BlockSpec((1, d), lambda i, j: (0, 0)),
            mod.spec(sc_idx, d, whole, tm),
            mod.spec(sh_idx, d, whole, tm),
            pl.BlockSpec((d, cols), lambda i, j: (0, j)),
        ],
        out_specs=pl.BlockSpec((tm, cols), lambda i, j: (i, j)),
        out_shape=jax.ShapeDtypeStruct((m, n), F32),
        scratch_shapes=[pltpu.VMEM((tm, d), BF16)],
        compiler_params=_cparams(("arbitrary", "arbitrary")),
        name=name,
    )(x, gain.reshape(1, d), mod.arr, mod.arr, w)


def _proj_resid_kernel(n_pairs, *refs):
    a_refs = refs[:n_pairs]
    w_refs = refs[n_pairs:2 * n_pairs]
    x_ref, g_ref, o_ref = refs[2 * n_pairs:]
    acc = jnp.dot(a_refs[0][...], w_refs[0][...].astype(BF16), preferred_element_type=F32)
    for a_ref, w_ref in zip(a_refs[1:], w_refs[1:]):
        acc = acc + jnp.dot(a_ref[...], w_ref[...].astype(BF16), preferred_element_type=F32)
    o_ref[...] = x_ref[...] + g_ref[...] * acc


def _proj_resid(a_list, w, x, mod, g_idx, cols, name):
    m, n = x.shape
    tm = min(ROW_TILE, m)
    in_specs, w_args = [], []
    row = 0
    for a in a_list:
        kdim = a.shape[1]
        in_specs.append(pl.BlockSpec((tm, kdim), lambda i, j: (i, 0)))
    for a in a_list:
        kdim = a.shape[1]
        in_specs.append(pl.BlockSpec((kdim, cols), lambda i, j, r=row // kdim: (r, j)))
        w_args.append(w)
        row += kdim
    in_specs.append(pl.BlockSpec((tm, cols), lambda i, j: (i, j)))
    in_specs.append(mod.spec(g_idx, cols, lambda j: j, tm))
    return pl.pallas_call(
        functools.partial(_proj_resid_kernel, len(a_list)),
        grid=(m // tm, n // cols),
        in_specs=in_specs,
        out_specs=pl.BlockSpec((tm, cols), lambda i, j: (i, j)),
        out_shape=jax.ShapeDtypeStruct((m, n), F32),
        compiler_params=_cparams(("arbitrary", "arbitrary")),
        name=name,
    )(*a_list, *w_args, x, mod.arr)


def _gelu(x):
    return 0.5 * x * (1.0 + lax.erf(x * (2.0 ** -0.5)))


def _ffn_up_kernel(tiles_per_batch, x_ref, g_ref, sc_ref, sh_ref, wg_ref, wv_ref, hist_ref,
                   cw_ref, cb_ref, act_ref, last_ref, h_ref, carry_ref):
    i = pl.program_id(0)
    j = pl.program_id(1)

    @pl.when(j == 0)
    def _():
        _store_normed(h_ref, x_ref, g_ref, sc_ref, sh_ref)

    @pl.when((i % tiles_per_batch) == 0)
    def _():
        carry_ref[j] = hist_ref[...]

    h = h_ref[...]
    gate = jnp.dot(h, wg_ref[...].astype(BF16), preferred_element_type=F32)
    val = jnp.dot(h, wv_ref[...].astype(BF16), preferred_element_type=F32)
    tm = gate.shape[0]
    prev = carry_ref[j]
    row = _iota(gate.shape, 0)
    g1 = pltpu.roll(gate, 1, 0)
    g2 = pltpu.roll(gate, 2, 0)
    g1 = jnp.where(row == 0, prev[1:2], g1)
    g2 = jnp.where(row == 0, prev[0:1], jnp.where(row == 1, prev[1:2], g2))
    cw = cw_ref[...]
    conv = cb_ref[...] + g2 * cw[0:1] + g1 * cw[1:2] + gate * cw[2:3]
    act_ref[...] = (_gelu(conv) * val).astype(BF16)
    tail = gate[tm - 2:tm]
    carry_ref[j] = tail
    last_ref[...] = tail


def _ffn_up_fused(x, gain, mod, w_up, hist, conv_w, conv_b, rows_per_batch):
    m, d = x.shape
    f = w_up.shape[1] // 2
    tm = min(ROW_TILE, rows_per_batch)
    cols = UP_COLS
    nj = f // cols
    tiles_per_batch = rows_per_batch // tm
    whole = lambda j: 0
    act, tile_tails = pl.pallas_call(
        functools.partial(_ffn_up_kernel, tiles_per_batch),
        grid=(m // tm, nj),
        in_specs=[
            pl.BlockSpec((tm, d), lambda i, j: (i, 0)),
            pl.BlockSpec((1, d), lambda i, j: (0, 0)),
            mod.spec(4, d, whole, tm),
            mod.spec(3, d, whole, tm),
            pl.BlockSpec((d, cols), lambda i, j: (0, j)),
            pl.BlockSpec((d, cols), lambda i, j: (0, nj + j)),
            pl.BlockSpec((None, CONV_W - 1, cols), lambda i, j: (i // tiles_per_batch, 0, j)),
            pl.BlockSpec((CONV_W, cols), lambda i, j: (0, j)),
            pl.BlockSpec((1, cols), lambda i, j: (0, j)),
        ],
        out_specs=[
            pl.BlockSpec((tm, cols), lambda i, j: (i, j)),
            pl.BlockSpec((None, CONV_W - 1, cols), lambda i, j: (i, 0, j)),
        ],
        out_shape=[
            jax.ShapeDtypeStruct((m, f), BF16),
            jax.ShapeDtypeStruct((m // tm, CONV_W - 1, f), F32),
        ],
        scratch_shapes=[pltpu.VMEM((tm, d), BF16), pltpu.VMEM((nj, CONV_W - 1, cols), F32)],
        compiler_params=_cparams(("arbitrary", "arbitrary")),
        name="ffn_up_prompt",
    )(x, gain.reshape(1, d), mod.arr, mod.arr, w_up, w_up, hist, conv_w, conv_b.reshape(1, f))
    return act, tile_tails[tiles_per_batch - 1::tiles_per_batch]


def _act_sample_kernel(gate_ref, val_ref, hist_ref, cw_ref, cb_ref, act_ref):
    gate = gate_ref[...]
    hist = hist_ref[...]
    t = _iota(gate.shape, 1)
    g1 = jnp.where(t == 0, hist[:, 1:2], pltpu.roll(gate, 1, 1))
    g2 = jnp.where(t == 0, hist[:, 0:1], jnp.where(t == 1, hist[:, 1:2], pltpu.roll(gate, 2, 1)))
    cw = cw_ref[...]
    conv = cb_ref[...] + g2 * cw[0:1] + g1 * cw[1:2] + gate * cw[2:3]
    act_ref[...] = (_gelu(conv) * val_ref[...]).astype(BF16)


def _act_sample(hu, hist, conv_w, conv_b, nb, t):
    f = hu.shape[1] // 2
    cols = UP_COLS
    nj = f // cols
    hu3 = hu.reshape(nb, t, 2 * f)
    act = pl.pallas_call(
        _act_sample_kernel,
        grid=(nj,),
        in_specs=[
            pl.BlockSpec((nb, t, cols), lambda j: (0, 0, j)),
            pl.BlockSpec((nb, t, cols), lambda j: (0, 0, nj + j)),
            pl.BlockSpec((nb, CONV_W - 1, cols), lambda j: (0, 0, j)),
            pl.BlockSpec((CONV_W, cols), lambda j: (0, j)),
            pl.BlockSpec((1, cols), lambda j: (0, j)),
        ],
        out_specs=pl.BlockSpec((nb, t, cols), lambda j: (0, 0, j)),
        out_shape=jax.ShapeDtypeStruct((nb, t, f), BF16),
        compiler_params=_cparams(("arbitrary",)),
        name="ffn_act_sample",
    )(hu3, hu3, hist, conv_w, conv_b.reshape(1, f))
    return act.reshape(nb * t, f)


def _pair_rms(x, gain):
    x2 = x * x
    first = _iota(x.shape, 1) < HEAD_DIM
    s0 = jnp.sum(jnp.where(first, x2, 0.0), axis=-1, keepdims=True)
    s1 = jnp.sum(jnp.where(first, 0.0, x2), axis=-1, keepdims=True)
    ms = jnp.where(first, s0, s1) * (1.0 / HEAD_DIM)
    return x * lax.rsqrt(ms + RMS_EPS) * gain


def _bias_rows(table):
    h = table.shape[0]
    far = jnp.broadcast_to(table[:, 2 * REL_CLIP:], (h, ATT_REACH - REL_CLIP))
    mid = table[:, ::-1]
    near_len = BIAS_LEN - ATT_QROWS - (ATT_REACH - REL_CLIP) - (2 * REL_CLIP + 1)
    near = jnp.broadcast_to(table[:, 0:1], (h, near_len))
    wrap = jnp.broadcast_to(table[:, 2 * REL_CLIP:], (h, ATT_QROWS))
    return jnp.concatenate([far, mid, near, wrap], axis=1)


def _toeplitz(u_row, rows):
    return pltpu.roll(jnp.broadcast_to(u_row, (rows, BIAS_LEN)), 0, 1, stride=1, stride_axis=0)


def _attn_prompt_kernel(q_ref, k_ref, v_ref, qg_ref, kg_ref, u_ref, o_ref, kn_ref,
                        bias_ref, kwin_ref, vwin_ref):
    b = pl.program_id(1)
    qb = pl.program_id(2)
    shape = (ATT_QROWS, ATT_WIN)
    pairs = range(q_ref.shape[1] // PAIR)
    cols = [slice(p * PAIR, (p + 1) * PAIR) for p in pairs]
    chains = [(p, h) for p in pairs for h in range(2)]

    @pl.when((b == 0) & (qb == 0))
    def _():
        r = _iota(shape, 0)
        w = _iota(shape, 1)
        chunk_lo = _blk(r, CHUNK) * CHUNK
        in_band = (w >= chunk_lo) & (w < chunk_lo + (ATT_REACH + CHUNK))
        for i, (p, h) in enumerate(chains):
            bias = _toeplitz(u_ref[p, h:h + 1, :], ATT_QROWS)[:, :ATT_WIN]
            bias_ref[i] = jnp.where(in_band, bias * LOG2E, -jnp.inf)

    @pl.when(qb == 0)
    def _():
        kwin_ref[0:ATT_REACH] = jnp.zeros((ATT_REACH, kwin_ref.shape[1]), BF16)
        vwin_ref[0:ATT_REACH] = jnp.zeros((ATT_REACH, vwin_ref.shape[1]), BF16)

    @pl.when(qb > 0)
    def _():
        kwin_ref[0:ATT_REACH] = kwin_ref[ATT_QROWS:ATT_WIN]
        vwin_ref[0:ATT_REACH] = vwin_ref[ATT_QROWS:ATT_WIN]

    kn = [_pair_rms(k_ref[:, c], kg_ref[...]) for c in cols]
    for p in pairs:
        kn_ref[:, cols[p]] = kn[p]
        kwin_ref[ATT_REACH:ATT_WIN, cols[p]] = kn[p].astype(BF16)
    vwin_ref[ATT_REACH:ATT_WIN] = v_ref[...].astype(BF16)

    def attend(mask_start):
        qn = [_pair_rms(q_ref[:, c], qg_ref[...]) * (ATT_SCALE * LOG2E) for c in cols]
        kb = [kwin_ref[:, c] for c in cols]
        vb = [vwin_ref[:, c] for c in cols]
        first = _iota((ATT_QROWS, PAIR), 1) < HEAD_DIM
        first_w = _iota((ATT_WIN, PAIR), 1) < HEAD_DIM
        qh = [jnp.where(first, qn[p], 0.0) if h == 0 else jnp.where(first, 0.0, qn[p]) for p, h in chains]
        s = [_dot(qh[i], kb[p], NT) + bias_ref[i] for i, (p, h) in enumerate(chains)]
        if mask_start:
            started = _iota(shape, 1) >= ATT_REACH - qb * ATT_QROWS
            s = [jnp.where(started, x, -jnp.inf) for x in s]
        m = [jnp.max(x, axis=-1, keepdims=True) for x in s]
        pr = [jnp.exp2(x - mm) for x, mm in zip(s, m)]
        one = jnp.ones((), BF16)
        v_aug = [jnp.where(first_w, vb[p], one) if h == 0 else jnp.where(first_w, one, vb[p]) for p, h in chains]
        o = [_dot(pr[i], v_aug[i]) for i in range(len(chains))]
        o = [x / pltpu.roll(x, HEAD_DIM, 1) for x in o]
        for p in pairs:
            o_ref[:, cols[p]] = jnp.where(first, o[2 * p], o[2 * p + 1]).astype(BF16)

    full_window_from = ATT_REACH // ATT_QROWS
    pl.when(qb < full_window_from)(lambda: attend(True))
    pl.when(qb >= full_window_from)(lambda: attend(False))


def _attn_prompt(z3, q_gain, k_gain, u):
    nb, t, _ = z3.shape
    npairs = N_ATT_HEADS // 2
    npp = ATT_PAIRS
    width = npp * PAIR
    ngroups = npairs // npp
    nq = t // ATT_QROWS
    kcol = D_ATT // width
    vcol = 2 * D_ATT // width
    keep_blocks = ATT_REACH // ATT_QROWS
    blk = (None, ATT_QROWS, width)

    att, kn = pl.pallas_call(
        _attn_prompt_kernel,
        grid=(ngroups, nb, nq),
        in_specs=[
            pl.BlockSpec(blk, lambda g, b, q: (b, q, g)),
            pl.BlockSpec(blk, lambda g, b, q: (b, q, kcol + g)),
            pl.BlockSpec(blk, lambda g, b, q: (b, q, vcol + g)),
            pl.BlockSpec((1, PAIR), lambda g, b, q: (0, 0)),
            pl.BlockSpec((1, PAIR), lambda g, b, q: (0, 0)),
            pl.BlockSpec((npp, 2, BIAS_LEN), lambda g, b, q: (g, 0, 0)),
        ],
        out_specs=[
            pl.BlockSpec(blk, lambda g, b, q: (b, q, g)),
            pl.BlockSpec(blk, lambda g, b, q: (b, jnp.maximum(q - (nq - keep_blocks), 0), g)),
        ],
        out_shape=[
            jax.ShapeDtypeStruct((nb, t, D_ATT), BF16),
            jax.ShapeDtypeStruct((nb, ATT_REACH, D_ATT), F32),
        ],
        scratch_shapes=[pltpu.VMEM((2 * npp, ATT_QROWS, ATT_WIN), F32),
                        pltpu.VMEM((ATT_WIN, width), BF16), pltpu.VMEM((ATT_WIN, width), BF16)],
        compiler_params=_cparams(("arbitrary", "arbitrary", "arbitrary")),
        name="attn_prompt",
    )(z3, z3, z3, jnp.tile(q_gain, 2).reshape(1, PAIR),
      jnp.tile(k_gain, 2).reshape(1, PAIR), u.reshape(npairs, 2, BIAS_LEN))
    return att, kn


def _attn_sample_kernel(q_ref, k_ref, v_ref, kp_ref, vp_ref, qg_ref, kg_ref, u_ref, o_ref, kn_ref):
    t = q_ref.shape[0]
    reach = kp_ref.shape[0]
    first = _iota((t, PAIR), 1) < HEAD_DIM
    pairs = range(N_ATT_HEADS // 2)
    cols = [slice(p * PAIR, (p + 1) * PAIR) for p in pairs]
    chains = [(p, h) for p in pairs for h in range(2)]
    qn = [_pair_rms(q_ref[:, c], qg_ref[...]) * ATT_SCALE for c in cols]
    kn = [_pair_rms(k_ref[:, c], kg_ref[...]) for c in cols]
    for p in pairs:
        kn_ref[:, cols[p]] = kn[p]
    kpast = [kp_ref[:, c].astype(BF16) for c in cols]
    vpast = [vp_ref[:, c].astype(BF16) for c in cols]
    vnew = [v_ref[:, c].astype(BF16) for c in cols]
    qh = [jnp.where(first, qn[p], 0.0) if h == 0 else jnp.where(first, 0.0, qn[p]) for p, h in chains]
    bias = [_toeplitz(u_ref[p, h:h + 1, :], t) for p, h in chains]
    s_past = [_dot(qh[i], kpast[p], NT) + bias[i][:, :reach] for i, (p, h) in enumerate(chains)]
    s_new = [_dot(qh[i], kn[p], NT) + bias[i][:, reach:reach + t] for i, (p, h) in enumerate(chains)]
    m = [jnp.maximum(jnp.max(a, axis=-1, keepdims=True), jnp.max(b, axis=-1, keepdims=True))
         for a, b in zip(s_past, s_new)]
    p_past = [jnp.exp(a - mm) for a, mm in zip(s_past, m)]
    p_new = [jnp.exp(b - mm) for b, mm in zip(s_new, m)]
    l = [jnp.sum(a, axis=-1, keepdims=True) + jnp.sum(b, axis=-1, keepdims=True)
         for a, b in zip(p_past, p_new)]
    o = [(_dot(p_past[i], vpast[p]) + _dot(p_new[i], vnew[p])) / l[i] for i, (p, h) in enumerate(chains)]
    for p in pairs:
        o_ref[:, cols[p]] = jnp.where(first, o[2 * p], o[2 * p + 1]).astype(BF16)


def _attn_sample(z3, k_past, v_past, q_gain, k_gain, u):
    nb, t, _ = z3.shape
    reach = k_past.shape[1]
    npairs = N_ATT_HEADS // 2
    att, kn = pl.pallas_call(
        _attn_sample_kernel,
        grid=(nb,),
        in_specs=[
            pl.BlockSpec((None, t, D_ATT), lambda b: (b, 0, 0)),
            pl.BlockSpec((None, t, D_ATT), lambda b: (b, 0, 1)),
            pl.BlockSpec((None, t, D_ATT), lambda b: (b, 0, 2)),
            pl.BlockSpec((None, reach, D_ATT), lambda b: (b, 0, 0)),
            pl.BlockSpec((None, reach, D_ATT), lambda b: (b, 0, 0)),
            pl.BlockSpec((1, PAIR), lambda b: (0, 0)),
            pl.BlockSpec((1, PAIR), lambda b: (0, 0)),
            pl.BlockSpec((npairs, 2, BIAS_LEN), lambda b: (0, 0, 0)),
        ],
        out_specs=[
            pl.BlockSpec((None, t, D_ATT), lambda b: (b, 0, 0)),
            pl.BlockSpec((None, t, D_ATT), lambda b: (b, 0, 0)),
        ],
        out_shape=[
            jax.ShapeDtypeStruct((nb, t, D_ATT), BF16),
            jax.ShapeDtypeStruct((nb, t, D_ATT), F32),
        ],
        compiler_params=_cparams(("arbitrary",)),
        name="attn_sample",
    )(z3, z3, z3, k_past.reshape(nb, reach, D_ATT), v_past.reshape(nb, reach, D_ATT),
      jnp.tile(q_gain, 2).reshape(1, PAIR), jnp.tile(k_gain, 2).reshape(1, PAIR),
      u.reshape(npairs, 2, BIAS_LEN))
    return att, kn


def _tri_inverse(l_mats, c):
    n = l_mats[0].shape[0]
    eye = jnp.where(_iota((n, n), 0) == _iota((n, n), 1), 1.0, 0.0).astype(F32)
    a_s = [(eye + l).astype(BF16) for l in l_mats]
    t_s = [eye - l for l in l_mats]
    for _ in range(c.bit_length() - 2):
        r_s = [eye - _dot(a, t) for a, t in zip(a_s, t_s)]
        t_s = [t + _dot(t, r) for t, r in zip(t_s, r_s)]
    return t_s


def _rwkv_kernel(c, r_ref, k_ref, v_ref, lo_ref, sr_ref, sk_ref, sv_ref, slo_ref, s0_ref,
                 mur_ref, muk_ref, muv_ref, mulo_ref, w0_ref, a0_ref, kkg_ref, ka_ref, rk_ref,
                 lnw_ref, lnb_ref, w2_ref, a2_ref, g2_ref,
                 o_ref, sT_ref, s_ref, cr_ref, ck_ref, cv_ref, clo_ref):
    tb = pl.program_id(2)
    rows, width = r_ref.shape
    npp = width // PAIR
    nchunks = rows // c
    h0 = _iota((rows, PAIR), 1) < HEAD_DIM
    bd = _blk(_iota((PAIR, PAIR), 0), HEAD_DIM) == _blk(_iota((PAIR, PAIR), 1), HEAD_DIM)

    @pl.when(tb == 0)
    def _():
        s_ref[...] = jnp.zeros(s_ref.shape, F32)
        for pp in range(npp):
            s_ref[pp, 0:HEAD_DIM, 0:HEAD_DIM] = s0_ref[2 * pp]
            s_ref[pp, HEAD_DIM:PAIR, HEAD_DIM:PAIR] = s0_ref[2 * pp + 1]
        cr_ref[...] = sr_ref[...]
        ck_ref[...] = sk_ref[...]
        cv_ref[...] = sv_ref[...]
        clo_ref[...] = slo_ref[...]

    def shifted(x_ref, carry_ref, mu_ref):
        x = x_ref[...]
        prev = jnp.where(_iota(x.shape, 0) == 0, carry_ref[...], pltpu.roll(x, 1, 0))
        carry_ref[...] = x[rows - 1:rows]
        return x + (prev - x) * mu_ref[...]

    r = shifted(r_ref, cr_ref, mur_ref)
    k = shifted(k_ref, ck_ref, muk_ref)
    v = shifted(v_ref, cv_ref, muv_ref)
    lo = shifted(lo_ref, clo_ref, mulo_ref)

    zeros_w = jnp.zeros((RANK_W, width), F32)
    w2p = jnp.concatenate([w2_ref[...], zeros_w], axis=0)
    a2p = jnp.concatenate([zeros_w, a2_ref[...]], axis=0)
    lo_wa = lo[:, 0:RANK_W + RANK_A]
    u = w0_ref[...] + _dot(jnp.tanh(lo_wa), w2p)
    lw = -jnp.exp(-_softplus(-u) - 0.5)
    a = _sigmoid(a0_ref[...] + _dot(lo_wa, a2p))
    g = _dot(_sigmoid(lo[:, RANK_W + RANK_A:]), g2_ref[...])

    kk = k * kkg_ref[...]
    kk = kk / jnp.maximum(jnp.sqrt(_head_sums(kk * kk)), 1e-12)
    k = k * (1.0 + (a - 1.0) * ka_ref[...])
    b = kk * a
    bonus = _head_sums(r * k * rk_ref[...]) * v

    tr = _iota((rows, rows), 0)
    tc = _iota((rows, rows), 1)
    same_chunk = _blk(tr, c) == _blk(tc, c)
    strict = same_chunk & (tr > tc)
    incl = same_chunk & (tr >= tc)
    lw_hi, lw_lo = _split2(lw)
    tril_ones = jnp.where(incl, 1.0, 0.0).astype(BF16)
    lp = jnp.dot(tril_ones, lw_hi, preferred_element_type=F32) + \
        jnp.dot(tril_ones, lw_lo, preferred_element_type=F32)
    lp_end = jnp.concatenate(
        [jnp.broadcast_to(lp[(ci + 1) * c - 1:(ci + 1) * c], (c, width)) for ci in range(nchunks)], axis=0)

    alpha_w = kk * jnp.exp(lp - lw)
    inv_p = jnp.exp(-lp)
    beta_w = b * inv_p
    kappa_w = k * inv_p
    rho_w = r * jnp.exp(lp)
    to_end = jnp.exp(lp_end - lp)
    beta_ew = b * to_end
    kappa_ew = k * to_end
    decay_end_w = jnp.exp(lp_end)

    wide = (rows, nchunks * PAIR)
    col_chunk = _blk(_iota(wide, 1), PAIR) == _blk(_iota(wide, 0), c)
    spread = lambda m: jnp.where(col_chunk, jnp.tile(m, (1, nchunks)), 0.0)
    eye_p = _iota((PAIR, PAIR), 0) == _iota((PAIR, PAIR), 1)

    pairs = range(npp)
    lanes = [slice(pp * PAIR, (pp + 1) * PAIR) for pp in pairs]
    alpha = [alpha_w[:, l] for l in lanes]
    rho = [rho_w[:, l] for l in lanes]
    vv = [v[:, l] for l in lanes]
    head_mask = [h0, jnp.logical_not(h0)]
    bk = [jnp.concatenate([beta_w[:, l], kappa_w[:, l]], axis=0).astype(BF16) for l in lanes]
    prod = [[_dot(jnp.concatenate([jnp.where(hm, alpha[pp], 0.0), jnp.where(hm, rho[pp], 0.0)], axis=0),
                  bk[pp], NT) for hm in head_mask] for pp in pairs]
    t_inv = _tri_inverse([jnp.where(strict, prod[pp][h][:rows, :rows], 0.0) for pp in pairs for h in range(2)], c)
    x = [[_dot(jnp.where(strict, prod[pp][h][:rows, rows:], 0.0), vv[pp]) for h in range(2)] for pp in pairs]
    ws = [[_dot(t_inv[2 * pp + h], jnp.concatenate([alpha[pp], x[pp][h]], axis=1)) for h in range(2)]
          for pp in pairs]
    w12 = [jnp.concatenate([jnp.where(h0, ws[pp][0][:, :PAIR], ws[pp][1][:, :PAIR]),
                            jnp.where(h0, ws[pp][0][:, PAIR:], ws[pp][1][:, PAIR:])], axis=1) for pp in pairs]
    q = [[_dot(jnp.where(incl, prod[pp][h][rows:, :rows], 0.0), w12[pp]) for h in range(2)] for pp in pairs]
    qk = [[_dot(jnp.where(incl, prod[pp][h][rows:, rows:], 0.0), vv[pp]) for h in range(2)] for pp in pairs]
    rp = [rho[pp] - jnp.where(h0, q[pp][0][:, :PAIR], q[pp][1][:, :PAIR]) for pp in pairs]
    y0 = [jnp.where(h0, qk[pp][0] - q[pp][0][:, PAIR:], qk[pp][1] - q[pp][1][:, PAIR:]) for pp in pairs]
    wtb = [_dot(w12[pp], spread(beta_ew[:, lanes[pp]]), TN) for pp in pairs]
    vtk = [_dot(vv[pp], spread(kappa_ew[:, lanes[pp]]), TN) for pp in pairs]

    s_cur = [s_ref[pp] for pp in pairs]
    ys = [[] for _ in pairs]
    for ci in range(nchunks):
        sl = slice(ci * c, (ci + 1) * c)
        cols = slice(ci * PAIR, (ci + 1) * PAIR)
        for pp in pairs:
            decay_end = decay_end_w[ci * c:ci * c + 1, lanes[pp]]
            gmat = jnp.where(eye_p, jnp.broadcast_to(decay_end, (PAIR, PAIR)), 0.0) \
                - jnp.where(bd, wtb[pp][:PAIR, cols], 0.0)
            hmat = jnp.where(bd, vtk[pp][:, cols] - wtb[pp][PAIR:, cols], 0.0)
            ys[pp].append(_dot(rp[pp][sl], s_cur[pp], NT) + y0[pp][sl])
            s_cur[pp] = _dot(s_cur[pp], gmat) + hmat
    for pp in pairs:
        s_ref[pp] = s_cur[pp]
    y_pairs = [ys[pp][0] if nchunks == 1 else jnp.concatenate(ys[pp], axis=0) for pp in pairs]

    @pl.when(tb == pl.num_programs(2) - 1)
    def _():
        for pp in range(npp):
            sT_ref[2 * pp] = s_ref[pp, 0:HEAD_DIM, 0:HEAD_DIM]
            sT_ref[2 * pp + 1] = s_ref[pp, HEAD_DIM:PAIR, HEAD_DIM:PAIR]

    y = y_pairs[0] if npp == 1 else jnp.concatenate(y_pairs, axis=1)
    mu = _head_sums(y) * (1.0 / HEAD_DIM)
    d = y - mu
    var = _head_sums(d * d) * (1.0 / HEAD_DIM)
    yn = d * lax.rsqrt(var + GN_EPS) * lnw_ref[...] + lnb_ref[...]
    o_ref[...] = ((yn + bonus) * g).astype(BF16)


def _rwkv(z3, shift_prev, s0, p, rows, c, npp):
    nb, t, _ = z3.shape
    width = npp * PAIR
    ngroups = D_RWKV // width
    col0 = 3 * D_ATT // width
    lo_blk = (3 * D_ATT + 3 * D_RWKV) // D_LORA
    sp = shift_prev.reshape(nb, 1, D_SHIFT)

    def zspec(off):
        return pl.BlockSpec((None, rows, width), lambda b, q, s: (b, s, col0 + off * ngroups + q))

    def sspec(off):
        return pl.BlockSpec((None, 1, width), lambda b, q, s: (b, 0, off * ngroups + q))

    def vec(off=0):
        return pl.BlockSpec((1, width), lambda b, q, s: (0, off * ngroups + q))

    def row2(x):
        return x.reshape(1, -1)

    out, s_fin = pl.pallas_call(
        functools.partial(_rwkv_kernel, c),
        grid=(nb, ngroups, t // rows),
        in_specs=[
            zspec(0), zspec(1), zspec(2),
            pl.BlockSpec((None, rows, D_LORA), lambda b, q, s: (b, s, lo_blk)),
            sspec(0), sspec(1), sspec(2),
            pl.BlockSpec((None, 1, D_LORA), lambda b, q, s: (b, 0, 3 * D_RWKV // D_LORA)),
            pl.BlockSpec((None, 2 * npp, HEAD_DIM, HEAD_DIM), lambda b, q, s: (b, q, 0, 0)),
            vec(0), vec(1), vec(2),
            pl.BlockSpec((1, D_LORA), lambda b, q, s: (0, 3 * D_RWKV // D_LORA)),
            vec(), vec(), vec(), vec(), vec(), vec(), vec(),
            pl.BlockSpec((RANK_W, width), lambda b, q, s: (0, q)),
            pl.BlockSpec((RANK_A, width), lambda b, q, s: (0, q)),
            pl.BlockSpec((RANK_G, width), lambda b, q, s: (0, q)),
        ],
        out_specs=[
            pl.BlockSpec((None, rows, width), lambda b, q, s: (b, s, q)),
            pl.BlockSpec((None, 2 * npp, HEAD_DIM, HEAD_DIM), lambda b, q, s: (b, q, 0, 0)),
        ],
        out_shape=[
            jax.ShapeDtypeStruct((nb, t, D_RWKV), BF16),
            jax.ShapeDtypeStruct((nb, N_RWKV_HEADS, HEAD_DIM, HEAD_DIM), F32),
        ],
        scratch_shapes=[
            pltpu.VMEM((npp, PAIR, PAIR), F32),
            pltpu.VMEM((1, width), F32), pltpu.VMEM((1, width), F32), pltpu.VMEM((1, width), F32),
            pltpu.VMEM((1, D_LORA), F32),
        ],
        compiler_params=_cparams(("arbitrary", "arbitrary", "arbitrary")),
        name="rwkv7_mix",
    )(z3, z3, z3, z3, sp, sp, sp, sp, s0,
      row2(p['mu_shift']), row2(p['mu_shift']), row2(p['mu_shift']), row2(p['mu_shift']),
      row2(p['w0']), row2(p['a0']), row2(p['k_k']), row2(p['k_a']), row2(p['r_k']),
      row2(p['ln_x_w']), row2(p['ln_x_b']), p['w2'], p['a2'], p['g2'])
    return out, s_fin


def _layer(x3, mod, p, u, k_past, v_past, s0, shift_prev, conv_prev):
    nb, t, d = x3.shape
    m = nb * t
    x = x3.reshape(m, d)
    if k_past is None:
        z = _norm_proj(x, p['norm_att_g'], mod, 1, 0, p['w_in'], IN_COLS, "in_proj", min(IN_ROW_TILE, t))
    else:
        z = _norm_proj(x, p['norm_att_g'], mod, 1, 0, p['w_in'], IN_COLS_SAMPLE, "in_proj", m)
    z3 = z.reshape(nb, t, D_IN)
    if k_past is None:
        att, k_keep = _attn_prompt(z3, p['q_norm_g'], p['k_norm_g'], u)
        keep = min(ATT_REACH, t)
        v_keep = z3[:, t - keep:, 2 * D_ATT:3 * D_ATT]
        rw, s_fin = _rwkv(z3, shift_prev, s0, p, RWKV_ROWS, CHUNK, RWKV_PAIRS_PROMPT)
    else:
        att, k_keep = _attn_sample(z3, k_past, v_past, p['q_norm_g'], p['k_norm_g'], u)
        v_keep = z3[:, :, 2 * D_ATT:3 * D_ATT]
        rw, s_fin = _rwkv(z3, shift_prev, s0, p, t, t, RWKV_PAIRS_SAMPLE)
    shift_last = z3[:, t - 1, 3 * D_ATT:]
    x1 = _proj_resid([att.reshape(m, D_ATT), rw.reshape(m, D_RWKV)], p['w_out'], x, mod, 2,
                     OUT_COLS, "out_proj")
    if k_past is None:
        act, conv_last = _ffn_up_fused(x1, p['norm_ffn_g'], mod, p['w_up'], conv_prev,
                                       p['dw_conv'], p['dw_bias'], t)
    else:
        hu = _norm_proj(x1, p['norm_ffn_g'], mod, 4, 3, p['w_up'], UP_COLS, "ffn_up_sample")
        f = hu.shape[1] // 2
        act = _act_sample(hu, conv_prev, p['dw_conv'], p['dw_bias'], nb, t)
        conv_last = hu.reshape(nb, t, 2 * f)[:, t - (CONV_W - 1):, :f]
    x2 = _proj_resid([act], p['w_down'], x1, mod, 5, DOWN_COLS, "ffn_down")
    heads = lambda a: a.reshape(nb, a.shape[1], N_ATT_HEADS, HEAD_DIM)
    return x2.reshape(nb, t, d), heads(k_keep), heads(v_keep), s_fin, shift_last, conv_last


def kernel(x_prompt, x_sample, c_prompt, c_sample, cache_att_k, cache_att_v, state_rwkv, state_shift, state_ffn_conv, norm_att_g, norm_ffn_g, w_ada, b_ada, w_in, q_norm_g, k_norm_g, rel_bias, mu_shift, w0, w2, a0, a2, g2, k_k, k_a, r_k, ln_x_w, ln_x_b, w_out, w_up, dw_conv, dw_bias, w_down):
    depth = w_in.shape[0]
    bp, tp, d = x_prompt.shape
    bs, ts, _ = x_sample.shape
    d_ff = w_down.shape[1]
    hp, hs = x_prompt, x_sample
    outs_p = [[] for _ in range(5)]
    outs_s = [[] for _ in range(5)]
    for l in range(depth):
        p = dict(norm_att_g=norm_att_g[l], norm_ffn_g=norm_ffn_g[l], w_in=w_in[l], q_norm_g=q_norm_g[l],
                 k_norm_g=k_norm_g[l], mu_shift=mu_shift[l], w0=w0[l], w2=w2[l], a0=a0[l], a2=a2[l],
                 g2=g2[l], k_k=k_k[l], k_a=k_a[l], r_k=r_k[l], ln_x_w=ln_x_w[l], ln_x_b=ln_x_b[l],
                 w_out=w_out[l], w_up=w_up[l], dw_conv=dw_conv[l], dw_bias=dw_bias[l], w_down=w_down[l])
        n_c = bp + bs
        pad = (-n_c) % 8
        c_all = jnp.concatenate([c_prompt, c_sample, jnp.zeros((pad, d), F32)], axis=0)
        mod = _ada(c_all, w_ada[l], b_ada[l])
        mod_p = _Mod(mod.reshape(n_c + pad, 6, 1, d), False, rows_per_batch=tp)
        mod_s = _Mod(jnp.repeat(mod[bp:bp + bs], ts, axis=0), True)
        u = _bias_rows(rel_bias[l])

        res = _layer(hp, mod_p, p, u, None, None,
                     jnp.zeros((bp, N_RWKV_HEADS, HEAD_DIM, HEAD_DIM), F32),
                     jnp.zeros((bp, D_SHIFT), F32),
                     jnp.zeros((bp, CONV_W - 1, d_ff), F32))
        hp = res[0]
        for lst, val in zip(outs_p, res[1:]):
            lst.append(val)
        res = _layer(hs, mod_s, p, u, cache_att_k[l], cache_att_v[l], state_rwkv[l],
                     state_shift[l], state_ffn_conv[l])
        hs = res[0]
        for lst, val in zip(outs_s, res[1:]):
            lst.append(val)
    st = lambda lst: jnp.stack(lst)
    return (hp, hs, *[st(x) for x in outs_p], *[st(x) for x in outs_s])
```

```python
import functools

import jax
import jax.numpy as jnp
from jax import lax
from jax.experimental import pallas as pl
from jax.experimental.pallas import tpu as pltpu

F32 = jnp.float32
BF16 = jnp.bfloat16

CHUNK = 64
N_PREV_CHUNKS = 8
ATT_REACH = N_PREV_CHUNKS * CHUNK
HEAD_DIM = 64
N_ATT_HEADS = 16
N_RWKV_HEADS = 16
D_ATT = N_ATT_HEADS * HEAD_DIM
D_RWKV = N_RWKV_HEADS * HEAD_DIM
REL_CLIP = 128
RANK_W = 64
RANK_A = 64
RANK_G = 128
D_LORA = RANK_W + RANK_A + RANK_G
D_SHIFT = 3 * D_RWKV + D_LORA
D_IN = 3 * D_ATT + D_SHIFT
CONV_W = 3
RMS_EPS = 1e-6
GN_EPS = 64e-5
ATT_SCALE = HEAD_DIM ** -0.5
LOG2E = 1.4426950408889634

LANES = 128
PAIR = 2 * HEAD_DIM
MXU_DIM = 256
VMEM_LIMIT = 60 * 1024 * 1024

ROW_TILE = 1024
IN_ROW_TILE = 2048
ADA_COLS = 512
IN_COLS = 256
IN_COLS_SAMPLE = 640
OUT_ROW_TILE = 512
UP_COLS = 512
DOWN_COLS = 256
ATT_QROWS = 256
ATT_WIN = ATT_QROWS + ATT_REACH
ATT_PAIRS = 2
BIAS_LEN = 1024
RWKV_ROWS = 256
RWKV_PAIRS_PROMPT = 8
RWKV_PAIRS_SAMPLE = 8


def _cparams(sem):
    return pltpu.CompilerParams(dimension_semantics=sem, vmem_limit_bytes=VMEM_LIMIT)


def _dot(a, b, dims=(((1,), (0,)), ((), ()))):
    return lax.dot_general(a.astype(BF16), b.astype(BF16), dims, preferred_element_type=F32)


def _split2(x):
    hi = x.astype(BF16)
    lo = (x - hi.astype(F32)).astype(BF16)
    return hi, lo


NT = (((1,), (1,)), ((), ()))
TN = (((0,), (0,)), ((), ()))


def _iota(shape, dim):
    return lax.broadcasted_iota(jnp.int32, shape, dim)


def _blk(x, size):
    return jnp.right_shift(x, size.bit_length() - 1)


def _head_ones(n):
    r = _blk(_iota((n, n), 0), HEAD_DIM)
    c = _blk(_iota((n, n), 1), HEAD_DIM)
    return jnp.where(r == c, 1.0, 0.0).astype(BF16)


def _head_sums(x):
    lanes = x.shape[1]
    group = min(lanes, MXU_DIM)
    ones = _head_ones(group)
    parts = [_dot(x[:, i:i + group], ones) for i in range(0, lanes, group)]
    return parts[0] if len(parts) == 1 else jnp.concatenate(parts, axis=1)


def _sigmoid(x):
    return 1.0 / (1.0 + jnp.exp(-x))


def _softplus(x):
    return jnp.maximum(x, 0.0) + jnp.log(1.0 + jnp.exp(-jnp.abs(x)))


def _ada_kernel(c_ref, w_ref, b_ref, o_ref):
    c = c_ref[...]
    s = c * _sigmoid(c)
    o_ref[...] = _dot(s, w_ref[...]) + b_ref[...]


def _ada(c_all, w_ada, b_ada):
    rows, d = c_all.shape
    n = w_ada.shape[1]
    return pl.pallas_call(
        _ada_kernel,
        grid=(n // ADA_COLS,),
        in_specs=[
            pl.BlockSpec((rows, d), lambda j: (0, 0)),
            pl.BlockSpec((d, ADA_COLS), lambda j: (0, j)),
            pl.BlockSpec((1, ADA_COLS), lambda j: (0, j)),
        ],
        out_specs=pl.BlockSpec((rows, ADA_COLS), lambda j: (0, j)),
        out_shape=jax.ShapeDtypeStruct((rows, n), F32),
        compiler_params=_cparams(("arbitrary",)),
        name="ada_mod",
    )(c_all, w_ada, b_ada.reshape(1, n))


class _Mod:
    def __init__(self, arr, per_row, rows_per_batch=None):
        self.arr = arr
        self.per_row = per_row
        self.rows_per_batch = rows_per_batch

    def spec(self, idx, cols, col_of, row_tile):
        if self.per_row:
            m = self.arr.shape[0]
            d = self.arr.shape[1] // 6
            nblk = d // cols
            return pl.BlockSpec((m, cols), lambda i, j: (0, idx * nblk + col_of(j)))
        tiles_per_batch = self.rows_per_batch // row_tile
        return pl.BlockSpec((None, None, 1, cols),
                            lambda i, j: (i // tiles_per_batch, idx, 0, col_of(j)))


NORM_ROWS = 128


def _store_normed(h_ref, x_ref, g_ref, sc_ref, sh_ref):
    rows = x_ref.shape[0]
    step = min(NORM_ROWS, rows)
    per_row = sc_ref.shape[0] == rows

    def body(r, carry):
        sl = pl.ds(pl.multiple_of(r * step, step), step)
        x = x_ref[sl, :]
        ms = jnp.mean(x * x, axis=-1, keepdims=True)
        xn = x * lax.rsqrt(ms + RMS_EPS) * g_ref[...]
        sc = sc_ref[sl, :] if per_row else sc_ref[...]
        sh = sh_ref[sl, :] if per_row else sh_ref[...]
        h_ref[sl, :] = (xn * (1.0 + sc) + sh).astype(BF16)
        return carry

    lax.fori_loop(0, rows // step, body, 0)


def _norm_proj_kernel(x_ref, g_ref, sc_ref, sh_ref, w_ref, o_ref, h_ref):
    @pl.when(pl.program_id(1) == 0)
    def _():
        _store_normed(h_ref, x_ref, g_ref, sc_ref, sh_ref)

    o_ref[...] = jnp.dot(h_ref[...], w_ref[...].astype(BF16), preferred_element_type=F32)


def _norm_proj(x, gain, mod, sc_idx, sh_idx, w, cols, name, row_tile=ROW_TILE):
    m, d = x.shape
    n = w.shape[1]
    tm = min(row_tile, m)
    whole = lambda j: 0
    x_mode = dict(pipeline_mode=pl.Buffered(1)) if tm > ROW_TILE else {}
    return pl.pallas_call(
        _norm_proj_kernel,
        grid=(m // tm, n // cols),
        in_specs=[
            pl.BlockSpec((tm, d), lambda i, j: (i, 0), **x_mode),
            pl.BlockSpec((1, d), lambda i, j: (0, 0)),
            mod.spec(sc_idx, d, whole, tm),
            mod.spec(sh_idx, d, whole, tm),
            pl.BlockSpec((d, cols), lambda i, j: (0, j)),
        ],
        out_specs=pl.BlockSpec((tm, cols), lambda i, j: (i, j)),
        out_shape=jax.ShapeDtypeStruct((m, n), F32),
        scratch_shapes=[pltpu.VMEM((tm, d), BF16)],
        compiler_params=_cparams(("arbitrary", "arbitrary")),
        name=name,
    )(x, gain.reshape(1, d), mod.arr, mod.arr, w)


def _proj_resid_kernel(n_pairs, *refs):
    a_refs = refs[:n_pairs]
    w_refs = refs[n_pairs:2 * n_pairs]
    x_ref, g_ref, o_ref = refs[2 * n_pairs:]
    acc = jnp.dot(a_refs[0][...], w_refs[0][...].astype(BF16), preferred_element_type=F32)
    for a_ref, w_ref in zip(a_refs[1:], w_refs[1:]):
        acc = acc + jnp.dot(a_ref[...], w_ref[...].astype(BF16), preferred_element_type=F32)
    o_ref[...] = x_ref[...] + g_ref[...] * acc


def _proj_resid(a_list, w, x, mod, g_idx, cols, name):
    m, n = x.shape
    tm = min(ROW_TILE, m)
    in_specs, w_args = [], []
    row = 0
    for a in a_list:
        kdim = a.shape[1]
        in_specs.append(pl.BlockSpec((tm, kdim), lambda i, j: (i, 0)))
    for a in a_list:
        kdim = a.shape[1]
        in_specs.append(pl.BlockSpec((kdim, cols), lambda i, j, r=row // kdim: (r, j)))
        w_args.append(w)
        row += kdim
    in_specs.append(pl.BlockSpec((tm, cols), lambda i, j: (i, j)))
    in_specs.append(mod.spec(g_idx, cols, lambda j: j, tm))
    return pl.pallas_call(
        functools.partial(_proj_resid_kernel, len(a_list)),
        grid=(m // tm, n // cols),
        in_specs=in_specs,
        out_specs=pl.BlockSpec((tm, cols), lambda i, j: (i, j)),
        out_shape=jax.ShapeDtypeStruct((m, n), F32),
        compiler_params=_cparams(("arbitrary", "arbitrary")),
        name=name,
    )(*a_list, *w_args, x, mod.arr)


def _out_proj_kernel(a1_ref, a2_ref, w_ref, x_ref, g_ref, gain_ref, sc_ref, sh_ref, o_ref, h_ref, wb_ref):
    @pl.when(pl.program_id(0) == 0)
    def _():
        step = MXU_DIM

        def cast_rows(r, carry):
            sl = pl.ds(pl.multiple_of(r * step, step), step)
            wb_ref[sl, :] = w_ref[sl, :].astype(BF16)
            return carry

        lax.fori_loop(0, w_ref.shape[0] // step, cast_rows, 0)

    k1 = a1_ref.shape[1]
    acc = jnp.dot(a1_ref[...], wb_ref[0:k1], preferred_element_type=F32) \
        + jnp.dot(a2_ref[...], wb_ref[k1:], preferred_element_type=F32)
    o_ref[...] = x_ref[...] + g_ref[...] * acc
    _store_normed(h_ref, o_ref, gain_ref, sc_ref, sh_ref)


def _out_proj(a1, a2, w, x, mod, gain, row_tile):
    m, d = x.shape
    tm = min(row_tile, m)
    whole = lambda j: 0
    row = lambda kdim: pl.BlockSpec((tm, kdim), lambda i, j: (i, 0))
    return pl.pallas_call(
        _out_proj_kernel,
        grid=(m // tm, 1),
        in_specs=[
            row(a1.shape[1]), row(a2.shape[1]),
            pl.BlockSpec(w.shape, lambda i, j: (0, 0), pipeline_mode=pl.Buffered(1)),
            row(d),
            mod.spec(2, d, whole, tm),
            pl.BlockSpec((1, d), lambda i, j: (0, 0)),
            mod.spec(4, d, whole, tm),
            mod.spec(3, d, whole, tm),
        ],
        out_specs=[row(d), row(d)],
        out_shape=[jax.ShapeDtypeStruct((m, d), F32), jax.ShapeDtypeStruct((m, d), BF16)],
        scratch_shapes=[pltpu.VMEM(w.shape, BF16)],
        compiler_params=_cparams(("arbitrary", "arbitrary")),
        name="out_proj",
    )(a1, a2, w, x, mod.arr, gain.reshape(1, d), mod.arr, mod.arr)


def _proj_kernel(h_ref, w_ref, o_ref):
    o_ref[...] = jnp.dot(h_ref[...], w_ref[...].astype(BF16), preferred_element_type=F32)


def _proj(h, w, cols, name):
    m, kdim = h.shape
    n = w.shape[1]
    return pl.pallas_call(
        _proj_kernel,
        grid=(n // cols,),
        in_specs=[pl.BlockSpec((m, kdim), lambda j: (0, 0)), pl.BlockSpec((kdim, cols), lambda j: (0, j))],
        out_specs=pl.BlockSpec((m, cols), lambda j: (0, j)),
        out_shape=jax.ShapeDtypeStruct((m, n), F32),
        compiler_params=_cparams(("arbitrary",)),
        name=name,
    )(h, w)


def _gelu(x):
    return 0.5 * x * (1.0 + lax.erf(x * (2.0 ** -0.5)))


def _ffn_up_kernel(tiles_per_batch, h_ref, wg_ref, wv_ref, hist_ref, cw_ref, cb_ref, act_ref, last_ref,
                   carry_ref):
    i = pl.program_id(0)
    j = pl.program_id(1)

    @pl.when((i % tiles_per_batch) == 0)
    def _():
        carry_ref[j] = hist_ref[...]

    h = h_ref[...]
    gate = jnp.dot(h, wg_ref[...].astype(BF16), preferred_element_type=F32)
    val = jnp.dot(h, wv_ref[...].astype(BF16), preferred_element_type=F32)
    tm = gate.shape[0]
    prev = carry_ref[j]
    row = _iota(gate.shape, 0)
    g1 = pltpu.roll(gate, 1, 0)
    g2 = pltpu.roll(gate, 2, 0)
    g1 = jnp.where(row == 0, prev[1:2], g1)
    g2 = jnp.where(row == 0, prev[0:1], jnp.where(row == 1, prev[1:2], g2))
    cw = cw_ref[...]
    conv = cb_ref[...] + g2 * cw[0:1] + g1 * cw[1:2] + gate * cw[2:3]
    act_ref[...] = (_gelu(conv) * val).astype(BF16)
    tail = gate[tm - 2:tm]
    carry_ref[j] = tail
    last_ref[...] = tail


def _ffn_up_fused(h, w_up, hist, conv_w, conv_b, rows_per_batch):
    m, d = h.shape
    f = w_up.shape[1] // 2
    tm = min(ROW_TILE, rows_per_batch)
    cols = UP_COLS
    nj = f // cols
    tiles_per_batch = rows_per_batch // tm
    act, tile_tails = pl.pallas_call(
        functools.partial(_ffn_up_kernel, tiles_per_batch),
        grid=(m // tm, nj),
        in_specs=[
            pl.BlockSpec((tm, d), lambda i, j: (i, 0)),
            pl.BlockSpec((d, cols), lambda i, j: (0, j)),
            pl.BlockSpec((d, cols), lambda i, j: (0, nj + j)),
            pl.BlockSpec((None, CONV_W - 1, cols), lambda i, j: (i // tiles_per_batch, 0, j)),
            pl.BlockSpec((CONV_W, cols), lambda i, j: (0, j)),
            pl.BlockSpec((1, cols), lambda i, j: (0, j)),
        ],
        out_specs=[
            pl.BlockSpec((tm, cols), lambda i, j: (i, j)),
            pl.BlockSpec((None, CONV_W - 1, cols), lambda i, j: (i, 0, j)),
        ],
        out_shape=[
            jax.ShapeDtypeStruct((m, f), BF16),
            jax.ShapeDtypeStruct((m // tm, CONV_W - 1, f), F32),
        ],
        scratch_shapes=[pltpu.VMEM((nj, CONV_W - 1, cols), F32)],
        compiler_params=_cparams(("arbitrary", "arbitrary")),
        name="ffn_up_prompt",
    )(h, w_up, w_up, hist, conv_w, conv_b.reshape(1, f))
    return act, tile_tails[tiles_per_batch - 1::tiles_per_batch]


def _act_sample_kernel(gate_ref, val_ref, hist_ref, cw_ref, cb_ref, act_ref):
    gate = gate_ref[...]
    hist = hist_ref[...]
    t = _iota(gate.shape, 1)
    g1 = jnp.where(t == 0, hist[:, 1:2], pltpu.roll(gate, 1, 1))
    g2 = jnp.where(t == 0, hist[:, 0:1], jnp.where(t == 1, hist[:, 1:2], pltpu.roll(gate, 2, 1)))
    cw = cw_ref[...]
    conv = cb_ref[...] + g2 * cw[0:1] + g1 * cw[1:2] + gate * cw[2:3]
    act_ref[...] = (_gelu(conv) * val_ref[...]).astype(BF16)


def _act_sample(hu, hist, conv_w, conv_b, nb, t):
    f = hu.shape[1] // 2
    cols = UP_COLS
    nj = f // cols
    hu3 = hu.reshape(nb, t, 2 * f)
    act = pl.pallas_call(
        _act_sample_kernel,
        grid=(nj,),
        in_specs=[
            pl.BlockSpec((nb, t, cols), lambda j: (0, 0, j)),
            pl.BlockSpec((nb, t, cols), lambda j: (0, 0, nj + j)),
            pl.BlockSpec((nb, CONV_W - 1, cols), lambda j: (0, 0, j)),
            pl.BlockSpec((CONV_W, cols), lambda j: (0, j)),
            pl.BlockSpec((1, cols), lambda j: (0, j)),
        ],
        out_specs=pl.BlockSpec((nb, t, cols), lambda j: (0, 0, j)),
        out_shape=jax.ShapeDtypeStruct((nb, t, f), BF16),
        compiler_params=_cparams(("arbitrary",)),
        name="ffn_act_sample",
    )(hu3, hu3, hist, conv_w, conv_b.reshape(1, f))
    return act.reshape(nb * t, f)


def _pair_rms(x, gain):
    x2 = x * x
    first = _iota(x.shape, 1) < HEAD_DIM
    s0 = jnp.sum(jnp.where(first, x2, 0.0), axis=-1, keepdims=True)
    s1 = jnp.sum(jnp.where(first, 0.0, x2), axis=-1, keepdims=True)
    ms = jnp.where(first, s0, s1) * (1.0 / HEAD_DIM)
    return x * lax.rsqrt(ms + RMS_EPS) * gain


def _bias_rows(table):
    h = table.shape[0]
    far = jnp.broadcast_to(table[:, 2 * REL_CLIP:], (h, ATT_REACH - REL_CLIP))
    mid = table[:, ::-1]
    near_len = BIAS_LEN - ATT_QROWS - (ATT_REACH - REL_CLIP) - (2 * REL_CLIP + 1)
    near = jnp.broadcast_to(table[:, 0:1], (h, near_len))
    wrap = jnp.broadcast_to(table[:, 2 * REL_CLIP:], (h, ATT_QROWS))
    return jnp.concatenate([far, mid, near, wrap], axis=1)


def _toeplitz(u_row, rows):
    return pltpu.roll(jnp.broadcast_to(u_row, (rows, BIAS_LEN)), 0, 1, stride=1, stride_axis=0)


def _attn_prompt_kernel(q_ref, k_ref, v_ref, qg_ref, kg_ref, u_ref, o_ref, kn_ref,
                        bias_ref, kwin_ref, vwin_ref):
    b = pl.program_id(1)
    qb = pl.program_id(2)
    shape = (ATT_QROWS, ATT_WIN)
    pairs = range(q_ref.shape[1] // PAIR)
    cols = [slice(p * PAIR, (p + 1) * PAIR) for p in pairs]
    chains = [(p, h) for p in pairs for h in range(2)]

    @pl.when((b == 0) & (qb == 0))
    def _():
        r = _iota(shape, 0)
        w = _iota(shape, 1)
        chunk_lo = _blk(r, CHUNK) * CHUNK
        in_band = (w >= chunk_lo) & (w < chunk_lo + (ATT_REACH + CHUNK))
        for i, (p, h) in enumerate(chains):
            bias = _toeplitz(u_ref[p, h:h + 1, :], ATT_QROWS)[:, :ATT_WIN]
            bias_ref[i] = jnp.where(in_band, bias * LOG2E, -jnp.inf)

    @pl.when(qb == 0)
    def _():
        kwin_ref[0:ATT_REACH] = jnp.zeros((ATT_REACH, kwin_ref.shape[1]), BF16)
        vwin_ref[0:ATT_REACH] = jnp.zeros((ATT_REACH, vwin_ref.shape[1]), BF16)

    @pl.when(qb > 0)
    def _():
        kwin_ref[0:ATT_REACH] = kwin_ref[ATT_QROWS:ATT_WIN]
        vwin_ref[0:ATT_REACH] = vwin_ref[ATT_QROWS:ATT_WIN]

    kn = [_pair_rms(k_ref[:, c], kg_ref[...]) for c in cols]
    for p in pairs:
        kn_ref[:, cols[p]] = kn[p]
        kwin_ref[ATT_REACH:ATT_WIN, cols[p]] = kn[p].astype(BF16)
    vwin_ref[ATT_REACH:ATT_WIN] = v_ref[...].astype(BF16)

    def attend(mask_start):
        qn = [_pair_rms(q_ref[:, c], qg_ref[...]) * (ATT_SCALE * LOG2E) for c in cols]
        kb = [kwin_ref[:, c] for c in cols]
        vb = [vwin_ref[:, c] for c in cols]
        first = _iota((ATT_QROWS, PAIR), 1) < HEAD_DIM
        first_w = _iota((ATT_WIN, PAIR), 1) < HEAD_DIM
        qh = [jnp.where(first, qn[p], 0.0) if h == 0 else jnp.where(first, 0.0, qn[p]) for p, h in chains]
        s = [_dot(qh[i], kb[p], NT) + bias_ref[i] for i, (p, h) in enumerate(chains)]
        if mask_start:
            started = _iota(shape, 1) >= ATT_REACH - qb * ATT_QROWS
            s = [jnp.where(started, x, -jnp.inf) for x in s]
        m = [jnp.max(x, axis=-1, keepdims=True) for x in s]
        pr = [jnp.exp2(x - mm) for x, mm in zip(s, m)]
        one = jnp.ones((), BF16)
        v_aug = [jnp.where(first_w, vb[p], one) if h == 0 else jnp.where(first_w, one, vb[p]) for p, h in chains]
        o = [_dot(pr[i], v_aug[i]) for i in range(len(chains))]
        o = [x / pltpu.roll(x, HEAD_DIM, 1) for x in o]
        for p in pairs:
            o_ref[:, cols[p]] = jnp.where(first, o[2 * p], o[2 * p + 1]).astype(BF16)

    full_window_from = ATT_REACH // ATT_QROWS
    pl.when(qb < full_window_from)(lambda: attend(True))
    pl.when(qb >= full_window_from)(lambda: attend(False))


def _attn_prompt(z3, q_gain, k_gain, u):
    nb, t, _ = z3.shape
    npairs = N_ATT_HEADS // 2
    npp = ATT_PAIRS
    width = npp * PAIR
    ngroups = npairs // npp
    nq = t // ATT_QROWS
    kcol = D_ATT // width
    vcol = 2 * D_ATT // width
    keep_blocks = ATT_REACH // ATT_QROWS
    blk = (None, ATT_QROWS, width)

    att, kn = pl.pallas_call(
        _attn_prompt_kernel,
        grid=(ngroups, nb, nq),
        in_specs=[
            pl.BlockSpec(blk, lambda g, b, q: (b, q, g)),
            pl.BlockSpec(blk, lambda g, b, q: (b, q, kcol + g)),
            pl.BlockSpec(blk, lambda g, b, q: (b, q, vcol + g)),
            pl.BlockSpec((1, PAIR), lambda g, b, q: (0, 0)),
            pl.BlockSpec((1, PAIR), lambda g, b, q: (0, 0)),
            pl.BlockSpec((npp, 2, BIAS_LEN), lambda g, b, q: (g, 0, 0)),
        ],
        out_specs=[
            pl.BlockSpec(blk, lambda g, b, q: (b, q, g)),
            pl.BlockSpec(blk, lambda g, b, q: (b, jnp.maximum(q - (nq - keep_blocks), 0), g)),
        ],
        out_shape=[
            jax.ShapeDtypeStruct((nb, t, D_ATT), BF16),
            jax.ShapeDtypeStruct((nb, ATT_REACH, D_ATT), F32),
        ],
        scratch_shapes=[pltpu.VMEM((2 * npp, ATT_QROWS, ATT_WIN), F32),
                        pltpu.VMEM((ATT_WIN, width), BF16), pltpu.VMEM((ATT_WIN, width), BF16)],
        compiler_params=_cparams(("arbitrary", "arbitrary", "arbitrary")),
        name="attn_prompt",
    )(z3, z3, z3, jnp.tile(q_gain, 2).reshape(1, PAIR),
      jnp.tile(k_gain, 2).reshape(1, PAIR), u.reshape(npairs, 2, BIAS_LEN))
    return att, kn


def _attn_sample_kernel(q_ref, k_ref, v_ref, kp_ref, vp_ref, qg_ref, kg_ref, u_ref, o_ref, kn_ref):
    t = q_ref.shape[0]
    reach = kp_ref.shape[0]
    first = _iota((t, PAIR), 1) < HEAD_DIM
    pairs = range(N_ATT_HEADS // 2)
    cols = [slice(p * PAIR, (p + 1) * PAIR) for p in pairs]
    chains = [(p, h) for p in pairs for h in range(2)]
    qn = [_pair_rms(q_ref[:, c], qg_ref[...]) * ATT_SCALE for c in cols]
    kn = [_pair_rms(k_ref[:, c], kg_ref[...]) for c in cols]
    for p in pairs:
        kn_ref[:, cols[p]] = kn[p]
    kpast = [kp_ref[:, c].astype(BF16) for c in cols]
    vpast = [vp_ref[:, c].astype(BF16) for c in cols]
    vnew = [v_ref[:, c].astype(BF16) for c in cols]
    qh = [jnp.where(first, qn[p], 0.0) if h == 0 else jnp.where(first, 0.0, qn[p]) for p, h in chains]
    bias = [_toeplitz(u_ref[p, h:h + 1, :], t) for p, h in chains]
    s_past = [_dot(qh[i], kpast[p], NT) + bias[i][:, :reach] for i, (p, h) in enumerate(chains)]
    s_new = [_dot(qh[i], kn[p], NT) + bias[i][:, reach:reach + t] for i, (p, h) in enumerate(chains)]
    m = [jnp.maximum(jnp.max(a, axis=-1, keepdims=True), jnp.max(b, axis=-1, keepdims=True))
         for a, b in zip(s_past, s_new)]
    p_past = [jnp.exp(a - mm) for a, mm in zip(s_past, m)]
    p_new = [jnp.exp(b - mm) for b, mm in zip(s_new, m)]
    l = [jnp.sum(a, axis=-1, keepdims=True) + jnp.sum(b, axis=-1, keepdims=True)
         for a, b in zip(p_past, p_new)]
    o = [(_dot(p_past[i], vpast[p]) + _dot(p_new[i], vnew[p])) / l[i] for i, (p, h) in enumerate(chains)]
    for p in pairs:
        o_ref[:, cols[p]] = jnp.where(first, o[2 * p], o[2 * p + 1]).astype(BF16)


def _attn_sample(z3, k_past, v_past, q_gain, k_gain, u):
    nb, t, _ = z3.shape
    reach = k_past.shape[1]
    npairs = N_ATT_HEADS // 2
    att, kn = pl.pallas_call(
        _attn_sample_kernel,
        grid=(nb,),
        in_specs=[
            pl.BlockSpec((None, t, D_ATT), lambda b: (b, 0, 0)),
            pl.BlockSpec((None, t, D_ATT), lambda b: (b, 0, 1)),
            pl.BlockSpec((None, t, D_ATT), lambda b: (b, 0, 2)),
            pl.BlockSpec((None, reach, D_ATT), lambda b: (b, 0, 0)),
            pl.BlockSpec((None, reach, D_ATT), lambda b: (b, 0, 0)),
            pl.BlockSpec((1, PAIR), lambda b: (0, 0)),
            pl.BlockSpec((1, PAIR), lambda b: (0, 0)),
            pl.BlockSpec((npairs, 2, BIAS_LEN), lambda b: (0, 0, 0)),
        ],
        out_specs=[
            pl.BlockSpec((None, t, D_ATT), lambda b: (b, 0, 0)),
            pl.BlockSpec((None, t, D_ATT), lambda b: (b, 0, 0)),
        ],
        out_shape=[
            jax.ShapeDtypeStruct((nb, t, D_ATT), BF16),
            jax.ShapeDtypeStruct((nb, t, D_ATT), F32),
        ],
        compiler_params=_cparams(("arbitrary",)),
        name="attn_sample",
    )(z3, z3, z3, k_past.reshape(nb, reach, D_ATT), v_past.reshape(nb, reach, D_ATT),
      jnp.tile(q_gain, 2).reshape(1, PAIR), jnp.tile(k_gain, 2).reshape(1, PAIR),
      u.reshape(npairs, 2, BIAS_LEN))
    return att, kn


def _tri_inverse(l_mats, c):
    n = l_mats[0].shape[0]
    eye = jnp.where(_iota((n, n), 0) == _iota((n, n), 1), 1.0, 0.0).astype(F32)
    a_s = [(eye + l).astype(BF16) for l in l_mats]
    t_s = [eye - l for l in l_mats]
    for _ in range(c.bit_length() - 2):
        r_s = [eye - _dot(a, t) for a, t in zip(a_s, t_s)]
        t_s = [t + _dot(t, r) for t, r in zip(t_s, r_s)]
    return t_s


def _rwkv_kernel(c, r_ref, k_ref, v_ref, lo_ref, sr_ref, sk_ref, sv_ref, slo_ref, s0_ref,
                 mur_ref, muk_ref, muv_ref, mulo_ref, w0_ref, a0_ref, kkg_ref, ka_ref, rk_ref,
                 lnw_ref, lnb_ref, w2_ref, a2_ref, g2_ref,
                 o_ref, sT_ref, s_ref, cr_ref, ck_ref, cv_ref, clo_ref):
    tb = pl.program_id(2)
    rows, width = r_ref.shape
    npp = width // PAIR
    nchunks = rows // c
    h0 = _iota((rows, PAIR), 1) < HEAD_DIM
    bd = _blk(_iota((PAIR, PAIR), 0), HEAD_DIM) == _blk(_iota((PAIR, PAIR), 1), HEAD_DIM)

    @pl.when(tb == 0)
    def _():
        s_ref[...] = jnp.zeros(s_ref.shape, F32)
        for pp in range(npp):
            s_ref[pp, 0:HEAD_DIM, 0:HEAD_DIM] = s0_ref[2 * pp]
            s_ref[pp, HEAD_DIM:PAIR, HEAD_DIM:PAIR] = s0_ref[2 * pp + 1]
        cr_ref[...] = sr_ref[...]
        ck_ref[...] = sk_ref[...]
        cv_ref[...] = sv_ref[...]
        clo_ref[...] = slo_ref[...]

    def shifted(x_ref, carry_ref, mu_ref):
        x = x_ref[...]
        prev = jnp.where(_iota(x.shape, 0) == 0, carry_ref[...], pltpu.roll(x, 1, 0))
        carry_ref[...] = x[rows - 1:rows]
        return x + (prev - x) * mu_ref[...]

    r = shifted(r_ref, cr_ref, mur_ref)
    k = shifted(k_ref, ck_ref, muk_ref)
    v = shifted(v_ref, cv_ref, muv_ref)
    lo = shifted(lo_ref, clo_ref, mulo_ref)

    zeros_w = jnp.zeros((RANK_W, width), F32)
    w2p = jnp.concatenate([w2_ref[...], zeros_w], axis=0)
    a2p = jnp.concatenate([zeros_w, a2_ref[...]], axis=0)
    lo_wa = lo[:, 0:RANK_W + RANK_A]
    u = w0_ref[...] + _dot(jnp.tanh(lo_wa), w2p)
    lw = -jnp.exp(-_softplus(-u) - 0.5)
    a = _sigmoid(a0_ref[...] + _dot(lo_wa, a2p))
    g = _dot(_sigmoid(lo[:, RANK_W + RANK_A:]), g2_ref[...])

    kk = k * kkg_ref[...]
    kk = kk / jnp.maximum(jnp.sqrt(_head_sums(kk * kk)), 1e-12)
    k = k * (1.0 + (a - 1.0) * ka_ref[...])
    b = kk * a
    bonus = _head_sums(r * k * rk_ref[...]) * v

    tr = _iota((rows, rows), 0)
    tc = _iota((rows, rows), 1)
    same_chunk = _blk(tr, c) == _blk(tc, c)
    strict = same_chunk & (tr > tc)
    incl = same_chunk & (tr >= tc)
    lw_hi, lw_lo = _split2(lw)
    tril_ones = jnp.where(incl, 1.0, 0.0).astype(BF16)
    lp = jnp.dot(tril_ones, lw_hi, preferred_element_type=F32) + \
        jnp.dot(tril_ones, lw_lo, preferred_element_type=F32)
    lp_end = jnp.concatenate(
        [jnp.broadcast_to(lp[(ci + 1) * c - 1:(ci + 1) * c], (c, width)) for ci in range(nchunks)], axis=0)

    alpha_w = kk * jnp.exp(lp - lw)
    inv_p = jnp.exp(-lp)
    beta_w = b * inv_p
    kappa_w = k * inv_p
    rho_w = r * jnp.exp(lp)
    to_end = jnp.exp(lp_end - lp)
    beta_ew = b * to_end
    kappa_ew = k * to_end
    decay_end_w = jnp.exp(lp_end)

    wide = (rows, nchunks * PAIR)
    col_chunk = _blk(_iota(wide, 1), PAIR) == _blk(_iota(wide, 0), c)
    spread = lambda m: jnp.where(col_chunk, jnp.tile(m, (1, nchunks)), 0.0)
    eye_p = _iota((PAIR, PAIR), 0) == _iota((PAIR, PAIR), 1)

    pairs = range(npp)
    lanes = [slice(pp * PAIR, (pp + 1) * PAIR) for pp in pairs]
    alpha = [alpha_w[:, l] for l in lanes]
    rho = [rho_w[:, l] for l in lanes]
    vv = [v[:, l] for l in lanes]
    head_mask = [h0, jnp.logical_not(h0)]
    bk = [jnp.concatenate([beta_w[:, l], kappa_w[:, l]], axis=0).astype(BF16) for l in lanes]
    prod = [[_dot(jnp.concatenate([jnp.where(hm, alpha[pp], 0.0), jnp.where(hm, rho[pp], 0.0)], axis=0),
                  bk[pp], NT) for hm in head_mask] for pp in pairs]
    t_inv = _tri_inverse([jnp.where(strict, prod[pp][h][:rows, :rows], 0.0) for pp in pairs for h in range(2)], c)
    x = [[_dot(jnp.where(strict, prod[pp][h][:rows, rows:], 0.0), vv[pp]) for h in range(2)] for pp in pairs]
    ws = [[_dot(t_inv[2 * pp + h], jnp.concatenate([alpha[pp], x[pp][h]], axis=1)) for h in range(2)]
          for pp in pairs]
    w12 = [jnp.concatenate([jnp.where(h0, ws[pp][0][:, :PAIR], ws[pp][1][:, :PAIR]),
                            jnp.where(h0, ws[pp][0][:, PAIR:], ws[pp][1][:, PAIR:])], axis=1) for pp in pairs]
    q = [[_dot(jnp.where(incl, prod[pp][h][rows:, :rows], 0.0), w12[pp]) for h in range(2)] for pp in pairs]
    qk = [[_dot(jnp.where(incl, prod[pp][h][rows:, rows:], 0.0), vv[pp]) for h in range(2)] for pp in pairs]
    rp = [rho[pp] - jnp.where(h0, q[pp][0][:, :PAIR], q[pp][1][:, :PAIR]) for pp in pairs]
    y0 = [jnp.where(h0, qk[pp][0] - q[pp][0][:, PAIR:], qk[pp][1] - q[pp][1][:, PAIR:]) for pp in pairs]
    wtb = [_dot(w12[pp], spread(beta_ew[:, lanes[pp]]), TN) for pp in pairs]
    vtk = [_dot(vv[pp], spread(kappa_ew[:, lanes[pp]]), TN) for pp in pairs]

    s_cur = [s_ref[pp] for pp in pairs]
    ys = [[] for _ in pairs]
    for ci in range(nchunks):
        sl = slice(ci * c, (ci + 1) * c)
        cols = slice(ci * PAIR, (ci + 1) * PAIR)
        for pp in pairs:
            decay_end = decay_end_w[ci * c:ci * c + 1, lanes[pp]]
            gmat = jnp.where(eye_p, jnp.broadcast_to(decay_end, (PAIR, PAIR)), 0.0) \
                - jnp.where(bd, wtb[pp][:PAIR, cols], 0.0)
            hmat = jnp.where(bd, vtk[pp][:, cols] - wtb[pp][PAIR:, cols], 0.0)
            ys[pp].append(_dot(rp[pp][sl], s_cur[pp], NT) + y0[pp][sl])
            s_cur[pp] = _dot(s_cur[pp], gmat) + hmat
    for pp in pairs:
        s_ref[pp] = s_cur[pp]
    y_pairs = [ys[pp][0] if nchunks == 1 else jnp.concatenate(ys[pp], axis=0) for pp in pairs]

    @pl.when(tb == pl.num_programs(2) - 1)
    def _():
        for pp in range(npp):
            sT_ref[2 * pp] = s_ref[pp, 0:HEAD_DIM, 0:HEAD_DIM]
            sT_ref[2 * pp + 1] = s_ref[pp, HEAD_DIM:PAIR, HEAD_DIM:PAIR]

    y = y_pairs[0] if npp == 1 else jnp.concatenate(y_pairs, axis=1)
    mu = _head_sums(y) * (1.0 / HEAD_DIM)
    d = y - mu
    var = _head_sums(d * d) * (1.0 / HEAD_DIM)
    yn = d * lax.rsqrt(var + GN_EPS) * lnw_ref[...] + lnb_ref[...]
    o_ref[...] = ((yn + bonus) * g).astype(BF16)


def _rwkv(z3, shift_prev, s0, p, rows, c, npp):
    nb, t, _ = z3.shape
    width = npp * PAIR
    ngroups = D_RWKV // width
    col0 = 3 * D_ATT // width
    lo_blk = (3 * D_ATT + 3 * D_RWKV) // D_LORA
    sp = shift_prev.reshape(nb, 1, D_SHIFT)

    def zspec(off):
        return pl.BlockSpec((None, rows, width), lambda b, q, s: (b, s, col0 + off * ngroups + q))

    def sspec(off):
        return pl.BlockSpec((None, 1, width), lambda b, q, s: (b, 0, off * ngroups + q))

    def vec(off=0):
        return pl.BlockSpec((1, width), lambda b, q, s: (0, off * ngroups + q))

    def row2(x):
        return x.reshape(1, -1)

    out, s_fin = pl.pallas_call(
        functools.partial(_rwkv_kernel, c),
        grid=(nb, ngroups, t // rows),
        in_specs=[
            zspec(0), zspec(1), zspec(2),
            pl.BlockSpec((None, rows, D_LORA), lambda b, q, s: (b, s, lo_blk)),
            sspec(0), sspec(1), sspec(2),
            pl.BlockSpec((None, 1, D_LORA), lambda b, q, s: (b, 0, 3 * D_RWKV // D_LORA)),
            pl.BlockSpec((None, 2 * npp, HEAD_DIM, HEAD_DIM), lambda b, q, s: (b, q, 0, 0)),
            vec(0), vec(1), vec(2),
            pl.BlockSpec((1, D_LORA), lambda b, q, s: (0, 3 * D_RWKV // D_LORA)),
            vec(), vec(), vec(), vec(), vec(), vec(), vec(),
            pl.BlockSpec((RANK_W, width), lambda b, q, s: (0, q)),
            pl.BlockSpec((RANK_A, width), lambda b, q, s: (0, q)),
            pl.BlockSpec((RANK_G, width), lambda b, q, s: (0, q)),
        ],
        out_specs=[
            pl.BlockSpec((None, rows, width), lambda b, q, s: (b, s, q)),
            pl.BlockSpec((None, 2 * npp, HEAD_DIM, HEAD_DIM), lambda b, q, s: (b, q, 0, 0)),
        ],
        out_shape=[
            jax.ShapeDtypeStruct((nb, t, D_RWKV), BF16),
            jax.ShapeDtypeStruct((nb, N_RWKV_HEADS, HEAD_DIM, HEAD_DIM), F32),
        ],
        scratch_shapes=[
            pltpu.VMEM((npp, PAIR, PAIR), F32),
            pltpu.VMEM((1, width), F32), pltpu.VMEM((1, width), F32), pltpu.VMEM((1, width), F32),
            pltpu.VMEM((1, D_LORA), F32),
        ],
        compiler_params=_cparams(("arbitrary", "arbitrary", "arbitrary")),
        name="rwkv7_mix",
    )(z3, z3, z3, z3, sp, sp, sp, sp, s0,
      row2(p['mu_shift']), row2(p['mu_shift']), row2(p['mu_shift']), row2(p['mu_shift']),
      row2(p['w0']), row2(p['a0']), row2(p['k_k']), row2(p['k_a']), row2(p['r_k']),
      row2(p['ln_x_w']), row2(p['ln_x_b']), p['w2'], p['a2'], p['g2'])
    return out, s_fin


def _layer(x3, mod, p, u, k_past, v_past, s0, shift_prev, conv_prev):
    nb, t, d = x3.shape
    m = nb * t
    x = x3.reshape(m, d)
    if k_past is None:
        z = _norm_proj(x, p['norm_att_g'], mod, 1, 0, p['w_in'], IN_COLS, "in_proj", min(IN_ROW_TILE, t))
    else:
        z = _norm_proj(x, p['norm_att_g'], mod, 1, 0, p['w_in'], IN_COLS_SAMPLE, "in_proj", m)
    z3 = z.reshape(nb, t, D_IN)
    if k_past is None:
        att, k_keep = _attn_prompt(z3, p['q_norm_g'], p['k_norm_g'], u)
        keep = min(ATT_REACH, t)
        v_keep = z3[:, t - keep:, 2 * D_ATT:3 * D_ATT]
        rw, s_fin = _rwkv(z3, shift_prev, s0, p, RWKV_ROWS, CHUNK, RWKV_PAIRS_PROMPT)
    else:
        att, k_keep = _attn_sample(z3, k_past, v_past, p['q_norm_g'], p['k_norm_g'], u)
        v_keep = z3[:, :, 2 * D_ATT:3 * D_ATT]
        rw, s_fin = _rwkv(z3, shift_prev, s0, p, t, t, RWKV_PAIRS_SAMPLE)
    shift_last = z3[:, t - 1, 3 * D_ATT:]
    x1, h2 = _out_proj(att.reshape(m, D_ATT), rw.reshape(m, D_RWKV), p['w_out'], x, mod,
                       p['norm_ffn_g'], OUT_ROW_TILE)
    if k_past is None:
        act, conv_last = _ffn_up_fused(h2, p['w_up'], conv_prev, p['dw_conv'], p['dw_bias'], t)
    else:
        hu = _proj(h2, p['w_up'], UP_COLS, "ffn_up_sample")
        f = hu.shape[1] // 2
        act = _act_sample(hu, conv_prev, p['dw_conv'], p['dw_bias'], nb, t)
        conv_last = hu.reshape(nb, t, 2 * f)[:, t - (CONV_W - 1):, :f]
    x2 = _proj_resid([act], p['w_down'], x1, mod, 5, DOWN_COLS, "ffn_down")
    heads = lambda a: a.reshape(nb, a.shape[1], N_ATT_HEADS, HEAD_DIM)
    return x2.reshape(nb, t, d), heads(k_keep), heads(v_keep), s_fin, shift_last, conv_last


def kernel(x_prompt, x_sample, c_prompt, c_sample, cache_att_k, cache_att_v, state_rwkv, state_shift, state_ffn_conv, norm_att_g, norm_ffn_g, w_ada, b_ada, w_in, q_norm_g, k_norm_g, rel_bias, mu_shift, w0, w2, a0, a2, g2, k_k, k_a, r_k, ln_x_w, ln_x_b, w_out, w_up, dw_conv, dw_bias, w_down):
    depth = w_in.shape[0]
    bp, tp, d = x_prompt.shape
    bs, ts, _ = x_sample.shape
    d_ff = w_down.shape[1]
    hp, hs = x_prompt, x_sample
    outs_p = [[] for _ in range(5)]
    outs_s = [[] for _ in range(5)]
    for l in range(depth):
        p = dict(norm_att_g=norm_att_g[l], norm_ffn_g=norm_ffn_g[l], w_in=w_in[l], q_norm_g=q_norm_g[l],
                 k_norm_g=k_norm_g[l], mu_shift=mu_shift[l], w0=w0[l], w2=w2[l], a0=a0[l], a2=a2[l],
                 g2=g2[l], k_k=k_k[l], k_a=k_a[l], r_k=r_k[l], ln_x_w=ln_x_w[l], ln_x_b=ln_x_b[l],
                 w_out=w_out[l], w_up=w_up[l], dw_conv=dw_conv[l], dw_bias=dw_bias[l], w_down=w_down[l])
        n_c = bp + bs
        pad = (-n_c) % 8
        c_all = jnp.concatenate([c_prompt, c_sample, jnp.zeros((pad, d), F32)], axis=0)
        mod = _ada(c_all, w_ada[l], b_ada[l])
        mod_p = _Mod(mod.reshape(n_c + pad, 6, 1, d), False, rows_per_batch=tp)
        mod_s = _Mod(jnp.repeat(mod[bp:bp + bs], ts, axis=0), True)
        u = _bias_rows(rel_bias[l])

        res = _layer(hp, mod_p, p, u, None, None,
                     jnp.zeros((bp, N_RWKV_HEADS, HEAD_DIM, HEAD_DIM), F32),
                     jnp.zeros((bp, D_SHIFT), F32),
                     jnp.zeros((bp, CONV_W - 1, d_ff), F32))
        hp = res[0]
        for lst, val in zip(outs_p, res[1:]):
            lst.append(val)
        res = _layer(hs, mod_s, p, u, cache_att_k[l], cache_att_v[l], state_rwkv[l],
                     state_shift[l], state_ffn_conv[l])
        hs = res[0]
        for lst, val in zip(outs_s, res[1:]):
            lst.append(val)
    st = lambda lst: jnp.stack(lst)
    return (hp, hs, *[st(x) for x in outs_p], *[st(x) for x in outs_s])
```

```python
import functools

import jax
import jax.numpy as jnp
from jax import lax
from jax.experimental import pallas as pl
from jax.experimental.pallas import tpu as pltpu

F32 = jnp.float32
BF16 = jnp.bfloat16

CHUNK = 64
N_PREV_CHUNKS = 8
ATT_REACH = N_PREV_CHUNKS * CHUNK
HEAD_DIM = 64
N_ATT_HEADS = 16
N_RWKV_HEADS = 16
D_ATT = N_ATT_HEADS * HEAD_DIM
D_RWKV = N_RWKV_HEADS * HEAD_DIM
REL_CLIP = 128
RANK_W = 64
RANK_A = 64
RANK_G = 128
D_LORA = RANK_W + RANK_A + RANK_G
D_SHIFT = 3 * D_RWKV + D_LORA
D_IN = 3 * D_ATT + D_SHIFT
CONV_W = 3
RMS_EPS = 1e-6
GN_EPS = 64e-5
ATT_SCALE = HEAD_DIM ** -0.5
LOG2E = 1.4426950408889634

LANES = 128
PAIR = 2 * HEAD_DIM
MXU_DIM = 256
VMEM_LIMIT = 60 * 1024 * 1024

ROW_TILE = 1024
IN_ROW_TILE = 2048
ADA_COLS = 512
IN_COLS = 256
IN_COLS_SAMPLE = 640
OUT_ROW_TILE = 512
UP_COLS = 512
DOWN_COLS = 256
ATT_QROWS = 256
ATT_WIN = ATT_QROWS + ATT_REACH
ATT_PAIRS = 4
BIAS_LEN = 1024
RWKV_ROWS = 256
RWKV_PAIRS_PROMPT = 8
RWKV_PAIRS_SAMPLE = 8


def _cparams(sem):
    return pltpu.CompilerParams(dimension_semantics=sem, vmem_limit_bytes=VMEM_LIMIT)


def _dot(a, b, dims=(((1,), (0,)), ((), ()))):
    return lax.dot_general(a.astype(BF16), b.astype(BF16), dims, preferred_element_type=F32)


def _split2(x):
    hi = x.astype(BF16)
    lo = (x - hi.astype(F32)).astype(BF16)
    return hi, lo


NT = (((1,), (1,)), ((), ()))
TN = (((0,), (0,)), ((), ()))


def _iota(shape, dim):
    return lax.broadcasted_iota(jnp.int32, shape, dim)


def _blk(x, size):
    return jnp.right_shift(x, size.bit_length() - 1)


def _head_ones(n):
    r = _blk(_iota((n, n), 0), HEAD_DIM)
    c = _blk(_iota((n, n), 1), HEAD_DIM)
    return jnp.where(r == c, 1.0, 0.0).astype(BF16)


def _head_sums(x):
    lanes = x.shape[1]
    group = min(lanes, MXU_DIM)
    ones = _head_ones(group)
    parts = [_dot(x[:, i:i + group], ones) for i in range(0, lanes, group)]
    return parts[0] if len(parts) == 1 else jnp.concatenate(parts, axis=1)


def _sigmoid(x):
    return 1.0 / (1.0 + jnp.exp(-x))


def _softplus(x):
    return jnp.maximum(x, 0.0) + jnp.log(1.0 + jnp.exp(-jnp.abs(x)))


def _ada_kernel(c_ref, w_ref, b_ref, o_ref):
    c = c_ref[...]
    s = c * _sigmoid(c)
    o_ref[...] = _dot(s, w_ref[...]) + b_ref[...]


def _ada(c_all, w_ada, b_ada):
    rows, d = c_all.shape
    n = w_ada.shape[1]
    return pl.pallas_call(
        _ada_kernel,
        grid=(n // ADA_COLS,),
        in_specs=[
            pl.BlockSpec((rows, d), lambda j: (0, 0)),
            pl.BlockSpec((d, ADA_COLS), lambda j: (0, j)),
            pl.BlockSpec((1, ADA_COLS), lambda j: (0, j)),
        ],
        out_specs=pl.BlockSpec((rows, ADA_COLS), lambda j: (0, j)),
        out_shape=jax.ShapeDtypeStruct((rows, n), F32),
        compiler_params=_cparams(("arbitrary",)),
        name="ada_mod",
    )(c_all, w_ada, b_ada.reshape(1, n))


class _Mod:
    def __init__(self, arr, per_row, rows_per_batch=None):
        self.arr = arr
        self.per_row = per_row
        self.rows_per_batch = rows_per_batch

    def spec(self, idx, cols, col_of, row_tile):
        if self.per_row:
            m = self.arr.shape[0]
            d = self.arr.shape[1] // 6
            nblk = d // cols
            return pl.BlockSpec((m, cols), lambda i, j: (0, idx * nblk + col_of(j)))
        tiles_per_batch = self.rows_per_batch // row_tile
        return pl.BlockSpec((None, None, 1, cols),
                            lambda i, j: (i // tiles_per_batch, idx, 0, col_of(j)))


NORM_ROWS = 128


def _store_normed(h_ref, x_ref, g_ref, sc_ref, sh_ref):
    rows = x_ref.shape[0]
    step = min(NORM_ROWS, rows)
    per_row = sc_ref.shape[0] == rows

    def body(r, carry):
        sl = pl.ds(pl.multiple_of(r * step, step), step)
        x = x_ref[sl, :]
        ms = jnp.mean(x * x, axis=-1, keepdims=True)
        xn = x * lax.rsqrt(ms + RMS_EPS) * g_ref[...]
        sc = sc_ref[sl, :] if per_row else sc_ref[...]
        sh = sh_ref[sl, :] if per_row else sh_ref[...]
        h_ref[sl, :] = (xn * (1.0 + sc) + sh).astype(BF16)
        return carry

    lax.fori_loop(0, rows // step, body, 0)


def _norm_proj_kernel(x_ref, g_ref, sc_ref, sh_ref, w_ref, o_ref, h_ref):
    @pl.when(pl.program_id(1) == 0)
    def _():
        _store_normed(h_ref, x_ref, g_ref, sc_ref, sh_ref)

    o_ref[...] = jnp.dot(h_ref[...], w_ref[...].astype(BF16), preferred_element_type=F32)


def _norm_proj(x, gain, mod, sc_idx, sh_idx, w, cols, name, row_tile=ROW_TILE):
    m, d = x.shape
    n = w.shape[1]
    tm = min(row_tile, m)
    whole = lambda j: 0
    x_mode = dict(pipeline_mode=pl.Buffered(1)) if tm > ROW_TILE else {}
    return pl.pallas_call(
        _norm_proj_kernel,
        grid=(m // tm, n // cols),
        in_specs=[
            pl.BlockSpec((tm, d), lambda i, j: (i, 0), **x_mode),
            pl.BlockSpec((1, d), lambda i, j: (0, 0)),
            mod.spec(sc_idx, d, whole, tm),
            mod.spec(sh_idx, d, whole, tm),
            pl.BlockSpec((d, cols), lambda i, j: (0, j)),
        ],
        out_specs=pl.BlockSpec((tm, cols), lambda i, j: (i, j)),
        out_shape=jax.ShapeDtypeStruct((m, n), F32),
        scratch_shapes=[pltpu.VMEM((tm, d), BF16)],
        compiler_params=_cparams(("arbitrary", "arbitrary")),
        name=name,
    )(x, gain.reshape(1, d), mod.arr, mod.arr, w)


def _proj_resid_kernel(n_pairs, *refs):
    a_refs = refs[:n_pairs]
    w_refs = refs[n_pairs:2 * n_pairs]
    x_ref, g_ref, o_ref = refs[2 * n_pairs:]
    acc = jnp.dot(a_refs[0][...], w_refs[0][...].astype(BF16), preferred_element_type=F32)
    for a_ref, w_ref in zip(a_refs[1:], w_refs[1:]):
        acc = acc + jnp.dot(a_ref[...], w_ref[...].astype(BF16), preferred_element_type=F32)
    o_ref[...] = x_ref[...] + g_ref[...] * acc


def _proj_resid(a_list, w, x, mod, g_idx, cols, name):
    m, n = x.shape
    tm = min(ROW_TILE, m)
    in_specs, w_args = [], []
    row = 0
    for a in a_list:
        kdim = a.shape[1]
        in_specs.append(pl.BlockSpec((tm, kdim), lambda i, j: (i, 0)))
    for a in a_list:
        kdim = a.shape[1]
        in_specs.append(pl.BlockSpec((kdim, cols), lambda i, j, r=row // kdim: (r, j)))
        w_args.append(w)
        row += kdim
    in_specs.append(pl.BlockSpec((tm, cols), lambda i, j: (i, j)))
    in_specs.append(mod.spec(g_idx, cols, lambda j: j, tm))
    return pl.pallas_call(
        functools.partial(_proj_resid_kernel, len(a_list)),
        grid=(m // tm, n // cols),
        in_specs=in_specs,
        out_specs=pl.BlockSpec((tm, cols), lambda i, j: (i, j)),
        out_shape=jax.ShapeDtypeStruct((m, n), F32),
        compiler_params=_cparams(("arbitrary", "arbitrary")),
        name=name,
    )(*a_list, *w_args, x, mod.arr)


def _out_proj_kernel(a1_ref, a2_ref, w_ref, x_ref, g_ref, gain_ref, sc_ref, sh_ref, o_ref, h_ref, wb_ref):
    @pl.when(pl.program_id(0) == 0)
    def _():
        step = MXU_DIM

        def cast_rows(r, carry):
            sl = pl.ds(pl.multiple_of(r * step, step), step)
            wb_ref[sl, :] = w_ref[sl, :].astype(BF16)
            return carry

        lax.fori_loop(0, w_ref.shape[0] // step, cast_rows, 0)

    k1 = a1_ref.shape[1]
    rows = x_ref.shape[0]
    step = min(MXU_DIM, rows)
    per_row = sc_ref.shape[0] == rows
    pieces = [slice(r0, r0 + step) for r0 in range(0, rows, step)]
    x1s = []
    for sl in pieces:
        acc = jnp.dot(a1_ref[sl, :], wb_ref[0:k1], preferred_element_type=F32) \
            + jnp.dot(a2_ref[sl, :], wb_ref[k1:], preferred_element_type=F32)
        g = g_ref[sl, :] if per_row else g_ref[...]
        x1 = x_ref[sl, :] + g * acc
        o_ref[sl, :] = x1
        x1s.append(x1)
    for sl, x1 in zip(pieces, x1s):
        ms = jnp.mean(x1 * x1, axis=-1, keepdims=True)
        xn = x1 * lax.rsqrt(ms + RMS_EPS) * gain_ref[...]
        sc = sc_ref[sl, :] if per_row else sc_ref[...]
        sh = sh_ref[sl, :] if per_row else sh_ref[...]
        h_ref[sl, :] = (xn * (1.0 + sc) + sh).astype(BF16)


def _out_proj(a1, a2, w, x, mod, gain, row_tile):
    m, d = x.shape
    tm = min(row_tile, m)
    whole = lambda j: 0
    row = lambda kdim: pl.BlockSpec((tm, kdim), lambda i, j: (i, 0))
    return pl.pallas_call(
        _out_proj_kernel,
        grid=(m // tm, 1),
        in_specs=[
            row(a1.shape[1]), row(a2.shape[1]),
            pl.BlockSpec(w.shape, lambda i, j: (0, 0), pipeline_mode=pl.Buffered(1)),
            row(d),
            mod.spec(2, d, whole, tm),
            pl.BlockSpec((1, d), lambda i, j: (0, 0)),
            mod.spec(4, d, whole, tm),
            mod.spec(3, d, whole, tm),
        ],
        out_specs=[row(d), row(d)],
        out_shape=[jax.ShapeDtypeStruct((m, d), F32), jax.ShapeDtypeStruct((m, d), BF16)],
        scratch_shapes=[pltpu.VMEM(w.shape, BF16)],
        compiler_params=_cparams(("arbitrary", "arbitrary")),
        name="out_proj",
    )(a1, a2, w, x, mod.arr, gain.reshape(1, d), mod.arr, mod.arr)


def _proj_kernel(h_ref, w_ref, o_ref):
    o_ref[...] = jnp.dot(h_ref[...], w_ref[...].astype(BF16), preferred_element_type=F32)


def _proj(h, w, cols, name):
    m, kdim = h.shape
    n = w.shape[1]
    return pl.pallas_call(
        _proj_kernel,
        grid=(n // cols,),
        in_specs=[pl.BlockSpec((m, kdim), lambda j: (0, 0)), pl.BlockSpec((kdim, cols), lambda j: (0, j))],
        out_specs=pl.BlockSpec((m, cols), lambda j: (0, j)),
        out_shape=jax.ShapeDtypeStruct((m, n), F32),
        compiler_params=_cparams(("arbitrary",)),
        name=name,
    )(h, w)


def _gelu(x):
    return 0.5 * x * (1.0 + lax.erf(x * (2.0 ** -0.5)))


def _ffn_up_kernel(tiles_per_batch, h_ref, wg_ref, wv_ref, hist_ref, cw_ref, cb_ref, act_ref, last_ref,
                   carry_ref):
    i = pl.program_id(0)
    j = pl.program_id(1)

    @pl.when((i % tiles_per_batch) == 0)
    def _():
        carry_ref[j] = hist_ref[...]

    h = h_ref[...]
    gate = jnp.dot(h, wg_ref[...].astype(BF16), preferred_element_type=F32)
    val = jnp.dot(h, wv_ref[...].astype(BF16), preferred_element_type=F32)
    tm = gate.shape[0]
    prev = carry_ref[j]
    row = _iota(gate.shape, 0)
    g1 = pltpu.roll(gate, 1, 0)
    g2 = pltpu.roll(gate, 2, 0)
    g1 = jnp.where(row == 0, prev[1:2], g1)
    g2 = jnp.where(row == 0, prev[0:1], jnp.where(row == 1, prev[1:2], g2))
    cw = cw_ref[...]
    conv = cb_ref[...] + g2 * cw[0:1] + g1 * cw[1:2] + gate * cw[2:3]
    act_ref[...] = (_gelu(conv) * val).astype(BF16)
    tail = gate[tm - 2:tm]
    carry_ref[j] = tail
    last_ref[...] = tail


def _ffn_up_fused(h, w_up, hist, conv_w, conv_b, rows_per_batch):
    m, d = h.shape
    f = w_up.shape[1] // 2
    tm = min(ROW_TILE, rows_per_batch)
    cols = UP_COLS
    nj = f // cols
    tiles_per_batch = rows_per_batch // tm
    act, tile_tails = pl.pallas_call(
        functools.partial(_ffn_up_kernel, tiles_per_batch),
        grid=(m // tm, nj),
        in_specs=[
            pl.BlockSpec((tm, d), lambda i, j: (i, 0)),
            pl.BlockSpec((d, cols), lambda i, j: (0, j)),
            pl.BlockSpec((d, cols), lambda i, j: (0, nj + j)),
            pl.BlockSpec((None, CONV_W - 1, cols), lambda i, j: (i // tiles_per_batch, 0, j)),
            pl.BlockSpec((CONV_W, cols), lambda i, j: (0, j)),
            pl.BlockSpec((1, cols), lambda i, j: (0, j)),
        ],
        out_specs=[
            pl.BlockSpec((tm, cols), lambda i, j: (i, j)),
            pl.BlockSpec((None, CONV_W - 1, cols), lambda i, j: (i, 0, j)),
        ],
        out_shape=[
            jax.ShapeDtypeStruct((m, f), BF16),
            jax.ShapeDtypeStruct((m // tm, CONV_W - 1, f), F32),
        ],
        scratch_shapes=[pltpu.VMEM((nj, CONV_W - 1, cols), F32)],
        compiler_params=_cparams(("arbitrary", "arbitrary")),
        name="ffn_up_prompt",
    )(h, w_up, w_up, hist, conv_w, conv_b.reshape(1, f))
    return act, tile_tails[tiles_per_batch - 1::tiles_per_batch]


def _act_sample_kernel(gate_ref, val_ref, hist_ref, cw_ref, cb_ref, act_ref):
    gate = gate_ref[...]
    hist = hist_ref[...]
    t = _iota(gate.shape, 1)
    g1 = jnp.where(t == 0, hist[:, 1:2], pltpu.roll(gate, 1, 1))
    g2 = jnp.where(t == 0, hist[:, 0:1], jnp.where(t == 1, hist[:, 1:2], pltpu.roll(gate, 2, 1)))
    cw = cw_ref[...]
    conv = cb_ref[...] + g2 * cw[0:1] + g1 * cw[1:2] + gate * cw[2:3]
    act_ref[...] = (_gelu(conv) * val_ref[...]).astype(BF16)


def _act_sample(hu, hist, conv_w, conv_b, nb, t):
    f = hu.shape[1] // 2
    cols = UP_COLS
    nj = f // cols
    hu3 = hu.reshape(nb, t, 2 * f)
    act = pl.pallas_call(
        _act_sample_kernel,
        grid=(nj,),
        in_specs=[
            pl.BlockSpec((nb, t, cols), lambda j: (0, 0, j)),
            pl.BlockSpec((nb, t, cols), lambda j: (0, 0, nj + j)),
            pl.BlockSpec((nb, CONV_W - 1, cols), lambda j: (0, 0, j)),
            pl.BlockSpec((CONV_W, cols), lambda j: (0, j)),
            pl.BlockSpec((1, cols), lambda j: (0, j)),
        ],
        out_specs=pl.BlockSpec((nb, t, cols), lambda j: (0, 0, j)),
        out_shape=jax.ShapeDtypeStruct((nb, t, f), BF16),
        compiler_params=_cparams(("arbitrary",)),
        name="ffn_act_sample",
    )(hu3, hu3, hist, conv_w, conv_b.reshape(1, f))
    return act.reshape(nb * t, f)


def _pair_rms(x, gain):
    x2 = x * x
    first = _iota(x.shape, 1) < HEAD_DIM
    s0 = jnp.sum(jnp.where(first, x2, 0.0), axis=-1, keepdims=True)
    s1 = jnp.sum(jnp.where(first, 0.0, x2), axis=-1, keepdims=True)
    ms = jnp.where(first, s0, s1) * (1.0 / HEAD_DIM)
    return x * lax.rsqrt(ms + RMS_EPS) * gain


def _bias_rows(table):
    h = table.shape[0]
    far = jnp.broadcast_to(table[:, 2 * REL_CLIP:], (h, ATT_REACH - REL_CLIP))
    mid = table[:, ::-1]
    near_len = BIAS_LEN - ATT_QROWS - (ATT_REACH - REL_CLIP) - (2 * REL_CLIP + 1)
    near = jnp.broadcast_to(table[:, 0:1], (h, near_len))
    wrap = jnp.broadcast_to(table[:, 2 * REL_CLIP:], (h, ATT_QROWS))
    return jnp.concatenate([far, mid, near, wrap], axis=1)


def _toeplitz(u_row, rows):
    return pltpu.roll(jnp.broadcast_to(u_row, (rows, BIAS_LEN)), 0, 1, stride=1, stride_axis=0)


def _attn_prompt_kernel(q_ref, k_ref, v_ref, qg_ref, kg_ref, u_ref, o_ref, kn_ref, vk_ref,
                        bias_ref, kwin_ref, vwin_ref):
    b = pl.program_id(1)
    qb = pl.program_id(2)
    shape = (ATT_QROWS, ATT_WIN)
    pairs = range(q_ref.shape[1] // PAIR)
    cols = [slice(p * PAIR, (p + 1) * PAIR) for p in pairs]
    chains = [(p, h) for p in pairs for h in range(2)]

    @pl.when((b == 0) & (qb == 0))
    def _():
        r = _iota(shape, 0)
        w = _iota(shape, 1)
        chunk_lo = _blk(r, CHUNK) * CHUNK
        in_band = (w >= chunk_lo) & (w < chunk_lo + (ATT_REACH + CHUNK))
        for i, (p, h) in enumerate(chains):
            bias = _toeplitz(u_ref[p, h:h + 1, :], ATT_QROWS)[:, :ATT_WIN]
            bias_ref[i] = jnp.where(in_band, bias * LOG2E, -jnp.inf)

    @pl.when(qb == 0)
    def _():
        kwin_ref[0:ATT_REACH] = jnp.zeros((ATT_REACH, kwin_ref.shape[1]), BF16)
        vwin_ref[0:ATT_REACH] = jnp.zeros((ATT_REACH, vwin_ref.shape[1]), BF16)

    @pl.when(qb > 0)
    def _():
        kwin_ref[0:ATT_REACH] = kwin_ref[ATT_QROWS:ATT_WIN]
        vwin_ref[0:ATT_REACH] = vwin_ref[ATT_QROWS:ATT_WIN]

    kn = [_pair_rms(k_ref[:, c], kg_ref[...]) for c in cols]
    for p in pairs:
        kn_ref[:, cols[p]] = kn[p]
        kwin_ref[ATT_REACH:ATT_WIN, cols[p]] = kn[p].astype(BF16)
    v_new = v_ref[...]
    vk_ref[...] = v_new
    vwin_ref[ATT_REACH:ATT_WIN] = v_new.astype(BF16)

    def attend(mask_start):
        qn = [_pair_rms(q_ref[:, c], qg_ref[...]) * (ATT_SCALE * LOG2E) for c in cols]
        kb = [kwin_ref[:, c] for c in cols]
        vb = [vwin_ref[:, c] for c in cols]
        first = _iota((ATT_QROWS, PAIR), 1) < HEAD_DIM
        first_w = _iota((ATT_WIN, PAIR), 1) < HEAD_DIM
        qh = [jnp.where(first, qn[p], 0.0) if h == 0 else jnp.where(first, 0.0, qn[p]) for p, h in chains]
        s = [_dot(qh[i], kb[p], NT) + bias_ref[i] for i, (p, h) in enumerate(chains)]
        if mask_start:
            started = _iota(shape, 1) >= ATT_REACH - qb * ATT_QROWS
            s = [jnp.where(started, x, -jnp.inf) for x in s]
        m = [jnp.max(x, axis=-1, keepdims=True) for x in s]
        pr = [jnp.exp2(x - mm) for x, mm in zip(s, m)]
        one = jnp.ones((), BF16)
        v_aug = [jnp.where(first_w, vb[p], one) if h == 0 else jnp.where(first_w, one, vb[p]) for p, h in chains]
        o = [_dot(pr[i], v_aug[i]) for i in range(len(chains))]
        o = [x / pltpu.roll(x, HEAD_DIM, 1) for x in o]
        for p in pairs:
            o_ref[:, cols[p]] = jnp.where(first, o[2 * p], o[2 * p + 1]).astype(BF16)

    full_window_from = ATT_REACH // ATT_QROWS
    pl.when(qb < full_window_from)(lambda: attend(True))
    pl.when(qb >= full_window_from)(lambda: attend(False))


def _attn_prompt(z3, q_gain, k_gain, u):
    nb, t, _ = z3.shape
    npairs = N_ATT_HEADS // 2
    npp = ATT_PAIRS
    width = npp * PAIR
    ngroups = npairs // npp
    nq = t // ATT_QROWS
    kcol = D_ATT // width
    vcol = 2 * D_ATT // width
    keep_blocks = ATT_REACH // ATT_QROWS
    blk = (None, ATT_QROWS, width)

    keep_spec = pl.BlockSpec(blk, lambda g, b, q: (b, jnp.maximum(q - (nq - keep_blocks), 0), g))
    att, kn, vk = pl.pallas_call(
        _attn_prompt_kernel,
        grid=(ngroups, nb, nq),
        in_specs=[
            pl.BlockSpec(blk, lambda g, b, q: (b, q, g)),
            pl.BlockSpec(blk, lambda g, b, q: (b, q, kcol + g)),
            pl.BlockSpec(blk, lambda g, b, q: (b, q, vcol + g)),
            pl.BlockSpec((1, PAIR), lambda g, b, q: (0, 0)),
            pl.BlockSpec((1, PAIR), lambda g, b, q: (0, 0)),
            pl.BlockSpec((npp, 2, BIAS_LEN), lambda g, b, q: (g, 0, 0)),
        ],
        out_specs=[
            pl.BlockSpec(blk, lambda g, b, q: (b, q, g)),
            keep_spec, keep_spec,
        ],
        out_shape=[
            jax.ShapeDtypeStruct((nb, t, D_ATT), BF16),
            jax.ShapeDtypeStruct((nb, ATT_REACH, D_ATT), F32),
            jax.ShapeDtypeStruct((nb, ATT_REACH, D_ATT), F32),
        ],
        scratch_shapes=[pltpu.VMEM((2 * npp, ATT_QROWS, ATT_WIN), F32),
                        pltpu.VMEM((ATT_WIN, width), BF16), pltpu.VMEM((ATT_WIN, width), BF16)],
        compiler_params=_cparams(("arbitrary", "arbitrary", "arbitrary")),
        name="attn_prompt",
    )(z3, z3, z3, jnp.tile(q_gain, 2).reshape(1, PAIR),
      jnp.tile(k_gain, 2).reshape(1, PAIR), u.reshape(npairs, 2, BIAS_LEN))
    return att, kn, vk


def _attn_sample_kernel(q_ref, k_ref, v_ref, kp_ref, vp_ref, qg_ref, kg_ref, u_ref, o_ref, kn_ref):
    t = q_ref.shape[0]
    reach = kp_ref.shape[0]
    first = _iota((t, PAIR), 1) < HEAD_DIM
    pairs = range(N_ATT_HEADS // 2)
    cols = [slice(p * PAIR, (p + 1) * PAIR) for p in pairs]
    chains = [(p, h) for p in pairs for h in range(2)]
    qn = [_pair_rms(q_ref[:, c], qg_ref[...]) * ATT_SCALE for c in cols]
    kn = [_pair_rms(k_ref[:, c], kg_ref[...]) for c in cols]
    for p in pairs:
        kn_ref[:, cols[p]] = kn[p]
    kpast = [kp_ref[:, c].astype(BF16) for c in cols]
    vpast = [vp_ref[:, c].astype(BF16) for c in cols]
    vnew = [v_ref[:, c].astype(BF16) for c in cols]
    qh = [jnp.where(first, qn[p], 0.0) if h == 0 else jnp.where(first, 0.0, qn[p]) for p, h in chains]
    bias = [_toeplitz(u_ref[p, h:h + 1, :], t) for p, h in chains]
    s_past = [_dot(qh[i], kpast[p], NT) + bias[i][:, :reach] for i, (p, h) in enumerate(chains)]
    s_new = [_dot(qh[i], kn[p], NT) + bias[i][:, reach:reach + t] for i, (p, h) in enumerate(chains)]
    m = [jnp.maximum(jnp.max(a, axis=-1, keepdims=True), jnp.max(b, axis=-1, keepdims=True))
         for a, b in zip(s_past, s_new)]
    p_past = [jnp.exp(a - mm) for a, mm in zip(s_past, m)]
    p_new = [jnp.exp(b - mm) for b, mm in zip(s_new, m)]
    l = [jnp.sum(a, axis=-1, keepdims=True) + jnp.sum(b, axis=-1, keepdims=True)
         for a, b in zip(p_past, p_new)]
    o = [(_dot(p_past[i], vpast[p]) + _dot(p_new[i], vnew[p])) / l[i] for i, (p, h) in enumerate(chains)]
    for p in pairs:
        o_ref[:, cols[p]] = jnp.where(first, o[2 * p], o[2 * p + 1]).astype(BF16)


def _attn_sample(z3, k_past, v_past, q_gain, k_gain, u):
    nb, t, _ = z3.shape
    reach = k_past.shape[1]
    npairs = N_ATT_HEADS // 2
    att, kn = pl.pallas_call(
        _attn_sample_kernel,
        grid=(nb,),
        in_specs=[
            pl.BlockSpec((None, t, D_ATT), lambda b: (b, 0, 0)),
            pl.BlockSpec((None, t, D_ATT), lambda b: (b, 0, 1)),
            pl.BlockSpec((None, t, D_ATT), lambda b: (b, 0, 2)),
            pl.BlockSpec((None, reach, D_ATT), lambda b: (b, 0, 0)),
            pl.BlockSpec((None, reach, D_ATT), lambda b: (b, 0, 0)),
            pl.BlockSpec((1, PAIR), lambda b: (0, 0)),
            pl.BlockSpec((1, PAIR), lambda b: (0, 0)),
            pl.BlockSpec((npairs, 2, BIAS_LEN), lambda b: (0, 0, 0)),
        ],
        out_specs=[
            pl.BlockSpec((None, t, D_ATT), lambda b: (b, 0, 0)),
            pl.BlockSpec((None, t, D_ATT), lambda b: (b, 0, 0)),
        ],
        out_shape=[
            jax.ShapeDtypeStruct((nb, t, D_ATT), BF16),
            jax.ShapeDtypeStruct((nb, t, D_ATT), F32),
        ],
        compiler_params=_cparams(("arbitrary",)),
        name="attn_sample",
    )(z3, z3, z3, k_past.reshape(nb, reach, D_ATT), v_past.reshape(nb, reach, D_ATT),
      jnp.tile(q_gain, 2).reshape(1, PAIR), jnp.tile(k_gain, 2).reshape(1, PAIR),
      u.reshape(npairs, 2, BIAS_LEN))
    return att, kn


def _tri_inverse(l_mats, c):
    n = l_mats[0].shape[0]
    eye = jnp.where(_iota((n, n), 0) == _iota((n, n), 1), 1.0, 0.0).astype(F32)
    a_s = [(eye + l).astype(BF16) for l in l_mats]
    t_s = [eye - l for l in l_mats]
    for _ in range(c.bit_length() - 2):
        r_s = [eye - _dot(a, t) for a, t in zip(a_s, t_s)]
        t_s = [t + _dot(t, r) for t, r in zip(t_s, r_s)]
    return t_s


def _rwkv_kernel(c, r_ref, k_ref, v_ref, lo_ref, sr_ref, sk_ref, sv_ref, slo_ref, s0_ref,
                 mur_ref, muk_ref, muv_ref, mulo_ref, w0_ref, a0_ref, kkg_ref, ka_ref, rk_ref,
                 lnw_ref, lnb_ref, w2_ref, a2_ref, g2_ref,
                 o_ref, sT_ref, s_ref, cr_ref, ck_ref, cv_ref, clo_ref):
    tb = pl.program_id(2)
    rows, width = r_ref.shape
    npp = width // PAIR
    nchunks = rows // c
    h0 = _iota((rows, PAIR), 1) < HEAD_DIM
    bd = _blk(_iota((PAIR, PAIR), 0), HEAD_DIM) == _blk(_iota((PAIR, PAIR), 1), HEAD_DIM)

    @pl.when(tb == 0)
    def _():
        s_ref[...] = jnp.zeros(s_ref.shape, F32)
        for pp in range(npp):
            s_ref[pp, 0:HEAD_DIM, 0:HEAD_DIM] = s0_ref[2 * pp]
            s_ref[pp, HEAD_DIM:PAIR, HEAD_DIM:PAIR] = s0_ref[2 * pp + 1]
        cr_ref[...] = sr_ref[...]
        ck_ref[...] = sk_ref[...]
        cv_ref[...] = sv_ref[...]
        clo_ref[...] = slo_ref[...]

    def shifted(x_ref, carry_ref, mu_ref):
        x = x_ref[...]
        prev = jnp.where(_iota(x.shape, 0) == 0, carry_ref[...], pltpu.roll(x, 1, 0))
        carry_ref[...] = x[rows - 1:rows]
        return x + (prev - x) * mu_ref[...]

    r = shifted(r_ref, cr_ref, mur_ref)
    k = shifted(k_ref, ck_ref, muk_ref)
    v = shifted(v_ref, cv_ref, muv_ref)
    lo = shifted(lo_ref, clo_ref, mulo_ref)

    zeros_w = jnp.zeros((RANK_W, width), F32)
    w2p = jnp.concatenate([w2_ref[...], zeros_w], axis=0)
    a2p = jnp.concatenate([zeros_w, a2_ref[...]], axis=0)
    lo_wa = lo[:, 0:RANK_W + RANK_A]
    u = w0_ref[...] + _dot(jnp.tanh(lo_wa), w2p)
    lw = -jnp.exp(-_softplus(-u) - 0.5)
    a = _sigmoid(a0_ref[...] + _dot(lo_wa, a2p))
    g = _dot(_sigmoid(lo[:, RANK_W + RANK_A:]), g2_ref[...])

    kk = k * kkg_ref[...]
    kk = kk / jnp.maximum(jnp.sqrt(_head_sums(kk * kk)), 1e-12)
    k = k * (1.0 + (a - 1.0) * ka_ref[...])
    b = kk * a
    bonus = _head_sums(r * k * rk_ref[...]) * v

    tr = _iota((rows, rows), 0)
    tc = _iota((rows, rows), 1)
    same_chunk = _blk(tr, c) == _blk(tc, c)
    strict = same_chunk & (tr > tc)
    incl = same_chunk & (tr >= tc)
    lw_hi, lw_lo = _split2(lw)
    tril_ones = jnp.where(incl, 1.0, 0.0).astype(BF16)
    lp = jnp.dot(tril_ones, lw_hi, preferred_element_type=F32) + \
        jnp.dot(tril_ones, lw_lo, preferred_element_type=F32)
    lp_end = jnp.concatenate(
        [jnp.broadcast_to(lp[(ci + 1) * c - 1:(ci + 1) * c], (c, width)) for ci in range(nchunks)], axis=0)

    alpha_w = kk * jnp.exp(lp - lw)
    inv_p = jnp.exp(-lp)
    beta_w = b * inv_p
    kappa_w = k * inv_p
    rho_w = r * jnp.exp(lp)
    to_end = jnp.exp(lp_end - lp)
    beta_ew = b * to_end
    kappa_ew = k * to_end
    decay_end_w = jnp.exp(lp_end)

    wide = (rows, nchunks * PAIR)
    col_chunk = _blk(_iota(wide, 1), PAIR) == _blk(_iota(wide, 0), c)
    spread = lambda m: jnp.where(col_chunk, jnp.tile(m, (1, nchunks)), 0.0)
    eye_p = _iota((PAIR, PAIR), 0) == _iota((PAIR, PAIR), 1)

    pairs = range(npp)
    lanes = [slice(pp * PAIR, (pp + 1) * PAIR) for pp in pairs]
    alpha = [alpha_w[:, l] for l in lanes]
    rho = [rho_w[:, l] for l in lanes]
    vv = [v[:, l] for l in lanes]
    head_mask = [h0, jnp.logical_not(h0)]
    bk = [jnp.concatenate([beta_w[:, l], kappa_w[:, l]], axis=0).astype(BF16) for l in lanes]
    prod = [[_dot(jnp.concatenate([jnp.where(hm, alpha[pp], 0.0), jnp.where(hm, rho[pp], 0.0)], axis=0),
                  bk[pp], NT) for hm in head_mask] for pp in pairs]
    t_inv = _tri_inverse([jnp.where(strict, prod[pp][h][:rows, :rows], 0.0) for pp in pairs for h in range(2)], c)
    x = [[_dot(jnp.where(strict, prod[pp][h][:rows, rows:], 0.0), vv[pp]) for h in range(2)] for pp in pairs]
    ws = [[_dot(t_inv[2 * pp + h], jnp.concatenate([alpha[pp], x[pp][h]], axis=1)) for h in range(2)]
          for pp in pairs]
    w12 = [jnp.concatenate([jnp.where(h0, ws[pp][0][:, :PAIR], ws[pp][1][:, :PAIR]),
                            jnp.where(h0, ws[pp][0][:, PAIR:], ws[pp][1][:, PAIR:])], axis=1) for pp in pairs]
    q = [[_dot(jnp.where(incl, prod[pp][h][rows:, :rows], 0.0), w12[pp]) for h in range(2)] for pp in pairs]
    qk = [[_dot(jnp.where(incl, prod[pp][h][rows:, rows:], 0.0), vv[pp]) for h in range(2)] for pp in pairs]
    rp = [rho[pp] - jnp.where(h0, q[pp][0][:, :PAIR], q[pp][1][:, :PAIR]) for pp in pairs]
    y0 = [jnp.where(h0, qk[pp][0] - q[pp][0][:, PAIR:], qk[pp][1] - q[pp][1][:, PAIR:]) for pp in pairs]
    wtb = [_dot(w12[pp], spread(beta_ew[:, lanes[pp]]), TN) for pp in pairs]
    vtk = [_dot(vv[pp], spread(kappa_ew[:, lanes[pp]]), TN) for pp in pairs]

    s_cur = [s_ref[pp] for pp in pairs]
    ys = [[] for _ in pairs]
    for ci in range(nchunks):
        sl = slice(ci * c, (ci + 1) * c)
        cols = slice(ci * PAIR, (ci + 1) * PAIR)
        for pp in pairs:
            decay_end = decay_end_w[ci * c:ci * c + 1, lanes[pp]]
            gmat = jnp.where(eye_p, jnp.broadcast_to(decay_end, (PAIR, PAIR)), 0.0) \
                - jnp.where(bd, wtb[pp][:PAIR, cols], 0.0)
            hmat = jnp.where(bd, vtk[pp][:, cols] - wtb[pp][PAIR:, cols], 0.0)
            ys[pp].append(_dot(rp[pp][sl], s_cur[pp], NT) + y0[pp][sl])
            s_cur[pp] = _dot(s_cur[pp], gmat) + hmat
    for pp in pairs:
        s_ref[pp] = s_cur[pp]
    y_pairs = [ys[pp][0] if nchunks == 1 else jnp.concatenate(ys[pp], axis=0) for pp in pairs]

    @pl.when(tb == pl.num_programs(2) - 1)
    def _():
        for pp in range(npp):
            sT_ref[2 * pp] = s_ref[pp, 0:HEAD_DIM, 0:HEAD_DIM]
            sT_ref[2 * pp + 1] = s_ref[pp, HEAD_DIM:PAIR, HEAD_DIM:PAIR]

    y = y_pairs[0] if npp == 1 else jnp.concatenate(y_pairs, axis=1)
    mu = _head_sums(y) * (1.0 / HEAD_DIM)
    d = y - mu
    var = _head_sums(d * d) * (1.0 / HEAD_DIM)
    yn = d * lax.rsqrt(var + GN_EPS) * lnw_ref[...] + lnb_ref[...]
    o_ref[...] = ((yn + bonus) * g).astype(BF16)


def _rwkv(z3, shift_prev, s0, p, rows, c, npp):
    nb, t, _ = z3.shape
    width = npp * PAIR
    ngroups = D_RWKV // width
    col0 = 3 * D_ATT // width
    lo_blk = (3 * D_ATT + 3 * D_RWKV) // D_LORA
    sp = shift_prev.reshape(nb, 1, D_SHIFT)

    def zspec(off):
        return pl.BlockSpec((None, rows, width), lambda b, q, s: (b, s, col0 + off * ngroups + q))

    def sspec(off):
        return pl.BlockSpec((None, 1, width), lambda b, q, s: (b, 0, off * ngroups + q))

    def vec(off=0):
        return pl.BlockSpec((1, width), lambda b, q, s: (0, off * ngroups + q))

    def row2(x):
        return x.reshape(1, -1)

    out, s_fin = pl.pallas_call(
        functools.partial(_rwkv_kernel, c),
        grid=(nb, ngroups, t // rows),
        in_specs=[
            zspec(0), zspec(1), zspec(2),
            pl.BlockSpec((None, rows, D_LORA), lambda b, q, s: (b, s, lo_blk)),
            sspec(0), sspec(1), sspec(2),
            pl.BlockSpec((None, 1, D_LORA), lambda b, q, s: (b, 0, 3 * D_RWKV // D_LORA)),
            pl.BlockSpec((None, 2 * npp, HEAD_DIM, HEAD_DIM), lambda b, q, s: (b, q, 0, 0)),
            vec(0), vec(1), vec(2),
            pl.BlockSpec((1, D_LORA), lambda b, q, s: (0, 3 * D_RWKV // D_LORA)),
            vec(), vec(), vec(), vec(), vec(), vec(), vec(),
            pl.BlockSpec((RANK_W, width), lambda b, q, s: (0, q)),
            pl.BlockSpec((RANK_A, width), lambda b, q, s: (0, q)),
            pl.BlockSpec((RANK_G, width), lambda b, q, s: (0, q)),
        ],
        out_specs=[
            pl.BlockSpec((None, rows, width), lambda b, q, s: (b, s, q)),
            pl.BlockSpec((None, 2 * npp, HEAD_DIM, HEAD_DIM), lambda b, q, s: (b, q, 0, 0)),
        ],
        out_shape=[
            jax.ShapeDtypeStruct((nb, t, D_RWKV), BF16),
            jax.ShapeDtypeStruct((nb, N_RWKV_HEADS, HEAD_DIM, HEAD_DIM), F32),
        ],
        scratch_shapes=[
            pltpu.VMEM((npp, PAIR, PAIR), F32),
            pltpu.VMEM((1, width), F32), pltpu.VMEM((1, width), F32), pltpu.VMEM((1, width), F32),
            pltpu.VMEM((1, D_LORA), F32),
        ],
        compiler_params=_cparams(("arbitrary", "arbitrary", "arbitrary")),
        name="rwkv7_mix",
    )(z3, z3, z3, z3, sp, sp, sp, sp, s0,
      row2(p['mu_shift']), row2(p['mu_shift']), row2(p['mu_shift']), row2(p['mu_shift']),
      row2(p['w0']), row2(p['a0']), row2(p['k_k']), row2(p['k_a']), row2(p['r_k']),
      row2(p['ln_x_w']), row2(p['ln_x_b']), p['w2'], p['a2'], p['g2'])
    return out, s_fin


def _layer(x3, mod, p, u, k_past, v_past, s0, shift_prev, conv_prev):
    nb, t, d = x3.shape
    m = nb * t
    x = x3.reshape(m, d)
    if k_past is None:
        z = _norm_proj(x, p['norm_att_g'], mod, 1, 0, p['w_in'], IN_COLS, "in_proj", min(IN_ROW_TILE, t))
    else:
        z = _norm_proj(x, p['norm_att_g'], mod, 1, 0, p['w_in'], IN_COLS_SAMPLE, "in_proj", m)
    z3 = z.reshape(nb, t, D_IN)
    if k_past is None:
        att, k_keep, v_keep = _attn_prompt(z3, p['q_norm_g'], p['k_norm_g'], u)
        rw, s_fin = _rwkv(z3, shift_prev, s0, p, RWKV_ROWS, CHUNK, RWKV_PAIRS_PROMPT)
    else:
        att, k_keep = _attn_sample(z3, k_past, v_past, p['q_norm_g'], p['k_norm_g'], u)
        v_keep = z3[:, :, 2 * D_ATT:3 * D_ATT]
        rw, s_fin = _rwkv(z3, shift_prev, s0, p, t, t, RWKV_PAIRS_SAMPLE)
    shift_last = z3[:, t - 1, 3 * D_ATT:]
    x1, h2 = _out_proj(att.reshape(m, D_ATT), rw.reshape(m, D_RWKV), p['w_out'], x, mod,
                       p['norm_ffn_g'], OUT_ROW_TILE)
    if k_past is None:
        act, conv_last = _ffn_up_fused(h2, p['w_up'], conv_prev, p['dw_conv'], p['dw_bias'], t)
    else:
        hu = _proj(h2, p['w_up'], UP_COLS, "ffn_up_sample")
        f = hu.shape[1] // 2
        act = _act_sample(hu, conv_prev, p['dw_conv'], p['dw_bias'], nb, t)
        conv_last = hu.reshape(nb, t, 2 * f)[:, t - (CONV_W - 1):, :f]
    x2 = _proj_resid([act], p['w_down'], x1, mod, 5, DOWN_COLS, "ffn_down")
    heads = lambda a: a.reshape(nb, a.shape[1], N_ATT_HEADS, HEAD_DIM)
    return x2.reshape(nb, t, d), heads(k_keep), heads(v_keep), s_fin, shift_last, conv_last


def kernel(x_prompt, x_sample, c_prompt, c_sample, cache_att_k, cache_att_v, state_rwkv, state_shift, state_ffn_conv, norm_att_g, norm_ffn_g, w_ada, b_ada, w_in, q_norm_g, k_norm_g, rel_bias, mu_shift, w0, w2, a0, a2, g2, k_k, k_a, r_k, ln_x_w, ln_x_b, w_out, w_up, dw_conv, dw_bias, w_down):
    depth = w_in.shape[0]
    bp, tp, d = x_prompt.shape
    bs, ts, _ = x_sample.shape
    d_ff = w_down.shape[1]
    hp, hs = x_prompt, x_sample
    outs_p = [[] for _ in range(5)]
    outs_s = [[] for _ in range(5)]
    for l in range(depth):
        p = dict(norm_att_g=norm_att_g[l], norm_ffn_g=norm_ffn_g[l], w_in=w_in[l], q_norm_g=q_norm_g[l],
                 k_norm_g=k_norm_g[l], mu_shift=mu_shift[l], w0=w0[l], w2=w2[l], a0=a0[l], a2=a2[l],
                 g2=g2[l], k_k=k_k[l], k_a=k_a[l], r_k=r_k[l], ln_x_w=ln_x_w[l], ln_x_b=ln_x_b[l],
                 w_out=w_out[l], w_up=w_up[l], dw_conv=dw_conv[l], dw_bias=dw_bias[l], w_down=w_down[l])
        n_c = bp + bs
        pad = (-n_c) % 8
        c_all = jnp.concatenate([c_prompt, c_sample, jnp.zeros((pad, d), F32)], axis=0)
        mod = _ada(c_all, w_ada[l], b_ada[l])
        mod_p = _Mod(mod.reshape(n_c + pad, 6, 1, d), False, rows_per_batch=tp)
        mod_s = _Mod(jnp.repeat(mod[bp:bp + bs], ts, axis=0), True)
        u = _bias_rows(rel_bias[l])

        res = _layer(hp, mod_p, p, u, None, None,
                     jnp.zeros((bp, N_RWKV_HEADS, HEAD_DIM, HEAD_DIM), F32),
                     jnp.zeros((bp, D_SHIFT), F32),
                     jnp.zeros((bp, CONV_W - 1, d_ff), F32))
        hp = res[0]
        for lst, val in zip(outs_p, res[1:]):
            lst.append(val)
        res = _layer(hs, mod_s, p, u, cache_att_k[l], cache_att_v[l], state_rwkv[l],
                     state_shift[l], state_ffn_conv[l])
        hs = res[0]
        for lst, val in zip(outs_s, res[1:]):
            lst.append(val)
    st = lambda lst: jnp.stack(lst)
    return (hp, hs, *[st(x) for x in outs_p], *[st(x) for x in outs_s])
```

```python
import functools

import jax
import jax.numpy as jnp
from jax import lax
from jax.experimental import pallas as pl
from jax.experimental.pallas import tpu as pltpu

F32 = jnp.float32
BF16 = jnp.bfloat16

CHUNK = 64
N_PREV_CHUNKS = 8
ATT_REACH = N_PREV_CHUNKS * CHUNK
HEAD_DIM = 64
N_ATT_HEADS = 16
N_RWKV_HEADS = 16
D_ATT = N_ATT_HEADS * HEAD_DIM
D_RWKV = N_RWKV_HEADS * HEAD_DIM
REL_CLIP = 128
RANK_W = 64
RANK_A = 64
RANK_G = 128
D_LORA = RANK_W + RANK_A + RANK_G
D_SHIFT = 3 * D_RWKV + D_LORA
D_IN = 3 * D_ATT + D_SHIFT
CONV_W = 3
RMS_EPS = 1e-6
GN_EPS = 64e-5
ATT_SCALE = HEAD_DIM ** -0.5
LOG2E = 1.4426950408889634

LANES = 128
PAIR = 2 * HEAD_DIM
MXU_DIM = 256
VMEM_LIMIT = 60 * 1024 * 1024

ROW_TILE = 1024
IN_ROW_TILE = 2048
ADA_COLS = 512
IN_COLS = 256
OUT_ROW_TILE = 512
UP_COLS = 512
DOWN_COLS = 256
ATT_QROWS = 256
ATT_WIN = ATT_QROWS + ATT_REACH
ATT_PAIRS = 4
BIAS_LEN = 1024
RWKV_ROWS = 256
RWKV_PAIRS_PROMPT = 8
RWKV_PAIRS_SAMPLE = 8


def _cparams(sem):
    return pltpu.CompilerParams(dimension_semantics=sem, vmem_limit_bytes=VMEM_LIMIT)


def _dot(a, b, dims=(((1,), (0,)), ((), ()))):
    return lax.dot_general(a.astype(BF16), b.astype(BF16), dims, preferred_element_type=F32)


def _split2(x):
    hi = x.astype(BF16)
    lo = (x - hi.astype(F32)).astype(BF16)
    return hi, lo


NT = (((1,), (1,)), ((), ()))
TN = (((0,), (0,)), ((), ()))


def _iota(shape, dim):
    return lax.broadcasted_iota(jnp.int32, shape, dim)


def _blk(x, size):
    return jnp.right_shift(x, size.bit_length() - 1)


def _head_ones(n):
    r = _blk(_iota((n, n), 0), HEAD_DIM)
    c = _blk(_iota((n, n), 1), HEAD_DIM)
    return jnp.where(r == c, 1.0, 0.0).astype(BF16)


def _head_sums(x):
    lanes = x.shape[1]
    group = min(lanes, MXU_DIM)
    ones = _head_ones(group)
    parts = [_dot(x[:, i:i + group], ones) for i in range(0, lanes, group)]
    return parts[0] if len(parts) == 1 else jnp.concatenate(parts, axis=1)


def _sigmoid(x):
    return 1.0 / (1.0 + jnp.exp(-x))


def _softplus(x):
    return jnp.maximum(x, 0.0) + jnp.log(1.0 + jnp.exp(-jnp.abs(x)))


def _ada_kernel(c_ref, w_ref, b_ref, o_ref):
    c = c_ref[...]
    s = c * _sigmoid(c)
    o_ref[...] = _dot(s, w_ref[...]) + b_ref[...]


def _ada(c_all, w_ada, b_ada):
    rows, d = c_all.shape
    n = w_ada.shape[1]
    return pl.pallas_call(
        _ada_kernel,
        grid=(n // ADA_COLS,),
        in_specs=[
            pl.BlockSpec((rows, d), lambda j: (0, 0)),
            pl.BlockSpec((d, ADA_COLS), lambda j: (0, j)),
            pl.BlockSpec((1, ADA_COLS), lambda j: (0, j)),
        ],
        out_specs=pl.BlockSpec((rows, ADA_COLS), lambda j: (0, j)),
        out_shape=jax.ShapeDtypeStruct((rows, n), F32),
        compiler_params=_cparams(("arbitrary",)),
        name="ada_mod",
    )(c_all, w_ada, b_ada.reshape(1, n))


class _Mod:
    def __init__(self, arr, per_row, rows_per_batch=None):
        self.arr = arr
        self.per_row = per_row
        self.rows_per_batch = rows_per_batch

    def spec(self, idx, cols, col_of, row_tile):
        if self.per_row:
            m = self.arr.shape[0]
            d = self.arr.shape[1] // 6
            nblk = d // cols
            return pl.BlockSpec((m, cols), lambda i, j: (0, idx * nblk + col_of(j)))
        tiles_per_batch = self.rows_per_batch // row_tile
        return pl.BlockSpec((None, None, 1, cols),
                            lambda i, j: (i // tiles_per_batch, idx, 0, col_of(j)))

    def rider_spec(self, idx, cols, nj):
        m = self.arr.shape[0]
        nblk = self.arr.shape[1] // 6 // cols
        return pl.BlockSpec((m, cols), lambda i, j: (0, idx * nblk + _rider_col(i, j, nj)))


def _rider_col(i, j, nj):
    return jnp.where(i == 0, j, nj - 1)


NORM_ROWS = 128


def _store_normed(h_ref, x_ref, g_ref, sc_ref, sh_ref):
    rows = x_ref.shape[0]
    step = min(NORM_ROWS, rows)
    per_row = sc_ref.shape[0] == rows

    def body(r, carry):
        sl = pl.ds(pl.multiple_of(r * step, step), step)
        x = x_ref[sl, :]
        ms = jnp.mean(x * x, axis=-1, keepdims=True)
        xn = x * lax.rsqrt(ms + RMS_EPS) * g_ref[...]
        sc = sc_ref[sl, :] if per_row else sc_ref[...]
        sh = sh_ref[sl, :] if per_row else sh_ref[...]
        h_ref[sl, :] = (xn * (1.0 + sc) + sh).astype(BF16)
        return carry

    lax.fori_loop(0, rows // step, body, 0)


def _norm_proj_kernel(x_ref, g_ref, sc_ref, sh_ref, w_ref, xs_ref, scs_ref, shs_ref, o_ref, os_ref,
                      h_ref, hs_ref):
    i = pl.program_id(0)
    j = pl.program_id(1)

    @pl.when(j == 0)
    def _():
        _store_normed(h_ref, x_ref, g_ref, sc_ref, sh_ref)

    @pl.when((i == 0) & (j == 0))
    def _():
        _store_normed(hs_ref, xs_ref, g_ref, scs_ref, shs_ref)

    @pl.when(i == 0)
    def _():
        os_ref[...] = jnp.dot(hs_ref[...], w_ref[...].astype(BF16), preferred_element_type=F32)

    o_ref[...] = jnp.dot(h_ref[...], w_ref[...].astype(BF16), preferred_element_type=F32)


def _norm_proj(x, xs, gain, mod, mod_s, sc_idx, sh_idx, w, cols, name, row_tile):
    m, d = x.shape
    ms = xs.shape[0]
    n = w.shape[1]
    tm = min(row_tile, m)
    nj = n // cols
    whole = lambda j: 0
    x_mode = dict(pipeline_mode=pl.Buffered(1)) if tm > ROW_TILE else {}
    return pl.pallas_call(
        _norm_proj_kernel,
        grid=(m // tm, nj),
        in_specs=[
            pl.BlockSpec((tm, d), lambda i, j: (i, 0), **x_mode),
            pl.BlockSpec((1, d), lambda i, j: (0, 0)),
            mod.spec(sc_idx, d, whole, tm),
            mod.spec(sh_idx, d, whole, tm),
            pl.BlockSpec((d, cols), lambda i, j: (0, j)),
            pl.BlockSpec((ms, d), lambda i, j: (0, 0)),
            mod_s.spec(sc_idx, d, whole, ms),
            mod_s.spec(sh_idx, d, whole, ms),
        ],
        out_specs=[
            pl.BlockSpec((tm, cols), lambda i, j: (i, j)),
            pl.BlockSpec((ms, cols), lambda i, j: (0, _rider_col(i, j, nj))),
        ],
        out_shape=[jax.ShapeDtypeStruct((m, n), F32), jax.ShapeDtypeStruct((ms, n), F32)],
        scratch_shapes=[pltpu.VMEM((tm, d), BF16), pltpu.VMEM((ms, d), BF16)],
        compiler_params=_cparams(("arbitrary", "arbitrary")),
        name=name,
    )(x, gain.reshape(1, d), mod.arr, mod.arr, w, xs, mod_s.arr, mod_s.arr)


def _proj_resid_kernel(a_ref, w_ref, x_ref, g_ref, as_ref, xs_ref, gs_ref, o_ref, os_ref):
    @pl.when(pl.program_id(0) == 0)
    def _():
        acc = jnp.dot(as_ref[...], w_ref[...].astype(BF16), preferred_element_type=F32)
        os_ref[...] = xs_ref[...] + gs_ref[...] * acc

    acc = jnp.dot(a_ref[...], w_ref[...].astype(BF16), preferred_element_type=F32)
    o_ref[...] = x_ref[...] + g_ref[...] * acc


def _proj_resid(a, a_s, w, x, xs, mod, mod_s, g_idx, cols, name):
    m, n = x.shape
    ms = xs.shape[0]
    kdim = a.shape[1]
    tm = min(ROW_TILE, m)
    nj = n // cols
    rider_block = pl.BlockSpec((ms, cols), lambda i, j: (0, _rider_col(i, j, nj)))
    return pl.pallas_call(
        _proj_resid_kernel,
        grid=(m // tm, nj),
        in_specs=[
            pl.BlockSpec((tm, kdim), lambda i, j: (i, 0)),
            pl.BlockSpec((kdim, cols), lambda i, j: (0, j)),
            pl.BlockSpec((tm, cols), lambda i, j: (i, j)),
            mod.spec(g_idx, cols, lambda j: j, tm),
            pl.BlockSpec((ms, kdim), lambda i, j: (0, 0)),
            rider_block,
            mod_s.rider_spec(g_idx, cols, nj),
        ],
        out_specs=[pl.BlockSpec((tm, cols), lambda i, j: (i, j)), rider_block],
        out_shape=[jax.ShapeDtypeStruct((m, n), F32), jax.ShapeDtypeStruct((ms, n), F32)],
        compiler_params=_cparams(("arbitrary", "arbitrary")),
        name=name,
    )(a, w, x, mod.arr, a_s, xs, mod_s.arr)


def _out_proj_kernel(a1_ref, a2_ref, w_ref, x_ref, g_ref, gain_ref, sc_ref, sh_ref, o_ref, h_ref, wb_ref):
    @pl.when(pl.program_id(0) == 0)
    def _():
        step = MXU_DIM

        def cast_rows(r, carry):
            sl = pl.ds(pl.multiple_of(r * step, step), step)
            wb_ref[sl, :] = w_ref[sl, :].astype(BF16)
            return carry

        lax.fori_loop(0, w_ref.shape[0] // step, cast_rows, 0)

    k1 = a1_ref.shape[1]
    rows = x_ref.shape[0]
    step = min(MXU_DIM, rows)
    per_row = sc_ref.shape[0] == rows
    pieces = [slice(r0, r0 + step) for r0 in range(0, rows, step)]
    x1s = []
    for sl in pieces:
        acc = jnp.dot(a1_ref[sl, :], wb_ref[0:k1], preferred_element_type=F32) \
            + jnp.dot(a2_ref[sl, :], wb_ref[k1:], preferred_element_type=F32)
        g = g_ref[sl, :] if per_row else g_ref[...]
        x1 = x_ref[sl, :] + g * acc
        o_ref[sl, :] = x1
        x1s.append(x1)
    for sl, x1 in zip(pieces, x1s):
        ms = jnp.mean(x1 * x1, axis=-1, keepdims=True)
        xn = x1 * lax.rsqrt(ms + RMS_EPS) * gain_ref[...]
        sc = sc_ref[sl, :] if per_row else sc_ref[...]
        sh = sh_ref[sl, :] if per_row else sh_ref[...]
        h_ref[sl, :] = (xn * (1.0 + sc) + sh).astype(BF16)


def _out_proj(a1, a2, w, x, mod, gain, row_tile):
    m, d = x.shape
    tm = min(row_tile, m)
    whole = lambda j: 0
    row = lambda kdim: pl.BlockSpec((tm, kdim), lambda i, j: (i, 0))
    return pl.pallas_call(
        _out_proj_kernel,
        grid=(m // tm, 1),
        in_specs=[
            row(a1.shape[1]), row(a2.shape[1]),
            pl.BlockSpec(w.shape, lambda i, j: (0, 0), pipeline_mode=pl.Buffered(1)),
            row(d),
            mod.spec(2, d, whole, tm),
            pl.BlockSpec((1, d), lambda i, j: (0, 0)),
            mod.spec(4, d, whole, tm),
            mod.spec(3, d, whole, tm),
        ],
        out_specs=[row(d), row(d)],
        out_shape=[jax.ShapeDtypeStruct((m, d), F32), jax.ShapeDtypeStruct((m, d), BF16)],
        scratch_shapes=[pltpu.VMEM(w.shape, BF16)],
        compiler_params=_cparams(("arbitrary", "arbitrary")),
        name="out_proj",
    )(a1, a2, w, x, mod.arr, gain.reshape(1, d), mod.arr, mod.arr)


def _gelu(x):
    return 0.5 * x * (1.0 + lax.erf(x * (2.0 ** -0.5)))


def _ffn_up_kernel(tiles_per_batch, h_ref, wg_ref, wv_ref, hist_ref, cw_ref, cb_ref, hs_ref,
                   act_ref, last_ref, gs_ref, vs_ref, carry_ref):
    i = pl.program_id(0)
    j = pl.program_id(1)

    @pl.when(i == 0)
    def _():
        hs = hs_ref[...]
        gs_ref[...] = jnp.dot(hs, wg_ref[...].astype(BF16), preferred_element_type=F32)
        vs_ref[...] = jnp.dot(hs, wv_ref[...].astype(BF16), preferred_element_type=F32)

    @pl.when((i % tiles_per_batch) == 0)
    def _():
        carry_ref[j] = hist_ref[...]

    h = h_ref[...]
    gate = jnp.dot(h, wg_ref[...].astype(BF16), preferred_element_type=F32)
    val = jnp.dot(h, wv_ref[...].astype(BF16), preferred_element_type=F32)
    tm = gate.shape[0]
    prev = carry_ref[j]
    row = _iota(gate.shape, 0)
    g1 = pltpu.roll(gate, 1, 0)
    g2 = pltpu.roll(gate, 2, 0)
    g1 = jnp.where(row == 0, prev[1:2], g1)
    g2 = jnp.where(row == 0, prev[0:1], jnp.where(row == 1, prev[1:2], g2))
    cw = cw_ref[...]
    conv = cb_ref[...] + g2 * cw[0:1] + g1 * cw[1:2] + gate * cw[2:3]
    act_ref[...] = (_gelu(conv) * val).astype(BF16)
    tail = gate[tm - 2:tm]
    carry_ref[j] = tail
    last_ref[...] = tail


def _ffn_up_fused(h, hs, w_up, hist, conv_w, conv_b, rows_per_batch):
    m, d = h.shape
    ms = hs.shape[0]
    f = w_up.shape[1] // 2
    tm = min(ROW_TILE, rows_per_batch)
    cols = UP_COLS
    nj = f // cols
    tiles_per_batch = rows_per_batch // tm
    rider_block = pl.BlockSpec((ms, cols), lambda i, j: (0, _rider_col(i, j, nj)))
    act, tile_tails, gate_s, val_s = pl.pallas_call(
        functools.partial(_ffn_up_kernel, tiles_per_batch),
        grid=(m // tm, nj),
        in_specs=[
            pl.BlockSpec((tm, d), lambda i, j: (i, 0)),
            pl.BlockSpec((d, cols), lambda i, j: (0, j)),
            pl.BlockSpec((d, cols), lambda i, j: (0, nj + j)),
            pl.BlockSpec((None, CONV_W - 1, cols), lambda i, j: (i // tiles_per_batch, 0, j)),
            pl.BlockSpec((CONV_W, cols), lambda i, j: (0, j)),
            pl.BlockSpec((1, cols), lambda i, j: (0, j)),
            pl.BlockSpec((ms, d), lambda i, j: (0, 0)),
        ],
        out_specs=[
            pl.BlockSpec((tm, cols), lambda i, j: (i, j)),
            pl.BlockSpec((None, CONV_W - 1, cols), lambda i, j: (i, 0, j)),
            rider_block, rider_block,
        ],
        out_shape=[
            jax.ShapeDtypeStruct((m, f), BF16),
            jax.ShapeDtypeStruct((m // tm, CONV_W - 1, f), F32),
            jax.ShapeDtypeStruct((ms, f), F32),
            jax.ShapeDtypeStruct((ms, f), F32),
        ],
        scratch_shapes=[pltpu.VMEM((nj, CONV_W - 1, cols), F32)],
        compiler_params=_cparams(("arbitrary", "arbitrary")),
        name="ffn_up_prompt",
    )(h, w_up, w_up, hist, conv_w, conv_b.reshape(1, f), hs)
    return act, tile_tails[tiles_per_batch - 1::tiles_per_batch], gate_s, val_s


def _act_sample_kernel(gate_ref, val_ref, hist_ref, cw_ref, cb_ref, act_ref):
    gate = gate_ref[...]
    hist = hist_ref[...]
    t = _iota(gate.shape, 1)
    g1 = jnp.where(t == 0, hist[:, 1:2], pltpu.roll(gate, 1, 1))
    g2 = jnp.where(t == 0, hist[:, 0:1], jnp.where(t == 1, hist[:, 1:2], pltpu.roll(gate, 2, 1)))
    cw = cw_ref[...]
    conv = cb_ref[...] + g2 * cw[0:1] + g1 * cw[1:2] + gate * cw[2:3]
    act_ref[...] = (_gelu(conv) * val_ref[...]).astype(BF16)


def _act_sample(gate, val, hist, conv_w, conv_b, nb, t):
    f = gate.shape[1]
    cols = UP_COLS
    nj = f // cols
    gate3 = gate.reshape(nb, t, f)
    val3 = val.reshape(nb, t, f)
    act = pl.pallas_call(
        _act_sample_kernel,
        grid=(nj,),
        in_specs=[
            pl.BlockSpec((nb, t, cols), lambda j: (0, 0, j)),
            pl.BlockSpec((nb, t, cols), lambda j: (0, 0, j)),
            pl.BlockSpec((nb, CONV_W - 1, cols), lambda j: (0, 0, j)),
            pl.BlockSpec((CONV_W, cols), lambda j: (0, j)),
            pl.BlockSpec((1, cols), lambda j: (0, j)),
        ],
        out_specs=pl.BlockSpec((nb, t, cols), lambda j: (0, 0, j)),
        out_shape=jax.ShapeDtypeStruct((nb, t, f), BF16),
        compiler_params=_cparams(("arbitrary",)),
        name="ffn_act_sample",
    )(gate3, val3, hist, conv_w, conv_b.reshape(1, f))
    return act.reshape(nb * t, f)


def _pair_rms(x, gain):
    x2 = x * x
    first = _iota(x.shape, 1) < HEAD_DIM
    s0 = jnp.sum(jnp.where(first, x2, 0.0), axis=-1, keepdims=True)
    s1 = jnp.sum(jnp.where(first, 0.0, x2), axis=-1, keepdims=True)
    ms = jnp.where(first, s0, s1) * (1.0 / HEAD_DIM)
    return x * lax.rsqrt(ms + RMS_EPS) * gain


def _bias_rows(table):
    h = table.shape[0]
    far = jnp.broadcast_to(table[:, 2 * REL_CLIP:], (h, ATT_REACH - REL_CLIP))
    mid = table[:, ::-1]
    near_len = BIAS_LEN - ATT_QROWS - (ATT_REACH - REL_CLIP) - (2 * REL_CLIP + 1)
    near = jnp.broadcast_to(table[:, 0:1], (h, near_len))
    wrap = jnp.broadcast_to(table[:, 2 * REL_CLIP:], (h, ATT_QROWS))
    return jnp.concatenate([far, mid, near, wrap], axis=1)


def _toeplitz(u_row, rows):
    return pltpu.roll(jnp.broadcast_to(u_row, (rows, BIAS_LEN)), 0, 1, stride=1, stride_axis=0)


def _attn_prompt_kernel(q_ref, k_ref, v_ref, qg_ref, kg_ref, u_ref, o_ref, kn_ref, vk_ref,
                        bias_ref, kwin_ref, vwin_ref):
    b = pl.program_id(1)
    qb = pl.program_id(2)
    shape = (ATT_QROWS, ATT_WIN)
    pairs = range(q_ref.shape[1] // PAIR)
    cols = [slice(p * PAIR, (p + 1) * PAIR) for p in pairs]
    chains = [(p, h) for p in pairs for h in range(2)]

    @pl.when((b == 0) & (qb == 0))
    def _():
        r = _iota(shape, 0)
        w = _iota(shape, 1)
        chunk_lo = _blk(r, CHUNK) * CHUNK
        in_band = (w >= chunk_lo) & (w < chunk_lo + (ATT_REACH + CHUNK))
        for i, (p, h) in enumerate(chains):
            bias = _toeplitz(u_ref[p, h:h + 1, :], ATT_QROWS)[:, :ATT_WIN]
            bias_ref[i] = jnp.where(in_band, bias * LOG2E, -jnp.inf)

    @pl.when(qb == 0)
    def _():
        kwin_ref[0:ATT_REACH] = jnp.zeros((ATT_REACH, kwin_ref.shape[1]), BF16)
        vwin_ref[0:ATT_REACH] = jnp.zeros((ATT_REACH, vwin_ref.shape[1]), BF16)

    @pl.when(qb > 0)
    def _():
        kwin_ref[0:ATT_REACH] = kwin_ref[ATT_QROWS:ATT_WIN]
        vwin_ref[0:ATT_REACH] = vwin_ref[ATT_QROWS:ATT_WIN]

    kn = [_pair_rms(k_ref[:, c], kg_ref[...]) for c in cols]
    for p in pairs:
        kn_ref[:, cols[p]] = kn[p]
        kwin_ref[ATT_REACH:ATT_WIN, cols[p]] = kn[p].astype(BF16)
    v_new = v_ref[...]
    vk_ref[...] = v_new
    vwin_ref[ATT_REACH:ATT_WIN] = v_new.astype(BF16)

    def attend(mask_start):
        qn = [_pair_rms(q_ref[:, c], qg_ref[...]) * (ATT_SCALE * LOG2E) for c in cols]
        kb = [kwin_ref[:, c] for c in cols]
        vb = [vwin_ref[:, c] for c in cols]
        first = _iota((ATT_QROWS, PAIR), 1) < HEAD_DIM
        first_w = _iota((ATT_WIN, PAIR), 1) < HEAD_DIM
        qh = [jnp.where(first, qn[p], 0.0) if h == 0 else jnp.where(first, 0.0, qn[p]) for p, h in chains]
        s = [_dot(qh[i], kb[p], NT) + bias_ref[i] for i, (p, h) in enumerate(chains)]
        if mask_start:
            started = _iota(shape, 1) >= ATT_REACH - qb * ATT_QROWS
            s = [jnp.where(started, x, -jnp.inf) for x in s]
        m = [jnp.max(x, axis=-1, keepdims=True) for x in s]
        pr = [jnp.exp2(x - mm) for x, mm in zip(s, m)]
        one = jnp.ones((), BF16)
        v_aug = [jnp.where(first_w, vb[p], one) if h == 0 else jnp.where(first_w, one, vb[p]) for p, h in chains]
        o = [_dot(pr[i], v_aug[i]) for i in range(len(chains))]
        o = [x / pltpu.roll(x, HEAD_DIM, 1) for x in o]
        for p in pairs:
            o_ref[:, cols[p]] = jnp.where(first, o[2 * p], o[2 * p + 1]).astype(BF16)

    full_window_from = ATT_REACH // ATT_QROWS
    pl.when(qb < full_window_from)(lambda: attend(True))
    pl.when(qb >= full_window_from)(lambda: attend(False))


def _attn_prompt(z3, q_gain, k_gain, u):
    nb, t, _ = z3.shape
    npairs = N_ATT_HEADS // 2
    npp = ATT_PAIRS
    width = npp * PAIR
    ngroups = npairs // npp
    nq = t // ATT_QROWS
    kcol = D_ATT // width
    vcol = 2 * D_ATT // width
    keep_blocks = ATT_REACH // ATT_QROWS
    blk = (None, ATT_QROWS, width)

    keep_spec = pl.BlockSpec(blk, lambda g, b, q: (b, jnp.maximum(q - (nq - keep_blocks), 0), g))
    att, kn, vk = pl.pallas_call(
        _attn_prompt_kernel,
        grid=(ngroups, nb, nq),
        in_specs=[
            pl.BlockSpec(blk, lambda g, b, q: (b, q, g)),
            pl.BlockSpec(blk, lambda g, b, q: (b, q, kcol + g)),
            pl.BlockSpec(blk, lambda g, b, q: (b, q, vcol + g)),
            pl.BlockSpec((1, PAIR), lambda g, b, q: (0, 0)),
            pl.BlockSpec((1, PAIR), lambda g, b, q: (0, 0)),
            pl.BlockSpec((npp, 2, BIAS_LEN), lambda g, b, q: (g, 0, 0)),
        ],
        out_specs=[
            pl.BlockSpec(blk, lambda g, b, q: (b, q, g)),
            keep_spec, keep_spec,
        ],
        out_shape=[
            jax.ShapeDtypeStruct((nb, t, D_ATT), BF16),
            jax.ShapeDtypeStruct((nb, ATT_REACH, D_ATT), F32),
            jax.ShapeDtypeStruct((nb, ATT_REACH, D_ATT), F32),
        ],
        scratch_shapes=[pltpu.VMEM((2 * npp, ATT_QROWS, ATT_WIN), F32),
                        pltpu.VMEM((ATT_WIN, width), BF16), pltpu.VMEM((ATT_WIN, width), BF16)],
        compiler_params=_cparams(("arbitrary", "arbitrary", "arbitrary")),
        name="attn_prompt",
    )(z3, z3, z3, jnp.tile(q_gain, 2).reshape(1, PAIR),
      jnp.tile(k_gain, 2).reshape(1, PAIR), u.reshape(npairs, 2, BIAS_LEN))
    return att, kn, vk


def _attn_sample_kernel(q_ref, k_ref, v_ref, kp_ref, vp_ref, qg_ref, kg_ref, u_ref, o_ref, kn_ref):
    t = q_ref.shape[0]
    reach = kp_ref.shape[0]
    first = _iota((t, PAIR), 1) < HEAD_DIM
    pairs = range(N_ATT_HEADS // 2)
    cols = [slice(p * PAIR, (p + 1) * PAIR) for p in pairs]
    chains = [(p, h) for p in pairs for h in range(2)]
    qn = [_pair_rms(q_ref[:, c], qg_ref[...]) * ATT_SCALE for c in cols]
    kn = [_pair_rms(k_ref[:, c], kg_ref[...]) for c in cols]
    for p in pairs:
        kn_ref[:, cols[p]] = kn[p]
    kpast = [kp_ref[:, c].astype(BF16) for c in cols]
    vpast = [vp_ref[:, c].astype(BF16) for c in cols]
    vnew = [v_ref[:, c].astype(BF16) for c in cols]
    qh = [jnp.where(first, qn[p], 0.0) if h == 0 else jnp.where(first, 0.0, qn[p]) for p, h in chains]
    bias = [_toeplitz(u_ref[p, h:h + 1, :], t) for p, h in chains]
    s_past = [_dot(qh[i], kpast[p], NT) + bias[i][:, :reach] for i, (p, h) in enumerate(chains)]
    s_new = [_dot(qh[i], kn[p], NT) + bias[i][:, reach:reach + t] for i, (p, h) in enumerate(chains)]
    m = [jnp.maximum(jnp.max(a, axis=-1, keepdims=True), jnp.max(b, axis=-1, keepdims=True))
         for a, b in zip(s_past, s_new)]
    p_past = [jnp.exp(a - mm) for a, mm in zip(s_past, m)]
    p_new = [jnp.exp(b - mm) for b, mm in zip(s_new, m)]
    l = [jnp.sum(a, axis=-1, keepdims=True) + jnp.sum(b, axis=-1, keepdims=True)
         for a, b in zip(p_past, p_new)]
    o = [(_dot(p_past[i], vpast[p]) + _dot(p_new[i], vnew[p])) / l[i] for i, (p, h) in enumerate(chains)]
    for p in pairs:
        o_ref[:, cols[p]] = jnp.where(first, o[2 * p], o[2 * p + 1]).astype(BF16)


def _attn_sample(z3, k_past, v_past, q_gain, k_gain, u):
    nb, t, _ = z3.shape
    reach = k_past.shape[1]
    npairs = N_ATT_HEADS // 2
    att, kn = pl.pallas_call(
        _attn_sample_kernel,
        grid=(nb,),
        in_specs=[
            pl.BlockSpec((None, t, D_ATT), lambda b: (b, 0, 0)),
            pl.BlockSpec((None, t, D_ATT), lambda b: (b, 0, 1)),
            pl.BlockSpec((None, t, D_ATT), lambda b: (b, 0, 2)),
            pl.BlockSpec((None, reach, D_ATT), lambda b: (b, 0, 0)),
            pl.BlockSpec((None, reach, D_ATT), lambda b: (b, 0, 0)),
            pl.BlockSpec((1, PAIR), lambda b: (0, 0)),
            pl.BlockSpec((1, PAIR), lambda b: (0, 0)),
            pl.BlockSpec((npairs, 2, BIAS_LEN), lambda b: (0, 0, 0)),
        ],
        out_specs=[
            pl.BlockSpec((None, t, D_ATT), lambda b: (b, 0, 0)),
            pl.BlockSpec((None, t, D_ATT), lambda b: (b, 0, 0)),
        ],
        out_shape=[
            jax.ShapeDtypeStruct((nb, t, D_ATT), BF16),
            jax.ShapeDtypeStruct((nb, t, D_ATT), F32),
        ],
        compiler_params=_cparams(("arbitrary",)),
        name="attn_sample",
    )(z3, z3, z3, k_past.reshape(nb, reach, D_ATT), v_past.reshape(nb, reach, D_ATT),
      jnp.tile(q_gain, 2).reshape(1, PAIR), jnp.tile(k_gain, 2).reshape(1, PAIR),
      u.reshape(npairs, 2, BIAS_LEN))
    return att, kn


def _tri_inverse(l_mats, c):
    n = l_mats[0].shape[0]
    eye = jnp.where(_iota((n, n), 0) == _iota((n, n), 1), 1.0, 0.0).astype(F32)
    a_s = [(eye + l).astype(BF16) for l in l_mats]
    t_s = [eye - l for l in l_mats]
    for _ in range(c.bit_length() - 2):
        r_s = [eye - _dot(a, t) for a, t in zip(a_s, t_s)]
        t_s = [t + _dot(t, r) for t, r in zip(t_s, r_s)]
    return t_s


def _rwkv_kernel(c, r_ref, k_ref, v_ref, lo_ref, sr_ref, sk_ref, sv_ref, slo_ref, s0_ref,
                 mur_ref, muk_ref, muv_ref, mulo_ref, w0_ref, a0_ref, kkg_ref, ka_ref, rk_ref,
                 lnw_ref, lnb_ref, w2_ref, a2_ref, g2_ref,
                 o_ref, sT_ref, s_ref, cr_ref, ck_ref, cv_ref, clo_ref):
    tb = pl.program_id(2)
    rows, width = r_ref.shape
    npp = width // PAIR
    nchunks = rows // c
    h0 = _iota((rows, PAIR), 1) < HEAD_DIM
    bd = _blk(_iota((PAIR, PAIR), 0), HEAD_DIM) == _blk(_iota((PAIR, PAIR), 1), HEAD_DIM)

    @pl.when(tb == 0)
    def _():
        s_ref[...] = jnp.zeros(s_ref.shape, F32)
        for pp in range(npp):
            s_ref[pp, 0:HEAD_DIM, 0:HEAD_DIM] = s0_ref[2 * pp]
            s_ref[pp, HEAD_DIM:PAIR, HEAD_DIM:PAIR] = s0_ref[2 * pp + 1]
        cr_ref[...] = sr_ref[...]
        ck_ref[...] = sk_ref[...]
        cv_ref[...] = sv_ref[...]
        clo_ref[...] = slo_ref[...]

    def shifted(x_ref, carry_ref, mu_ref):
        x = x_ref[...]
        prev = jnp.where(_iota(x.shape, 0) == 0, carry_ref[...], pltpu.roll(x, 1, 0))
        carry_ref[...] = x[rows - 1:rows]
        return x + (prev - x) * mu_ref[...]

    r = shifted(r_ref, cr_ref, mur_ref)
    k = shifted(k_ref, ck_ref, muk_ref)
    v = shifted(v_ref, cv_ref, muv_ref)
    lo = shifted(lo_ref, clo_ref, mulo_ref)

    zeros_w = jnp.zeros((RANK_W, width), F32)
    w2p = jnp.concatenate([w2_ref[...], zeros_w], axis=0)
    a2p = jnp.concatenate([zeros_w, a2_ref[...]], axis=0)
    lo_wa = lo[:, 0:RANK_W + RANK_A]
    u = w0_ref[...] + _dot(jnp.tanh(lo_wa), w2p)
    lw = -jnp.exp(-_softplus(-u) - 0.5)
    a = _sigmoid(a0_ref[...] + _dot(lo_wa, a2p))
    g = _dot(_sigmoid(lo[:, RANK_W + RANK_A:]), g2_ref[...])

    kk = k * kkg_ref[...]
    kk = kk / jnp.maximum(jnp.sqrt(_head_sums(kk * kk)), 1e-12)
    k = k * (1.0 + (a - 1.0) * ka_ref[...])
    b = kk * a
    bonus = _head_sums(r * k * rk_ref[...]) * v

    tr = _iota((rows, rows), 0)
    tc = _iota((rows, rows), 1)
    same_chunk = _blk(tr, c) == _blk(tc, c)
    strict = same_chunk & (tr > tc)
    incl = same_chunk & (tr >= tc)
    lw_hi, lw_lo = _split2(lw)
    tril_ones = jnp.where(incl, 1.0, 0.0).astype(BF16)
    lp = jnp.dot(tril_ones, lw_hi, preferred_element_type=F32) + \
        jnp.dot(tril_ones, lw_lo, preferred_element_type=F32)
    lp_end = jnp.concatenate(
        [jnp.broadcast_to(lp[(ci + 1) * c - 1:(ci + 1) * c], (c, width)) for ci in range(nchunks)], axis=0)

    alpha_w = kk * jnp.exp(lp - lw)
    inv_p = jnp.exp(-lp)
    beta_w = b * inv_p
    kappa_w = k * inv_p
    rho_w = r * jnp.exp(lp)
    to_end = jnp.exp(lp_end - lp)
    beta_ew = b * to_end
    kappa_ew = k * to_end
    decay_end_w = jnp.exp(lp_end)

    wide = (rows, nchunks * PAIR)
    col_chunk = _blk(_iota(wide, 1), PAIR) == _blk(_iota(wide, 0), c)
    spread = lambda m: jnp.where(col_chunk, jnp.tile(m, (1, nchunks)), 0.0)
    eye_p = _iota((PAIR, PAIR), 0) == _iota((PAIR, PAIR), 1)

    pairs = range(npp)
    lanes = [slice(pp * PAIR, (pp + 1) * PAIR) for pp in pairs]
    alpha = [alpha_w[:, l] for l in lanes]
    rho = [rho_w[:, l] for l in lanes]
    vv = [v[:, l] for l in lanes]
    head_mask = [h0, jnp.logical_not(h0)]
    bk = [jnp.concatenate([beta_w[:, l], kappa_w[:, l]], axis=0).astype(BF16) for l in lanes]
    prod = [[_dot(jnp.concatenate([jnp.where(hm, alpha[pp], 0.0), jnp.where(hm, rho[pp], 0.0)], axis=0),
                  bk[pp], NT) for hm in head_mask] for pp in pairs]
    t_inv = _tri_inverse([jnp.where(strict, prod[pp][h][:rows, :rows], 0.0) for pp in pairs for h in range(2)], c)
    x = [[_dot(jnp.where(strict, prod[pp][h][:rows, rows:], 0.0), vv[pp]) for h in range(2)] for pp in pairs]
    ws = [[_dot(t_inv[2 * pp + h], jnp.concatenate([alpha[pp], x[pp][h]], axis=1)) for h in range(2)]
          for pp in pairs]
    w12 = [jnp.concatenate([jnp.where(h0, ws[pp][0][:, :PAIR], ws[pp][1][:, :PAIR]),
                            jnp.where(h0, ws[pp][0][:, PAIR:], ws[pp][1][:, PAIR:])], axis=1) for pp in pairs]
    q = [[_dot(jnp.where(incl, prod[pp][h][rows:, :rows], 0.0), w12[pp]) for h in range(2)] for pp in pairs]
    qk = [[_dot(jnp.where(incl, prod[pp][h][rows:, rows:], 0.0), vv[pp]) for h in range(2)] for pp in pairs]
    rp = [rho[pp] - jnp.where(h0, q[pp][0][:, :PAIR], q[pp][1][:, :PAIR]) for pp in pairs]
    y0 = [jnp.where(h0, qk[pp][0] - q[pp][0][:, PAIR:], qk[pp][1] - q[pp][1][:, PAIR:]) for pp in pairs]
    wtb = [_dot(w12[pp], spread(beta_ew[:, lanes[pp]]), TN) for pp in pairs]
    vtk = [_dot(vv[pp], spread(kappa_ew[:, lanes[pp]]), TN) for pp in pairs]

    s_cur = [s_ref[pp] for pp in pairs]
    ys = [[] for _ in pairs]
    for ci in range(nchunks):
        sl = slice(ci * c, (ci + 1) * c)
        cols = slice(ci * PAIR, (ci + 1) * PAIR)
        for pp in pairs:
            decay_end = decay_end_w[ci * c:ci * c + 1, lanes[pp]]
            gmat = jnp.where(eye_p, jnp.broadcast_to(decay_end, (PAIR, PAIR)), 0.0) \
                - jnp.where(bd, wtb[pp][:PAIR, cols], 0.0)
            hmat = jnp.where(bd, vtk[pp][:, cols] - wtb[pp][PAIR:, cols], 0.0)
            ys[pp].append(_dot(rp[pp][sl], s_cur[pp], NT) + y0[pp][sl])
            s_cur[pp] = _dot(s_cur[pp], gmat) + hmat
    for pp in pairs:
        s_ref[pp] = s_cur[pp]
    y_pairs = [ys[pp][0] if nchunks == 1 else jnp.concatenate(ys[pp], axis=0) for pp in pairs]

    @pl.when(tb == pl.num_programs(2) - 1)
    def _():
        for pp in range(npp):
            sT_ref[2 * pp] = s_ref[pp, 0:HEAD_DIM, 0:HEAD_DIM]
            sT_ref[2 * pp + 1] = s_ref[pp, HEAD_DIM:PAIR, HEAD_DIM:PAIR]

    y = y_pairs[0] if npp == 1 else jnp.concatenate(y_pairs, axis=1)
    mu = _head_sums(y) * (1.0 / HEAD_DIM)
    d = y - mu
    var = _head_sums(d * d) * (1.0 / HEAD_DIM)
    yn = d * lax.rsqrt(var + GN_EPS) * lnw_ref[...] + lnb_ref[...]
    o_ref[...] = ((yn + bonus) * g).astype(BF16)


def _rwkv(z3, shift_prev, s0, p, rows, c, npp):
    nb, t, _ = z3.shape
    width = npp * PAIR
    ngroups = D_RWKV // width
    col0 = 3 * D_ATT // width
    lo_blk = (3 * D_ATT + 3 * D_RWKV) // D_LORA
    sp = shift_prev.reshape(nb, 1, D_SHIFT)

    def zspec(off):
        return pl.BlockSpec((None, rows, width), lambda b, q, s: (b, s, col0 + off * ngroups + q))

    def sspec(off):
        return pl.BlockSpec((None, 1, width), lambda b, q, s: (b, 0, off * ngroups + q))

    def vec(off=0):
        return pl.BlockSpec((1, width), lambda b, q, s: (0, off * ngroups + q))

    def row2(x):
        return x.reshape(1, -1)

    out, s_fin = pl.pallas_call(
        functools.partial(_rwkv_kernel, c),
        grid=(nb, ngroups, t // rows),
        in_specs=[
            zspec(0), zspec(1), zspec(2),
            pl.BlockSpec((None, rows, D_LORA), lambda b, q, s: (b, s, lo_blk)),
            sspec(0), sspec(1), sspec(2),
            pl.BlockSpec((None, 1, D_LORA), lambda b, q, s: (b, 0, 3 * D_RWKV // D_LORA)),
            pl.BlockSpec((None, 2 * npp, HEAD_DIM, HEAD_DIM), lambda b, q, s: (b, q, 0, 0)),
            vec(0), vec(1), vec(2),
            pl.BlockSpec((1, D_LORA), lambda b, q, s: (0, 3 * D_RWKV // D_LORA)),
            vec(), vec(), vec(), vec(), vec(), vec(), vec(),
            pl.BlockSpec((RANK_W, width), lambda b, q, s: (0, q)),
            pl.BlockSpec((RANK_A, width), lambda b, q, s: (0, q)),
            pl.BlockSpec((RANK_G, width), lambda b, q, s: (0, q)),
        ],
        out_specs=[
            pl.BlockSpec((None, rows, width), lambda b, q, s: (b, s, q)),
            pl.BlockSpec((None, 2 * npp, HEAD_DIM, HEAD_DIM), lambda b, q, s: (b, q, 0, 0)),
        ],
        out_shape=[
            jax.ShapeDtypeStruct((nb, t, D_RWKV), BF16),
            jax.ShapeDtypeStruct((nb, N_RWKV_HEADS, HEAD_DIM, HEAD_DIM), F32),
        ],
        scratch_shapes=[
            pltpu.VMEM((npp, PAIR, PAIR), F32),
            pltpu.VMEM((1, width), F32), pltpu.VMEM((1, width), F32), pltpu.VMEM((1, width), F32),
            pltpu.VMEM((1, D_LORA), F32),
        ],
        compiler_params=_cparams(("arbitrary", "arbitrary", "arbitrary")),
        name="rwkv7_mix",
    )(z3, z3, z3, z3, sp, sp, sp, sp, s0,
      row2(p['mu_shift']), row2(p['mu_shift']), row2(p['mu_shift']), row2(p['mu_shift']),
      row2(p['w0']), row2(p['a0']), row2(p['k_k']), row2(p['k_a']), row2(p['r_k']),
      row2(p['ln_x_w']), row2(p['ln_x_b']), p['w2'], p['a2'], p['g2'])
    return out, s_fin


def _layer(xp3, xs3, mod_p, mod_s, p, u, k_past, v_past, s0_p, s0_s, shift_p, shift_s, conv_p, conv_s):
    bp, tp, d = xp3.shape
    bs, ts, _ = xs3.shape
    mp, msr = bp * tp, bs * ts
    xp = xp3.reshape(mp, d)
    xs = xs3.reshape(msr, d)
    zp, zs = _norm_proj(xp, xs, p['norm_att_g'], mod_p, mod_s, 1, 0, p['w_in'], IN_COLS, "in_proj",
                        min(IN_ROW_TILE, tp))
    zp3 = zp.reshape(bp, tp, D_IN)
    zs3 = zs.reshape(bs, ts, D_IN)
    att_p, k_keep_p, v_keep_p = _attn_prompt(zp3, p['q_norm_g'], p['k_norm_g'], u)
    rw_p, s_fin_p = _rwkv(zp3, shift_p, s0_p, p, RWKV_ROWS, CHUNK, RWKV_PAIRS_PROMPT)
    att_s, k_keep_s = _attn_sample(zs3, k_past, v_past, p['q_norm_g'], p['k_norm_g'], u)
    v_keep_s = zs3[:, :, 2 * D_ATT:3 * D_ATT]
    rw_s, s_fin_s = _rwkv(zs3, shift_s, s0_s, p, ts, ts, RWKV_PAIRS_SAMPLE)
    x1p, h2p = _out_proj(att_p.reshape(mp, D_ATT), rw_p.reshape(mp, D_RWKV), p['w_out'], xp, mod_p,
                         p['norm_ffn_g'], OUT_ROW_TILE)
    x1s, h2s = _out_proj(att_s.reshape(msr, D_ATT), rw_s.reshape(msr, D_RWKV), p['w_out'], xs, mod_s,
                         p['norm_ffn_g'], OUT_ROW_TILE)
    act_p, conv_last_p, gate_s, val_s = _ffn_up_fused(h2p, h2s, p['w_up'], conv_p, p['dw_conv'],
                                                      p['dw_bias'], tp)
    act_s = _act_sample(gate_s, val_s, conv_s, p['dw_conv'], p['dw_bias'], bs, ts)
    conv_last_s = gate_s.reshape(bs, ts, -1)[:, ts - (CONV_W - 1):]
    x2p, x2s = _proj_resid(act_p, act_s, p['w_down'], x1p, x1s, mod_p, mod_s, 5, DOWN_COLS, "ffn_down")
    heads = lambda a: a.reshape(a.shape[0], a.shape[1], N_ATT_HEADS, HEAD_DIM)
    out_p = (x2p.reshape(bp, tp, d), heads(k_keep_p), heads(v_keep_p), s_fin_p, zp3[:, tp - 1, 3 * D_ATT:],
             conv_last_p)
    out_s = (x2s.reshape(bs, ts, d), heads(k_keep_s), heads(v_keep_s), s_fin_s, zs3[:, ts - 1, 3 * D_ATT:],
             conv_last_s)
    return out_p, out_s


def kernel(x_prompt, x_sample, c_prompt, c_sample, cache_att_k, cache_att_v, state_rwkv, state_shift, state_ffn_conv, norm_att_g, norm_ffn_g, w_ada, b_ada, w_in, q_norm_g, k_norm_g, rel_bias, mu_shift, w0, w2, a0, a2, g2, k_k, k_a, r_k, ln_x_w, ln_x_b, w_out, w_up, dw_conv, dw_bias, w_down):
    depth = w_in.shape[0]
    bp, tp, d = x_prompt.shape
    bs, ts, _ = x_sample.shape
    d_ff = w_down.shape[1]
    hp, hs = x_prompt, x_sample
    outs_p = [[] for _ in range(5)]
    outs_s = [[] for _ in range(5)]
    for l in range(depth):
        p = dict(norm_att_g=norm_att_g[l], norm_ffn_g=norm_ffn_g[l], w_in=w_in[l], q_norm_g=q_norm_g[l],
                 k_norm_g=k_norm_g[l], mu_shift=mu_shift[l], w0=w0[l], w2=w2[l], a0=a0[l], a2=a2[l],
                 g2=g2[l], k_k=k_k[l], k_a=k_a[l], r_k=r_k[l], ln_x_w=ln_x_w[l], ln_x_b=ln_x_b[l],
                 w_out=w_out[l], w_up=w_up[l], dw_conv=dw_conv[l], dw_bias=dw_bias[l], w_down=w_down[l])
        n_c = bp + bs
        pad = (-n_c) % 8
        c_all = jnp.concatenate([c_prompt, c_sample, jnp.zeros((pad, d), F32)], axis=0)
        mod = _ada(c_all, w_ada[l], b_ada[l])
        mod_p = _Mod(mod.reshape(n_c + pad, 6, 1, d), False, rows_per_batch=tp)
        mod_s = _Mod(jnp.repeat(mod[bp:bp + bs], ts, axis=0), True)
        u = _bias_rows(rel_bias[l])

        res_p, res_s = _layer(hp, hs, mod_p, mod_s, p, u, cache_att_k[l], cache_att_v[l],
                              jnp.zeros((bp, N_RWKV_HEADS, HEAD_DIM, HEAD_DIM), F32), state_rwkv[l],
                              jnp.zeros((bp, D_SHIFT), F32), state_shift[l],
                              jnp.zeros((bp, CONV_W - 1, d_ff), F32), state_ffn_conv[l])
        hp, hs = res_p[0], res_s[0]
        for lst, val in zip(outs_p, res_p[1:]):
            lst.append(val)
        for lst, val in zip(outs_s, res_s[1:]):
            lst.append(val)
    st = lambda lst: jnp.stack(lst)
    return (hp, hs, *[st(x) for x in outs_p], *[st(x) for x in outs_s])
```

```python
import functools

import jax
import jax.numpy as jnp
from jax import lax
from jax.experimental import pallas as pl
from jax.experimental.pallas import tpu as pltpu

F32 = jnp.float32
BF16 = jnp.bfloat16

CHUNK = 64
N_PREV_CHUNKS = 8
ATT_REACH = N_PREV_CHUNKS * CHUNK
HEAD_DIM = 64
N_ATT_HEADS = 16
N_RWKV_HEADS = 16
D_ATT = N_ATT_HEADS * HEAD_DIM
D_RWKV = N_RWKV_HEADS * HEAD_DIM
REL_CLIP = 128
RANK_W = 64
RANK_A = 64
RANK_G = 128
D_LORA = RANK_W + RANK_A + RANK_G
D_SHIFT = 3 * D_RWKV + D_LORA
D_IN = 3 * D_ATT + D_SHIFT
CONV_W = 3
RMS_EPS = 1e-6
GN_EPS = 64e-5
ATT_SCALE = HEAD_DIM ** -0.5
LOG2E = 1.4426950408889634
DECAY_SCALE = 0.6065306597126334

LANES = 128
PAIR = 2 * HEAD_DIM
MXU_DIM = 256
VMEM_LIMIT = 60 * 1024 * 1024

ROW_TILE = 1024
IN_ROW_TILE = 2048
ADA_COLS = 512
IN_COLS = 256
OUT_ROW_TILE = 512
UP_COLS = 512
DOWN_COLS = 256
DOWN_ROW_TILE = 2048
ATT_QROWS = 256
ATT_WIN = ATT_QROWS + ATT_REACH
ATT_PAIRS = 4
BIAS_LEN = 1024
RWKV_ROWS = 256
RWKV_PAIRS_PROMPT = 8
RWKV_PAIRS_SAMPLE = 8


def _cparams(sem):
    return pltpu.CompilerParams(dimension_semantics=sem, vmem_limit_bytes=VMEM_LIMIT)


def _dot(a, b, dims=(((1,), (0,)), ((), ()))):
    return lax.dot_general(a.astype(BF16), b.astype(BF16), dims, preferred_element_type=F32)


def _split2(x):
    hi = x.astype(BF16)
    lo = (x - hi.astype(F32)).astype(BF16)
    return hi, lo


NT = (((1,), (1,)), ((), ()))
TN = (((0,), (0,)), ((), ()))


def _iota(shape, dim):
    return lax.broadcasted_iota(jnp.int32, shape, dim)


def _blk(x, size):
    return jnp.right_shift(x, size.bit_length() - 1)


def _head_ones(n):
    r = _blk(_iota((n, n), 0), HEAD_DIM)
    c = _blk(_iota((n, n), 1), HEAD_DIM)
    return jnp.where(r == c, 1.0, 0.0).astype(BF16)


def _head_sums(x):
    lanes = x.shape[1]
    group = min(lanes, MXU_DIM)
    ones = _head_ones(group)
    parts = [_dot(x[:, i:i + group], ones) for i in range(0, lanes, group)]
    return parts[0] if len(parts) == 1 else jnp.concatenate(parts, axis=1)


def _sigmoid(x):
    return 1.0 / (1.0 + jnp.exp(-x))


def _ada_kernel(c_ref, w_ref, b_ref, o_ref):
    c = c_ref[...]
    s = c * _sigmoid(c)
    o_ref[...] = _dot(s, w_ref[...]) + b_ref[...]


def _ada(c_all, w_ada, b_ada):
    rows, d = c_all.shape
    n = w_ada.shape[1]
    return pl.pallas_call(
        _ada_kernel,
        grid=(n // ADA_COLS,),
        in_specs=[
            pl.BlockSpec((rows, d), lambda j: (0, 0)),
            pl.BlockSpec((d, ADA_COLS), lambda j: (0, j)),
            pl.BlockSpec((1, ADA_COLS), lambda j: (0, j)),
        ],
        out_specs=pl.BlockSpec((rows, ADA_COLS), lambda j: (0, j)),
        out_shape=jax.ShapeDtypeStruct((rows, n), F32),
        compiler_params=_cparams(("arbitrary",)),
        name="ada_mod",
    )(c_all, w_ada, b_ada.reshape(1, n))


class _Mod:
    def __init__(self, arr, per_row, rows_per_batch=None):
        self.arr = arr
        self.per_row = per_row
        self.rows_per_batch = rows_per_batch

    def spec(self, idx, cols, col_of, row_tile):
        if self.per_row:
            m = self.arr.shape[0]
            d = self.arr.shape[1] // 6
            nblk = d // cols
            return pl.BlockSpec((m, cols), lambda i, j: (0, idx * nblk + col_of(j)))
        tiles_per_batch = self.rows_per_batch // row_tile
        return pl.BlockSpec((None, None, 1, cols),
                            lambda i, j: (i // tiles_per_batch, idx, 0, col_of(j)))

    def rider_spec(self, idx, cols, nj):
        m = self.arr.shape[0]
        nblk = self.arr.shape[1] // 6 // cols
        return pl.BlockSpec((m, cols), lambda i, j: (0, idx * nblk + _rider_col(i, j, nj)))


def _rider_col(i, j, nj):
    return jnp.where(i == 0, j, nj - 1)


NORM_ROWS = 128


def _store_normed(h_ref, x_ref, g_ref, sc_ref, sh_ref):
    rows = x_ref.shape[0]
    step = min(NORM_ROWS, rows)
    per_row = sc_ref.shape[0] == rows

    def body(r, carry):
        sl = pl.ds(pl.multiple_of(r * step, step), step)
        x = x_ref[sl, :]
        ms = jnp.mean(x * x, axis=-1, keepdims=True)
        xn = x * lax.rsqrt(ms + RMS_EPS) * g_ref[...]
        sc = sc_ref[sl, :] if per_row else sc_ref[...]
        sh = sh_ref[sl, :] if per_row else sh_ref[...]
        h_ref[sl, :] = (xn * (1.0 + sc) + sh).astype(BF16)
        return carry

    lax.fori_loop(0, rows // step, body, 0)


def _norm_proj_kernel(x_ref, g_ref, sc_ref, sh_ref, w_ref, xs_ref, scs_ref, shs_ref, o_ref, os_ref,
                      h_ref, hs_ref):
    i = pl.program_id(0)
    j = pl.program_id(1)

    @pl.when(j == 0)
    def _():
        _store_normed(h_ref, x_ref, g_ref, sc_ref, sh_ref)

    @pl.when((i == 0) & (j == 0))
    def _():
        _store_normed(hs_ref, xs_ref, g_ref, scs_ref, shs_ref)

    @pl.when(i == 0)
    def _():
        os_ref[...] = jnp.dot(hs_ref[...], w_ref[...].astype(BF16), preferred_element_type=F32)

    o_ref[...] = jnp.dot(h_ref[...], w_ref[...].astype(BF16), preferred_element_type=F32)


def _norm_proj(x, xs, gain, mod, mod_s, sc_idx, sh_idx, w, cols, name, row_tile):
    m, d = x.shape
    ms = xs.shape[0]
    n = w.shape[1]
    tm = min(row_tile, m)
    nj = n // cols
    whole = lambda j: 0
    x_mode = dict(pipeline_mode=pl.Buffered(1)) if tm > ROW_TILE else {}
    return pl.pallas_call(
        _norm_proj_kernel,
        grid=(m // tm, nj),
        in_specs=[
            pl.BlockSpec((tm, d), lambda i, j: (i, 0), **x_mode),
            pl.BlockSpec((1, d), lambda i, j: (0, 0)),
            mod.spec(sc_idx, d, whole, tm),
            mod.spec(sh_idx, d, whole, tm),
            pl.BlockSpec((d, cols), lambda i, j: (0, j)),
            pl.BlockSpec((ms, d), lambda i, j: (0, 0)),
            mod_s.spec(sc_idx, d, whole, ms),
            mod_s.spec(sh_idx, d, whole, ms),
        ],
        out_specs=[
            pl.BlockSpec((tm, cols), lambda i, j: (i, j)),
            pl.BlockSpec((ms, cols), lambda i, j: (0, _rider_col(i, j, nj))),
        ],
        out_shape=[jax.ShapeDtypeStruct((m, n), F32), jax.ShapeDtypeStruct((ms, n), F32)],
        scratch_shapes=[pltpu.VMEM((tm, d), BF16), pltpu.VMEM((ms, d), BF16)],
        compiler_params=_cparams(("arbitrary", "arbitrary")),
        name=name,
    )(x, gain.reshape(1, d), mod.arr, mod.arr, w, xs, mod_s.arr, mod_s.arr)


def _proj_resid_kernel(a_ref, w_ref, x_ref, g_ref, as_ref, xs_ref, gs_ref, o_ref, os_ref):
    @pl.when(pl.program_id(0) == 0)
    def _():
        acc = jnp.dot(as_ref[...], w_ref[...].astype(BF16), preferred_element_type=F32)
        os_ref[...] = xs_ref[...] + gs_ref[...] * acc

    acc = jnp.dot(a_ref[...], w_ref[...].astype(BF16), preferred_element_type=F32)
    o_ref[...] = x_ref[...] + g_ref[...] * acc


def _proj_resid(a, a_s, w, x, xs, mod, mod_s, g_idx, cols, name, row_tile):
    m, n = x.shape
    ms = xs.shape[0]
    kdim = a.shape[1]
    tm = min(row_tile, m)
    nj = n // cols
    a_mode = dict(pipeline_mode=pl.Buffered(1)) if tm > ROW_TILE else {}
    rider_block = pl.BlockSpec((ms, cols), lambda i, j: (0, _rider_col(i, j, nj)))
    return pl.pallas_call(
        _proj_resid_kernel,
        grid=(m // tm, nj),
        in_specs=[
            pl.BlockSpec((tm, kdim), lambda i, j: (i, 0), **a_mode),
            pl.BlockSpec((kdim, cols), lambda i, j: (0, j)),
            pl.BlockSpec((tm, cols), lambda i, j: (i, j)),
            mod.spec(g_idx, cols, lambda j: j, tm),
            pl.BlockSpec((ms, kdim), lambda i, j: (0, 0)),
            rider_block,
            mod_s.rider_spec(g_idx, cols, nj),
        ],
        out_specs=[pl.BlockSpec((tm, cols), lambda i, j: (i, j)), rider_block],
        out_shape=[jax.ShapeDtypeStruct((m, n), F32), jax.ShapeDtypeStruct((ms, n), F32)],
        compiler_params=_cparams(("arbitrary", "arbitrary")),
        name=name,
    )(a, w, x, mod.arr, a_s, xs, mod_s.arr)


def _out_proj_kernel(a1_ref, a2_ref, w_ref, x_ref, g_ref, gain_ref, sc_ref, sh_ref, o_ref, h_ref, wb_ref):
    @pl.when(pl.program_id(0) == 0)
    def _():
        step = MXU_DIM

        def cast_rows(r, carry):
            sl = pl.ds(pl.multiple_of(r * step, step), step)
            wb_ref[sl, :] = w_ref[sl, :].astype(BF16)
            return carry

        lax.fori_loop(0, w_ref.shape[0] // step, cast_rows, 0)

    k1 = a1_ref.shape[1]
    rows = x_ref.shape[0]
    step = min(MXU_DIM, rows)
    per_row = sc_ref.shape[0] == rows
    pieces = [slice(r0, r0 + step) for r0 in range(0, rows, step)]
    x1s = []
    for sl in pieces:
        acc = jnp.dot(a1_ref[sl, :], wb_ref[0:k1], preferred_element_type=F32) \
            + jnp.dot(a2_ref[sl, :], wb_ref[k1:], preferred_element_type=F32)
        g = g_ref[sl, :] if per_row else g_ref[...]
        x1 = x_ref[sl, :] + g * acc
        o_ref[sl, :] = x1
        x1s.append(x1)
    for sl, x1 in zip(pieces, x1s):
        ms = jnp.mean(x1 * x1, axis=-1, keepdims=True)
        xn = x1 * lax.rsqrt(ms + RMS_EPS) * gain_ref[...]
        sc = sc_ref[sl, :] if per_row else sc_ref[...]
        sh = sh_ref[sl, :] if per_row else sh_ref[...]
        h_ref[sl, :] = (xn * (1.0 + sc) + sh).astype(BF16)


def _out_proj(a1, a2, w, x, mod, gain, row_tile):
    m, d = x.shape
    tm = min(row_tile, m)
    whole = lambda j: 0
    row = lambda kdim: pl.BlockSpec((tm, kdim), lambda i, j: (i, 0))
    return pl.pallas_call(
        _out_proj_kernel,
        grid=(m // tm, 1),
        in_specs=[
            row(a1.shape[1]), row(a2.shape[1]),
            pl.BlockSpec(w.shape, lambda i, j: (0, 0), pipeline_mode=pl.Buffered(1)),
            row(d),
            mod.spec(2, d, whole, tm),
            pl.BlockSpec((1, d), lambda i, j: (0, 0)),
            mod.spec(4, d, whole, tm),
            mod.spec(3, d, whole, tm),
        ],
        out_specs=[row(d), row(d)],
        out_shape=[jax.ShapeDtypeStruct((m, d), F32), jax.ShapeDtypeStruct((m, d), BF16)],
        scratch_shapes=[pltpu.VMEM(w.shape, BF16)],
        compiler_params=_cparams(("arbitrary", "arbitrary")),
        name="out_proj",
    )(a1, a2, w, x, mod.arr, gain.reshape(1, d), mod.arr, mod.arr)


def _gelu(x):
    return 0.5 * x * (1.0 + lax.erf(x * (2.0 ** -0.5)))


def _ffn_up_kernel(tiles_per_batch, h_ref, wg_ref, wv_ref, hist_ref, cw_ref, cb_ref, hs_ref,
                   act_ref, last_ref, gs_ref, vs_ref, carry_ref):
    i = pl.program_id(0)
    j = pl.program_id(1)

    @pl.when(i == 0)
    def _():
        hs = hs_ref[...]
        gs_ref[...] = jnp.dot(hs, wg_ref[...].astype(BF16), preferred_element_type=F32)
        vs_ref[...] = jnp.dot(hs, wv_ref[...].astype(BF16), preferred_element_type=F32)

    @pl.when((i % tiles_per_batch) == 0)
    def _():
        carry_ref[j] = hist_ref[...]

    h = h_ref[...]
    gate = jnp.dot(h, wg_ref[...].astype(BF16), preferred_element_type=F32)
    val = jnp.dot(h, wv_ref[...].astype(BF16), preferred_element_type=F32)
    tm = gate.shape[0]
    prev = carry_ref[j]
    row = _iota(gate.shape, 0)
    g1 = pltpu.roll(gate, 1, 0)
    g2 = pltpu.roll(gate, 2, 0)
    g1 = jnp.where(row == 0, prev[1:2], g1)
    g2 = jnp.where(row == 0, prev[0:1], jnp.where(row == 1, prev[1:2], g2))
    cw = cw_ref[...]
    conv = cb_ref[...] + g2 * cw[0:1] + g1 * cw[1:2] + gate * cw[2:3]
    act_ref[...] = (_gelu(conv) * val).astype(BF16)
    tail = gate[tm - 2:tm]
    carry_ref[j] = tail
    last_ref[...] = tail


def _ffn_up_fused(h, hs, w_up, hist, conv_w, conv_b, rows_per_batch):
    m, d = h.shape
    ms = hs.shape[0]
    f = w_up.shape[1] // 2
    tm = min(ROW_TILE, rows_per_batch)
    cols = UP_COLS
    nj = f // cols
    tiles_per_batch = rows_per_batch // tm
    rider_block = pl.BlockSpec((ms, cols), lambda i, j: (0, _rider_col(i, j, nj)))
    act, tile_tails, gate_s, val_s = pl.pallas_call(
        functools.partial(_ffn_up_kernel, tiles_per_batch),
        grid=(m // tm, nj),
        in_specs=[
            pl.BlockSpec((tm, d), lambda i, j: (i, 0)),
            pl.BlockSpec((d, cols), lambda i, j: (0, j)),
            pl.BlockSpec((d, cols), lambda i, j: (0, nj + j)),
            pl.BlockSpec((None, CONV_W - 1, cols), lambda i, j: (i // tiles_per_batch, 0, j)),
            pl.BlockSpec((CONV_W, cols), lambda i, j: (0, j)),
            pl.BlockSpec((1, cols), lambda i, j: (0, j)),
            pl.BlockSpec((ms, d), lambda i, j: (0, 0)),
        ],
        out_specs=[
            pl.BlockSpec((tm, cols), lambda i, j: (i, j)),
            pl.BlockSpec((None, CONV_W - 1, cols), lambda i, j: (i, 0, j)),
            rider_block, rider_block,
        ],
        out_shape=[
            jax.ShapeDtypeStruct((m, f), BF16),
            jax.ShapeDtypeStruct((m // tm, CONV_W - 1, f), F32),
            jax.ShapeDtypeStruct((ms, f), F32),
            jax.ShapeDtypeStruct((ms, f), F32),
        ],
        scratch_shapes=[pltpu.VMEM((nj, CONV_W - 1, cols), F32)],
        compiler_params=_cparams(("arbitrary", "arbitrary")),
        name="ffn_up_prompt",
    )(h, w_up, w_up, hist, conv_w, conv_b.reshape(1, f), hs)
    return act, tile_tails[tiles_per_batch - 1::tiles_per_batch], gate_s, val_s


def _act_sample_kernel(gate_ref, val_ref, hist_ref, cw_ref, cb_ref, act_ref):
    gate = gate_ref[...]
    hist = hist_ref[...]
    t = _iota(gate.shape, 1)
    g1 = jnp.where(t == 0, hist[:, 1:2], pltpu.roll(gate, 1, 1))
    g2 = jnp.where(t == 0, hist[:, 0:1], jnp.where(t == 1, hist[:, 1:2], pltpu.roll(gate, 2, 1)))
    cw = cw_ref[...]
    conv = cb_ref[...] + g2 * cw[0:1] + g1 * cw[1:2] + gate * cw[2:3]
    act_ref[...] = (_gelu(conv) * val_ref[...]).astype(BF16)


def _act_sample(gate, val, hist, conv_w, conv_b, nb, t):
    f = gate.shape[1]
    cols = UP_COLS
    nj = f // cols
    gate3 = gate.reshape(nb, t, f)
    val3 = val.reshape(nb, t, f)
    act = pl.pallas_call(
        _act_sample_kernel,
        grid=(nj,),
        in_specs=[
            pl.BlockSpec((nb, t, cols), lambda j: (0, 0, j)),
            pl.BlockSpec((nb, t, cols), lambda j: (0, 0, j)),
            pl.BlockSpec((nb, CONV_W - 1, cols), lambda j: (0, 0, j)),
            pl.BlockSpec((CONV_W, cols), lambda j: (0, j)),
            pl.BlockSpec((1, cols), lambda j: (0, j)),
        ],
        out_specs=pl.BlockSpec((nb, t, cols), lambda j: (0, 0, j)),
        out_shape=jax.ShapeDtypeStruct((nb, t, f), BF16),
        compiler_params=_cparams(("arbitrary",)),
        name="ffn_act_sample",
    )(gate3, val3, hist, conv_w, conv_b.reshape(1, f))
    return act.reshape(nb * t, f)


def _pair_rms(x, gain):
    x2 = x * x
    first = _iota(x.shape, 1) < HEAD_DIM
    s0 = jnp.sum(jnp.where(first, x2, 0.0), axis=-1, keepdims=True)
    s1 = jnp.sum(jnp.where(first, 0.0, x2), axis=-1, keepdims=True)
    ms = jnp.where(first, s0, s1) * (1.0 / HEAD_DIM)
    return x * lax.rsqrt(ms + RMS_EPS) * gain


def _bias_rows(table):
    h = table.shape[0]
    far = jnp.broadcast_to(table[:, 2 * REL_CLIP:], (h, ATT_REACH - REL_CLIP))
    mid = table[:, ::-1]
    near_len = BIAS_LEN - ATT_QROWS - (ATT_REACH - REL_CLIP) - (2 * REL_CLIP + 1)
    near = jnp.broadcast_to(table[:, 0:1], (h, near_len))
    wrap = jnp.broadcast_to(table[:, 2 * REL_CLIP:], (h, ATT_QROWS))
    return jnp.concatenate([far, mid, near, wrap], axis=1)


def _toeplitz(u_row, rows):
    return pltpu.roll(jnp.broadcast_to(u_row, (rows, BIAS_LEN)), 0, 1, stride=1, stride_axis=0)


def _attn_prompt_kernel(q_ref, k_ref, v_ref, qg_ref, kg_ref, u_ref, o_ref, kn_ref, vk_ref,
                        bias_ref, kwin_ref, vwin_ref):
    b = pl.program_id(1)
    qb = pl.program_id(2)
    shape = (ATT_QROWS, ATT_WIN)
    pairs = range(q_ref.shape[1] // PAIR)
    cols = [slice(p * PAIR, (p + 1) * PAIR) for p in pairs]
    chains = [(p, h) for p in pairs for h in range(2)]

    @pl.when((b == 0) & (qb == 0))
    def _():
        r = _iota(shape, 0)
        w = _iota(shape, 1)
        chunk_lo = _blk(r, CHUNK) * CHUNK
        in_band = (w >= chunk_lo) & (w < chunk_lo + (ATT_REACH + CHUNK))
        for i, (p, h) in enumerate(chains):
            bias = _toeplitz(u_ref[p, h:h + 1, :], ATT_QROWS)[:, :ATT_WIN]
            bias_ref[i] = jnp.where(in_band, bias * LOG2E, -jnp.inf)

    @pl.when(qb == 0)
    def _():
        kwin_ref[0:ATT_REACH] = jnp.zeros((ATT_REACH, kwin_ref.shape[1]), BF16)
        vwin_ref[0:ATT_REACH] = jnp.zeros((ATT_REACH, vwin_ref.shape[1]), BF16)

    @pl.when(qb > 0)
    def _():
        kwin_ref[0:ATT_REACH] = kwin_ref[ATT_QROWS:ATT_WIN]
        vwin_ref[0:ATT_REACH] = vwin_ref[ATT_QROWS:ATT_WIN]

    kn = [_pair_rms(k_ref[:, c], kg_ref[...]) for c in cols]
    for p in pairs:
        kn_ref[:, cols[p]] = kn[p]
        kwin_ref[ATT_REACH:ATT_WIN, cols[p]] = kn[p].astype(BF16)
    v_new = v_ref[...]
    vk_ref[...] = v_new
    vwin_ref[ATT_REACH:ATT_WIN] = v_new.astype(BF16)

    def attend(mask_start):
        qn = [_pair_rms(q_ref[:, c], qg_ref[...]) * (ATT_SCALE * LOG2E) for c in cols]
        kb = [kwin_ref[:, c] for c in cols]
        vb = [vwin_ref[:, c] for c in cols]
        first = _iota((ATT_QROWS, PAIR), 1) < HEAD_DIM
        first_w = _iota((ATT_WIN, PAIR), 1) < HEAD_DIM
        qh = [jnp.where(first, qn[p], 0.0) if h == 0 else jnp.where(first, 0.0, qn[p]) for p, h in chains]
        s = [_dot(qh[i], kb[p], NT) + bias_ref[i] for i, (p, h) in enumerate(chains)]
        if mask_start:
            started = _iota(shape, 1) >= ATT_REACH - qb * ATT_QROWS
            s = [jnp.where(started, x, -jnp.inf) for x in s]
        m = [jnp.max(x, axis=-1, keepdims=True) for x in s]
        pr = [jnp.exp2(x - mm) for x, mm in zip(s, m)]
        one = jnp.ones((), BF16)
        v_aug = [jnp.where(first_w, vb[p], one) if h == 0 else jnp.where(first_w, one, vb[p]) for p, h in chains]
        o = [_dot(pr[i], v_aug[i]) for i in range(len(chains))]
        o = [x / pltpu.roll(x, HEAD_DIM, 1) for x in o]
        for p in pairs:
            o_ref[:, cols[p]] = jnp.where(first, o[2 * p], o[2 * p + 1]).astype(BF16)

    full_window_from = ATT_REACH // ATT_QROWS
    pl.when(qb < full_window_from)(lambda: attend(True))
    pl.when(qb >= full_window_from)(lambda: attend(False))


def _attn_prompt(z3, q_gain, k_gain, u):
    nb, t, _ = z3.shape
    npairs = N_ATT_HEADS // 2
    npp = ATT_PAIRS
    width = npp * PAIR
    ngroups = npairs // npp
    nq = t // ATT_QROWS
    kcol = D_ATT // width
    vcol = 2 * D_ATT // width
    keep_blocks = ATT_REACH // ATT_QROWS
    blk = (None, ATT_QROWS, width)

    keep_spec = pl.BlockSpec(blk, lambda g, b, q: (b, jnp.maximum(q - (nq - keep_blocks), 0), g))
    att, kn, vk = pl.pallas_call(
        _attn_prompt_kernel,
        grid=(ngroups, nb, nq),
        in_specs=[
            pl.BlockSpec(blk, lambda g, b, q: (b, q, g)),
            pl.BlockSpec(blk, lambda g, b, q: (b, q, kcol + g)),
            pl.BlockSpec(blk, lambda g, b, q: (b, q, vcol + g)),
            pl.BlockSpec((1, PAIR), lambda g, b, q: (0, 0)),
            pl.BlockSpec((1, PAIR), lambda g, b, q: (0, 0)),
            pl.BlockSpec((npp, 2, BIAS_LEN), lambda g, b, q: (g, 0, 0)),
        ],
        out_specs=[
            pl.BlockSpec(blk, lambda g, b, q: (b, q, g)),
            keep_spec, keep_spec,
        ],
        out_shape=[
            jax.ShapeDtypeStruct((nb, t, D_ATT), BF16),
            jax.ShapeDtypeStruct((nb, ATT_REACH, D_ATT), F32),
            jax.ShapeDtypeStruct((nb, ATT_REACH, D_ATT), F32),
        ],
        scratch_shapes=[pltpu.VMEM((2 * npp, ATT_QROWS, ATT_WIN), F32),
                        pltpu.VMEM((ATT_WIN, width), BF16), pltpu.VMEM((ATT_WIN, width), BF16)],
        compiler_params=_cparams(("arbitrary", "arbitrary", "arbitrary")),
        name="attn_prompt",
    )(z3, z3, z3, jnp.tile(q_gain, 2).reshape(1, PAIR),
      jnp.tile(k_gain, 2).reshape(1, PAIR), u.reshape(npairs, 2, BIAS_LEN))
    return att, kn, vk


def _attn_sample_kernel(q_ref, k_ref, v_ref, kp_ref, vp_ref, qg_ref, kg_ref, u_ref, o_ref, kn_ref):
    t = q_ref.shape[0]
    reach = kp_ref.shape[0]
    first = _iota((t, PAIR), 1) < HEAD_DIM
    pairs = range(N_ATT_HEADS // 2)
    cols = [slice(p * PAIR, (p + 1) * PAIR) for p in pairs]
    chains = [(p, h) for p in pairs for h in range(2)]
    qn = [_pair_rms(q_ref[:, c], qg_ref[...]) * ATT_SCALE for c in cols]
    kn = [_pair_rms(k_ref[:, c], kg_ref[...]) for c in cols]
    for p in pairs:
        kn_ref[:, cols[p]] = kn[p]
    kpast = [kp_ref[:, c].astype(BF16) for c in cols]
    vpast = [vp_ref[:, c].astype(BF16) for c in cols]
    vnew = [v_ref[:, c].astype(BF16) for c in cols]
    qh = [jnp.where(first, qn[p], 0.0) if h == 0 else jnp.where(first, 0.0, qn[p]) for p, h in chains]
    bias = [_toeplitz(u_ref[p, h:h + 1, :], t) for p, h in chains]
    s_past = [_dot(qh[i], kpast[p], NT) + bias[i][:, :reach] for i, (p, h) in enumerate(chains)]
    s_new = [_dot(qh[i], kn[p], NT) + bias[i][:, reach:reach + t] for i, (p, h) in enumerate(chains)]
    m = [jnp.maximum(jnp.max(a, axis=-1, keepdims=True), jnp.max(b, axis=-1, keepdims=True))
         for a, b in zip(s_past, s_new)]
    p_past = [jnp.exp(a - mm) for a, mm in zip(s_past, m)]
    p_new = [jnp.exp(b - mm) for b, mm in zip(s_new, m)]
    l = [jnp.sum(a, axis=-1, keepdims=True) + jnp.sum(b, axis=-1, keepdims=True)
         for a, b in zip(p_past, p_new)]
    o = [(_dot(p_past[i], vpast[p]) + _dot(p_new[i], vnew[p])) / l[i] for i, (p, h) in enumerate(chains)]
    for p in pairs:
        o_ref[:, cols[p]] = jnp.where(first, o[2 * p], o[2 * p + 1]).astype(BF16)


def _attn_sample(z3, k_past, v_past, q_gain, k_gain, u):
    nb, t, _ = z3.shape
    reach = k_past.shape[1]
    npairs = N_ATT_HEADS // 2
    att, kn = pl.pallas_call(
        _attn_sample_kernel,
        grid=(nb,),
        in_specs=[
            pl.BlockSpec((None, t, D_ATT), lambda b: (b, 0, 0)),
            pl.BlockSpec((None, t, D_ATT), lambda b: (b, 0, 1)),
            pl.BlockSpec((None, t, D_ATT), lambda b: (b, 0, 2)),
            pl.BlockSpec((None, reach, D_ATT), lambda b: (b, 0, 0)),
            pl.BlockSpec((None, reach, D_ATT), lambda b: (b, 0, 0)),
            pl.BlockSpec((1, PAIR), lambda b: (0, 0)),
            pl.BlockSpec((1, PAIR), lambda b: (0, 0)),
            pl.BlockSpec((npairs, 2, BIAS_LEN), lambda b: (0, 0, 0)),
        ],
        out_specs=[
            pl.BlockSpec((None, t, D_ATT), lambda b: (b, 0, 0)),
            pl.BlockSpec((None, t, D_ATT), lambda b: (b, 0, 0)),
        ],
        out_shape=[
            jax.ShapeDtypeStruct((nb, t, D_ATT), BF16),
            jax.ShapeDtypeStruct((nb, t, D_ATT), F32),
        ],
        compiler_params=_cparams(("arbitrary",)),
        name="attn_sample",
    )(z3, z3, z3, k_past.reshape(nb, reach, D_ATT), v_past.reshape(nb, reach, D_ATT),
      jnp.tile(q_gain, 2).reshape(1, PAIR), jnp.tile(k_gain, 2).reshape(1, PAIR),
      u.reshape(npairs, 2, BIAS_LEN))
    return att, kn


def _tri_inverse(l_mats, c):
    n = l_mats[0].shape[0]
    eye = jnp.where(_iota((n, n), 0) == _iota((n, n), 1), 1.0, 0.0).astype(F32)
    a_s = [(eye + l).astype(BF16) for l in l_mats]
    t_s = [eye - l for l in l_mats]
    for _ in range(c.bit_length() - 2):
        r_s = [eye - _dot(a, t) for a, t in zip(a_s, t_s)]
        t_s = [t + _dot(t, r) for t, r in zip(t_s, r_s)]
    return t_s


def _rwkv_kernel(c, r_ref, k_ref, v_ref, lo_ref, sr_ref, sk_ref, sv_ref, slo_ref, s0_ref,
                 mur_ref, muk_ref, muv_ref, mulo_ref, w0_ref, a0_ref, kkg_ref, ka_ref, rk_ref,
                 lnw_ref, lnb_ref, w2_ref, a2_ref, g2_ref,
                 o_ref, sT_ref, s_ref, cr_ref, ck_ref, cv_ref, clo_ref):
    tb = pl.program_id(2)
    rows, width = r_ref.shape
    npp = width // PAIR
    nchunks = rows // c
    h0 = _iota((rows, PAIR), 1) < HEAD_DIM
    bd = _blk(_iota((PAIR, PAIR), 0), HEAD_DIM) == _blk(_iota((PAIR, PAIR), 1), HEAD_DIM)

    @pl.when(tb == 0)
    def _():
        s_ref[...] = jnp.zeros(s_ref.shape, F32)
        for pp in range(npp):
            s_ref[pp, 0:HEAD_DIM, 0:HEAD_DIM] = s0_ref[2 * pp]
            s_ref[pp, HEAD_DIM:PAIR, HEAD_DIM:PAIR] = s0_ref[2 * pp + 1]
        cr_ref[...] = sr_ref[...]
        ck_ref[...] = sk_ref[...]
        cv_ref[...] = sv_ref[...]
        clo_ref[...] = slo_ref[...]

    def shifted(x_ref, carry_ref, mu_ref):
        x = x_ref[...]
        prev = jnp.where(_iota(x.shape, 0) == 0, carry_ref[...], pltpu.roll(x, 1, 0))
        carry_ref[...] = x[rows - 1:rows]
        return x + (prev - x) * mu_ref[...]

    r = shifted(r_ref, cr_ref, mur_ref)
    k = shifted(k_ref, ck_ref, muk_ref)
    v = shifted(v_ref, cv_ref, muv_ref)
    lo = shifted(lo_ref, clo_ref, mulo_ref)

    zeros_w = jnp.zeros((RANK_W, width), F32)
    w2p = jnp.concatenate([w2_ref[...], zeros_w], axis=0)
    a2p = jnp.concatenate([zeros_w, a2_ref[...]], axis=0)
    lo_wa = lo[:, 0:RANK_W + RANK_A]
    u = w0_ref[...] + _dot(jnp.tanh(lo_wa), w2p)
    lw = -DECAY_SCALE * _sigmoid(u)
    a = _sigmoid(a0_ref[...] + _dot(lo_wa, a2p))
    g = _dot(_sigmoid(lo[:, RANK_W + RANK_A:]), g2_ref[...])

    kk = k * kkg_ref[...]
    kk = kk * lax.rsqrt(jnp.maximum(_head_sums(kk * kk), 1e-24))
    k = k * (1.0 + (a - 1.0) * ka_ref[...])
    b = kk * a
    bonus = _head_sums(r * k * rk_ref[...]) * v

    tr = _iota((rows, rows), 0)
    tc = _iota((rows, rows), 1)
    same_chunk = _blk(tr, c) == _blk(tc, c)
    strict = same_chunk & (tr > tc)
    incl = same_chunk & (tr >= tc)
    lw_hi, lw_lo = _split2(lw)
    tril_ones = jnp.where(incl, 1.0, 0.0).astype(BF16)
    lp = jnp.dot(tril_ones, lw_hi, preferred_element_type=F32) + \
        jnp.dot(tril_ones, lw_lo, preferred_element_type=F32)
    decay_end = [jnp.exp(lp[(ci + 1) * c - 1:(ci + 1) * c]) for ci in range(nchunks)]

    alpha_w = kk * jnp.exp(lp - lw)
    inv_p = jnp.exp(-lp)
    beta_w = b * inv_p
    kappa_w = k * inv_p
    rho_w = r * jnp.exp(lp)
    to_end = [inv_p[ci * c:(ci + 1) * c] * decay_end[ci] for ci in range(nchunks)]
    to_end = to_end[0] if nchunks == 1 else jnp.concatenate(to_end, axis=0)
    beta_ew = b * to_end
    kappa_ew = k * to_end

    wide = (rows, nchunks * PAIR)
    col_chunk = _blk(_iota(wide, 1), PAIR) == _blk(_iota(wide, 0), c)
    spread = lambda m: jnp.where(col_chunk, jnp.tile(m, (1, nchunks)), 0.0)
    eye_p = _iota((PAIR, PAIR), 0) == _iota((PAIR, PAIR), 1)

    pairs = range(npp)
    lanes = [slice(pp * PAIR, (pp + 1) * PAIR) for pp in pairs]
    alpha = [alpha_w[:, l] for l in lanes]
    rho = [rho_w[:, l] for l in lanes]
    vv = [v[:, l] for l in lanes]
    head_mask = [h0, jnp.logical_not(h0)]
    bk = [jnp.concatenate([beta_w[:, l], kappa_w[:, l]], axis=0).astype(BF16) for l in lanes]
    prod = [[_dot(jnp.concatenate([jnp.where(hm, alpha[pp], 0.0), jnp.where(hm, rho[pp], 0.0)], axis=0),
                  bk[pp], NT) for hm in head_mask] for pp in pairs]
    t_inv = _tri_inverse([jnp.where(strict, prod[pp][h][:rows, :rows], 0.0) for pp in pairs for h in range(2)], c)
    x = [[_dot(jnp.where(strict, prod[pp][h][:rows, rows:], 0.0), vv[pp]) for h in range(2)] for pp in pairs]
    ws = [[_dot(t_inv[2 * pp + h], jnp.concatenate([alpha[pp], x[pp][h]], axis=1)) for h in range(2)]
          for pp in pairs]
    w12 = [jnp.concatenate([jnp.where(h0, ws[pp][0][:, :PAIR], ws[pp][1][:, :PAIR]),
                            jnp.where(h0, ws[pp][0][:, PAIR:], ws[pp][1][:, PAIR:])], axis=1) for pp in pairs]
    q = [[_dot(jnp.where(incl, prod[pp][h][rows:, :rows], 0.0), w12[pp]) for h in range(2)] for pp in pairs]
    qk = [[_dot(jnp.where(incl, prod[pp][h][rows:, rows:], 0.0), vv[pp]) for h in range(2)] for pp in pairs]
    rp = [rho[pp] - jnp.where(h0, q[pp][0][:, :PAIR], q[pp][1][:, :PAIR]) for pp in pairs]
    y0 = [jnp.where(h0, qk[pp][0] - q[pp][0][:, PAIR:], qk[pp][1] - q[pp][1][:, PAIR:]) for pp in pairs]
    wtb = [_dot(w12[pp], spread(beta_ew[:, lanes[pp]]), TN) for pp in pairs]
    vtk = [_dot(vv[pp], spread(kappa_ew[:, lanes[pp]]), TN) for pp in pairs]

    s_cur = [s_ref[pp] for pp in pairs]
    ys = [[] for _ in pairs]
    for ci in range(nchunks):
        sl = slice(ci * c, (ci + 1) * c)
        cols = slice(ci * PAIR, (ci + 1) * PAIR)
        for pp in pairs:
            p_end = decay_end[ci][:, lanes[pp]]
            gmat = jnp.where(eye_p, jnp.broadcast_to(p_end, (PAIR, PAIR)), 0.0) \
                - jnp.where(bd, wtb[pp][:PAIR, cols], 0.0)
            hmat = jnp.where(bd, vtk[pp][:, cols] - wtb[pp][PAIR:, cols], 0.0)
            ys[pp].append(_dot(rp[pp][sl], s_cur[pp], NT) + y0[pp][sl])
            s_cur[pp] = _dot(s_cur[pp], gmat) + hmat
    for pp in pairs:
        s_ref[pp] = s_cur[pp]
    y_pairs = [ys[pp][0] if nchunks == 1 else jnp.concatenate(ys[pp], axis=0) for pp in pairs]

    @pl.when(tb == pl.num_programs(2) - 1)
    def _():
        for pp in range(npp):
            sT_ref[2 * pp] = s_ref[pp, 0:HEAD_DIM, 0:HEAD_DIM]
            sT_ref[2 * pp + 1] = s_ref[pp, HEAD_DIM:PAIR, HEAD_DIM:PAIR]

    y = y_pairs[0] if npp == 1 else jnp.concatenate(y_pairs, axis=1)
    mu = _head_sums(y) * (1.0 / HEAD_DIM)
    d = y - mu
    var = _head_sums(d * d) * (1.0 / HEAD_DIM)
    yn = d * lax.rsqrt(var + GN_EPS) * lnw_ref[...] + lnb_ref[...]
    o_ref[...] = ((yn + bonus) * g).astype(BF16)


def _rwkv(z3, shift_prev, s0, p, rows, c, npp):
    nb, t, _ = z3.shape
    width = npp * PAIR
    ngroups = D_RWKV // width
    col0 = 3 * D_ATT // width
    lo_blk = (3 * D_ATT + 3 * D_RWKV) // D_LORA
    sp = shift_prev.reshape(nb, 1, D_SHIFT)

    def zspec(off):
        return pl.BlockSpec((None, rows, width), lambda b, q, s: (b, s, col0 + off * ngroups + q))

    def sspec(off):
        return pl.BlockSpec((None, 1, width), lambda b, q, s: (b, 0, off * ngroups + q))

    def vec(off=0):
        return pl.BlockSpec((1, width), lambda b, q, s: (0, off * ngroups + q))

    def row2(x):
        return x.reshape(1, -1)

    out, s_fin = pl.pallas_call(
        functools.partial(_rwkv_kernel, c),
        grid=(nb, ngroups, t // rows),
        in_specs=[
            zspec(0), zspec(1), zspec(2),
            pl.BlockSpec((None, rows, D_LORA), lambda b, q, s: (b, s, lo_blk)),
            sspec(0), sspec(1), sspec(2),
            pl.BlockSpec((None, 1, D_LORA), lambda b, q, s: (b, 0, 3 * D_RWKV // D_LORA)),
            pl.BlockSpec((None, 2 * npp, HEAD_DIM, HEAD_DIM), lambda b, q, s: (b, q, 0, 0)),
            vec(0), vec(1), vec(2),
            pl.BlockSpec((1, D_LORA), lambda b, q, s: (0, 3 * D_RWKV // D_LORA)),
            vec(), vec(), vec(), vec(), vec(), vec(), vec(),
            pl.BlockSpec((RANK_W, width), lambda b, q, s: (0, q)),
            pl.BlockSpec((RANK_A, width), lambda b, q, s: (0, q)),
            pl.BlockSpec((RANK_G, width), lambda b, q, s: (0, q)),
        ],
        out_specs=[
            pl.BlockSpec((None, rows, width), lambda b, q, s: (b, s, q)),
            pl.BlockSpec((None, 2 * npp, HEAD_DIM, HEAD_DIM), lambda b, q, s: (b, q, 0, 0)),
        ],
        out_shape=[
            jax.ShapeDtypeStruct((nb, t, D_RWKV), BF16),
            jax.ShapeDtypeStruct((nb, N_RWKV_HEADS, HEAD_DIM, HEAD_DIM), F32),
        ],
        scratch_shapes=[
            pltpu.VMEM((npp, PAIR, PAIR), F32),
            pltpu.VMEM((1, width), F32), pltpu.VMEM((1, width), F32), pltpu.VMEM((1, width), F32),
            pltpu.VMEM((1, D_LORA), F32),
        ],
        compiler_params=_cparams(("arbitrary", "arbitrary", "arbitrary")),
        name="rwkv7_mix",
    )(z3, z3, z3, z3, sp, sp, sp, sp, s0,
      row2(p['mu_shift']), row2(p['mu_shift']), row2(p['mu_shift']), row2(p['mu_shift']),
      row2(p['w0']), row2(p['a0']), row2(p['k_k']), row2(p['k_a']), row2(p['r_k']),
      row2(p['ln_x_w']), row2(p['ln_x_b']), p['w2'], p['a2'], p['g2'])
    return out, s_fin


def _layer(xp3, xs3, mod_p, mod_s, p, u, k_past, v_past, s0_p, s0_s, shift_p, shift_s, conv_p, conv_s):
    bp, tp, d = xp3.shape
    bs, ts, _ = xs3.shape
    mp, msr = bp * tp, bs * ts
    xp = xp3.reshape(mp, d)
    xs = xs3.reshape(msr, d)
    zp, zs = _norm_proj(xp, xs, p['norm_att_g'], mod_p, mod_s, 1, 0, p['w_in'], IN_COLS, "in_proj",
                        min(IN_ROW_TILE, tp))
    zp3 = zp.reshape(bp, tp, D_IN)
    zs3 = zs.reshape(bs, ts, D_IN)
    att_p, k_keep_p, v_keep_p = _attn_prompt(zp3, p['q_norm_g'], p['k_norm_g'], u)
    rw_p, s_fin_p = _rwkv(zp3, shift_p, s0_p, p, RWKV_ROWS, CHUNK, RWKV_PAIRS_PROMPT)
    att_s, k_keep_s = _attn_sample(zs3, k_past, v_past, p['q_norm_g'], p['k_norm_g'], u)
    v_keep_s = zs3[:, :, 2 * D_ATT:3 * D_ATT]
    rw_s, s_fin_s = _rwkv(zs3, shift_s, s0_s, p, ts, ts, RWKV_PAIRS_SAMPLE)
    x1p, h2p = _out_proj(att_p.reshape(mp, D_ATT), rw_p.reshape(mp, D_RWKV), p['w_out'], xp, mod_p,
                         p['norm_ffn_g'], OUT_ROW_TILE)
    x1s, h2s = _out_proj(att_s.reshape(msr, D_ATT), rw_s.reshape(msr, D_RWKV), p['w_out'], xs, mod_s,
                         p['norm_ffn_g'], OUT_ROW_TILE)
    act_p, conv_last_p, gate_s, val_s = _ffn_up_fused(h2p, h2s, p['w_up'], conv_p, p['dw_conv'],
                                                      p['dw_bias'], tp)
    act_s = _act_sample(gate_s, val_s, conv_s, p['dw_conv'], p['dw_bias'], bs, ts)
    conv_last_s = gate_s.reshape(bs, ts, -1)[:, ts - (CONV_W - 1):]
    x2p, x2s = _proj_resid(act_p, act_s, p['w_down'], x1p, x1s, mod_p, mod_s, 5, DOWN_COLS, "ffn_down",
                           min(DOWN_ROW_TILE, tp))
    heads = lambda a: a.reshape(a.shape[0], a.shape[1], N_ATT_HEADS, HEAD_DIM)
    out_p = (x2p.reshape(bp, tp, d), heads(k_keep_p), heads(v_keep_p), s_fin_p, zp3[:, tp - 1, 3 * D_ATT:],
             conv_last_p)
    out_s = (x2s.reshape(bs, ts, d), heads(k_keep_s), heads(v_keep_s), s_fin_s, zs3[:, ts - 1, 3 * D_ATT:],
             conv_last_s)
    return out_p, out_s


def kernel(x_prompt, x_sample, c_prompt, c_sample, cache_att_k, cache_att_v, state_rwkv, state_shift, state_ffn_conv, norm_att_g, norm_ffn_g, w_ada, b_ada, w_in, q_norm_g, k_norm_g, rel_bias, mu_shift, w0, w2, a0, a2, g2, k_k, k_a, r_k, ln_x_w, ln_x_b, w_out, w_up, dw_conv, dw_bias, w_down):
    depth = w_in.shape[0]
    bp, tp, d = x_prompt.shape
    bs, ts, _ = x_sample.shape
    d_ff = w_down.shape[1]
    hp, hs = x_prompt, x_sample
    outs_p = [[] for _ in range(5)]
    outs_s = [[] for _ in range(5)]
    for l in range(depth):
        p = dict(norm_att_g=norm_att_g[l], norm_ffn_g=norm_ffn_g[l], w_in=w_in[l], q_norm_g=q_norm_g[l],
                 k_norm_g=k_norm_g[l], mu_shift=mu_shift[l], w0=w0[l], w2=w2[l], a0=a0[l], a2=a2[l],
                 g2=g2[l], k_k=k_k[l], k_a=k_a[l], r_k=r_k[l], ln_x_w=ln_x_w[l], ln_x_b=ln_x_b[l],
                 w_out=w_out[l], w_up=w_up[l], dw_conv=dw_conv[l], dw_bias=dw_bias[l], w_down=w_down[l])
        n_c = bp + bs
        pad = (-n_c) % 8
        c_all = jnp.concatenate([c_prompt, c_sample, jnp.zeros((pad, d), F32)], axis=0)
        mod = _ada(c_all, w_ada[l], b_ada[l])
        mod_p = _Mod(mod.reshape(n_c + pad, 6, 1, d), False, rows_per_batch=tp)
        mod_s = _Mod(jnp.repeat(mod[bp:bp + bs], ts, axis=0), True)
        u = _bias_rows(rel_bias[l])

        res_p, res_s = _layer(hp, hs, mod_p, mod_s, p, u, cache_att_k[l], cache_att_v[l],
                              jnp.zeros((bp, N_RWKV_HEADS, HEAD_DIM, HEAD_DIM), F32), state_rwkv[l],
                              jnp.zeros((bp, D_SHIFT), F32), state_shift[l],
                              jnp.zeros((bp, CONV_W - 1, d_ff), F32), state_ffn_conv[l])
        hp, hs = res_p[0], res_s[0]
        for lst, val in zip(outs_p, res_p[1:]):
            lst.append(val)
        for lst, val in zip(outs_s, res_s[1:]):
            lst.append(val)
    st = lambda lst: jnp.stack(lst)
    return (hp, hs, *[st(x) for x in outs_p], *[st(x) for x in outs_s])
```

```python
import functools

import jax
import jax.numpy as jnp
from jax import lax
from jax.experimental import pallas as pl
from jax.experimental.pallas import tpu as pltpu

F32 = jnp.float32
BF16 = jnp.bfloat16

CHUNK = 64
N_PREV_CHUNKS = 8
ATT_REACH = N_PREV_CHUNKS * CHUNK
HEAD_DIM = 64
N_ATT_HEADS = 16
N_RWKV_HEADS = 16
D_ATT = N_ATT_HEADS * HEAD_DIM
D_RWKV = N_RWKV_HEADS * HEAD_DIM
REL_CLIP = 128
RANK_W = 64
RANK_A = 64
RANK_G = 128
D_LORA = RANK_W + RANK_A + RANK_G
D_SHIFT = 3 * D_RWKV + D_LORA
D_IN = 3 * D_ATT + D_SHIFT
CONV_W = 3
RMS_EPS = 1e-6
GN_EPS = 64e-5
ATT_SCALE = HEAD_DIM ** -0.5
LOG2E = 1.4426950408889634
DECAY_SCALE = 0.6065306597126334

LANES = 128
PAIR = 2 * HEAD_DIM
MXU_DIM = 256
VMEM_LIMIT = 60 * 1024 * 1024

ROW_TILE = 1024
IN_ROW_TILE = 2048
ADA_COLS = 1024
IN_COLS = 256
OUT_ROW_TILE = 512
UP_COLS = 512
DOWN_COLS = 256
DOWN_ROW_TILE = 1024
ATT_QROWS = 256
ATT_WIN = ATT_QROWS + ATT_REACH
ATT_PAIRS = 4
BIAS_LEN = 1024
RWKV_ROWS = 256
RWKV_PAIRS_PROMPT = 8
RWKV_PAIRS_SAMPLE = 8


def _cparams(sem):
    return pltpu.CompilerParams(dimension_semantics=sem, vmem_limit_bytes=VMEM_LIMIT)


def _dot(a, b, dims=(((1,), (0,)), ((), ()))):
    return lax.dot_general(a.astype(BF16), b.astype(BF16), dims, preferred_element_type=F32)


def _split2(x):
    hi = x.astype(BF16)
    lo = (x - hi.astype(F32)).astype(BF16)
    return hi, lo


NT = (((1,), (1,)), ((), ()))
TN = (((0,), (0,)), ((), ()))


def _iota(shape, dim):
    return lax.broadcasted_iota(jnp.int32, shape, dim)


def _blk(x, size):
    return jnp.right_shift(x, size.bit_length() - 1)


def _head_ones(n):
    r = _blk(_iota((n, n), 0), HEAD_DIM)
    c = _blk(_iota((n, n), 1), HEAD_DIM)
    return jnp.where(r == c, 1.0, 0.0).astype(BF16)


def _head_sums(x):
    lanes = x.shape[1]
    group = min(lanes, MXU_DIM)
    ones = _head_ones(group)
    parts = [_dot(x[:, i:i + group], ones) for i in range(0, lanes, group)]
    return parts[0] if len(parts) == 1 else jnp.concatenate(parts, axis=1)


def _sigmoid(x):
    return 1.0 / (1.0 + jnp.exp(-x))


def _ada_kernel(c_ref, w_ref, b_ref, o_ref):
    c = c_ref[...]
    s = c * _sigmoid(c)
    o_ref[...] = _dot(s, w_ref[...]) + b_ref[...]


def _ada(c_all, w_ada, b_ada):
    rows, d = c_all.shape
    n = w_ada.shape[1]
    return pl.pallas_call(
        _ada_kernel,
        grid=(n // ADA_COLS,),
        in_specs=[
            pl.BlockSpec((rows, d), lambda j: (0, 0)),
            pl.BlockSpec((d, ADA_COLS), lambda j: (0, j)),
            pl.BlockSpec((1, ADA_COLS), lambda j: (0, j)),
        ],
        out_specs=pl.BlockSpec((rows, ADA_COLS), lambda j: (0, j)),
        out_shape=jax.ShapeDtypeStruct((rows, n), F32),
        compiler_params=_cparams(("arbitrary",)),
        name="ada_mod",
    )(c_all, w_ada, b_ada.reshape(1, n))


class _Mod:
    def __init__(self, arr, per_row, rows_per_batch=None):
        self.arr = arr
        self.per_row = per_row
        self.rows_per_batch = rows_per_batch

    def spec(self, idx, cols, col_of, row_tile):
        if self.per_row:
            m = self.arr.shape[0]
            d = self.arr.shape[1] // 6
            nblk = d // cols
            return pl.BlockSpec((m, cols), lambda i, j: (0, idx * nblk + col_of(j)))
        tiles_per_batch = self.rows_per_batch // row_tile
        return pl.BlockSpec((None, None, 1, cols),
                            lambda i, j: (i // tiles_per_batch, idx, 0, col_of(j)))

    def rider_spec(self, idx, cols, nj):
        m = self.arr.shape[0]
        nblk = self.arr.shape[1] // 6 // cols
        return pl.BlockSpec((m, cols), lambda i, j: (0, idx * nblk + _rider_col(i, j, nj)))


def _rider_col(i, j, nj):
    return jnp.where(i == 0, j, nj - 1)


NORM_ROWS = 128


def _store_normed(h_ref, x_ref, g_ref, sc_ref, sh_ref):
    rows = x_ref.shape[0]
    step = min(NORM_ROWS, rows)
    per_row = sc_ref.shape[0] == rows

    def body(r, carry):
        sl = pl.ds(pl.multiple_of(r * step, step), step)
        x = x_ref[sl, :]
        ms = jnp.mean(x * x, axis=-1, keepdims=True)
        xn = x * lax.rsqrt(ms + RMS_EPS) * g_ref[...]
        sc = sc_ref[sl, :] if per_row else sc_ref[...]
        sh = sh_ref[sl, :] if per_row else sh_ref[...]
        h_ref[sl, :] = (xn * (1.0 + sc) + sh).astype(BF16)
        return carry

    lax.fori_loop(0, rows // step, body, 0)


def _norm_proj_kernel(x_ref, g_ref, sc_ref, sh_ref, w_ref, xs_ref, scs_ref, shs_ref, o_ref, os_ref, h_ref):
    i = pl.program_id(0)
    j = pl.program_id(1)
    tm = x_ref.shape[0]
    host = h_ref.at[0:tm]
    riders = h_ref.at[tm:]

    @pl.when(j == 0)
    def _():
        _store_normed(host, x_ref, g_ref, sc_ref, sh_ref)

    @pl.when((i == 0) & (j == 0))
    def _():
        _store_normed(riders, xs_ref, g_ref, scs_ref, shs_ref)

    @pl.when(i == 0)
    def _():
        both = jnp.dot(h_ref[...], w_ref[...].astype(BF16), preferred_element_type=F32)
        o_ref[...] = both[0:tm]
        os_ref[...] = both[tm:]

    @pl.when(i > 0)
    def _():
        o_ref[...] = jnp.dot(host[...], w_ref[...].astype(BF16), preferred_element_type=F32)


def _norm_proj(x, xs, gain, mod, mod_s, sc_idx, sh_idx, w, cols, name, row_tile):
    m, d = x.shape
    ms = xs.shape[0]
    n = w.shape[1]
    tm = min(row_tile, m)
    nj = n // cols
    whole = lambda j: 0
    x_mode = dict(pipeline_mode=pl.Buffered(1)) if tm > ROW_TILE else {}
    return pl.pallas_call(
        _norm_proj_kernel,
        grid=(m // tm, nj),
        in_specs=[
            pl.BlockSpec((tm, d), lambda i, j: (i, 0), **x_mode),
            pl.BlockSpec((1, d), lambda i, j: (0, 0)),
            mod.spec(sc_idx, d, whole, tm),
            mod.spec(sh_idx, d, whole, tm),
            pl.BlockSpec((d, cols), lambda i, j: (0, j)),
            pl.BlockSpec((ms, d), lambda i, j: (0, 0)),
            mod_s.spec(sc_idx, d, whole, ms),
            mod_s.spec(sh_idx, d, whole, ms),
        ],
        out_specs=[
            pl.BlockSpec((tm, cols), lambda i, j: (i, j)),
            pl.BlockSpec((ms, cols), lambda i, j: (0, _rider_col(i, j, nj))),
        ],
        out_shape=[jax.ShapeDtypeStruct((m, n), F32), jax.ShapeDtypeStruct((ms, n), F32)],
        scratch_shapes=[pltpu.VMEM((tm + ms, d), BF16)],
        compiler_params=_cparams(("arbitrary", "arbitrary")),
        name=name,
    )(x, gain.reshape(1, d), mod.arr, mod.arr, w, xs, mod_s.arr, mod_s.arr)


def _proj_resid_kernel(a_ref, w_ref, x_ref, g_ref, as_ref, xs_ref, gs_ref, o_ref, os_ref):
    @pl.when(pl.program_id(0) == 0)
    def _():
        acc = jnp.dot(as_ref[...], w_ref[...].astype(BF16), preferred_element_type=F32)
        os_ref[...] = xs_ref[...] + gs_ref[...] * acc

    acc = jnp.dot(a_ref[...], w_ref[...].astype(BF16), preferred_element_type=F32)
    o_ref[...] = x_ref[...] + g_ref[...] * acc


def _proj_resid(a, a_s, w, x, xs, mod, mod_s, g_idx, cols, name, row_tile):
    m, n = x.shape
    ms = xs.shape[0]
    kdim = a.shape[1]
    tm = min(row_tile, m)
    nj = n // cols
    a_mode = dict(pipeline_mode=pl.Buffered(1)) if tm > ROW_TILE else {}
    rider_block = pl.BlockSpec((ms, cols), lambda i, j: (0, _rider_col(i, j, nj)))
    return pl.pallas_call(
        _proj_resid_kernel,
        grid=(m // tm, nj),
        in_specs=[
            pl.BlockSpec((tm, kdim), lambda i, j: (i, 0), **a_mode),
            pl.BlockSpec((kdim, cols), lambda i, j: (0, j)),
            pl.BlockSpec((tm, cols), lambda i, j: (i, j)),
            mod.spec(g_idx, cols, lambda j: j, tm),
            pl.BlockSpec((ms, kdim), lambda i, j: (0, 0)),
            rider_block,
            mod_s.rider_spec(g_idx, cols, nj),
        ],
        out_specs=[pl.BlockSpec((tm, cols), lambda i, j: (i, j)), rider_block],
        out_shape=[jax.ShapeDtypeStruct((m, n), F32), jax.ShapeDtypeStruct((ms, n), F32)],
        compiler_params=_cparams(("arbitrary", "arbitrary")),
        name=name,
    )(a, w, x, mod.arr, a_s, xs, mod_s.arr)


def _out_proj_kernel(a1_ref, a2_ref, w_ref, x_ref, g_ref, gain_ref, sc_ref, sh_ref, o_ref, h_ref, wb_ref):
    @pl.when(pl.program_id(0) == 0)
    def _():
        step = MXU_DIM

        def cast_rows(r, carry):
            sl = pl.ds(pl.multiple_of(r * step, step), step)
            wb_ref[sl, :] = w_ref[sl, :].astype(BF16)
            return carry

        lax.fori_loop(0, w_ref.shape[0] // step, cast_rows, 0)

    k1 = a1_ref.shape[1]
    rows = x_ref.shape[0]
    step = min(MXU_DIM, rows)
    per_row = sc_ref.shape[0] == rows
    pieces = [slice(r0, r0 + step) for r0 in range(0, rows, step)]
    x1s = []
    for sl in pieces:
        acc = jnp.dot(a1_ref[sl, :], wb_ref[0:k1], preferred_element_type=F32) \
            + jnp.dot(a2_ref[sl, :], wb_ref[k1:], preferred_element_type=F32)
        g = g_ref[sl, :] if per_row else g_ref[...]
        x1 = x_ref[sl, :] + g * acc
        o_ref[sl, :] = x1
        x1s.append(x1)
    for sl, x1 in zip(pieces, x1s):
        ms = jnp.mean(x1 * x1, axis=-1, keepdims=True)
        xn = x1 * lax.rsqrt(ms + RMS_EPS) * gain_ref[...]
        sc = sc_ref[sl, :] if per_row else sc_ref[...]
        sh = sh_ref[sl, :] if per_row else sh_ref[...]
        h_ref[sl, :] = (xn * (1.0 + sc) + sh).astype(BF16)


def _out_proj(a1, a2, w, x, mod, gain, row_tile):
    m, d = x.shape
    tm = min(row_tile, m)
    whole = lambda j: 0
    row = lambda kdim: pl.BlockSpec((tm, kdim), lambda i, j: (i, 0))
    return pl.pallas_call(
        _out_proj_kernel,
        grid=(m // tm, 1),
        in_specs=[
            row(a1.shape[1]), row(a2.shape[1]),
            pl.BlockSpec(w.shape, lambda i, j: (0, 0), pipeline_mode=pl.Buffered(1)),
            row(d),
            mod.spec(2, d, whole, tm),
            pl.BlockSpec((1, d), lambda i, j: (0, 0)),
            mod.spec(4, d, whole, tm),
            mod.spec(3, d, whole, tm),
        ],
        out_specs=[row(d), row(d)],
        out_shape=[jax.ShapeDtypeStruct((m, d), F32), jax.ShapeDtypeStruct((m, d), BF16)],
        scratch_shapes=[pltpu.VMEM(w.shape, BF16)],
        compiler_params=_cparams(("arbitrary", "arbitrary")),
        name="out_proj",
    )(a1, a2, w, x, mod.arr, gain.reshape(1, d), mod.arr, mod.arr)


def _gelu(x):
    return 0.5 * x * (1.0 + lax.erf(x * (2.0 ** -0.5)))


def _ffn_up_kernel(tiles_per_batch, h_ref, wg_ref, wv_ref, hist_ref, cw_ref, cb_ref, hs_ref,
                   act_ref, last_ref, gs_ref, vs_ref, carry_ref):
    i = pl.program_id(0)
    j = pl.program_id(1)

    @pl.when(i == 0)
    def _():
        hs = hs_ref[...]
        gs_ref[...] = jnp.dot(hs, wg_ref[...].astype(BF16), preferred_element_type=F32)
        vs_ref[...] = jnp.dot(hs, wv_ref[...].astype(BF16), preferred_element_type=F32)

    @pl.when((i % tiles_per_batch) == 0)
    def _():
        carry_ref[j] = hist_ref[...]

    h = h_ref[...]
    gate = jnp.dot(h, wg_ref[...].astype(BF16), preferred_element_type=F32)
    val = jnp.dot(h, wv_ref[...].astype(BF16), preferred_element_type=F32)
    tm = gate.shape[0]
    prev = carry_ref[j]
    row = _iota(gate.shape, 0)
    g1 = pltpu.roll(gate, 1, 0)
    g2 = pltpu.roll(gate, 2, 0)
    g1 = jnp.where(row == 0, prev[1:2], g1)
    g2 = jnp.where(row == 0, prev[0:1], jnp.where(row == 1, prev[1:2], g2))
    cw = cw_ref[...]
    conv = cb_ref[...] + g2 * cw[0:1] + g1 * cw[1:2] + gate * cw[2:3]
    act_ref[...] = (_gelu(conv) * val).astype(BF16)
    tail = gate[tm - 2:tm]
    carry_ref[j] = tail
    last_ref[...] = tail


def _ffn_up_fused(h, hs, w_up, hist, conv_w, conv_b, rows_per_batch):
    m, d = h.shape
    ms = hs.shape[0]
    f = w_up.shape[1] // 2
    tm = min(ROW_TILE, rows_per_batch)
    cols = UP_COLS
    nj = f // cols
    tiles_per_batch = rows_per_batch // tm
    rider_block = pl.BlockSpec((ms, cols), lambda i, j: (0, _rider_col(i, j, nj)))
    act, tile_tails, gate_s, val_s = pl.pallas_call(
        functools.partial(_ffn_up_kernel, tiles_per_batch),
        grid=(m // tm, nj),
        in_specs=[
            pl.BlockSpec((tm, d), lambda i, j: (i, 0)),
            pl.BlockSpec((d, cols), lambda i, j: (0, j)),
            pl.BlockSpec((d, cols), lambda i, j: (0, nj + j)),
            pl.BlockSpec((None, CONV_W - 1, cols), lambda i, j: (i // tiles_per_batch, 0, j)),
            pl.BlockSpec((CONV_W, cols), lambda i, j: (0, j)),
            pl.BlockSpec((1, cols), lambda i, j: (0, j)),
            pl.BlockSpec((ms, d), lambda i, j: (0, 0)),
        ],
        out_specs=[
            pl.BlockSpec((tm, cols), lambda i, j: (i, j)),
            pl.BlockSpec((None, CONV_W - 1, cols), lambda i, j: (i, 0, j)),
            rider_block, rider_block,
        ],
        out_shape=[
            jax.ShapeDtypeStruct((m, f), BF16),
            jax.ShapeDtypeStruct((m // tm, CONV_W - 1, f), F32),
            jax.ShapeDtypeStruct((ms, f), F32),
            jax.ShapeDtypeStruct((ms, f), F32),
        ],
        scratch_shapes=[pltpu.VMEM((nj, CONV_W - 1, cols), F32)],
        compiler_params=_cparams(("arbitrary", "arbitrary")),
        name="ffn_up_prompt",
    )(h, w_up, w_up, hist, conv_w, conv_b.reshape(1, f), hs)
    return act, tile_tails[tiles_per_batch - 1::tiles_per_batch], gate_s, val_s


def _act_sample_kernel(gate_ref, val_ref, hist_ref, cw_ref, cb_ref, act_ref):
    gate = gate_ref[...]
    hist = hist_ref[...]
    t = _iota(gate.shape, 1)
    g1 = jnp.where(t == 0, hist[:, 1:2], pltpu.roll(gate, 1, 1))
    g2 = jnp.where(t == 0, hist[:, 0:1], jnp.where(t == 1, hist[:, 1:2], pltpu.roll(gate, 2, 1)))
    cw = cw_ref[...]
    conv = cb_ref[...] + g2 * cw[0:1] + g1 * cw[1:2] + gate * cw[2:3]
    act_ref[...] = (_gelu(conv) * val_ref[...]).astype(BF16)


def _act_sample(gate, val, hist, conv_w, conv_b, nb, t):
    f = gate.shape[1]
    cols = UP_COLS
    nj = f // cols
    gate3 = gate.reshape(nb, t, f)
    val3 = val.reshape(nb, t, f)
    act = pl.pallas_call(
        _act_sample_kernel,
        grid=(nj,),
        in_specs=[
            pl.BlockSpec((nb, t, cols), lambda j: (0, 0, j)),
            pl.BlockSpec((nb, t, cols), lambda j: (0, 0, j)),
            pl.BlockSpec((nb, CONV_W - 1, cols), lambda j: (0, 0, j)),
            pl.BlockSpec((CONV_W, cols), lambda j: (0, j)),
            pl.BlockSpec((1, cols), lambda j: (0, j)),
        ],
        out_specs=pl.BlockSpec((nb, t, cols), lambda j: (0, 0, j)),
        out_shape=jax.ShapeDtypeStruct((nb, t, f), BF16),
        compiler_params=_cparams(("arbitrary",)),
        name="ffn_act_sample",
    )(gate3, val3, hist, conv_w, conv_b.reshape(1, f))
    return act.reshape(nb * t, f)


def _pair_rms(x, gain):
    x2 = x * x
    first = _iota(x.shape, 1) < HEAD_DIM
    s0 = jnp.sum(jnp.where(first, x2, 0.0), axis=-1, keepdims=True)
    s1 = jnp.sum(jnp.where(first, 0.0, x2), axis=-1, keepdims=True)
    ms = jnp.where(first, s0, s1) * (1.0 / HEAD_DIM)
    return x * lax.rsqrt(ms + RMS_EPS) * gain


def _bias_rows(table):
    h = table.shape[0]
    far = jnp.broadcast_to(table[:, 2 * REL_CLIP:], (h, ATT_REACH - REL_CLIP))
    mid = table[:, ::-1]
    near_len = BIAS_LEN - ATT_QROWS - (ATT_REACH - REL_CLIP) - (2 * REL_CLIP + 1)
    near = jnp.broadcast_to(table[:, 0:1], (h, near_len))
    wrap = jnp.broadcast_to(table[:, 2 * REL_CLIP:], (h, ATT_QROWS))
    return jnp.concatenate([far, mid, near, wrap], axis=1)


def _toeplitz(u_row, rows):
    return pltpu.roll(jnp.broadcast_to(u_row, (rows, BIAS_LEN)), 0, 1, stride=1, stride_axis=0)


def _attn_prompt_kernel(q_ref, k_ref, v_ref, qg_ref, kg_ref, u_ref, o_ref, kn_ref, vk_ref,
                        bias_ref, kwin_ref, vwin_ref):
    b = pl.program_id(1)
    qb = pl.program_id(2)
    shape = (ATT_QROWS, ATT_WIN)
    pairs = range(q_ref.shape[1] // PAIR)
    cols = [slice(p * PAIR, (p + 1) * PAIR) for p in pairs]
    chains = [(p, h) for p in pairs for h in range(2)]

    @pl.when((b == 0) & (qb == 0))
    def _():
        r = _iota(shape, 0)
        w = _iota(shape, 1)
        chunk_lo = _blk(r, CHUNK) * CHUNK
        in_band = (w >= chunk_lo) & (w < chunk_lo + (ATT_REACH + CHUNK))
        for i, (p, h) in enumerate(chains):
            bias = _toeplitz(u_ref[p, h:h + 1, :], ATT_QROWS)[:, :ATT_WIN]
            bias_ref[i] = jnp.where(in_band, bias * LOG2E, -jnp.inf)

    @pl.when(qb == 0)
    def _():
        kwin_ref[0:ATT_REACH] = jnp.zeros((ATT_REACH, kwin_ref.shape[1]), BF16)
        vwin_ref[0:ATT_REACH] = jnp.zeros((ATT_REACH, vwin_ref.shape[1]), BF16)

    @pl.when(qb > 0)
    def _():
        kwin_ref[0:ATT_REACH] = kwin_ref[ATT_QROWS:ATT_WIN]
        vwin_ref[0:ATT_REACH] = vwin_ref[ATT_QROWS:ATT_WIN]

    kn = [_pair_rms(k_ref[:, c], kg_ref[...]) for c in cols]
    for p in pairs:
        kn_ref[:, cols[p]] = kn[p]
        kwin_ref[ATT_REACH:ATT_WIN, cols[p]] = kn[p].astype(BF16)
    v_new = v_ref[...]
    vk_ref[...] = v_new
    vwin_ref[ATT_REACH:ATT_WIN] = v_new.astype(BF16)

    def attend(mask_start):
        qn = [_pair_rms(q_ref[:, c], qg_ref[...]) * (ATT_SCALE * LOG2E) for c in cols]
        kb = [kwin_ref[:, c] for c in cols]
        vb = [vwin_ref[:, c] for c in cols]
        first = _iota((ATT_QROWS, PAIR), 1) < HEAD_DIM
        first_w = _iota((ATT_WIN, PAIR), 1) < HEAD_DIM
        qh = [jnp.where(first, qn[p], 0.0) if h == 0 else jnp.where(first, 0.0, qn[p]) for p, h in chains]
        s = [_dot(qh[i], kb[p], NT) + bias_ref[i] for i, (p, h) in enumerate(chains)]
        if mask_start:
            started = _iota(shape, 1) >= ATT_REACH - qb * ATT_QROWS
            s = [jnp.where(started, x, -jnp.inf) for x in s]
        m = [jnp.max(x, axis=-1, keepdims=True) for x in s]
        pr = [jnp.exp2(x - mm) for x, mm in zip(s, m)]
        one = jnp.ones((), BF16)
        v_aug = [jnp.where(first_w, vb[p], one) if h == 0 else jnp.where(first_w, one, vb[p]) for p, h in chains]
        o = [_dot(pr[i], v_aug[i]) for i in range(len(chains))]
        o = [x / pltpu.roll(x, HEAD_DIM, 1) for x in o]
        for p in pairs:
            o_ref[:, cols[p]] = jnp.where(first, o[2 * p], o[2 * p + 1]).astype(BF16)

    full_window_from = ATT_REACH // ATT_QROWS
    pl.when(qb < full_window_from)(lambda: attend(True))
    pl.when(qb >= full_window_from)(lambda: attend(False))


def _attn_prompt(z3, q_gain, k_gain, u):
    nb, t, _ = z3.shape
    npairs = N_ATT_HEADS // 2
    npp = ATT_PAIRS
    width = npp * PAIR
    ngroups = npairs // npp
    nq = t // ATT_QROWS
    kcol = D_ATT // width
    vcol = 2 * D_ATT // width
    keep_blocks = ATT_REACH // ATT_QROWS
    blk = (None, ATT_QROWS, width)

    keep_spec = pl.BlockSpec(blk, lambda g, b, q: (b, jnp.maximum(q - (nq - keep_blocks), 0), g))
    att, kn, vk = pl.pallas_call(
        _attn_prompt_kernel,
        grid=(ngroups, nb, nq),
        in_specs=[
            pl.BlockSpec(blk, lambda g, b, q: (b, q, g)),
            pl.BlockSpec(blk, lambda g, b, q: (b, q, kcol + g)),
            pl.BlockSpec(blk, lambda g, b, q: (b, q, vcol + g)),
            pl.BlockSpec((1, PAIR), lambda g, b, q: (0, 0)),
            pl.BlockSpec((1, PAIR), lambda g, b, q: (0, 0)),
            pl.BlockSpec((npp, 2, BIAS_LEN), lambda g, b, q: (g, 0, 0)),
        ],
        out_specs=[
            pl.BlockSpec(blk, lambda g, b, q: (b, q, g)),
            keep_spec, keep_spec,
        ],
        out_shape=[
            jax.ShapeDtypeStruct((nb, t, D_ATT), BF16),
            jax.ShapeDtypeStruct((nb, ATT_REACH, D_ATT), F32),
            jax.ShapeDtypeStruct((nb, ATT_REACH, D_ATT), F32),
        ],
        scratch_shapes=[pltpu.VMEM((2 * npp, ATT_QROWS, ATT_WIN), F32),
                        pltpu.VMEM((ATT_WIN, width), BF16), pltpu.VMEM((ATT_WIN, width), BF16)],
        compiler_params=_cparams(("arbitrary", "arbitrary", "arbitrary")),
        name="attn_prompt",
    )(z3, z3, z3, jnp.tile(q_gain, 2).reshape(1, PAIR),
      jnp.tile(k_gain, 2).reshape(1, PAIR), u.reshape(npairs, 2, BIAS_LEN))
    return att, kn, vk


def _attn_sample_kernel(q_ref, k_ref, v_ref, kp_ref, vp_ref, qg_ref, kg_ref, u_ref, o_ref, kn_ref):
    t = q_ref.shape[0]
    reach = kp_ref.shape[0]
    first = _iota((t, PAIR), 1) < HEAD_DIM
    pairs = range(N_ATT_HEADS // 2)
    cols = [slice(p * PAIR, (p + 1) * PAIR) for p in pairs]
    chains = [(p, h) for p in pairs for h in range(2)]
    qn = [_pair_rms(q_ref[:, c], qg_ref[...]) * ATT_SCALE for c in cols]
    kn = [_pair_rms(k_ref[:, c], kg_ref[...]) for c in cols]
    for p in pairs:
        kn_ref[:, cols[p]] = kn[p]
    kpast = [kp_ref[:, c].astype(BF16) for c in cols]
    vpast = [vp_ref[:, c].astype(BF16) for c in cols]
    vnew = [v_ref[:, c].astype(BF16) for c in cols]
    qh = [jnp.where(first, qn[p], 0.0) if h == 0 else jnp.where(first, 0.0, qn[p]) for p, h in chains]
    bias = [_toeplitz(u_ref[p, h:h + 1, :], t) for p, h in chains]
    s_past = [_dot(qh[i], kpast[p], NT) + bias[i][:, :reach] for i, (p, h) in enumerate(chains)]
    s_new = [_dot(qh[i], kn[p], NT) + bias[i][:, reach:reach + t] for i, (p, h) in enumerate(chains)]
    m = [jnp.maximum(jnp.max(a, axis=-1, keepdims=True), jnp.max(b, axis=-1, keepdims=True))
         for a, b in zip(s_past, s_new)]
    p_past = [jnp.exp(a - mm) for a, mm in zip(s_past, m)]
    p_new = [jnp.exp(b - mm) for b, mm in zip(s_new, m)]
    l = [jnp.sum(a, axis=-1, keepdims=True) + jnp.sum(b, axis=-1, keepdims=True)
         for a, b in zip(p_past, p_new)]
    o = [(_dot(p_past[i], vpast[p]) + _dot(p_new[i], vnew[p])) / l[i] for i, (p, h) in enumerate(chains)]
    for p in pairs:
        o_ref[:, cols[p]] = jnp.where(first, o[2 * p], o[2 * p + 1]).astype(BF16)


def _attn_sample(z3, k_past, v_past, q_gain, k_gain, u):
    nb, t, _ = z3.shape
    reach = k_past.shape[1]
    npairs = N_ATT_HEADS // 2
    att, kn = pl.pallas_call(
        _attn_sample_kernel,
        grid=(nb,),
        in_specs=[
            pl.BlockSpec((None, t, D_ATT), lambda b: (b, 0, 0)),
            pl.BlockSpec((None, t, D_ATT), lambda b: (b, 0, 1)),
            pl.BlockSpec((None, t, D_ATT), lambda b: (b, 0, 2)),
            pl.BlockSpec((None, reach, D_ATT), lambda b: (b, 0, 0)),
            pl.BlockSpec((None, reach, D_ATT), lambda b: (b, 0, 0)),
            pl.BlockSpec((1, PAIR), lambda b: (0, 0)),
            pl.BlockSpec((1, PAIR), lambda b: (0, 0)),
            pl.BlockSpec((npairs, 2, BIAS_LEN), lambda b: (0, 0, 0)),
        ],
        out_specs=[
            pl.BlockSpec((None, t, D_ATT), lambda b: (b, 0, 0)),
            pl.BlockSpec((None, t, D_ATT), lambda b: (b, 0, 0)),
        ],
        out_shape=[
            jax.ShapeDtypeStruct((nb, t, D_ATT), BF16),
            jax.ShapeDtypeStruct((nb, t, D_ATT), F32),
        ],
        compiler_params=_cparams(("arbitrary",)),
        name="attn_sample",
    )(z3, z3, z3, k_past.reshape(nb, reach, D_ATT), v_past.reshape(nb, reach, D_ATT),
      jnp.tile(q_gain, 2).reshape(1, PAIR), jnp.tile(k_gain, 2).reshape(1, PAIR),
      u.reshape(npairs, 2, BIAS_LEN))
    return att, kn


def _tri_inverse(l_mats, c):
    n = l_mats[0].shape[0]
    eye = jnp.where(_iota((n, n), 0) == _iota((n, n), 1), 1.0, 0.0).astype(F32)
    a_s = [(eye + l).astype(BF16) for l in l_mats]
    t_s = [eye - l for l in l_mats]
    for _ in range(c.bit_length() - 2):
        r_s = [eye - _dot(a, t) for a, t in zip(a_s, t_s)]
        t_s = [t + _dot(t, r) for t, r in zip(t_s, r_s)]
    return t_s


def _rwkv_kernel(c, r_ref, k_ref, v_ref, lo_ref, sr_ref, sk_ref, sv_ref, slo_ref, s0_ref,
                 mur_ref, muk_ref, muv_ref, mulo_ref, w0_ref, a0_ref, kkg_ref, ka_ref, rk_ref,
                 lnw_ref, lnb_ref, w2_ref, a2_ref, g2_ref,
                 o_ref, sT_ref, s_ref, cr_ref, ck_ref, cv_ref, clo_ref):
    tb = pl.program_id(2)
    rows, width = r_ref.shape
    npp = width // PAIR
    nchunks = rows // c
    h0 = _iota((rows, PAIR), 1) < HEAD_DIM
    bd = _blk(_iota((PAIR, PAIR), 0), HEAD_DIM) == _blk(_iota((PAIR, PAIR), 1), HEAD_DIM)

    @pl.when(tb == 0)
    def _():
        s_ref[...] = jnp.zeros(s_ref.shape, F32)
        for pp in range(npp):
            s_ref[pp, 0:HEAD_DIM, 0:HEAD_DIM] = s0_ref[2 * pp]
            s_ref[pp, HEAD_DIM:PAIR, HEAD_DIM:PAIR] = s0_ref[2 * pp + 1]
        cr_ref[...] = sr_ref[...]
        ck_ref[...] = sk_ref[...]
        cv_ref[...] = sv_ref[...]
        clo_ref[...] = slo_ref[...]

    def shifted(x_ref, carry_ref, mu_ref):
        x = x_ref[...]
        prev = jnp.where(_iota(x.shape, 0) == 0, carry_ref[...], pltpu.roll(x, 1, 0))
        carry_ref[...] = x[rows - 1:rows]
        return x + (prev - x) * mu_ref[...]

    r = shifted(r_ref, cr_ref, mur_ref)
    k = shifted(k_ref, ck_ref, muk_ref)
    v = shifted(v_ref, cv_ref, muv_ref)
    lo = shifted(lo_ref, clo_ref, mulo_ref)

    zeros_w = jnp.zeros((RANK_W, width), F32)
    w2p = jnp.concatenate([w2_ref[...], zeros_w], axis=0)
    a2p = jnp.concatenate([zeros_w, a2_ref[...]], axis=0)
    lo_wa = lo[:, 0:RANK_W + RANK_A]
    u = w0_ref[...] + _dot(jnp.tanh(lo_wa), w2p)
    lw = -DECAY_SCALE * _sigmoid(u)
    a = _sigmoid(a0_ref[...] + _dot(lo_wa, a2p))
    g = _dot(_sigmoid(lo[:, RANK_W + RANK_A:]), g2_ref[...])

    kk = k * kkg_ref[...]
    kk = kk * lax.rsqrt(jnp.maximum(_head_sums(kk * kk), 1e-24))
    k = k * (1.0 + (a - 1.0) * ka_ref[...])
    b = kk * a
    bonus = _head_sums(r * k * rk_ref[...]) * v

    tr = _iota((rows, rows), 0)
    tc = _iota((rows, rows), 1)
    same_chunk = _blk(tr, c) == _blk(tc, c)
    strict = same_chunk & (tr > tc)
    incl = same_chunk & (tr >= tc)
    lw_hi, lw_lo = _split2(lw)
    tril_ones = jnp.where(incl, 1.0, 0.0).astype(BF16)
    lp = jnp.dot(tril_ones, lw_hi, preferred_element_type=F32) + \
        jnp.dot(tril_ones, lw_lo, preferred_element_type=F32)
    decay_end = [jnp.exp(lp[(ci + 1) * c - 1:(ci + 1) * c]) for ci in range(nchunks)]

    alpha_w = kk * jnp.exp(lp - lw)
    inv_p = jnp.exp(-lp)
    beta_w = b * inv_p
    kappa_w = k * inv_p
    rho_w = r * jnp.exp(lp)
    to_end = [inv_p[ci * c:(ci + 1) * c] * decay_end[ci] for ci in range(nchunks)]
    to_end = to_end[0] if nchunks == 1 else jnp.concatenate(to_end, axis=0)
    beta_ew = b * to_end
    kappa_ew = k * to_end

    wide = (rows, nchunks * PAIR)
    col_chunk = _blk(_iota(wide, 1), PAIR) == _blk(_iota(wide, 0), c)
    spread = lambda m: jnp.where(col_chunk, jnp.tile(m, (1, nchunks)), 0.0)
    eye_p = _iota((PAIR, PAIR), 0) == _iota((PAIR, PAIR), 1)

    pairs = range(npp)
    lanes = [slice(pp * PAIR, (pp + 1) * PAIR) for pp in pairs]
    alpha = [alpha_w[:, l] for l in lanes]
    rho = [rho_w[:, l] for l in lanes]
    vv = [v[:, l] for l in lanes]
    head_mask = [h0, jnp.logical_not(h0)]
    bk = [jnp.concatenate([beta_w[:, l], kappa_w[:, l]], axis=0).astype(BF16) for l in lanes]
    prod = [[_dot(jnp.concatenate([jnp.where(hm, alpha[pp], 0.0), jnp.where(hm, rho[pp], 0.0)], axis=0),
                  bk[pp], NT) for hm in head_mask] for pp in pairs]
    t_inv = _tri_inverse([jnp.where(strict, prod[pp][h][:rows, :rows], 0.0) for pp in pairs for h in range(2)], c)
    x = [[_dot(jnp.where(strict, prod[pp][h][:rows, rows:], 0.0), vv[pp]) for h in range(2)] for pp in pairs]
    ws = [[_dot(t_inv[2 * pp + h], jnp.concatenate([alpha[pp], x[pp][h]], axis=1)) for h in range(2)]
          for pp in pairs]
    w12 = [jnp.concatenate([jnp.where(h0, ws[pp][0][:, :PAIR], ws[pp][1][:, :PAIR]),
                            jnp.where(h0, ws[pp][0][:, PAIR:], ws[pp][1][:, PAIR:])], axis=1) for pp in pairs]
    q = [[_dot(jnp.where(incl, prod[pp][h][rows:, :rows], 0.0), w12[pp]) for h in range(2)] for pp in pairs]
    qk = [[_dot(jnp.where(incl, prod[pp][h][rows:, rows:], 0.0), vv[pp]) for h in range(2)] for pp in pairs]
    rp = [rho[pp] - jnp.where(h0, q[pp][0][:, :PAIR], q[pp][1][:, :PAIR]) for pp in pairs]
    y0 = [jnp.where(h0, qk[pp][0] - q[pp][0][:, PAIR:], qk[pp][1] - q[pp][1][:, PAIR:]) for pp in pairs]
    wtb = [_dot(w12[pp], spread(beta_ew[:, lanes[pp]]), TN) for pp in pairs]
    vtk = [_dot(vv[pp], spread(kappa_ew[:, lanes[pp]]), TN) for pp in pairs]

    s_cur = [s_ref[pp] for pp in pairs]
    ys = [[] for _ in pairs]
    for ci in range(nchunks):
        sl = slice(ci * c, (ci + 1) * c)
        cols = slice(ci * PAIR, (ci + 1) * PAIR)
        for pp in pairs:
            p_end = decay_end[ci][:, lanes[pp]]
            gmat = jnp.where(eye_p, jnp.broadcast_to(p_end, (PAIR, PAIR)), 0.0) \
                - jnp.where(bd, wtb[pp][:PAIR, cols], 0.0)
            hmat = jnp.where(bd, vtk[pp][:, cols] - wtb[pp][PAIR:, cols], 0.0)
            ys[pp].append(_dot(rp[pp][sl], s_cur[pp], NT) + y0[pp][sl])
            s_cur[pp] = _dot(s_cur[pp], gmat) + hmat
    for pp in pairs:
        s_ref[pp] = s_cur[pp]
    y_pairs = [ys[pp][0] if nchunks == 1 else jnp.concatenate(ys[pp], axis=0) for pp in pairs]

    @pl.when(tb == pl.num_programs(2) - 1)
    def _():
        for pp in range(npp):
            sT_ref[2 * pp] = s_ref[pp, 0:HEAD_DIM, 0:HEAD_DIM]
            sT_ref[2 * pp + 1] = s_ref[pp, HEAD_DIM:PAIR, HEAD_DIM:PAIR]

    y = y_pairs[0] if npp == 1 else jnp.concatenate(y_pairs, axis=1)
    mu = _head_sums(y) * (1.0 / HEAD_DIM)
    d = y - mu
    var = _head_sums(d * d) * (1.0 / HEAD_DIM)
    yn = d * lax.rsqrt(var + GN_EPS) * lnw_ref[...] + lnb_ref[...]
    o_ref[...] = ((yn + bonus) * g).astype(BF16)


def _rwkv(z3, shift_prev, s0, p, rows, c, npp):
    nb, t, _ = z3.shape
    width = npp * PAIR
    ngroups = D_RWKV // width
    col0 = 3 * D_ATT // width
    lo_blk = (3 * D_ATT + 3 * D_RWKV) // D_LORA
    sp = shift_prev.reshape(nb, 1, D_SHIFT)

    def zspec(off):
        return pl.BlockSpec((None, rows, width), lambda b, q, s: (b, s, col0 + off * ngroups + q))

    def sspec(off):
        return pl.BlockSpec((None, 1, width), lambda b, q, s: (b, 0, off * ngroups + q))

    def vec(off=0):
        return pl.BlockSpec((1, width), lambda b, q, s: (0, off * ngroups + q))

    def row2(x):
        return x.reshape(1, -1)

    out, s_fin = pl.pallas_call(
        functools.partial(_rwkv_kernel, c),
        grid=(nb, ngroups, t // rows),
        in_specs=[
            zspec(0), zspec(1), zspec(2),
            pl.BlockSpec((None, rows, D_LORA), lambda b, q, s: (b, s, lo_blk)),
            sspec(0), sspec(1), sspec(2),
            pl.BlockSpec((None, 1, D_LORA), lambda b, q, s: (b, 0, 3 * D_RWKV // D_LORA)),
            pl.BlockSpec((None, 2 * npp, HEAD_DIM, HEAD_DIM), lambda b, q, s: (b, q, 0, 0)),
            vec(0), vec(1), vec(2),
            pl.BlockSpec((1, D_LORA), lambda b, q, s: (0, 3 * D_RWKV // D_LORA)),
            vec(), vec(), vec(), vec(), vec(), vec(), vec(),
            pl.BlockSpec((RANK_W, width), lambda b, q, s: (0, q)),
            pl.BlockSpec((RANK_A, width), lambda b, q, s: (0, q)),
            pl.BlockSpec((RANK_G, width), lambda b, q, s: (0, q)),
        ],
        out_specs=[
            pl.BlockSpec((None, rows, width), lambda b, q, s: (b, s, q)),
            pl.BlockSpec((None, 2 * npp, HEAD_DIM, HEAD_DIM), lambda b, q, s: (b, q, 0, 0)),
        ],
        out_shape=[
            jax.ShapeDtypeStruct((nb, t, D_RWKV), BF16),
            jax.ShapeDtypeStruct((nb, N_RWKV_HEADS, HEAD_DIM, HEAD_DIM), F32),
        ],
        scratch_shapes=[
            pltpu.VMEM((npp, PAIR, PAIR), F32),
            pltpu.VMEM((1, width), F32), pltpu.VMEM((1, width), F32), pltpu.VMEM((1, width), F32),
            pltpu.VMEM((1, D_LORA), F32),
        ],
        compiler_params=_cparams(("arbitrary", "arbitrary", "arbitrary")),
        name="rwkv7_mix",
    )(z3, z3, z3, z3, sp, sp, sp, sp, s0,
      row2(p['mu_shift']), row2(p['mu_shift']), row2(p['mu_shift']), row2(p['mu_shift']),
      row2(p['w0']), row2(p['a0']), row2(p['k_k']), row2(p['k_a']), row2(p['r_k']),
      row2(p['ln_x_w']), row2(p['ln_x_b']), p['w2'], p['a2'], p['g2'])
    return out, s_fin


def _layer(xp3, xs3, mod_p, mod_s, p, u, k_past, v_past, s0_p, s0_s, shift_p, shift_s, conv_p, conv_s):
    bp, tp, d = xp3.shape
    bs, ts, _ = xs3.shape
    mp, msr = bp * tp, bs * ts
    xp = xp3.reshape(mp, d)
    xs = xs3.reshape(msr, d)
    zp, zs = _norm_proj(xp, xs, p['norm_att_g'], mod_p, mod_s, 1, 0, p['w_in'], IN_COLS, "in_proj",
                        min(IN_ROW_TILE, tp))
    zp3 = zp.reshape(bp, tp, D_IN)
    zs3 = zs.reshape(bs, ts, D_IN)
    att_p, k_keep_p, v_keep_p = _attn_prompt(zp3, p['q_norm_g'], p['k_norm_g'], u)
    rw_p, s_fin_p = _rwkv(zp3, shift_p, s0_p, p, RWKV_ROWS, CHUNK, RWKV_PAIRS_PROMPT)
    att_s, k_keep_s = _attn_sample(zs3, k_past, v_past, p['q_norm_g'], p['k_norm_g'], u)
    v_keep_s = zs3[:, :, 2 * D_ATT:3 * D_ATT]
    rw_s, s_fin_s = _rwkv(zs3, shift_s, s0_s, p, ts, ts, RWKV_PAIRS_SAMPLE)
    x1p, h2p = _out_proj(att_p.reshape(mp, D_ATT), rw_p.reshape(mp, D_RWKV), p['w_out'], xp, mod_p,
                         p['norm_ffn_g'], OUT_ROW_TILE)
    x1s, h2s = _out_proj(att_s.reshape(msr, D_ATT), rw_s.reshape(msr, D_RWKV), p['w_out'], xs, mod_s,
                         p['norm_ffn_g'], OUT_ROW_TILE)
    act_p, conv_last_p, gate_s, val_s = _ffn_up_fused(h2p, h2s, p['w_up'], conv_p, p['dw_conv'],
                                                      p['dw_bias'], tp)
    act_s = _act_sample(gate_s, val_s, conv_s, p['dw_conv'], p['dw_bias'], bs, ts)
    conv_last_s = gate_s.reshape(bs, ts, -1)[:, ts - (CONV_W - 1):]
    x2p, x2s = _proj_resid(act_p, act_s, p['w_down'], x1p, x1s, mod_p, mod_s, 5, DOWN_COLS, "ffn_down",
                           min(DOWN_ROW_TILE, tp))
    heads = lambda a: a.reshape(a.shape[0], a.shape[1], N_ATT_HEADS, HEAD_DIM)
    out_p = (x2p.reshape(bp, tp, d), heads(k_keep_p), heads(v_keep_p), s_fin_p, zp3[:, tp - 1, 3 * D_ATT:],
             conv_last_p)
    out_s = (x2s.reshape(bs, ts, d), heads(k_keep_s), heads(v_keep_s), s_fin_s, zs3[:, ts - 1, 3 * D_ATT:],
             conv_last_s)
    return out_p, out_s


def kernel(x_prompt, x_sample, c_prompt, c_sample, cache_att_k, cache_att_v, state_rwkv, state_shift, state_ffn_conv, norm_att_g, norm_ffn_g, w_ada, b_ada, w_in, q_norm_g, k_norm_g, rel_bias, mu_shift, w0, w2, a0, a2, g2, k_k, k_a, r_k, ln_x_w, ln_x_b, w_out, w_up, dw_conv, dw_bias, w_down):
    depth = w_in.shape[0]
    bp, tp, d = x_prompt.shape
    bs, ts, _ = x_sample.shape
    d_ff = w_down.shape[1]
    hp, hs = x_prompt, x_sample
    outs_p = [[] for _ in range(5)]
    outs_s = [[] for _ in range(5)]
    for l in range(depth):
        p = dict(norm_att_g=norm_att_g[l], norm_ffn_g=norm_ffn_g[l], w_in=w_in[l], q_norm_g=q_norm_g[l],
                 k_norm_g=k_norm_g[l], mu_shift=mu_shift[l], w0=w0[l], w2=w2[l], a0=a0[l], a2=a2[l],
                 g2=g2[l], k_k=k_k[l], k_a=k_a[l], r_k=r_k[l], ln_x_w=ln_x_w[l], ln_x_b=ln_x_b[l],
                 w_out=w_out[l], w_up=w_up[l], dw_conv=dw_conv[l], dw_bias=dw_bias[l], w_down=w_down[l])
        n_c = bp + bs
        pad = (-n_c) % 8
        c_all = jnp.concatenate([c_prompt, c_sample, jnp.zeros((pad, d), F32)], axis=0)
        mod = _ada(c_all, w_ada[l], b_ada[l])
        mod_p = _Mod(mod.reshape(n_c + pad, 6, 1, d), False, rows_per_batch=tp)
        mod_s = _Mod(jnp.repeat(mod[bp:bp + bs], ts, axis=0), True)
        u = _bias_rows(rel_bias[l])

        res_p, res_s = _layer(hp, hs, mod_p, mod_s, p, u, cache_att_k[l], cache_att_v[l],
                              jnp.zeros((bp, N_RWKV_HEADS, HEAD_DIM, HEAD_DIM), F32), state_rwkv[l],
                              jnp.zeros((bp, D_SHIFT), F32), state_shift[l],
                              jnp.zeros((bp, CONV_W - 1, d_ff), F32), state_ffn_conv[l])
        hp, hs = res_p[0], res_s[0]
        for lst, val in zip(outs_p, res_p[1:]):
            lst.append(val)
        for lst, val in zip(outs_s, res_s[1:]):
            lst.append(val)
    st = lambda lst: jnp.stack(lst)
    return (hp, hs, *[st(x) for x in outs_p], *[st(x) for x in outs_s])
```

```python
import functools

import jax
import jax.numpy as jnp
from jax import lax
from jax.experimental import pallas as pl
from jax.experimental.pallas import tpu as pltpu

F32 = jnp.float32
BF16 = jnp.bfloat16

CHUNK = 64
N_PREV_CHUNKS = 8
ATT_REACH = N_PREV_CHUNKS * CHUNK
HEAD_DIM = 64
N_ATT_HEADS = 16
N_RWKV_HEADS = 16
D_ATT = N_ATT_HEADS * HEAD_DIM
D_RWKV = N_RWKV_HEADS * HEAD_DIM
REL_CLIP = 128
RANK_W = 64
RANK_A = 64
RANK_G = 128
D_LORA = RANK_W + RANK_A + RANK_G
D_SHIFT = 3 * D_RWKV + D_LORA
D_IN = 3 * D_ATT + D_SHIFT
CONV_W = 3
RMS_EPS = 1e-6
GN_EPS = 64e-5
ATT_SCALE = HEAD_DIM ** -0.5
LOG2E = 1.4426950408889634
DECAY_SCALE = 0.6065306597126334

LANES = 128
PAIR = 2 * HEAD_DIM
MXU_DIM = 256
VMEM_LIMIT = 60 * 1024 * 1024

ROW_TILE = 1024
IN_ROW_TILE = 2048
ADA_COLS = 1024
IN_COLS = 256
OUT_ROW_TILE = 512
UP_COLS = 512
DOWN_COLS = 512
ATT_QROWS = 256
ATT_WIN = ATT_QROWS + ATT_REACH
ATT_PAIRS = 4
BIAS_LEN = 1024
RWKV_ROWS = 256
RWKV_PAIRS_PROMPT = 8
RWKV_PAIRS_SAMPLE = 8


def _cparams(sem):
    return pltpu.CompilerParams(dimension_semantics=sem, vmem_limit_bytes=VMEM_LIMIT)


def _dot(a, b, dims=(((1,), (0,)), ((), ()))):
    return lax.dot_general(a.astype(BF16), b.astype(BF16), dims, preferred_element_type=F32)


def _split2(x):
    hi = x.astype(BF16)
    lo = (x - hi.astype(F32)).astype(BF16)
    return hi, lo


NT = (((1,), (1,)), ((), ()))
TN = (((0,), (0,)), ((), ()))


def _iota(shape, dim):
    return lax.broadcasted_iota(jnp.int32, shape, dim)


def _blk(x, size):
    return jnp.right_shift(x, size.bit_length() - 1)


def _head_ones(n):
    r = _blk(_iota((n, n), 0), HEAD_DIM)
    c = _blk(_iota((n, n), 1), HEAD_DIM)
    return jnp.where(r == c, 1.0, 0.0).astype(BF16)


def _head_sums(x):
    lanes = x.shape[1]
    group = min(lanes, MXU_DIM)
    ones = _head_ones(group)
    parts = [_dot(x[:, i:i + group], ones) for i in range(0, lanes, group)]
    return parts[0] if len(parts) == 1 else jnp.concatenate(parts, axis=1)


def _sigmoid(x):
    return 1.0 / (1.0 + jnp.exp(-x))


def _ada_kernel(c_ref, w_ref, b_ref, o_ref):
    c = c_ref[...]
    s = c * _sigmoid(c)
    o_ref[...] = _dot(s, w_ref[...]) + b_ref[...]


def _ada(c_all, w_ada, b_ada):
    rows, d = c_all.shape
    n = w_ada.shape[1]
    return pl.pallas_call(
        _ada_kernel,
        grid=(n // ADA_COLS,),
        in_specs=[
            pl.BlockSpec((rows, d), lambda j: (0, 0)),
            pl.BlockSpec((d, ADA_COLS), lambda j: (0, j)),
            pl.BlockSpec((1, ADA_COLS), lambda j: (0, j)),
        ],
        out_specs=pl.BlockSpec((rows, ADA_COLS), lambda j: (0, j)),
        out_shape=jax.ShapeDtypeStruct((rows, n), F32),
        compiler_params=_cparams(("arbitrary",)),
        name="ada_mod",
    )(c_all, w_ada, b_ada.reshape(1, n))


class _Mod:
    def __init__(self, arr, per_row, rows_per_batch=None):
        self.arr = arr
        self.per_row = per_row
        self.rows_per_batch = rows_per_batch

    def spec(self, idx, cols, col_of, row_tile):
        if self.per_row:
            m = self.arr.shape[0]
            d = self.arr.shape[1] // 6
            nblk = d // cols
            return pl.BlockSpec((m, cols), lambda i, j: (0, idx * nblk + col_of(j)))
        tiles_per_batch = self.rows_per_batch // row_tile
        return pl.BlockSpec((None, None, 1, cols),
                            lambda i, j: (i // tiles_per_batch, idx, 0, col_of(j)))

    def rider_spec(self, idx, cols, nj):
        m = self.arr.shape[0]
        nblk = self.arr.shape[1] // 6 // cols
        return pl.BlockSpec((m, cols), lambda i, j: (0, idx * nblk + _rider_col(i, j, nj)))


def _rider_col(i, j, nj):
    return jnp.where(i == 0, j, nj - 1)


NORM_ROWS = 128


def _store_normed(h_ref, x_ref, g_ref, sc_ref, sh_ref):
    rows = x_ref.shape[0]
    step = min(NORM_ROWS, rows)
    per_row = sc_ref.shape[0] == rows

    def body(r, carry):
        sl = pl.ds(pl.multiple_of(r * step, step), step)
        x = x_ref[sl, :]
        ms = jnp.mean(x * x, axis=-1, keepdims=True)
        xn = x * lax.rsqrt(ms + RMS_EPS) * g_ref[...]
        sc = sc_ref[sl, :] if per_row else sc_ref[...]
        sh = sh_ref[sl, :] if per_row else sh_ref[...]
        h_ref[sl, :] = (xn * (1.0 + sc) + sh).astype(BF16)
        return carry

    lax.fori_loop(0, rows // step, body, 0)


def _norm_proj_kernel(x_ref, g_ref, sc_ref, sh_ref, w_ref, xs_ref, scs_ref, shs_ref, o_ref, os_ref, h_ref):
    i = pl.program_id(0)
    j = pl.program_id(1)
    tm = x_ref.shape[0]
    host = h_ref.at[0:tm]
    riders = h_ref.at[tm:]

    @pl.when(j == 0)
    def _():
        _store_normed(host, x_ref, g_ref, sc_ref, sh_ref)

    @pl.when((i == 0) & (j == 0))
    def _():
        _store_normed(riders, xs_ref, g_ref, scs_ref, shs_ref)

    @pl.when(i == 0)
    def _():
        both = jnp.dot(h_ref[...], w_ref[...].astype(BF16), preferred_element_type=F32)
        o_ref[...] = both[0:tm]
        os_ref[...] = both[tm:]

    @pl.when(i > 0)
    def _():
        o_ref[...] = jnp.dot(host[...], w_ref[...].astype(BF16), preferred_element_type=F32)


def _norm_proj(x, xs, gain, mod, mod_s, sc_idx, sh_idx, w, cols, name, row_tile):
    m, d = x.shape
    ms = xs.shape[0]
    n = w.shape[1]
    tm = min(row_tile, m)
    nj = n // cols
    whole = lambda j: 0
    x_mode = dict(pipeline_mode=pl.Buffered(1)) if tm > ROW_TILE else {}
    return pl.pallas_call(
        _norm_proj_kernel,
        grid=(m // tm, nj),
        in_specs=[
            pl.BlockSpec((tm, d), lambda i, j: (i, 0), **x_mode),
            pl.BlockSpec((1, d), lambda i, j: (0, 0)),
            mod.spec(sc_idx, d, whole, tm),
            mod.spec(sh_idx, d, whole, tm),
            pl.BlockSpec((d, cols), lambda i, j: (0, j)),
            pl.BlockSpec((ms, d), lambda i, j: (0, 0)),
            mod_s.spec(sc_idx, d, whole, ms),
            mod_s.spec(sh_idx, d, whole, ms),
        ],
        out_specs=[
            pl.BlockSpec((tm, cols), lambda i, j: (i, j)),
            pl.BlockSpec((ms, cols), lambda i, j: (0, _rider_col(i, j, nj))),
        ],
        out_shape=[jax.ShapeDtypeStruct((m, n), F32), jax.ShapeDtypeStruct((ms, n), F32)],
        scratch_shapes=[pltpu.VMEM((tm + ms, d), BF16)],
        compiler_params=_cparams(("arbitrary", "arbitrary")),
        name=name,
    )(x, gain.reshape(1, d), mod.arr, mod.arr, w, xs, mod_s.arr, mod_s.arr)


def _proj_resid_kernel(a_ref, w_ref, x_ref, g_ref, as_ref, xs_ref, gs_ref, o_ref, os_ref):
    @pl.when(pl.program_id(0) == 0)
    def _():
        acc = jnp.dot(as_ref[...], w_ref[...], preferred_element_type=F32)
        os_ref[...] = xs_ref[...] + gs_ref[...] * acc

    acc = jnp.dot(a_ref[...], w_ref[...], preferred_element_type=F32)
    o_ref[...] = x_ref[...] + g_ref[...] * acc


def _proj_resid(a, a_s, w, x, xs, mod, mod_s, g_idx, cols, name):
    m, n = x.shape
    ms = xs.shape[0]
    kdim = a.shape[1]
    tm = min(ROW_TILE, m)
    nj = n // cols
    rider_block = pl.BlockSpec((ms, cols), lambda i, j: (0, _rider_col(i, j, nj)))
    return pl.pallas_call(
        _proj_resid_kernel,
        grid=(m // tm, nj),
        in_specs=[
            pl.BlockSpec((tm, kdim), lambda i, j: (i, 0)),
            pl.BlockSpec((kdim, cols), lambda i, j: (0, j)),
            pl.BlockSpec((tm, cols), lambda i, j: (i, j)),
            mod.spec(g_idx, cols, lambda j: j, tm),
            pl.BlockSpec((ms, kdim), lambda i, j: (0, 0)),
            rider_block,
            mod_s.rider_spec(g_idx, cols, nj),
        ],
        out_specs=[pl.BlockSpec((tm, cols), lambda i, j: (i, j)), rider_block],
        out_shape=[jax.ShapeDtypeStruct((m, n), F32), jax.ShapeDtypeStruct((ms, n), F32)],
        compiler_params=_cparams(("arbitrary", "arbitrary")),
        name=name,
    )(a, w, x, mod.arr, a_s, xs, mod_s.arr)


def _out_proj_kernel(a1_ref, a2_ref, w_ref, x_ref, g_ref, gain_ref, sc_ref, sh_ref, o_ref, h_ref, wb_ref):
    @pl.when(pl.program_id(0) == 0)
    def _():
        step = MXU_DIM

        def cast_rows(r, carry):
            sl = pl.ds(pl.multiple_of(r * step, step), step)
            wb_ref[sl, :] = w_ref[sl, :].astype(BF16)
            return carry

        lax.fori_loop(0, w_ref.shape[0] // step, cast_rows, 0)

    k1 = a1_ref.shape[1]
    rows = x_ref.shape[0]
    step = min(MXU_DIM, rows)
    per_row = sc_ref.shape[0] == rows
    pieces = [slice(r0, r0 + step) for r0 in range(0, rows, step)]
    x1s = []
    for sl in pieces:
        acc = jnp.dot(a1_ref[sl, :], wb_ref[0:k1], preferred_element_type=F32) \
            + jnp.dot(a2_ref[sl, :], wb_ref[k1:], preferred_element_type=F32)
        g = g_ref[sl, :] if per_row else g_ref[...]
        x1 = x_ref[sl, :] + g * acc
        o_ref[sl, :] = x1
        x1s.append(x1)
    for sl, x1 in zip(pieces, x1s):
        ms = jnp.mean(x1 * x1, axis=-1, keepdims=True)
        xn = x1 * lax.rsqrt(ms + RMS_EPS) * gain_ref[...]
        sc = sc_ref[sl, :] if per_row else sc_ref[...]
        sh = sh_ref[sl, :] if per_row else sh_ref[...]
        h_ref[sl, :] = (xn * (1.0 + sc) + sh).astype(BF16)


def _out_proj(a1, a2, w, x, mod, gain, row_tile):
    m, d = x.shape
    tm = min(row_tile, m)
    whole = lambda j: 0
    row = lambda kdim: pl.BlockSpec((tm, kdim), lambda i, j: (i, 0))
    return pl.pallas_call(
        _out_proj_kernel,
        grid=(m // tm, 1),
        in_specs=[
            row(a1.shape[1]), row(a2.shape[1]),
            pl.BlockSpec(w.shape, lambda i, j: (0, 0), pipeline_mode=pl.Buffered(1)),
            row(d),
            mod.spec(2, d, whole, tm),
            pl.BlockSpec((1, d), lambda i, j: (0, 0)),
            mod.spec(4, d, whole, tm),
            mod.spec(3, d, whole, tm),
        ],
        out_specs=[row(d), row(d)],
        out_shape=[jax.ShapeDtypeStruct((m, d), F32), jax.ShapeDtypeStruct((m, d), BF16)],
        scratch_shapes=[pltpu.VMEM(w.shape, BF16)],
        compiler_params=_cparams(("arbitrary", "arbitrary")),
        name="out_proj",
    )(a1, a2, w, x, mod.arr, gain.reshape(1, d), mod.arr, mod.arr)


def _gelu(x):
    return 0.5 * x * (1.0 + lax.erf(x * (2.0 ** -0.5)))


def _ffn_up_kernel(tiles_per_batch, h_ref, wg_ref, wv_ref, hist_ref, cw_ref, cb_ref, hs_ref, wd_ref,
                   act_ref, last_ref, gs_ref, vs_ref, wdb_ref, carry_ref):
    i = pl.program_id(0)
    j = pl.program_id(1)

    @pl.when(i == 0)
    def _():
        hs = hs_ref[...]
        gs_ref[...] = jnp.dot(hs, wg_ref[...].astype(BF16), preferred_element_type=F32)
        vs_ref[...] = jnp.dot(hs, wv_ref[...].astype(BF16), preferred_element_type=F32)
        wdb_ref[...] = wd_ref[...].astype(BF16)

    @pl.when((i % tiles_per_batch) == 0)
    def _():
        carry_ref[j] = hist_ref[...]

    h = h_ref[...]
    gate = jnp.dot(h, wg_ref[...].astype(BF16), preferred_element_type=F32)
    val = jnp.dot(h, wv_ref[...].astype(BF16), preferred_element_type=F32)
    tm = gate.shape[0]
    prev = carry_ref[j]
    row = _iota(gate.shape, 0)
    g1 = pltpu.roll(gate, 1, 0)
    g2 = pltpu.roll(gate, 2, 0)
    g1 = jnp.where(row == 0, prev[1:2], g1)
    g2 = jnp.where(row == 0, prev[0:1], jnp.where(row == 1, prev[1:2], g2))
    cw = cw_ref[...]
    conv = cb_ref[...] + g2 * cw[0:1] + g1 * cw[1:2] + gate * cw[2:3]
    act_ref[...] = (_gelu(conv) * val).astype(BF16)
    tail = gate[tm - 2:tm]
    carry_ref[j] = tail
    last_ref[...] = tail


def _ffn_up_fused(h, hs, w_up, w_down, hist, conv_w, conv_b, rows_per_batch):
    m, d = h.shape
    ms = hs.shape[0]
    f = w_up.shape[1] // 2
    tm = min(ROW_TILE, rows_per_batch)
    cols = UP_COLS
    nj = f // cols
    tiles_per_batch = rows_per_batch // tm
    rider_block = pl.BlockSpec((ms, cols), lambda i, j: (0, _rider_col(i, j, nj)))
    wd_block = pl.BlockSpec((f // nj, w_down.shape[1]), lambda i, j: (_rider_col(i, j, nj), 0))
    act, tile_tails, gate_s, val_s, w_down_bf16 = pl.pallas_call(
        functools.partial(_ffn_up_kernel, tiles_per_batch),
        grid=(m // tm, nj),
        in_specs=[
            pl.BlockSpec((tm, d), lambda i, j: (i, 0)),
            pl.BlockSpec((d, cols), lambda i, j: (0, j)),
            pl.BlockSpec((d, cols), lambda i, j: (0, nj + j)),
            pl.BlockSpec((None, CONV_W - 1, cols), lambda i, j: (i // tiles_per_batch, 0, j)),
            pl.BlockSpec((CONV_W, cols), lambda i, j: (0, j)),
            pl.BlockSpec((1, cols), lambda i, j: (0, j)),
            pl.BlockSpec((ms, d), lambda i, j: (0, 0)),
            wd_block,
        ],
        out_specs=[
            pl.BlockSpec((tm, cols), lambda i, j: (i, j)),
            pl.BlockSpec((None, CONV_W - 1, cols), lambda i, j: (i, 0, j)),
            rider_block, rider_block, wd_block,
        ],
        out_shape=[
            jax.ShapeDtypeStruct((m, f), BF16),
            jax.ShapeDtypeStruct((m // tm, CONV_W - 1, f), F32),
            jax.ShapeDtypeStruct((ms, f), F32),
            jax.ShapeDtypeStruct((ms, f), F32),
            jax.ShapeDtypeStruct(w_down.shape, BF16),
        ],
        scratch_shapes=[pltpu.VMEM((nj, CONV_W - 1, cols), F32)],
        compiler_params=_cparams(("arbitrary", "arbitrary")),
        name="ffn_up_prompt",
    )(h, w_up, w_up, hist, conv_w, conv_b.reshape(1, f), hs, w_down)
    return act, tile_tails[tiles_per_batch - 1::tiles_per_batch], gate_s, val_s, w_down_bf16


def _act_sample_kernel(gate_ref, val_ref, hist_ref, cw_ref, cb_ref, act_ref):
    gate = gate_ref[...]
    hist = hist_ref[...]
    t = _iota(gate.shape, 1)
    g1 = jnp.where(t == 0, hist[:, 1:2], pltpu.roll(gate, 1, 1))
    g2 = jnp.where(t == 0, hist[:, 0:1], jnp.where(t == 1, hist[:, 1:2], pltpu.roll(gate, 2, 1)))
    cw = cw_ref[...]
    conv = cb_ref[...] + g2 * cw[0:1] + g1 * cw[1:2] + gate * cw[2:3]
    act_ref[...] = (_gelu(conv) * val_ref[...]).astype(BF16)


def _act_sample(gate, val, hist, conv_w, conv_b, nb, t):
    f = gate.shape[1]
    cols = UP_COLS
    nj = f // cols
    gate3 = gate.reshape(nb, t, f)
    val3 = val.reshape(nb, t, f)
    act = pl.pallas_call(
        _act_sample_kernel,
        grid=(nj,),
        in_specs=[
            pl.BlockSpec((nb, t, cols), lambda j: (0, 0, j)),
            pl.BlockSpec((nb, t, cols), lambda j: (0, 0, j)),
            pl.BlockSpec((nb, CONV_W - 1, cols), lambda j: (0, 0, j)),
            pl.BlockSpec((CONV_W, cols), lambda j: (0, j)),
            pl.BlockSpec((1, cols), lambda j: (0, j)),
        ],
        out_specs=pl.BlockSpec((nb, t, cols), lambda j: (0, 0, j)),
        out_shape=jax.ShapeDtypeStruct((nb, t, f), BF16),
        compiler_params=_cparams(("arbitrary",)),
        name="ffn_act_sample",
    )(gate3, val3, hist, conv_w, conv_b.reshape(1, f))
    return act.reshape(nb * t, f)


def _pair_rms(x, gain):
    x2 = x * x
    first = _iota(x.shape, 1) < HEAD_DIM
    s0 = jnp.sum(jnp.where(first, x2, 0.0), axis=-1, keepdims=True)
    s1 = jnp.sum(jnp.where(first, 0.0, x2), axis=-1, keepdims=True)
    ms = jnp.where(first, s0, s1) * (1.0 / HEAD_DIM)
    return x * lax.rsqrt(ms + RMS_EPS) * gain


def _bias_rows(table):
    h = table.shape[0]
    far = jnp.broadcast_to(table[:, 2 * REL_CLIP:], (h, ATT_REACH - REL_CLIP))
    mid = table[:, ::-1]
    near_len = BIAS_LEN - ATT_QROWS - (ATT_REACH - REL_CLIP) - (2 * REL_CLIP + 1)
    near = jnp.broadcast_to(table[:, 0:1], (h, near_len))
    wrap = jnp.broadcast_to(table[:, 2 * REL_CLIP:], (h, ATT_QROWS))
    return jnp.concatenate([far, mid, near, wrap], axis=1)


def _toeplitz(u_row, rows):
    return pltpu.roll(jnp.broadcast_to(u_row, (rows, BIAS_LEN)), 0, 1, stride=1, stride_axis=0)


def _attn_prompt_kernel(q_ref, k_ref, v_ref, qg_ref, kg_ref, u_ref, o_ref, kn_ref, vk_ref,
                        bias_ref, kwin_ref, vwin_ref):
    b = pl.program_id(1)
    qb = pl.program_id(2)
    shape = (ATT_QROWS, ATT_WIN)
    pairs = range(q_ref.shape[1] // PAIR)
    cols = [slice(p * PAIR, (p + 1) * PAIR) for p in pairs]
    chains = [(p, h) for p in pairs for h in range(2)]

    @pl.when((b == 0) & (qb == 0))
    def _():
        r = _iota(shape, 0)
        w = _iota(shape, 1)
        chunk_lo = _blk(r, CHUNK) * CHUNK
        in_band = (w >= chunk_lo) & (w < chunk_lo + (ATT_REACH + CHUNK))
        for i, (p, h) in enumerate(chains):
            bias = _toeplitz(u_ref[p, h:h + 1, :], ATT_QROWS)[:, :ATT_WIN]
            bias_ref[i] = jnp.where(in_band, bias * LOG2E, -jnp.inf)

    @pl.when(qb == 0)
    def _():
        kwin_ref[0:ATT_REACH] = jnp.zeros((ATT_REACH, kwin_ref.shape[1]), BF16)
        vwin_ref[0:ATT_REACH] = jnp.zeros((ATT_REACH, vwin_ref.shape[1]), BF16)

    @pl.when(qb > 0)
    def _():
        kwin_ref[0:ATT_REACH] = kwin_ref[ATT_QROWS:ATT_WIN]
        vwin_ref[0:ATT_REACH] = vwin_ref[ATT_QROWS:ATT_WIN]

    kn = [_pair_rms(k_ref[:, c], kg_ref[...]) for c in cols]
    for p in pairs:
        kn_ref[:, cols[p]] = kn[p]
        kwin_ref[ATT_REACH:ATT_WIN, cols[p]] = kn[p].astype(BF16)
    v_new = v_ref[...]
    vk_ref[...] = v_new
    vwin_ref[ATT_REACH:ATT_WIN] = v_new.astype(BF16)

    def attend(mask_start):
        qn = [_pair_rms(q_ref[:, c], qg_ref[...]) * (ATT_SCALE * LOG2E) for c in cols]
        kb = [kwin_ref[:, c] for c in cols]
        vb = [vwin_ref[:, c] for c in cols]
        first = _iota((ATT_QROWS, PAIR), 1) < HEAD_DIM
        first_w = _iota((ATT_WIN, PAIR), 1) < HEAD_DIM
        qh = [jnp.where(first, qn[p], 0.0) if h == 0 else jnp.where(first, 0.0, qn[p]) for p, h in chains]
        s = [_dot(qh[i], kb[p], NT) + bias_ref[i] for i, (p, h) in enumerate(chains)]
        if mask_start:
            started = _iota(shape, 1) >= ATT_REACH - qb * ATT_QROWS
            s = [jnp.where(started, x, -jnp.inf) for x in s]
        m = [jnp.max(x, axis=-1, keepdims=True) for x in s]
        pr = [jnp.exp2(x - mm) for x, mm in zip(s, m)]
        one = jnp.ones((), BF16)
        v_aug = [jnp.where(first_w, vb[p], one) if h == 0 else jnp.where(first_w, one, vb[p]) for p, h in chains]
        o = [_dot(pr[i], v_aug[i]) for i in range(len(chains))]
        o = [x / pltpu.roll(x, HEAD_DIM, 1) for x in o]
        for p in pairs:
            o_ref[:, cols[p]] = jnp.where(first, o[2 * p], o[2 * p + 1]).astype(BF16)

    full_window_from = ATT_REACH // ATT_QROWS
    pl.when(qb < full_window_from)(lambda: attend(True))
    pl.when(qb >= full_window_from)(lambda: attend(False))


def _attn_prompt(z3, q_gain, k_gain, u):
    nb, t, _ = z3.shape
    npairs = N_ATT_HEADS // 2
    npp = ATT_PAIRS
    width = npp * PAIR
    ngroups = npairs // npp
    nq = t // ATT_QROWS
    kcol = D_ATT // width
    vcol = 2 * D_ATT // width
    keep_blocks = ATT_REACH // ATT_QROWS
    blk = (None, ATT_QROWS, width)

    keep_spec = pl.BlockSpec(blk, lambda g, b, q: (b, jnp.maximum(q - (nq - keep_blocks), 0), g))
    att, kn, vk = pl.pallas_call(
        _attn_prompt_kernel,
        grid=(ngroups, nb, nq),
        in_specs=[
            pl.BlockSpec(blk, lambda g, b, q: (b, q, g)),
            pl.BlockSpec(blk, lambda g, b, q: (b, q, kcol + g)),
            pl.BlockSpec(blk, lambda g, b, q: (b, q, vcol + g)),
            pl.BlockSpec((1, PAIR), lambda g, b, q: (0, 0)),
            pl.BlockSpec((1, PAIR), lambda g, b, q: (0, 0)),
            pl.BlockSpec((npp, 2, BIAS_LEN), lambda g, b, q: (g, 0, 0)),
        ],
        out_specs=[
            pl.BlockSpec(blk, lambda g, b, q: (b, q, g)),
            keep_spec, keep_spec,
        ],
        out_shape=[
            jax.ShapeDtypeStruct((nb, t, D_ATT), BF16),
            jax.ShapeDtypeStruct((nb, ATT_REACH, D_ATT), F32),
            jax.ShapeDtypeStruct((nb, ATT_REACH, D_ATT), F32),
        ],
        scratch_shapes=[pltpu.VMEM((2 * npp, ATT_QROWS, ATT_WIN), F32),
                        pltpu.VMEM((ATT_WIN, width), BF16), pltpu.VMEM((ATT_WIN, width), BF16)],
        compiler_params=_cparams(("arbitrary", "arbitrary", "arbitrary")),
        name="attn_prompt",
    )(z3, z3, z3, jnp.tile(q_gain, 2).reshape(1, PAIR),
      jnp.tile(k_gain, 2).reshape(1, PAIR), u.reshape(npairs, 2, BIAS_LEN))
    return att, kn, vk


def _attn_sample_kernel(q_ref, k_ref, v_ref, kp_ref, vp_ref, qg_ref, kg_ref, u_ref, o_ref, kn_ref):
    t = q_ref.shape[0]
    reach = kp_ref.shape[0]
    first = _iota((t, PAIR), 1) < HEAD_DIM
    pairs = range(N_ATT_HEADS // 2)
    cols = [slice(p * PAIR, (p + 1) * PAIR) for p in pairs]
    chains = [(p, h) for p in pairs for h in range(2)]
    qn = [_pair_rms(q_ref[:, c], qg_ref[...]) * ATT_SCALE for c in cols]
    kn = [_pair_rms(k_ref[:, c], kg_ref[...]) for c in cols]
    for p in pairs:
        kn_ref[:, cols[p]] = kn[p]
    kpast = [kp_ref[:, c].astype(BF16) for c in cols]
    vpast = [vp_ref[:, c].astype(BF16) for c in cols]
    vnew = [v_ref[:, c].astype(BF16) for c in cols]
    qh = [jnp.where(first, qn[p], 0.0) if h == 0 else jnp.where(first, 0.0, qn[p]) for p, h in chains]
    bias = [_toeplitz(u_ref[p, h:h + 1, :], t) for p, h in chains]
    s_past = [_dot(qh[i], kpast[p], NT) + bias[i][:, :reach] for i, (p, h) in enumerate(chains)]
    s_new = [_dot(qh[i], kn[p], NT) + bias[i][:, reach:reach + t] for i, (p, h) in enumerate(chains)]
    m = [jnp.maximum(jnp.max(a, axis=-1, keepdims=True), jnp.max(b, axis=-1, keepdims=True))
         for a, b in zip(s_past, s_new)]
    p_past = [jnp.exp(a - mm) for a, mm in zip(s_past, m)]
    p_new = [jnp.exp(b - mm) for b, mm in zip(s_new, m)]
    l = [jnp.sum(a, axis=-1, keepdims=True) + jnp.sum(b, axis=-1, keepdims=True)
         for a, b in zip(p_past, p_new)]
    o = [(_dot(p_past[i], vpast[p]) + _dot(p_new[i], vnew[p])) / l[i] for i, (p, h) in enumerate(chains)]
    for p in pairs:
        o_ref[:, cols[p]] = jnp.where(first, o[2 * p], o[2 * p + 1]).astype(BF16)


def _attn_sample(z3, k_past, v_past, q_gain, k_gain, u):
    nb, t, _ = z3.shape
    reach = k_past.shape[1]
    npairs = N_ATT_HEADS // 2
    att, kn = pl.pallas_call(
        _attn_sample_kernel,
        grid=(nb,),
        in_specs=[
            pl.BlockSpec((None, t, D_ATT), lambda b: (b, 0, 0)),
            pl.BlockSpec((None, t, D_ATT), lambda b: (b, 0, 1)),
            pl.BlockSpec((None, t, D_ATT), lambda b: (b, 0, 2)),
            pl.BlockSpec((None, reach, D_ATT), lambda b: (b, 0, 0)),
            pl.BlockSpec((None, reach, D_ATT), lambda b: (b, 0, 0)),
            pl.BlockSpec((1, PAIR), lambda b: (0, 0)),
            pl.BlockSpec((1, PAIR), lambda b: (0, 0)),
            pl.BlockSpec((npairs, 2, BIAS_LEN), lambda b: (0, 0, 0)),
        ],
        out_specs=[
            pl.BlockSpec((None, t, D_ATT), lambda b: (b, 0, 0)),
            pl.BlockSpec((None, t, D_ATT), lambda b: (b, 0, 0)),
        ],
        out_shape=[
            jax.ShapeDtypeStruct((nb, t, D_ATT), BF16),
            jax.ShapeDtypeStruct((nb, t, D_ATT), F32),
        ],
        compiler_params=_cparams(("arbitrary",)),
        name="attn_sample",
    )(z3, z3, z3, k_past.reshape(nb, reach, D_ATT), v_past.reshape(nb, reach, D_ATT),
      jnp.tile(q_gain, 2).reshape(1, PAIR), jnp.tile(k_gain, 2).reshape(1, PAIR),
      u.reshape(npairs, 2, BIAS_LEN))
    return att, kn


def _tri_inverse(l_mats, c):
    n = l_mats[0].shape[0]
    eye = jnp.where(_iota((n, n), 0) == _iota((n, n), 1), 1.0, 0.0).astype(F32)
    a_s = [(eye + l).astype(BF16) for l in l_mats]
    t_s = [eye - l for l in l_mats]
    for _ in range(c.bit_length() - 2):
        r_s = [eye - _dot(a, t) for a, t in zip(a_s, t_s)]
        t_s = [t + _dot(t, r) for t, r in zip(t_s, r_s)]
    return t_s


def _rwkv_kernel(c, r_ref, k_ref, v_ref, lo_ref, sr_ref, sk_ref, sv_ref, slo_ref, s0_ref,
                 mur_ref, muk_ref, muv_ref, mulo_ref, w0_ref, a0_ref, kkg_ref, ka_ref, rk_ref,
                 lnw_ref, lnb_ref, w2_ref, a2_ref, g2_ref,
                 o_ref, sT_ref, s_ref, cr_ref, ck_ref, cv_ref, clo_ref):
    tb = pl.program_id(2)
    rows, width = r_ref.shape
    npp = width // PAIR
    nchunks = rows // c
    h0 = _iota((rows, PAIR), 1) < HEAD_DIM
    bd = _blk(_iota((PAIR, PAIR), 0), HEAD_DIM) == _blk(_iota((PAIR, PAIR), 1), HEAD_DIM)

    @pl.when(tb == 0)
    def _():
        s_ref[...] = jnp.zeros(s_ref.shape, F32)
        for pp in range(npp):
            s_ref[pp, 0:HEAD_DIM, 0:HEAD_DIM] = s0_ref[2 * pp]
            s_ref[pp, HEAD_DIM:PAIR, HEAD_DIM:PAIR] = s0_ref[2 * pp + 1]
        cr_ref[...] = sr_ref[...]
        ck_ref[...] = sk_ref[...]
        cv_ref[...] = sv_ref[...]
        clo_ref[...] = slo_ref[...]

    def shifted(x_ref, carry_ref, mu_ref):
        x = x_ref[...]
        prev = jnp.where(_iota(x.shape, 0) == 0, carry_ref[...], pltpu.roll(x, 1, 0))
        carry_ref[...] = x[rows - 1:rows]
        return x + (prev - x) * mu_ref[...]

    r = shifted(r_ref, cr_ref, mur_ref)
    k = shifted(k_ref, ck_ref, muk_ref)
    v = shifted(v_ref, cv_ref, muv_ref)
    lo = shifted(lo_ref, clo_ref, mulo_ref)

    zeros_w = jnp.zeros((RANK_W, width), F32)
    w2p = jnp.concatenate([w2_ref[...], zeros_w], axis=0)
    a2p = jnp.concatenate([zeros_w, a2_ref[...]], axis=0)
    lo_wa = lo[:, 0:RANK_W + RANK_A]
    u = w0_ref[...] + _dot(jnp.tanh(lo_wa), w2p)
    lw = -DECAY_SCALE * _sigmoid(u)
    a = _sigmoid(a0_ref[...] + _dot(lo_wa, a2p))
    g = _dot(_sigmoid(lo[:, RANK_W + RANK_A:]), g2_ref[...])

    kk = k * kkg_ref[...]
    kk = kk * lax.rsqrt(jnp.maximum(_head_sums(kk * kk), 1e-24))
    k = k * (1.0 + (a - 1.0) * ka_ref[...])
    b = kk * a
    bonus = _head_sums(r * k * rk_ref[...]) * v

    tr = _iota((rows, rows), 0)
    tc = _iota((rows, rows), 1)
    same_chunk = _blk(tr, c) == _blk(tc, c)
    strict = same_chunk & (tr > tc)
    incl = same_chunk & (tr >= tc)
    lw_hi, lw_lo = _split2(lw)
    tril_ones = jnp.where(incl, 1.0, 0.0).astype(BF16)
    lp = jnp.dot(tril_ones, lw_hi, preferred_element_type=F32) + \
        jnp.dot(tril_ones, lw_lo, preferred_element_type=F32)
    decay_end = [jnp.exp(lp[(ci + 1) * c - 1:(ci + 1) * c]) for ci in range(nchunks)]

    alpha_w = kk * jnp.exp(lp - lw)
    inv_p = jnp.exp(-lp)
    beta_w = b * inv_p
    kappa_w = k * inv_p
    rho_w = r * jnp.exp(lp)
    to_end = [inv_p[ci * c:(ci + 1) * c] * decay_end[ci] for ci in range(nchunks)]
    to_end = to_end[0] if nchunks == 1 else jnp.concatenate(to_end, axis=0)
    beta_ew = b * to_end
    kappa_ew = k * to_end

    wide = (rows, nchunks * PAIR)
    col_chunk = _blk(_iota(wide, 1), PAIR) == _blk(_iota(wide, 0), c)
    spread = lambda m: jnp.where(col_chunk, jnp.tile(m, (1, nchunks)), 0.0)
    eye_p = _iota((PAIR, PAIR), 0) == _iota((PAIR, PAIR), 1)

    pairs = range(npp)
    lanes = [slice(pp * PAIR, (pp + 1) * PAIR) for pp in pairs]
    alpha = [alpha_w[:, l] for l in lanes]
    rho = [rho_w[:, l] for l in lanes]
    vv = [v[:, l] for l in lanes]
    head_mask = [h0, jnp.logical_not(h0)]
    bk = [jnp.concatenate([beta_w[:, l], kappa_w[:, l]], axis=0).astype(BF16) for l in lanes]
    prod = [[_dot(jnp.concatenate([jnp.where(hm, alpha[pp], 0.0), jnp.where(hm, rho[pp], 0.0)], axis=0),
                  bk[pp], NT) for hm in head_mask] for pp in pairs]
    t_inv = _tri_inverse([jnp.where(strict, prod[pp][h][:rows, :rows], 0.0) for pp in pairs for h in range(2)], c)
    x = [[_dot(jnp.where(strict, prod[pp][h][:rows, rows:], 0.0), vv[pp]) for h in range(2)] for pp in pairs]
    ws = [[_dot(t_inv[2 * pp + h], jnp.concatenate([alpha[pp], x[pp][h]], axis=1)) for h in range(2)]
          for pp in pairs]
    w12 = [jnp.concatenate([jnp.where(h0, ws[pp][0][:, :PAIR], ws[pp][1][:, :PAIR]),
                            jnp.where(h0, ws[pp][0][:, PAIR:], ws[pp][1][:, PAIR:])], axis=1) for pp in pairs]
    q = [[_dot(jnp.where(incl, prod[pp][h][rows:, :rows], 0.0), w12[pp]) for h in range(2)] for pp in pairs]
    qk = [[_dot(jnp.where(incl, prod[pp][h][rows:, rows:], 0.0), vv[pp]) for h in range(2)] for pp in pairs]
    rp = [rho[pp] - jnp.where(h0, q[pp][0][:, :PAIR], q[pp][1][:, :PAIR]) for pp in pairs]
    y0 = [jnp.where(h0, qk[pp][0] - q[pp][0][:, PAIR:], qk[pp][1] - q[pp][1][:, PAIR:]) for pp in pairs]
    wtb = [_dot(w12[pp], spread(beta_ew[:, lanes[pp]]), TN) for pp in pairs]
    vtk = [_dot(vv[pp], spread(kappa_ew[:, lanes[pp]]), TN) for pp in pairs]

    s_cur = [s_ref[pp] for pp in pairs]
    ys = [[] for _ in pairs]
    for ci in range(nchunks):
        sl = slice(ci * c, (ci + 1) * c)
        cols = slice(ci * PAIR, (ci + 1) * PAIR)
        for pp in pairs:
            p_end = decay_end[ci][:, lanes[pp]]
            gmat = jnp.where(eye_p, jnp.broadcast_to(p_end, (PAIR, PAIR)), 0.0) \
                - jnp.where(bd, wtb[pp][:PAIR, cols], 0.0)
            hmat = jnp.where(bd, vtk[pp][:, cols] - wtb[pp][PAIR:, cols], 0.0)
            ys[pp].append(_dot(rp[pp][sl], s_cur[pp], NT) + y0[pp][sl])
            s_cur[pp] = _dot(s_cur[pp], gmat) + hmat
    for pp in pairs:
        s_ref[pp] = s_cur[pp]
    y_pairs = [ys[pp][0] if nchunks == 1 else jnp.concatenate(ys[pp], axis=0) for pp in pairs]

    @pl.when(tb == pl.num_programs(2) - 1)
    def _():
        for pp in range(npp):
            sT_ref[2 * pp] = s_ref[pp, 0:HEAD_DIM, 0:HEAD_DIM]
            sT_ref[2 * pp + 1] = s_ref[pp, HEAD_DIM:PAIR, HEAD_DIM:PAIR]

    y = y_pairs[0] if npp == 1 else jnp.concatenate(y_pairs, axis=1)
    mu = _head_sums(y) * (1.0 / HEAD_DIM)
    d = y - mu
    var = _head_sums(d * d) * (1.0 / HEAD_DIM)
    yn = d * lax.rsqrt(var + GN_EPS) * lnw_ref[...] + lnb_ref[...]
    o_ref[...] = ((yn + bonus) * g).astype(BF16)


def _rwkv(z3, shift_prev, s0, p, rows, c, npp):
    nb, t, _ = z3.shape
    width = npp * PAIR
    ngroups = D_RWKV // width
    col0 = 3 * D_ATT // width
    lo_blk = (3 * D_ATT + 3 * D_RWKV) // D_LORA
    sp = shift_prev.reshape(nb, 1, D_SHIFT)

    def zspec(off):
        return pl.BlockSpec((None, rows, width), lambda b, q, s: (b, s, col0 + off * ngroups + q))

    def sspec(off):
        return pl.BlockSpec((None, 1, width), lambda b, q, s: (b, 0, off * ngroups + q))

    def vec(off=0):
        return pl.BlockSpec((1, width), lambda b, q, s: (0, off * ngroups + q))

    def row2(x):
        return x.reshape(1, -1)

    out, s_fin = pl.pallas_call(
        functools.partial(_rwkv_kernel, c),
        grid=(nb, ngroups, t // rows),
        in_specs=[
            zspec(0), zspec(1), zspec(2),
            pl.BlockSpec((None, rows, D_LORA), lambda b, q, s: (b, s, lo_blk)),
            sspec(0), sspec(1), sspec(2),
            pl.BlockSpec((None, 1, D_LORA), lambda b, q, s: (b, 0, 3 * D_RWKV // D_LORA)),
            pl.BlockSpec((None, 2 * npp, HEAD_DIM, HEAD_DIM), lambda b, q, s: (b, q, 0, 0)),
            vec(0), vec(1), vec(2),
            pl.BlockSpec((1, D_LORA), lambda b, q, s: (0, 3 * D_RWKV // D_LORA)),
            vec(), vec(), vec(), vec(), vec(), vec(), vec(),
            pl.BlockSpec((RANK_W, width), lambda b, q, s: (0, q)),
            pl.BlockSpec((RANK_A, width), lambda b, q, s: (0, q)),
            pl.BlockSpec((RANK_G, width), lambda b, q, s: (0, q)),
        ],
        out_specs=[
            pl.BlockSpec((None, rows, width), lambda b, q, s: (b, s, q)),
            pl.BlockSpec((None, 2 * npp, HEAD_DIM, HEAD_DIM), lambda b, q, s: (b, q, 0, 0)),
        ],
        out_shape=[
            jax.ShapeDtypeStruct((nb, t, D_RWKV), BF16),
            jax.ShapeDtypeStruct((nb, N_RWKV_HEADS, HEAD_DIM, HEAD_DIM), F32),
        ],
        scratch_shapes=[
            pltpu.VMEM((npp, PAIR, PAIR), F32),
            pltpu.VMEM((1, width), F32), pltpu.VMEM((1, width), F32), pltpu.VMEM((1, width), F32),
            pltpu.VMEM((1, D_LORA), F32),
        ],
        compiler_params=_cparams(("arbitrary", "arbitrary", "arbitrary")),
        name="rwkv7_mix",
    )(z3, z3, z3, z3, sp, sp, sp, sp, s0,
      row2(p['mu_shift']), row2(p['mu_shift']), row2(p['mu_shift']), row2(p['mu_shift']),
      row2(p['w0']), row2(p['a0']), row2(p['k_k']), row2(p['k_a']), row2(p['r_k']),
      row2(p['ln_x_w']), row2(p['ln_x_b']), p['w2'], p['a2'], p['g2'])
    return out, s_fin


def _layer(xp3, xs3, mod_p, mod_s, p, u, k_past, v_past, s0_p, s0_s, shift_p, shift_s, conv_p, conv_s):
    bp, tp, d = xp3.shape
    bs, ts, _ = xs3.shape
    mp, msr = bp * tp, bs * ts
    xp = xp3.reshape(mp, d)
    xs = xs3.reshape(msr, d)
    zp, zs = _norm_proj(xp, xs, p['norm_att_g'], mod_p, mod_s, 1, 0, p['w_in'], IN_COLS, "in_proj",
                        min(IN_ROW_TILE, tp))
    zp3 = zp.reshape(bp, tp, D_IN)
    zs3 = zs.reshape(bs, ts, D_IN)
    att_p, k_keep_p, v_keep_p = _attn_prompt(zp3, p['q_norm_g'], p['k_norm_g'], u)
    rw_p, s_fin_p = _rwkv(zp3, shift_p, s0_p, p, RWKV_ROWS, CHUNK, RWKV_PAIRS_PROMPT)
    att_s, k_keep_s = _attn_sample(zs3, k_past, v_past, p['q_norm_g'], p['k_norm_g'], u)
    v_keep_s = zs3[:, :, 2 * D_ATT:3 * D_ATT]
    rw_s, s_fin_s = _rwkv(zs3, shift_s, s0_s, p, ts, ts, RWKV_PAIRS_SAMPLE)
    x1p, h2p = _out_proj(att_p.reshape(mp, D_ATT), rw_p.reshape(mp, D_RWKV), p['w_out'], xp, mod_p,
                         p['norm_ffn_g'], OUT_ROW_TILE)
    x1s, h2s = _out_proj(att_s.reshape(msr, D_ATT), rw_s.reshape(msr, D_RWKV), p['w_out'], xs, mod_s,
                         p['norm_ffn_g'], OUT_ROW_TILE)
    act_p, conv_last_p, gate_s, val_s, w_down = _ffn_up_fused(h2p, h2s, p['w_up'], p['w_down'], conv_p,
                                                              p['dw_conv'], p['dw_bias'], tp)
    act_s = _act_sample(gate_s, val_s, conv_s, p['dw_conv'], p['dw_bias'], bs, ts)
    conv_last_s = gate_s.reshape(bs, ts, -1)[:, ts - (CONV_W - 1):]
    x2p, x2s = _proj_resid(act_p, act_s, w_down, x1p, x1s, mod_p, mod_s, 5, DOWN_COLS, "ffn_down")
    heads = lambda a: a.reshape(a.shape[0], a.shape[1], N_ATT_HEADS, HEAD_DIM)
    out_p = (x2p.reshape(bp, tp, d), heads(k_keep_p), heads(v_keep_p), s_fin_p, zp3[:, tp - 1, 3 * D_ATT:],
             conv_last_p)
    out_s = (x2s.reshape(bs, ts, d), heads(k_keep_s), heads(v_keep_s), s_fin_s, zs3[:, ts - 1, 3 * D_ATT:],
             conv_last_s)
    return out_p, out_s


def kernel(x_prompt, x_sample, c_prompt, c_sample, cache_att_k, cache_att_v, state_rwkv, state_shift, state_ffn_conv, norm_att_g, norm_ffn_g, w_ada, b_ada, w_in, q_norm_g, k_norm_g, rel_bias, mu_shift, w0, w2, a0, a2, g2, k_k, k_a, r_k, ln_x_w, ln_x_b, w_out, w_up, dw_conv, dw_bias, w_down):
    depth = w_in.shape[0]
    bp, tp, d = x_prompt.shape
    bs, ts, _ = x_sample.shape
    d_ff = w_down.shape[1]
    hp, hs = x_prompt, x_sample
    outs_p = [[] for _ in range(5)]
    outs_s = [[] for _ in range(5)]
    for l in range(depth):
        p = dict(norm_att_g=norm_att_g[l], norm_ffn_g=norm_ffn_g[l], w_in=w_in[l], q_norm_g=q_norm_g[l],
                 k_norm_g=k_norm_g[l], mu_shift=mu_shift[l], w0=w0[l], w2=w2[l], a0=a0[l], a2=a2[l],
                 g2=g2[l], k_k=k_k[l], k_a=k_a[l], r_k=r_k[l], ln_x_w=ln_x_w[l], ln_x_b=ln_x_b[l],
                 w_out=w_out[l], w_up=w_up[l], dw_conv=dw_conv[l], dw_bias=dw_bias[l], w_down=w_down[l])
        n_c = bp + bs
        pad = (-n_c) % 8
        c_all = jnp.concatenate([c_prompt, c_sample, jnp.zeros((pad, d), F32)], axis=0)
        mod = _ada(c_all, w_ada[l], b_ada[l])
        mod_p = _Mod(mod.reshape(n_c + pad, 6, 1, d), False, rows_per_batch=tp)
        mod_s = _Mod(jnp.repeat(mod[bp:bp + bs], ts, axis=0), True)
        u = _bias_rows(rel_bias[l])

        res_p, res_s = _layer(hp, hs, mod_p, mod_s, p, u, cache_att_k[l], cache_att_v[l],
                              jnp.zeros((bp, N_RWKV_HEADS, HEAD_DIM, HEAD_DIM), F32), state_rwkv[l],
                              jnp.zeros((bp, D_SHIFT), F32), state_shift[l],
                              jnp.zeros((bp, CONV_W - 1, d_ff), F32), state_ffn_conv[l])
        hp, hs = res_p[0], res_s[0]
        for lst, val in zip(outs_p, res_p[1:]):
            lst.append(val)
        for lst, val in zip(outs_s, res_s[1:]):
            lst.append(val)
    st = lambda lst: jnp.stack(lst)
    return (hp, hs, *[st(x) for x in outs_p], *[st(x) for x in outs_s])
```

```python
import functools

import jax
import jax.numpy as jnp
from jax import lax
from jax.experimental import pallas as pl
from jax.experimental.pallas import tpu as pltpu

F32 = jnp.float32
BF16 = jnp.bfloat16

CHUNK = 64
N_PREV_CHUNKS = 8
ATT_REACH = N_PREV_CHUNKS * CHUNK
HEAD_DIM = 64
N_ATT_HEADS = 16
N_RWKV_HEADS = 16
D_ATT = N_ATT_HEADS * HEAD_DIM
D_RWKV = N_RWKV_HEADS * HEAD_DIM
REL_CLIP = 128
RANK_W = 64
RANK_A = 64
RANK_G = 128
D_LORA = RANK_W + RANK_A + RANK_G
D_SHIFT = 3 * D_RWKV + D_LORA
D_IN = 3 * D_ATT + D_SHIFT
CONV_W = 3
RMS_EPS = 1e-6
GN_EPS = 64e-5
KK_EPS = 1e-12
ATT_SCALE = HEAD_DIM ** -0.5
LOG2E = 1.4426950408889634
DECAY_SCALE = 0.6065306597126334

PAIR = 2 * HEAD_DIM
MXU_DIM = 256
VMEM_LIMIT = 60 * 1024 * 1024

ROW_TILE = 1024
IN_ROW_TILE = 2048
ADA_COLS = 1024
IN_COLS = 512
OUT_ROW_TILE = 512
UP_COLS = 512
DOWN_COLS = 512
ATT_QROWS = 256
ATT_WIN = ATT_QROWS + ATT_REACH
ATT_PAIRS = 4
BIAS_LEN = 1024
RWKV_ROWS = 256
RWKV_PAIRS_PROMPT = 8
RWKV_PAIRS_SAMPLE = 8


def _cparams(sem):
    return pltpu.CompilerParams(dimension_semantics=sem, vmem_limit_bytes=VMEM_LIMIT)


def _dot(a, b, dims=(((1,), (0,)), ((), ()))):
    return lax.dot_general(a.astype(BF16), b.astype(BF16), dims, preferred_element_type=F32)


def _split2(x):
    hi = x.astype(BF16)
    lo = (x - hi.astype(F32)).astype(BF16)
    return hi, lo


NT = (((1,), (1,)), ((), ()))
TN = (((0,), (0,)), ((), ()))


def _iota(shape, dim):
    return lax.broadcasted_iota(jnp.int32, shape, dim)


def _blk(x, size):
    return jnp.right_shift(x, size.bit_length() - 1)


def _head_ones(n):
    r = _blk(_iota((n, n), 0), HEAD_DIM)
    c = _blk(_iota((n, n), 1), HEAD_DIM)
    return jnp.where(r == c, 1.0, 0.0).astype(BF16)


def _head_sums(x):
    lanes = x.shape[1]
    group = min(lanes, MXU_DIM)
    ones = _head_ones(group)
    parts = [_dot(x[:, i:i + group], ones) for i in range(0, lanes, group)]
    return parts[0] if len(parts) == 1 else jnp.concatenate(parts, axis=1)


def _sigmoid(x):
    return 1.0 / (1.0 + jnp.exp(-x))


def _ada_kernel(c_ref, w_ref, b_ref, o_ref):
    c = c_ref[...]
    s = c * _sigmoid(c)
    o_ref[...] = _dot(s, w_ref[...]) + b_ref[...]


def _ada(c_all, w_ada, b_ada):
    rows, d = c_all.shape
    n = w_ada.shape[1]
    return pl.pallas_call(
        _ada_kernel,
        grid=(n // ADA_COLS,),
        in_specs=[
            pl.BlockSpec((rows, d), lambda j: (0, 0)),
            pl.BlockSpec((d, ADA_COLS), lambda j: (0, j)),
            pl.BlockSpec((1, ADA_COLS), lambda j: (0, j)),
        ],
        out_specs=pl.BlockSpec((rows, ADA_COLS), lambda j: (0, j)),
        out_shape=jax.ShapeDtypeStruct((rows, n), F32),
        compiler_params=_cparams(("arbitrary",)),
        name="ada_mod",
    )(c_all, w_ada, b_ada.reshape(1, n))


class _Mod:
    def __init__(self, arr, per_row, rows_per_batch=None):
        self.arr = arr
        self.per_row = per_row
        self.rows_per_batch = rows_per_batch

    def spec(self, idx, cols, col_of, row_tile):
        if self.per_row:
            m = self.arr.shape[0]
            d = self.arr.shape[1] // 6
            nblk = d // cols
            return pl.BlockSpec((m, cols), lambda i, j: (0, idx * nblk + col_of(j)))
        tiles_per_batch = self.rows_per_batch // row_tile
        return pl.BlockSpec((None, None, 1, cols),
                            lambda i, j: (i // tiles_per_batch, idx, 0, col_of(j)))

    def rider_spec(self, idx, cols, nj):
        m = self.arr.shape[0]
        nblk = self.arr.shape[1] // 6 // cols
        return pl.BlockSpec((m, cols), lambda i, j: (0, idx * nblk + _rider_col(i, j, nj)))


def _rider_col(i, j, nj):
    return jnp.where(i == 0, j, nj - 1)


NORM_ROWS = 128


def _store_normed(h_ref, x_ref, g_ref, sc_ref, sh_ref):
    rows = x_ref.shape[0]
    step = min(NORM_ROWS, rows)
    per_row = sc_ref.shape[0] == rows

    def body(r, carry):
        sl = pl.ds(pl.multiple_of(r * step, step), step)
        x = x_ref[sl, :]
        ms = jnp.mean(x * x, axis=-1, keepdims=True)
        xn = x * lax.rsqrt(ms + RMS_EPS) * g_ref[...]
        sc = sc_ref[sl, :] if per_row else sc_ref[...]
        sh = sh_ref[sl, :] if per_row else sh_ref[...]
        h_ref[sl, :] = (xn * (1.0 + sc) + sh).astype(BF16)
        return carry

    lax.fori_loop(0, rows // step, body, 0)


def _norm_proj_kernel(nj, x_ref, g_ref, sc_ref, sh_ref, w_ref, wt_ref, xs_ref, scs_ref, shs_ref,
                      o_ref, ot_ref, os_ref, ost_ref, h_ref):
    i = pl.program_id(0)
    j = pl.program_id(1)
    tm = x_ref.shape[0]
    host = h_ref.at[0:tm]
    riders = h_ref.at[tm:]

    @pl.when(j == 0)
    def _():
        _store_normed(host, x_ref, g_ref, sc_ref, sh_ref)

    @pl.when((i == 0) & (j == 0))
    def _():
        _store_normed(riders, xs_ref, g_ref, scs_ref, shs_ref)

    def project(weight_ref, out_ref, rider_out_ref):
        @pl.when(i == 0)
        def _():
            both = jnp.dot(h_ref[...], weight_ref[...].astype(BF16), preferred_element_type=F32)
            out_ref[...] = both[0:tm]
            rider_out_ref[...] = both[tm:]

        @pl.when(i > 0)
        def _():
            out_ref[...] = jnp.dot(host[...], weight_ref[...].astype(BF16), preferred_element_type=F32)

    pl.when(j < nj)(lambda: project(w_ref, o_ref, os_ref))
    pl.when(j == nj)(lambda: project(wt_ref, ot_ref, ost_ref))


def _norm_proj(x, xs, gain, mod, mod_s, sc_idx, sh_idx, w, cols, name, row_tile):
    m, d = x.shape
    ms = xs.shape[0]
    n = w.shape[1]
    tm = min(row_tile, m)
    nj = n // cols
    n_main = nj * cols
    tail = n - n_main
    whole = lambda j: 0
    main_col = lambda j: jnp.minimum(j, nj - 1)
    x_mode = dict(pipeline_mode=pl.Buffered(1)) if tm > ROW_TILE else {}
    return pl.pallas_call(
        functools.partial(_norm_proj_kernel, nj),
        grid=(m // tm, nj + 1),
        in_specs=[
            pl.BlockSpec((tm, d), lambda i, j: (i, 0), **x_mode),
            pl.BlockSpec((1, d), lambda i, j: (0, 0)),
            mod.spec(sc_idx, d, whole, tm),
            mod.spec(sh_idx, d, whole, tm),
            pl.BlockSpec((d, cols), lambda i, j: (0, main_col(j))),
            pl.BlockSpec((d, tail), lambda i, j: (0, n_main // tail)),
            pl.BlockSpec((ms, d), lambda i, j: (0, 0)),
            mod_s.spec(sc_idx, d, whole, ms),
            mod_s.spec(sh_idx, d, whole, ms),
        ],
        out_specs=[
            pl.BlockSpec((tm, cols), lambda i, j: (i, main_col(j))),
            pl.BlockSpec((tm, tail), lambda i, j: (i, 0)),
            pl.BlockSpec((ms, cols), lambda i, j: (0, _rider_col(i, main_col(j), nj))),
            pl.BlockSpec((ms, tail), lambda i, j: (0, 0)),
        ],
        out_shape=[jax.ShapeDtypeStruct((m, n_main), F32), jax.ShapeDtypeStruct((m, tail), F32),
                   jax.ShapeDtypeStruct((ms, n_main), F32), jax.ShapeDtypeStruct((ms, tail), F32)],
        scratch_shapes=[pltpu.VMEM((tm + ms, d), BF16)],
        compiler_params=_cparams(("arbitrary", "arbitrary")),
        name=name,
    )(x, gain.reshape(1, d), mod.arr, mod.arr, w, w, xs, mod_s.arr, mod_s.arr)


def _proj_resid_kernel(a_ref, w_ref, x_ref, g_ref, as_ref, xs_ref, gs_ref, o_ref, os_ref):
    @pl.when(pl.program_id(0) == 0)
    def _():
        acc = jnp.dot(as_ref[...], w_ref[...], preferred_element_type=F32)
        os_ref[...] = xs_ref[...] + gs_ref[...] * acc

    acc = jnp.dot(a_ref[...], w_ref[...], preferred_element_type=F32)
    o_ref[...] = x_ref[...] + g_ref[...] * acc


def _proj_resid(a, a_s, w, x, xs, mod, mod_s, g_idx, cols, name):
    m, n = x.shape
    ms = xs.shape[0]
    kdim = a.shape[1]
    tm = min(ROW_TILE, m)
    nj = n // cols
    rider_block = pl.BlockSpec((ms, cols), lambda i, j: (0, _rider_col(i, j, nj)))
    return pl.pallas_call(
        _proj_resid_kernel,
        grid=(m // tm, nj),
        in_specs=[
            pl.BlockSpec((tm, kdim), lambda i, j: (i, 0)),
            pl.BlockSpec((kdim, cols), lambda i, j: (0, j)),
            pl.BlockSpec((tm, cols), lambda i, j: (i, j)),
            mod.spec(g_idx, cols, lambda j: j, tm),
            pl.BlockSpec((ms, kdim), lambda i, j: (0, 0)),
            rider_block,
            mod_s.rider_spec(g_idx, cols, nj),
        ],
        out_specs=[pl.BlockSpec((tm, cols), lambda i, j: (i, j)), rider_block],
        out_shape=[jax.ShapeDtypeStruct((m, n), F32), jax.ShapeDtypeStruct((ms, n), F32)],
        compiler_params=_cparams(("arbitrary", "arbitrary")),
        name=name,
    )(a, w, x, mod.arr, a_s, xs, mod_s.arr)


def _out_proj_kernel(a1_ref, a2_ref, w_ref, x_ref, g_ref, gain_ref, sc_ref, sh_ref, o_ref, h_ref, wb_ref):
    @pl.when(pl.program_id(0) == 0)
    def _():
        step = MXU_DIM

        def cast_rows(r, carry):
            sl = pl.ds(pl.multiple_of(r * step, step), step)
            wb_ref[sl, :] = w_ref[sl, :].astype(BF16)
            return carry

        lax.fori_loop(0, w_ref.shape[0] // step, cast_rows, 0)

    k1 = a1_ref.shape[1]
    rows = x_ref.shape[0]
    step = min(MXU_DIM, rows)
    per_row = sc_ref.shape[0] == rows
    pieces = [slice(r0, r0 + step) for r0 in range(0, rows, step)]
    x1s = []
    for sl in pieces:
        acc = jnp.dot(a1_ref[sl, :], wb_ref[0:k1], preferred_element_type=F32) \
            + jnp.dot(a2_ref[sl, :], wb_ref[k1:], preferred_element_type=F32)
        g = g_ref[sl, :] if per_row else g_ref[...]
        x1 = x_ref[sl, :] + g * acc
        o_ref[sl, :] = x1
        x1s.append(x1)
    for sl, x1 in zip(pieces, x1s):
        ms = jnp.mean(x1 * x1, axis=-1, keepdims=True)
        xn = x1 * lax.rsqrt(ms + RMS_EPS) * gain_ref[...]
        sc = sc_ref[sl, :] if per_row else sc_ref[...]
        sh = sh_ref[sl, :] if per_row else sh_ref[...]
        h_ref[sl, :] = (xn * (1.0 + sc) + sh).astype(BF16)


def _out_proj(a1, a2, w, x, mod, gain, row_tile):
    m, d = x.shape
    tm = min(row_tile, m)
    whole = lambda j: 0
    row = lambda kdim: pl.BlockSpec((tm, kdim), lambda i, j: (i, 0))
    return pl.pallas_call(
        _out_proj_kernel,
        grid=(m // tm, 1),
        in_specs=[
            row(a1.shape[1]), row(a2.shape[1]),
            pl.BlockSpec(w.shape, lambda i, j: (0, 0), pipeline_mode=pl.Buffered(1)),
            row(d),
            mod.spec(2, d, whole, tm),
            pl.BlockSpec((1, d), lambda i, j: (0, 0)),
            mod.spec(4, d, whole, tm),
            mod.spec(3, d, whole, tm),
        ],
        out_specs=[row(d), row(d)],
        out_shape=[jax.ShapeDtypeStruct((m, d), F32), jax.ShapeDtypeStruct((m, d), BF16)],
        scratch_shapes=[pltpu.VMEM(w.shape, BF16)],
        compiler_params=_cparams(("arbitrary", "arbitrary")),
        name="out_proj",
    )(a1, a2, w, x, mod.arr, gain.reshape(1, d), mod.arr, mod.arr)


def _gelu(x):
    return 0.5 * x * (1.0 + lax.erf(x * (2.0 ** -0.5)))


def _ffn_up_kernel(tiles_per_batch, h_ref, wg_ref, wv_ref, hist_ref, cw_ref, cb_ref, hs_ref, wd_ref,
                   act_ref, last_ref, gs_ref, vs_ref, wdb_ref, carry_ref):
    i = pl.program_id(0)
    j = pl.program_id(1)

    @pl.when(i == 0)
    def _():
        hs = hs_ref[...]
        gs_ref[...] = jnp.dot(hs, wg_ref[...].astype(BF16), preferred_element_type=F32)
        vs_ref[...] = jnp.dot(hs, wv_ref[...].astype(BF16), preferred_element_type=F32)
        wdb_ref[...] = wd_ref[...].astype(BF16)

    @pl.when((i % tiles_per_batch) == 0)
    def _():
        carry_ref[j] = hist_ref[...]

    h = h_ref[...]
    gate = jnp.dot(h, wg_ref[...].astype(BF16), preferred_element_type=F32)
    val = jnp.dot(h, wv_ref[...].astype(BF16), preferred_element_type=F32)
    tm = gate.shape[0]
    prev = carry_ref[j]
    row = _iota(gate.shape, 0)
    g1 = pltpu.roll(gate, 1, 0)
    g2 = pltpu.roll(gate, 2, 0)
    g1 = jnp.where(row == 0, prev[1:2], g1)
    g2 = jnp.where(row == 0, prev[0:1], jnp.where(row == 1, prev[1:2], g2))
    cw = cw_ref[...]
    conv = cb_ref[...] + g2 * cw[0:1] + g1 * cw[1:2] + gate * cw[2:3]
    act_ref[...] = (_gelu(conv) * val).astype(BF16)
    tail = gate[tm - 2:tm]
    carry_ref[j] = tail
    last_ref[...] = tail


def _ffn_up_fused(h, hs, w_up, w_down, hist, conv_w, conv_b, rows_per_batch):
    m, d = h.shape
    ms = hs.shape[0]
    f = w_up.shape[1] // 2
    tm = min(ROW_TILE, rows_per_batch)
    cols = UP_COLS
    nj = f // cols
    tiles_per_batch = rows_per_batch // tm
    rider_block = pl.BlockSpec((ms, cols), lambda i, j: (0, _rider_col(i, j, nj)))
    wd_block = pl.BlockSpec((f // nj, w_down.shape[1]), lambda i, j: (_rider_col(i, j, nj), 0))
    act, tile_tails, gate_s, val_s, w_down_bf16 = pl.pallas_call(
        functools.partial(_ffn_up_kernel, tiles_per_batch),
        grid=(m // tm, nj),
        in_specs=[
            pl.BlockSpec((tm, d), lambda i, j: (i, 0)),
            pl.BlockSpec((d, cols), lambda i, j: (0, j)),
            pl.BlockSpec((d, cols), lambda i, j: (0, nj + j)),
            pl.BlockSpec((None, CONV_W - 1, cols), lambda i, j: (i // tiles_per_batch, 0, j)),
            pl.BlockSpec((CONV_W, cols), lambda i, j: (0, j)),
            pl.BlockSpec((1, cols), lambda i, j: (0, j)),
            pl.BlockSpec((ms, d), lambda i, j: (0, 0)),
            wd_block,
        ],
        out_specs=[
            pl.BlockSpec((tm, cols), lambda i, j: (i, j)),
            pl.BlockSpec((None, CONV_W - 1, cols), lambda i, j: (i, 0, j)),
            rider_block, rider_block, wd_block,
        ],
        out_shape=[
            jax.ShapeDtypeStruct((m, f), BF16),
            jax.ShapeDtypeStruct((m // tm, CONV_W - 1, f), F32),
            jax.ShapeDtypeStruct((ms, f), F32),
            jax.ShapeDtypeStruct((ms, f), F32),
            jax.ShapeDtypeStruct(w_down.shape, BF16),
        ],
        scratch_shapes=[pltpu.VMEM((nj, CONV_W - 1, cols), F32)],
        compiler_params=_cparams(("arbitrary", "arbitrary")),
        name="ffn_up_prompt",
    )(h, w_up, w_up, hist, conv_w, conv_b.reshape(1, f), hs, w_down)
    return act, tile_tails[tiles_per_batch - 1::tiles_per_batch], gate_s, val_s, w_down_bf16


def _act_sample_kernel(gate_ref, val_ref, hist_ref, cw_ref, cb_ref, act_ref):
    gate = gate_ref[...]
    hist = hist_ref[...]
    t = _iota(gate.shape, 1)
    g1 = jnp.where(t == 0, hist[:, 1:2], pltpu.roll(gate, 1, 1))
    g2 = jnp.where(t == 0, hist[:, 0:1], jnp.where(t == 1, hist[:, 1:2], pltpu.roll(gate, 2, 1)))
    cw = cw_ref[...]
    conv = cb_ref[...] + g2 * cw[0:1] + g1 * cw[1:2] + gate * cw[2:3]
    act_ref[...] = (_gelu(conv) * val_ref[...]).astype(BF16)


def _act_sample(gate, val, hist, conv_w, conv_b, nb, t):
    f = gate.shape[1]
    cols = UP_COLS
    nj = f // cols
    gate3 = gate.reshape(nb, t, f)
    val3 = val.reshape(nb, t, f)
    act = pl.pallas_call(
        _act_sample_kernel,
        grid=(nj,),
        in_specs=[
            pl.BlockSpec((nb, t, cols), lambda j: (0, 0, j)),
            pl.BlockSpec((nb, t, cols), lambda j: (0, 0, j)),
            pl.BlockSpec((nb, CONV_W - 1, cols), lambda j: (0, 0, j)),
            pl.BlockSpec((CONV_W, cols), lambda j: (0, j)),
            pl.BlockSpec((1, cols), lambda j: (0, j)),
        ],
        out_specs=pl.BlockSpec((nb, t, cols), lambda j: (0, 0, j)),
        out_shape=jax.ShapeDtypeStruct((nb, t, f), BF16),
        compiler_params=_cparams(("arbitrary",)),
        name="ffn_act_sample",
    )(gate3, val3, hist, conv_w, conv_b.reshape(1, f))
    return act.reshape(nb * t, f)


def _pair_rms(x, gain):
    x2 = x * x
    first = _iota(x.shape, 1) < HEAD_DIM
    s0 = jnp.sum(jnp.where(first, x2, 0.0), axis=-1, keepdims=True)
    s1 = jnp.sum(jnp.where(first, 0.0, x2), axis=-1, keepdims=True)
    ms = jnp.where(first, s0, s1) * (1.0 / HEAD_DIM)
    return x * lax.rsqrt(ms + RMS_EPS) * gain


def _bias_rows(table):
    h = table.shape[0]
    far = jnp.broadcast_to(table[:, 2 * REL_CLIP:], (h, ATT_REACH - REL_CLIP))
    mid = table[:, ::-1]
    near_len = BIAS_LEN - ATT_QROWS - (ATT_REACH - REL_CLIP) - (2 * REL_CLIP + 1)
    near = jnp.broadcast_to(table[:, 0:1], (h, near_len))
    wrap = jnp.broadcast_to(table[:, 2 * REL_CLIP:], (h, ATT_QROWS))
    return jnp.concatenate([far, mid, near, wrap], axis=1)


def _toeplitz(u_row, rows):
    return pltpu.roll(jnp.broadcast_to(u_row, (rows, BIAS_LEN)), 0, 1, stride=1, stride_axis=0)


def _attn_prompt_kernel(q_ref, k_ref, v_ref, qg_ref, kg_ref, u_ref, o_ref, kn_ref, vk_ref,
                        bias_ref, kwin_ref, vwin_ref):
    b = pl.program_id(1)
    qb = pl.program_id(2)
    shape = (ATT_QROWS, ATT_WIN)
    pairs = range(q_ref.shape[1] // PAIR)
    cols = [slice(p * PAIR, (p + 1) * PAIR) for p in pairs]
    chains = [(p, h) for p in pairs for h in range(2)]

    @pl.when((b == 0) & (qb == 0))
    def _():
        r = _iota(shape, 0)
        w = _iota(shape, 1)
        chunk_lo = _blk(r, CHUNK) * CHUNK
        in_band = (w >= chunk_lo) & (w < chunk_lo + (ATT_REACH + CHUNK))
        for i, (p, h) in enumerate(chains):
            bias = _toeplitz(u_ref[p, h:h + 1, :], ATT_QROWS)[:, :ATT_WIN]
            bias_ref[i] = jnp.where(in_band, bias * LOG2E, -jnp.inf)

    @pl.when(qb == 0)
    def _():
        kwin_ref[0:ATT_REACH] = jnp.zeros((ATT_REACH, kwin_ref.shape[1]), BF16)
        vwin_ref[0:ATT_REACH] = jnp.zeros((ATT_REACH, vwin_ref.shape[1]), BF16)

    @pl.when(qb > 0)
    def _():
        kwin_ref[0:ATT_REACH] = kwin_ref[ATT_QROWS:ATT_WIN]
        vwin_ref[0:ATT_REACH] = vwin_ref[ATT_QROWS:ATT_WIN]

    kn = [_pair_rms(k_ref[:, c], kg_ref[...]) for c in cols]
    for p in pairs:
        kn_ref[:, cols[p]] = kn[p]
        kwin_ref[ATT_REACH:ATT_WIN, cols[p]] = kn[p].astype(BF16)
    v_new = v_ref[...]
    vk_ref[...] = v_new
    vwin_ref[ATT_REACH:ATT_WIN] = v_new.astype(BF16)

    def attend(mask_start):
        qn = [_pair_rms(q_ref[:, c], qg_ref[...]) * (ATT_SCALE * LOG2E) for c in cols]
        kb = [kwin_ref[:, c] for c in cols]
        vb = [vwin_ref[:, c] for c in cols]
        first = _iota((ATT_QROWS, PAIR), 1) < HEAD_DIM
        first_w = _iota((ATT_WIN, PAIR), 1) < HEAD_DIM
        qh = [jnp.where(first, qn[p], 0.0) if h == 0 else jnp.where(first, 0.0, qn[p]) for p, h in chains]
        s = [_dot(qh[i], kb[p], NT) + bias_ref[i] for i, (p, h) in enumerate(chains)]
        if mask_start:
            started = _iota(shape, 1) >= ATT_REACH - qb * ATT_QROWS
            s = [jnp.where(started, x, -jnp.inf) for x in s]
        m = [jnp.max(x, axis=-1, keepdims=True) for x in s]
        pr = [jnp.exp2(x - mm) for x, mm in zip(s, m)]
        one = jnp.ones((), BF16)
        v_aug = [jnp.where(first_w, vb[p], one) if h == 0 else jnp.where(first_w, one, vb[p]) for p, h in chains]
        o = [_dot(pr[i], v_aug[i]) for i in range(len(chains))]
        o = [x / pltpu.roll(x, HEAD_DIM, 1) for x in o]
        for p in pairs:
            o_ref[:, cols[p]] = jnp.where(first, o[2 * p], o[2 * p + 1]).astype(BF16)

    full_window_from = ATT_REACH // ATT_QROWS
    pl.when(qb < full_window_from)(lambda: attend(True))
    pl.when(qb >= full_window_from)(lambda: attend(False))


def _attn_prompt(z3, q_gain, k_gain, u):
    nb, t, _ = z3.shape
    npairs = N_ATT_HEADS // 2
    npp = ATT_PAIRS
    width = npp * PAIR
    ngroups = npairs // npp
    nq = t // ATT_QROWS
    kcol = D_ATT // width
    vcol = 2 * D_ATT // width
    keep_blocks = ATT_REACH // ATT_QROWS
    blk = (None, ATT_QROWS, width)

    keep_spec = pl.BlockSpec(blk, lambda g, b, q: (b, jnp.maximum(q - (nq - keep_blocks), 0), g))
    att, kn, vk = pl.pallas_call(
        _attn_prompt_kernel,
        grid=(ngroups, nb, nq),
        in_specs=[
            pl.BlockSpec(blk, lambda g, b, q: (b, q, g)),
            pl.BlockSpec(blk, lambda g, b, q: (b, q, kcol + g)),
            pl.BlockSpec(blk, lambda g, b, q: (b, q, vcol + g)),
            pl.BlockSpec((1, PAIR), lambda g, b, q: (0, 0)),
            pl.BlockSpec((1, PAIR), lambda g, b, q: (0, 0)),
            pl.BlockSpec((npp, 2, BIAS_LEN), lambda g, b, q: (g, 0, 0)),
        ],
        out_specs=[
            pl.BlockSpec(blk, lambda g, b, q: (b, q, g)),
            keep_spec, keep_spec,
        ],
        out_shape=[
            jax.ShapeDtypeStruct((nb, t, D_ATT), BF16),
            jax.ShapeDtypeStruct((nb, ATT_REACH, D_ATT), F32),
            jax.ShapeDtypeStruct((nb, ATT_REACH, D_ATT), F32),
        ],
        scratch_shapes=[pltpu.VMEM((2 * npp, ATT_QROWS, ATT_WIN), F32),
                        pltpu.VMEM((ATT_WIN, width), BF16), pltpu.VMEM((ATT_WIN, width), BF16)],
        compiler_params=_cparams(("arbitrary", "arbitrary", "arbitrary")),
        name="attn_prompt",
    )(z3, z3, z3, jnp.tile(q_gain, 2).reshape(1, PAIR),
      jnp.tile(k_gain, 2).reshape(1, PAIR), u.reshape(npairs, 2, BIAS_LEN))
    return att, kn, vk


def _attn_sample_kernel(q_ref, k_ref, v_ref, kp_ref, vp_ref, qg_ref, kg_ref, u_ref, o_ref, kn_ref):
    t = q_ref.shape[0]
    reach = kp_ref.shape[0]
    first = _iota((t, PAIR), 1) < HEAD_DIM
    pairs = range(N_ATT_HEADS // 2)
    cols = [slice(p * PAIR, (p + 1) * PAIR) for p in pairs]
    chains = [(p, h) for p in pairs for h in range(2)]
    qn = [_pair_rms(q_ref[:, c], qg_ref[...]) * ATT_SCALE for c in cols]
    kn = [_pair_rms(k_ref[:, c], kg_ref[...]) for c in cols]
    for p in pairs:
        kn_ref[:, cols[p]] = kn[p]
    kpast = [kp_ref[:, c].astype(BF16) for c in cols]
    vpast = [vp_ref[:, c].astype(BF16) for c in cols]
    vnew = [v_ref[:, c].astype(BF16) for c in cols]
    qh = [jnp.where(first, qn[p], 0.0) if h == 0 else jnp.where(first, 0.0, qn[p]) for p, h in chains]
    bias = [_toeplitz(u_ref[p, h:h + 1, :], t) for p, h in chains]
    s_past = [_dot(qh[i], kpast[p], NT) + bias[i][:, :reach] for i, (p, h) in enumerate(chains)]
    s_new = [_dot(qh[i], kn[p], NT) + bias[i][:, reach:reach + t] for i, (p, h) in enumerate(chains)]
    m = [jnp.maximum(jnp.max(a, axis=-1, keepdims=True), jnp.max(b, axis=-1, keepdims=True))
         for a, b in zip(s_past, s_new)]
    p_past = [jnp.exp(a - mm) for a, mm in zip(s_past, m)]
    p_new = [jnp.exp(b - mm) for b, mm in zip(s_new, m)]
    l = [jnp.sum(a, axis=-1, keepdims=True) + jnp.sum(b, axis=-1, keepdims=True)
         for a, b in zip(p_past, p_new)]
    o = [(_dot(p_past[i], vpast[p]) + _dot(p_new[i], vnew[p])) / l[i] for i, (p, h) in enumerate(chains)]
    for p in pairs:
        o_ref[:, cols[p]] = jnp.where(first, o[2 * p], o[2 * p + 1]).astype(BF16)


def _attn_sample(z3, k_past, v_past, q_gain, k_gain, u):
    nb, t, _ = z3.shape
    reach = k_past.shape[1]
    npairs = N_ATT_HEADS // 2
    att, kn = pl.pallas_call(
        _attn_sample_kernel,
        grid=(nb,),
        in_specs=[
            pl.BlockSpec((None, t, D_ATT), lambda b: (b, 0, 0)),
            pl.BlockSpec((None, t, D_ATT), lambda b: (b, 0, 1)),
            pl.BlockSpec((None, t, D_ATT), lambda b: (b, 0, 2)),
            pl.BlockSpec((None, reach, D_ATT), lambda b: (b, 0, 0)),
            pl.BlockSpec((None, reach, D_ATT), lambda b: (b, 0, 0)),
            pl.BlockSpec((1, PAIR), lambda b: (0, 0)),
            pl.BlockSpec((1, PAIR), lambda b: (0, 0)),
            pl.BlockSpec((npairs, 2, BIAS_LEN), lambda b: (0, 0, 0)),
        ],
        out_specs=[
            pl.BlockSpec((None, t, D_ATT), lambda b: (b, 0, 0)),
            pl.BlockSpec((None, t, D_ATT), lambda b: (b, 0, 0)),
        ],
        out_shape=[
            jax.ShapeDtypeStruct((nb, t, D_ATT), BF16),
            jax.ShapeDtypeStruct((nb, t, D_ATT), F32),
        ],
        compiler_params=_cparams(("arbitrary",)),
        name="attn_sample",
    )(z3, z3, z3, k_past.reshape(nb, reach, D_ATT), v_past.reshape(nb, reach, D_ATT),
      jnp.tile(q_gain, 2).reshape(1, PAIR), jnp.tile(k_gain, 2).reshape(1, PAIR),
      u.reshape(npairs, 2, BIAS_LEN))
    return att, kn


def _tri_inverse(l_mats, c):
    n = l_mats[0].shape[0]
    eye = jnp.where(_iota((n, n), 0) == _iota((n, n), 1), 1.0, 0.0).astype(F32)
    a_s = [(eye + l).astype(BF16) for l in l_mats]
    t_s = [eye - l for l in l_mats]
    for _ in range(c.bit_length() - 2):
        r_s = [eye - _dot(a, t) for a, t in zip(a_s, t_s)]
        t_s = [t + _dot(t, r) for t, r in zip(t_s, r_s)]
    return t_s


def _rwkv_kernel(c, r_ref, k_ref, v_ref, lo_ref, sr_ref, sk_ref, sv_ref, slo_ref, s0_ref,
                 mur_ref, muk_ref, muv_ref, mulo_ref, w0_ref, a0_ref, kkg_ref, ka_ref, rk_ref,
                 lnw_ref, lnb_ref, w2_ref, a2_ref, g2_ref,
                 o_ref, sT_ref, s_ref, cr_ref, ck_ref, cv_ref, clo_ref):
    tb = pl.program_id(2)
    rows, width = r_ref.shape
    npp = width // PAIR
    nchunks = rows // c
    h0 = _iota((rows, PAIR), 1) < HEAD_DIM
    bd = _blk(_iota((PAIR, PAIR), 0), HEAD_DIM) == _blk(_iota((PAIR, PAIR), 1), HEAD_DIM)

    @pl.when(tb == 0)
    def _():
        s_ref[...] = jnp.zeros(s_ref.shape, F32)
        for pp in range(npp):
            s_ref[pp, 0:HEAD_DIM, 0:HEAD_DIM] = s0_ref[2 * pp]
            s_ref[pp, HEAD_DIM:PAIR, HEAD_DIM:PAIR] = s0_ref[2 * pp + 1]
        cr_ref[...] = sr_ref[...]
        ck_ref[...] = sk_ref[...]
        cv_ref[...] = sv_ref[...]
        clo_ref[...] = slo_ref[...]

    def shifted(x_ref, carry_ref, mu_ref):
        x = x_ref[...]
        prev = jnp.where(_iota(x.shape, 0) == 0, carry_ref[...], pltpu.roll(x, 1, 0))
        carry_ref[...] = x[rows - 1:rows]
        return x + (prev - x) * mu_ref[...]

    r = shifted(r_ref, cr_ref, mur_ref)
    k = shifted(k_ref, ck_ref, muk_ref)
    v = shifted(v_ref, cv_ref, muv_ref)
    lo = shifted(lo_ref, clo_ref, mulo_ref)

    zeros_w = jnp.zeros((RANK_W, width), F32)
    w2p = jnp.concatenate([w2_ref[...], zeros_w], axis=0)
    a2p = jnp.concatenate([zeros_w, a2_ref[...]], axis=0)
    lo_wa = lo[:, 0:RANK_W + RANK_A]
    u = w0_ref[...] + _dot(jnp.tanh(lo_wa), w2p)
    lw = -DECAY_SCALE * _sigmoid(u)
    a = _sigmoid(a0_ref[...] + _dot(lo_wa, a2p))
    g = _dot(_sigmoid(lo[:, RANK_W + RANK_A:]), g2_ref[...])

    kk = k * kkg_ref[...]
    kk = kk * lax.rsqrt(jnp.maximum(_head_sums(kk * kk), KK_EPS * KK_EPS))
    k = k * (1.0 + (a - 1.0) * ka_ref[...])
    b = kk * a
    bonus = _head_sums(r * k * rk_ref[...]) * v

    tr = _iota((rows, rows), 0)
    tc = _iota((rows, rows), 1)
    same_chunk = _blk(tr, c) == _blk(tc, c)
    strict = same_chunk & (tr > tc)
    incl = same_chunk & (tr >= tc)
    lw_hi, lw_lo = _split2(lw)
    tril_ones = jnp.where(incl, 1.0, 0.0).astype(BF16)
    lp = jnp.dot(tril_ones, lw_hi, preferred_element_type=F32) + \
        jnp.dot(tril_ones, lw_lo, preferred_element_type=F32)
    decay_end = [jnp.exp(lp[(ci + 1) * c - 1:(ci + 1) * c]) for ci in range(nchunks)]

    alpha_w = kk * jnp.exp(lp - lw)
    inv_p = jnp.exp(-lp)
    beta_w = b * inv_p
    kappa_w = k * inv_p
    rho_w = r * jnp.exp(lp)
    to_end = [inv_p[ci * c:(ci + 1) * c] * decay_end[ci] for ci in range(nchunks)]
    to_end = to_end[0] if nchunks == 1 else jnp.concatenate(to_end, axis=0)
    beta_ew = b * to_end
    kappa_ew = k * to_end

    wide = (rows, nchunks * PAIR)
    col_chunk = _blk(_iota(wide, 1), PAIR) == _blk(_iota(wide, 0), c)
    spread = lambda m: jnp.where(col_chunk, jnp.tile(m, (1, nchunks)), 0.0)
    eye_p = _iota((PAIR, PAIR), 0) == _iota((PAIR, PAIR), 1)

    pairs = range(npp)
    lanes = [slice(pp * PAIR, (pp + 1) * PAIR) for pp in pairs]
    alpha = [alpha_w[:, l] for l in lanes]
    rho = [rho_w[:, l] for l in lanes]
    vv = [v[:, l] for l in lanes]
    head_mask = [h0, jnp.logical_not(h0)]
    bk = [jnp.concatenate([beta_w[:, l], kappa_w[:, l]], axis=0).astype(BF16) for l in lanes]
    prod = [[_dot(jnp.concatenate([jnp.where(hm, alpha[pp], 0.0), jnp.where(hm, rho[pp], 0.0)], axis=0),
                  bk[pp], NT) for hm in head_mask] for pp in pairs]
    t_inv = _tri_inverse([jnp.where(strict, prod[pp][h][:rows, :rows], 0.0) for pp in pairs for h in range(2)], c)
    x = [[_dot(jnp.where(strict, prod[pp][h][:rows, rows:], 0.0), vv[pp]) for h in range(2)] for pp in pairs]
    ws = [[_dot(t_inv[2 * pp + h], jnp.concatenate([alpha[pp], x[pp][h]], axis=1)) for h in range(2)]
          for pp in pairs]
    w12 = [jnp.concatenate([jnp.where(h0, ws[pp][0][:, :PAIR], ws[pp][1][:, :PAIR]),
                            jnp.where(h0, ws[pp][0][:, PAIR:], ws[pp][1][:, PAIR:])], axis=1) for pp in pairs]
    q = [[_dot(jnp.where(incl, prod[pp][h][rows:, :rows], 0.0), w12[pp]) for h in range(2)] for pp in pairs]
    qk = [[_dot(jnp.where(incl, prod[pp][h][rows:, rows:], 0.0), vv[pp]) for h in range(2)] for pp in pairs]
    rp = [rho[pp] - jnp.where(h0, q[pp][0][:, :PAIR], q[pp][1][:, :PAIR]) for pp in pairs]
    y0 = [jnp.where(h0, qk[pp][0] - q[pp][0][:, PAIR:], qk[pp][1] - q[pp][1][:, PAIR:]) for pp in pairs]
    wtb = [_dot(w12[pp], spread(beta_ew[:, lanes[pp]]), TN) for pp in pairs]
    vtk = [_dot(vv[pp], spread(kappa_ew[:, lanes[pp]]), TN) for pp in pairs]

    s_cur = [s_ref[pp] for pp in pairs]
    ys = [[] for _ in pairs]
    for ci in range(nchunks):
        sl = slice(ci * c, (ci + 1) * c)
        cols = slice(ci * PAIR, (ci + 1) * PAIR)
        for pp in pairs:
            p_end = decay_end[ci][:, lanes[pp]]
            gmat = jnp.where(eye_p, jnp.broadcast_to(p_end, (PAIR, PAIR)), 0.0) \
                - jnp.where(bd, wtb[pp][:PAIR, cols], 0.0)
            hmat = jnp.where(bd, vtk[pp][:, cols] - wtb[pp][PAIR:, cols], 0.0)
            ys[pp].append(_dot(rp[pp][sl], s_cur[pp], NT) + y0[pp][sl])
            s_cur[pp] = _dot(s_cur[pp], gmat) + hmat
    for pp in pairs:
        s_ref[pp] = s_cur[pp]
    y_pairs = [ys[pp][0] if nchunks == 1 else jnp.concatenate(ys[pp], axis=0) for pp in pairs]

    @pl.when(tb == pl.num_programs(2) - 1)
    def _():
        for pp in range(npp):
            sT_ref[2 * pp] = s_ref[pp, 0:HEAD_DIM, 0:HEAD_DIM]
            sT_ref[2 * pp + 1] = s_ref[pp, HEAD_DIM:PAIR, HEAD_DIM:PAIR]

    y = y_pairs[0] if npp == 1 else jnp.concatenate(y_pairs, axis=1)
    mu = _head_sums(y) * (1.0 / HEAD_DIM)
    d = y - mu
    var = _head_sums(d * d) * (1.0 / HEAD_DIM)
    yn = d * lax.rsqrt(var + GN_EPS) * lnw_ref[...] + lnb_ref[...]
    o_ref[...] = ((yn + bonus) * g).astype(BF16)


def _rwkv(z3, zlo3, shift_prev, s0, p, rows, c, npp):
    nb, t, _ = z3.shape
    width = npp * PAIR
    ngroups = D_RWKV // width
    col0 = 3 * D_ATT // width
    sp = shift_prev.reshape(nb, 1, D_SHIFT)

    def zspec(off):
        return pl.BlockSpec((None, rows, width), lambda b, q, s: (b, s, col0 + off * ngroups + q))

    def sspec(off):
        return pl.BlockSpec((None, 1, width), lambda b, q, s: (b, 0, off * ngroups + q))

    def vec(off=0):
        return pl.BlockSpec((1, width), lambda b, q, s: (0, off * ngroups + q))

    def row2(x):
        return x.reshape(1, -1)

    out, s_fin = pl.pallas_call(
        functools.partial(_rwkv_kernel, c),
        grid=(nb, ngroups, t // rows),
        in_specs=[
            zspec(0), zspec(1), zspec(2),
            pl.BlockSpec((None, rows, D_LORA), lambda b, q, s: (b, s, 0)),
            sspec(0), sspec(1), sspec(2),
            pl.BlockSpec((None, 1, D_LORA), lambda b, q, s: (b, 0, 3 * D_RWKV // D_LORA)),
            pl.BlockSpec((None, 2 * npp, HEAD_DIM, HEAD_DIM), lambda b, q, s: (b, q, 0, 0)),
            vec(0), vec(1), vec(2),
            pl.BlockSpec((1, D_LORA), lambda b, q, s: (0, 3 * D_RWKV // D_LORA)),
            vec(), vec(), vec(), vec(), vec(), vec(), vec(),
            pl.BlockSpec((RANK_W, width), lambda b, q, s: (0, q)),
            pl.BlockSpec((RANK_A, width), lambda b, q, s: (0, q)),
            pl.BlockSpec((RANK_G, width), lambda b, q, s: (0, q)),
        ],
        out_specs=[
            pl.BlockSpec((None, rows, width), lambda b, q, s: (b, s, q)),
            pl.BlockSpec((None, 2 * npp, HEAD_DIM, HEAD_DIM), lambda b, q, s: (b, q, 0, 0)),
        ],
        out_shape=[
            jax.ShapeDtypeStruct((nb, t, D_RWKV), BF16),
            jax.ShapeDtypeStruct((nb, N_RWKV_HEADS, HEAD_DIM, HEAD_DIM), F32),
        ],
        scratch_shapes=[
            pltpu.VMEM((npp, PAIR, PAIR), F32),
            pltpu.VMEM((1, width), F32), pltpu.VMEM((1, width), F32), pltpu.VMEM((1, width), F32),
            pltpu.VMEM((1, D_LORA), F32),
        ],
        compiler_params=_cparams(("arbitrary", "arbitrary", "arbitrary")),
        name="rwkv7_mix",
    )(z3, z3, z3, zlo3, sp, sp, sp, sp, s0,
      row2(p['mu_shift']), row2(p['mu_shift']), row2(p['mu_shift']), row2(p['mu_shift']),
      row2(p['w0']), row2(p['a0']), row2(p['k_k']), row2(p['k_a']), row2(p['r_k']),
      row2(p['ln_x_w']), row2(p['ln_x_b']), p['w2'], p['a2'], p['g2'])
    return out, s_fin


def _layer(xp3, xs3, mod_p, mod_s, p, u, k_past, v_past, s0_p, s0_s, shift_p, shift_s, conv_p, conv_s):
    bp, tp, d = xp3.shape
    bs, ts, _ = xs3.shape
    mp, msr = bp * tp, bs * ts
    xp = xp3.reshape(mp, d)
    xs = xs3.reshape(msr, d)
    zp, zp_lo, zs, zs_lo = _norm_proj(xp, xs, p['norm_att_g'], mod_p, mod_s, 1, 0, p['w_in'], IN_COLS,
                                      "in_proj", min(IN_ROW_TILE, tp))
    assert zp.shape[1] == 3 * D_ATT + 3 * D_RWKV and zp_lo.shape[1] == D_LORA
    zp3, zp_lo3 = zp.reshape(bp, tp, -1), zp_lo.reshape(bp, tp, -1)
    zs3, zs_lo3 = zs.reshape(bs, ts, -1), zs_lo.reshape(bs, ts, -1)
    att_p, k_keep_p, v_keep_p = _attn_prompt(zp3, p['q_norm_g'], p['k_norm_g'], u)
    rw_p, s_fin_p = _rwkv(zp3, zp_lo3, shift_p, s0_p, p, RWKV_ROWS, CHUNK, RWKV_PAIRS_PROMPT)
    att_s, k_keep_s = _attn_sample(zs3, k_past, v_past, p['q_norm_g'], p['k_norm_g'], u)
    v_keep_s = zs3[:, :, 2 * D_ATT:3 * D_ATT]
    rw_s, s_fin_s = _rwkv(zs3, zs_lo3, shift_s, s0_s, p, ts, ts, RWKV_PAIRS_SAMPLE)
    shift_last_p = jnp.concatenate([zp3[:, tp - 1, 3 * D_ATT:], zp_lo3[:, tp - 1]], axis=-1)
    shift_last_s = jnp.concatenate([zs3[:, ts - 1, 3 * D_ATT:], zs_lo3[:, ts - 1]], axis=-1)
    x1p, h2p = _out_proj(att_p.reshape(mp, D_ATT), rw_p.reshape(mp, D_RWKV), p['w_out'], xp, mod_p,
                         p['norm_ffn_g'], OUT_ROW_TILE)
    x1s, h2s = _out_proj(att_s.reshape(msr, D_ATT), rw_s.reshape(msr, D_RWKV), p['w_out'], xs, mod_s,
                         p['norm_ffn_g'], OUT_ROW_TILE)
    act_p, conv_last_p, gate_s, val_s, w_down = _ffn_up_fused(h2p, h2s, p['w_up'], p['w_down'], conv_p,
                                                              p['dw_conv'], p['dw_bias'], tp)
    act_s = _act_sample(gate_s, val_s, conv_s, p['dw_conv'], p['dw_bias'], bs, ts)
    conv_last_s = gate_s.reshape(bs, ts, -1)[:, ts - (CONV_W - 1):]
    x2p, x2s = _proj_resid(act_p, act_s, w_down, x1p, x1s, mod_p, mod_s, 5, DOWN_COLS, "ffn_down")
    heads = lambda a: a.reshape(a.shape[0], a.shape[1], N_ATT_HEADS, HEAD_DIM)
    out_p = (x2p.reshape(bp, tp, d), heads(k_keep_p), heads(v_keep_p), s_fin_p, shift_last_p, conv_last_p)
    out_s = (x2s.reshape(bs, ts, d), heads(k_keep_s), heads(v_keep_s), s_fin_s, shift_last_s, conv_last_s)
    return out_p, out_s


def kernel(x_prompt, x_sample, c_prompt, c_sample, cache_att_k, cache_att_v, state_rwkv, state_shift, state_ffn_conv, norm_att_g, norm_ffn_g, w_ada, b_ada, w_in, q_norm_g, k_norm_g, rel_bias, mu_shift, w0, w2, a0, a2, g2, k_k, k_a, r_k, ln_x_w, ln_x_b, w_out, w_up, dw_conv, dw_bias, w_down):
    depth = w_in.shape[0]
    bp, tp, d = x_prompt.shape
    bs, ts, _ = x_sample.shape
    d_ff = w_down.shape[1]
    hp, hs = x_prompt, x_sample
    outs_p = [[] for _ in range(5)]
    outs_s = [[] for _ in range(5)]
    for l in range(depth):
        p = dict(norm_att_g=norm_att_g[l], norm_ffn_g=norm_ffn_g[l], w_in=w_in[l], q_norm_g=q_norm_g[l],
                 k_norm_g=k_norm_g[l], mu_shift=mu_shift[l], w0=w0[l], w2=w2[l], a0=a0[l], a2=a2[l],
                 g2=g2[l], k_k=k_k[l], k_a=k_a[l], r_k=r_k[l], ln_x_w=ln_x_w[l], ln_x_b=ln_x_b[l],
                 w_out=w_out[l], w_up=w_up[l], dw_conv=dw_conv[l], dw_bias=dw_bias[l], w_down=w_down[l])
        n_c = bp + bs
        pad = (-n_c) % 8
        c_all = jnp.concatenate([c_prompt, c_sample, jnp.zeros((pad, d), F32)], axis=0)
        mod = _ada(c_all, w_ada[l], b_ada[l])
        mod_p = _Mod(mod.reshape(n_c + pad, 6, 1, d), False, rows_per_batch=tp)
        mod_s = _Mod(jnp.repeat(mod[bp:bp + bs], ts, axis=0), True)
        u = _bias_rows(rel_bias[l])

        res_p, res_s = _layer(hp, hs, mod_p, mod_s, p, u, cache_att_k[l], cache_att_v[l],
                              jnp.zeros((bp, N_RWKV_HEADS, HEAD_DIM, HEAD_DIM), F32), state_rwkv[l],
                              jnp.zeros((bp, D_SHIFT), F32), state_shift[l],
                              jnp.zeros((bp, CONV_W - 1, d_ff), F32), state_ffn_conv[l])
        hp, hs = res_p[0], res_s[0]
        for lst, val in zip(outs_p, res_p[1:]):
            lst.append(val)
        for lst, val in zip(outs_s, res_s[1:]):
            lst.append(val)
    st = lambda lst: jnp.stack(lst)
    return (hp, hs, *[st(x) for x in outs_p], *[st(x) for x in outs_s])
```

```python
import functools

import jax
import jax.numpy as jnp
from jax import lax
from jax.experimental import pallas as pl
from jax.experimental.pallas import tpu as pltpu

F32 = jnp.float32
BF16 = jnp.bfloat16

CHUNK = 64
N_PREV_CHUNKS = 8
ATT_REACH = N_PREV_CHUNKS * CHUNK
HEAD_DIM = 64
N_ATT_HEADS = 16
N_RWKV_HEADS = 16
D_ATT = N_ATT_HEADS * HEAD_DIM
D_RWKV = N_RWKV_HEADS * HEAD_DIM
REL_CLIP = 128
RANK_W = 64
RANK_A = 64
RANK_G = 128
D_LORA = RANK_W + RANK_A + RANK_G
D_SHIFT = 3 * D_RWKV + D_LORA
D_IN = 3 * D_ATT + D_SHIFT
CONV_W = 3
RMS_EPS = 1e-6
GN_EPS = 64e-5
KK_EPS = 1e-12
ATT_SCALE = HEAD_DIM ** -0.5
LOG2E = 1.4426950408889634
DECAY_SCALE = 0.6065306597126334

PAIR = 2 * HEAD_DIM
MXU_DIM = 256
VMEM_LIMIT = 60 * 1024 * 1024

ROW_TILE = 1024
IN_ROW_TILE = 2048
ADA_COLS = 1024
IN_COLS = 512
OUT_ROW_TILE = 512
UP_COLS = 512
DOWN_COLS = 512
ATT_QROWS = 256
ATT_WIN = ATT_QROWS + ATT_REACH
ATT_PAIRS = 4
BIAS_LEN = 1024
RWKV_ROWS = 256
RWKV_PAIRS_PROMPT = 8
RWKV_PAIRS_SAMPLE = 8


def _cparams(sem):
    return pltpu.CompilerParams(dimension_semantics=sem, vmem_limit_bytes=VMEM_LIMIT)


def _dot(a, b, dims=(((1,), (0,)), ((), ()))):
    return lax.dot_general(a.astype(BF16), b.astype(BF16), dims, preferred_element_type=F32)


def _split2(x):
    hi = x.astype(BF16)
    lo = (x - hi.astype(F32)).astype(BF16)
    return hi, lo


NT = (((1,), (1,)), ((), ()))
TN = (((0,), (0,)), ((), ()))


def _iota(shape, dim):
    return lax.broadcasted_iota(jnp.int32, shape, dim)


def _blk(x, size):
    return jnp.right_shift(x, size.bit_length() - 1)


def _head_ones(n):
    r = _blk(_iota((n, n), 0), HEAD_DIM)
    c = _blk(_iota((n, n), 1), HEAD_DIM)
    return jnp.where(r == c, 1.0, 0.0).astype(BF16)


def _head_sums(x):
    lanes = x.shape[1]
    group = min(lanes, MXU_DIM)
    ones = _head_ones(group)
    parts = [_dot(x[:, i:i + group], ones) for i in range(0, lanes, group)]
    return parts[0] if len(parts) == 1 else jnp.concatenate(parts, axis=1)


def _sigmoid(x):
    return 1.0 / (1.0 + jnp.exp(-x))


def _ada_kernel(c_ref, w_ref, b_ref, o_ref):
    c = c_ref[...]
    s = c * _sigmoid(c)
    o_ref[...] = _dot(s, w_ref[...]) + b_ref[...]


def _ada(c_all, w_ada, b_ada):
    rows, d = c_all.shape
    n = w_ada.shape[1]
    return pl.pallas_call(
        _ada_kernel,
        grid=(n // ADA_COLS,),
        in_specs=[
            pl.BlockSpec((rows, d), lambda j: (0, 0)),
            pl.BlockSpec((d, ADA_COLS), lambda j: (0, j)),
            pl.BlockSpec((1, ADA_COLS), lambda j: (0, j)),
        ],
        out_specs=pl.BlockSpec((rows, ADA_COLS), lambda j: (0, j)),
        out_shape=jax.ShapeDtypeStruct((rows, n), F32),
        compiler_params=_cparams(("arbitrary",)),
        name="ada_mod",
    )(c_all, w_ada, b_ada.reshape(1, n))


class _Mod:
    def __init__(self, arr, per_row, rows_per_batch=None):
        self.arr = arr
        self.per_row = per_row
        self.rows_per_batch = rows_per_batch

    def spec(self, idx, cols, col_of, row_tile):
        if self.per_row:
            m = self.arr.shape[0]
            d = self.arr.shape[1] // 6
            nblk = d // cols
            return pl.BlockSpec((m, cols), lambda i, j: (0, idx * nblk + col_of(j)))
        tiles_per_batch = self.rows_per_batch // row_tile
        return pl.BlockSpec((None, None, 1, cols),
                            lambda i, j: (i // tiles_per_batch, idx, 0, col_of(j)))

    def rider_spec(self, idx, cols, nj):
        m = self.arr.shape[0]
        nblk = self.arr.shape[1] // 6 // cols
        return pl.BlockSpec((m, cols), lambda i, j: (0, idx * nblk + _rider_col(i, j, nj)))


def _rider_col(i, j, nj):
    return jnp.where(i == 0, j, nj - 1)


NORM_ROWS = 128


def _store_normed(h_ref, x_ref, g_ref, sc_ref, sh_ref):
    rows = x_ref.shape[0]
    step = min(NORM_ROWS, rows)
    per_row = sc_ref.shape[0] == rows

    def body(r, carry):
        sl = pl.ds(pl.multiple_of(r * step, step), step)
        x = x_ref[sl, :]
        ms = jnp.mean(x * x, axis=-1, keepdims=True)
        xn = x * lax.rsqrt(ms + RMS_EPS) * g_ref[...]
        sc = sc_ref[sl, :] if per_row else sc_ref[...]
        sh = sh_ref[sl, :] if per_row else sh_ref[...]
        h_ref[sl, :] = (xn * (1.0 + sc) + sh).astype(BF16)
        return carry

    lax.fori_loop(0, rows // step, body, 0)


def _norm_proj_kernel(nj, x_ref, g_ref, sc_ref, sh_ref, w_ref, wt_ref, xs_ref, scs_ref, shs_ref,
                      o_ref, ot_ref, os_ref, ost_ref, h_ref):
    i = pl.program_id(0)
    j = pl.program_id(1)
    tm = x_ref.shape[0]
    host = h_ref.at[0:tm]
    riders = h_ref.at[tm:]

    @pl.when(j == 0)
    def _():
        _store_normed(host, x_ref, g_ref, sc_ref, sh_ref)

    @pl.when((i == 0) & (j == 0))
    def _():
        _store_normed(riders, xs_ref, g_ref, scs_ref, shs_ref)

    def project(weight_ref, out_ref, rider_out_ref):
        @pl.when(i == 0)
        def _():
            both = jnp.dot(h_ref[...], weight_ref[...].astype(BF16), preferred_element_type=F32)
            out_ref[...] = both[0:tm]
            rider_out_ref[...] = both[tm:]

        @pl.when(i > 0)
        def _():
            out_ref[...] = jnp.dot(host[...], weight_ref[...].astype(BF16), preferred_element_type=F32)

    pl.when(j < nj)(lambda: project(w_ref, o_ref, os_ref))
    pl.when(j == nj)(lambda: project(wt_ref, ot_ref, ost_ref))


def _norm_proj(x, xs, gain, mod, mod_s, sc_idx, sh_idx, w, cols, name, row_tile):
    m, d = x.shape
    ms = xs.shape[0]
    n = w.shape[1]
    tm = min(row_tile, m)
    nj = n // cols
    n_main = nj * cols
    tail = n - n_main
    whole = lambda j: 0
    main_col = lambda j: jnp.minimum(j, nj - 1)
    x_mode = dict(pipeline_mode=pl.Buffered(1)) if tm > ROW_TILE else {}
    return pl.pallas_call(
        functools.partial(_norm_proj_kernel, nj),
        grid=(m // tm, nj + 1),
        in_specs=[
            pl.BlockSpec((tm, d), lambda i, j: (i, 0), **x_mode),
            pl.BlockSpec((1, d), lambda i, j: (0, 0)),
            mod.spec(sc_idx, d, whole, tm),
            mod.spec(sh_idx, d, whole, tm),
            pl.BlockSpec((d, cols), lambda i, j: (0, main_col(j))),
            pl.BlockSpec((d, tail), lambda i, j: (0, n_main // tail)),
            pl.BlockSpec((ms, d), lambda i, j: (0, 0)),
            mod_s.spec(sc_idx, d, whole, ms),
            mod_s.spec(sh_idx, d, whole, ms),
        ],
        out_specs=[
            pl.BlockSpec((tm, cols), lambda i, j: (i, main_col(j))),
            pl.BlockSpec((tm, tail), lambda i, j: (i, 0)),
            pl.BlockSpec((ms, cols), lambda i, j: (0, _rider_col(i, main_col(j), nj))),
            pl.BlockSpec((ms, tail), lambda i, j: (0, 0)),
        ],
        out_shape=[jax.ShapeDtypeStruct((m, n_main), F32), jax.ShapeDtypeStruct((m, tail), F32),
                   jax.ShapeDtypeStruct((ms, n_main), F32), jax.ShapeDtypeStruct((ms, tail), F32)],
        scratch_shapes=[pltpu.VMEM((tm + ms, d), BF16)],
        compiler_params=_cparams(("arbitrary", "arbitrary")),
        name=name,
    )(x, gain.reshape(1, d), mod.arr, mod.arr, w, w, xs, mod_s.arr, mod_s.arr)


def _proj_resid_kernel(a_ref, w_ref, x_ref, g_ref, as_ref, xs_ref, gs_ref, o_ref, os_ref):
    @pl.when(pl.program_id(0) == 0)
    def _():
        acc = jnp.dot(as_ref[...], w_ref[...], preferred_element_type=F32)
        os_ref[...] = xs_ref[...] + gs_ref[...] * acc

    acc = jnp.dot(a_ref[...], w_ref[...], preferred_element_type=F32)
    o_ref[...] = x_ref[...] + g_ref[...] * acc


def _proj_resid(a, a_s, w, x, xs, mod, mod_s, g_idx, cols, name):
    m, n = x.shape
    ms = xs.shape[0]
    kdim = a.shape[1]
    tm = min(ROW_TILE, m)
    nj = n // cols
    rider_block = pl.BlockSpec((ms, cols), lambda i, j: (0, _rider_col(i, j, nj)))
    return pl.pallas_call(
        _proj_resid_kernel,
        grid=(m // tm, nj),
        in_specs=[
            pl.BlockSpec((tm, kdim), lambda i, j: (i, 0)),
            pl.BlockSpec((kdim, cols), lambda i, j: (0, j)),
            pl.BlockSpec((tm, cols), lambda i, j: (i, j)),
            mod.spec(g_idx, cols, lambda j: j, tm),
            pl.BlockSpec((ms, kdim), lambda i, j: (0, 0)),
            rider_block,
            mod_s.rider_spec(g_idx, cols, nj),
        ],
        out_specs=[pl.BlockSpec((tm, cols), lambda i, j: (i, j)), rider_block],
        out_shape=[jax.ShapeDtypeStruct((m, n), F32), jax.ShapeDtypeStruct((ms, n), F32)],
        compiler_params=_cparams(("arbitrary", "arbitrary")),
        name=name,
    )(a, w, x, mod.arr, a_s, xs, mod_s.arr)


def _out_proj_kernel(a1_ref, a2_ref, w_ref, x_ref, g_ref, gain_ref, sc_ref, sh_ref, o_ref, h_ref, wb_ref):
    @pl.when(pl.program_id(0) == 0)
    def _():
        step = MXU_DIM

        def cast_rows(r, carry):
            sl = pl.ds(pl.multiple_of(r * step, step), step)
            wb_ref[sl, :] = w_ref[sl, :].astype(BF16)
            return carry

        lax.fori_loop(0, w_ref.shape[0] // step, cast_rows, 0)

    k1 = a1_ref.shape[1]
    rows = x_ref.shape[0]
    step = min(MXU_DIM, rows)
    per_row = sc_ref.shape[0] == rows
    pieces = [slice(r0, r0 + step) for r0 in range(0, rows, step)]
    x1s = []
    for sl in pieces:
        acc = jnp.dot(a1_ref[sl, :], wb_ref[0:k1], preferred_element_type=F32) \
            + jnp.dot(a2_ref[sl, :], wb_ref[k1:], preferred_element_type=F32)
        g = g_ref[sl, :] if per_row else g_ref[...]
        x1 = x_ref[sl, :] + g * acc
        o_ref[sl, :] = x1
        x1s.append(x1)
    for sl, x1 in zip(pieces, x1s):
        ms = jnp.mean(x1 * x1, axis=-1, keepdims=True)
        xn = x1 * lax.rsqrt(ms + RMS_EPS) * gain_ref[...]
        sc = sc_ref[sl, :] if per_row else sc_ref[...]
        sh = sh_ref[sl, :] if per_row else sh_ref[...]
        h_ref[sl, :] = (xn * (1.0 + sc) + sh).astype(BF16)


def _out_proj(a1, a2, w, x, mod, gain, row_tile):
    m, d = x.shape
    tm = min(row_tile, m)
    whole = lambda j: 0
    row = lambda kdim: pl.BlockSpec((tm, kdim), lambda i, j: (i, 0))
    return pl.pallas_call(
        _out_proj_kernel,
        grid=(m // tm, 1),
        in_specs=[
            row(a1.shape[1]), row(a2.shape[1]),
            pl.BlockSpec(w.shape, lambda i, j: (0, 0), pipeline_mode=pl.Buffered(1)),
            row(d),
            mod.spec(2, d, whole, tm),
            pl.BlockSpec((1, d), lambda i, j: (0, 0)),
            mod.spec(4, d, whole, tm),
            mod.spec(3, d, whole, tm),
        ],
        out_specs=[row(d), row(d)],
        out_shape=[jax.ShapeDtypeStruct((m, d), F32), jax.ShapeDtypeStruct((m, d), BF16)],
        scratch_shapes=[pltpu.VMEM(w.shape, BF16)],
        compiler_params=_cparams(("arbitrary", "arbitrary")),
        name="out_proj",
    )(a1, a2, w, x, mod.arr, gain.reshape(1, d), mod.arr, mod.arr)


def _gelu(x):
    return 0.5 * x * (1.0 + lax.erf(x * (2.0 ** -0.5)))


def _ffn_up_kernel(tiles_per_batch, h_ref, wg_ref, wv_ref, hist_ref, cw_ref, cb_ref, hs_ref, wd_ref,
                   act_ref, last_ref, gs_ref, vs_ref, wdb_ref, carry_ref):
    i = pl.program_id(0)
    j = pl.program_id(1)

    @pl.when(i == 0)
    def _():
        hs = hs_ref[...]
        gs_ref[...] = jnp.dot(hs, wg_ref[...].astype(BF16), preferred_element_type=F32)
        vs_ref[...] = jnp.dot(hs, wv_ref[...].astype(BF16), preferred_element_type=F32)
        wdb_ref[...] = wd_ref[...].astype(BF16)

    @pl.when((i % tiles_per_batch) == 0)
    def _():
        carry_ref[j] = hist_ref[...]

    h = h_ref[...]
    gate = jnp.dot(h, wg_ref[...].astype(BF16), preferred_element_type=F32)
    val = jnp.dot(h, wv_ref[...].astype(BF16), preferred_element_type=F32)
    tm = gate.shape[0]
    prev = carry_ref[j]
    row = _iota(gate.shape, 0)
    g1 = pltpu.roll(gate, 1, 0)
    g2 = pltpu.roll(gate, 2, 0)
    g1 = jnp.where(row == 0, prev[1:2], g1)
    g2 = jnp.where(row == 0, prev[0:1], jnp.where(row == 1, prev[1:2], g2))
    cw = cw_ref[...]
    conv = cb_ref[...] + g2 * cw[0:1] + g1 * cw[1:2] + gate * cw[2:3]
    act_ref[...] = (_gelu(conv) * val).astype(BF16)
    tail = gate[tm - 2:tm]
    carry_ref[j] = tail
    last_ref[...] = tail


def _ffn_up_fused(h, hs, w_up, w_down, hist, conv_w, conv_b, rows_per_batch):
    m, d = h.shape
    ms = hs.shape[0]
    f = w_up.shape[1] // 2
    tm = min(ROW_TILE, rows_per_batch)
    cols = UP_COLS
    nj = f // cols
    tiles_per_batch = rows_per_batch // tm
    rider_block = pl.BlockSpec((ms, cols), lambda i, j: (0, _rider_col(i, j, nj)))
    wd_block = pl.BlockSpec((f // nj, w_down.shape[1]), lambda i, j: (_rider_col(i, j, nj), 0))
    act, tile_tails, gate_s, val_s, w_down_bf16 = pl.pallas_call(
        functools.partial(_ffn_up_kernel, tiles_per_batch),
        grid=(m // tm, nj),
        in_specs=[
            pl.BlockSpec((tm, d), lambda i, j: (i, 0)),
            pl.BlockSpec((d, cols), lambda i, j: (0, j)),
            pl.BlockSpec((d, cols), lambda i, j: (0, nj + j)),
            pl.BlockSpec((None, CONV_W - 1, cols), lambda i, j: (i // tiles_per_batch, 0, j)),
            pl.BlockSpec((CONV_W, cols), lambda i, j: (0, j)),
            pl.BlockSpec((1, cols), lambda i, j: (0, j)),
            pl.BlockSpec((ms, d), lambda i, j: (0, 0)),
            wd_block,
        ],
        out_specs=[
            pl.BlockSpec((tm, cols), lambda i, j: (i, j)),
            pl.BlockSpec((None, CONV_W - 1, cols), lambda i, j: (i, 0, j)),
            rider_block, rider_block, wd_block,
        ],
        out_shape=[
            jax.ShapeDtypeStruct((m, f), BF16),
            jax.ShapeDtypeStruct((m // tm, CONV_W - 1, f), F32),
            jax.ShapeDtypeStruct((ms, f), F32),
            jax.ShapeDtypeStruct((ms, f), F32),
            jax.ShapeDtypeStruct(w_down.shape, BF16),
        ],
        scratch_shapes=[pltpu.VMEM((nj, CONV_W - 1, cols), F32)],
        compiler_params=_cparams(("arbitrary", "arbitrary")),
        name="ffn_up_prompt",
    )(h, w_up, w_up, hist, conv_w, conv_b.reshape(1, f), hs, w_down)
    return act, tile_tails[tiles_per_batch - 1::tiles_per_batch], gate_s, val_s, w_down_bf16


def _act_sample_kernel(gate_ref, val_ref, hist_ref, cw_ref, cb_ref, act_ref):
    gate = gate_ref[...]
    hist = hist_ref[...]
    t = _iota(gate.shape, 1)
    g1 = jnp.where(t == 0, hist[:, 1:2], pltpu.roll(gate, 1, 1))
    g2 = jnp.where(t == 0, hist[:, 0:1], jnp.where(t == 1, hist[:, 1:2], pltpu.roll(gate, 2, 1)))
    cw = cw_ref[...]
    conv = cb_ref[...] + g2 * cw[0:1] + g1 * cw[1:2] + gate * cw[2:3]
    act_ref[...] = (_gelu(conv) * val_ref[...]).astype(BF16)


def _act_sample(gate, val, hist, conv_w, conv_b, nb, t):
    f = gate.shape[1]
    cols = UP_COLS
    nj = f // cols
    gate3 = gate.reshape(nb, t, f)
    val3 = val.reshape(nb, t, f)
    act = pl.pallas_call(
        _act_sample_kernel,
        grid=(nj,),
        in_specs=[
            pl.BlockSpec((nb, t, cols), lambda j: (0, 0, j)),
            pl.BlockSpec((nb, t, cols), lambda j: (0, 0, j)),
            pl.BlockSpec((nb, CONV_W - 1, cols), lambda j: (0, 0, j)),
            pl.BlockSpec((CONV_W, cols), lambda j: (0, j)),
            pl.BlockSpec((1, cols), lambda j: (0, j)),
        ],
        out_specs=pl.BlockSpec((nb, t, cols), lambda j: (0, 0, j)),
        out_shape=jax.ShapeDtypeStruct((nb, t, f), BF16),
        compiler_params=_cparams(("arbitrary",)),
        name="ffn_act_sample",
    )(gate3, val3, hist, conv_w, conv_b.reshape(1, f))
    return act.reshape(nb * t, f)


def _pair_rms(x, gain):
    x2 = x * x
    first = _iota(x.shape, 1) < HEAD_DIM
    s0 = jnp.sum(jnp.where(first, x2, 0.0), axis=-1, keepdims=True)
    s1 = jnp.sum(jnp.where(first, 0.0, x2), axis=-1, keepdims=True)
    ms = jnp.where(first, s0, s1) * (1.0 / HEAD_DIM)
    return x * lax.rsqrt(ms + RMS_EPS) * gain


def _bias_rows(table):
    h = table.shape[0]
    far = jnp.broadcast_to(table[:, 2 * REL_CLIP:], (h, ATT_REACH - REL_CLIP))
    mid = table[:, ::-1]
    near_len = BIAS_LEN - ATT_QROWS - (ATT_REACH - REL_CLIP) - (2 * REL_CLIP + 1)
    near = jnp.broadcast_to(table[:, 0:1], (h, near_len))
    wrap = jnp.broadcast_to(table[:, 2 * REL_CLIP:], (h, ATT_QROWS))
    return jnp.concatenate([far, mid, near, wrap], axis=1)


def _toeplitz(u_row, rows):
    return pltpu.roll(jnp.broadcast_to(u_row, (rows, BIAS_LEN)), 0, 1, stride=1, stride_axis=0)


def _attn_prompt_kernel(q_ref, k_ref, v_ref, qg_ref, kg_ref, u_ref, o_ref, kn_ref, vk_ref,
                        bias_ref, kwin_ref, vwin_ref):
    b = pl.program_id(1)
    qb = pl.program_id(2)
    shape = (ATT_QROWS, ATT_WIN)
    pairs = range(q_ref.shape[1] // PAIR)
    cols = [slice(p * PAIR, (p + 1) * PAIR) for p in pairs]
    chains = [(p, h) for p in pairs for h in range(2)]

    @pl.when((b == 0) & (qb == 0))
    def _():
        r = _iota(shape, 0)
        w = _iota(shape, 1)
        chunk_lo = _blk(r, CHUNK) * CHUNK
        in_band = (w >= chunk_lo) & (w < chunk_lo + (ATT_REACH + CHUNK))
        for i, (p, h) in enumerate(chains):
            bias = _toeplitz(u_ref[p, h:h + 1, :], ATT_QROWS)[:, :ATT_WIN]
            bias_ref[i] = jnp.where(in_band, bias * LOG2E, -jnp.inf)

    @pl.when(qb == 0)
    def _():
        kwin_ref[0:ATT_REACH] = jnp.zeros((ATT_REACH, kwin_ref.shape[1]), BF16)
        vwin_ref[0:ATT_REACH] = jnp.zeros((ATT_REACH, vwin_ref.shape[1]), BF16)

    @pl.when(qb > 0)
    def _():
        kwin_ref[0:ATT_REACH] = kwin_ref[ATT_QROWS:ATT_WIN]
        vwin_ref[0:ATT_REACH] = vwin_ref[ATT_QROWS:ATT_WIN]

    kn = [_pair_rms(k_ref[:, c], kg_ref[...]) for c in cols]
    for p in pairs:
        kn_ref[:, cols[p]] = kn[p]
        kwin_ref[ATT_REACH:ATT_WIN, cols[p]] = kn[p].astype(BF16)
    v_new = v_ref[...]
    vk_ref[...] = v_new
    vwin_ref[ATT_REACH:ATT_WIN] = v_new.astype(BF16)

    def attend(mask_start):
        qn = [_pair_rms(q_ref[:, c], qg_ref[...]) * (ATT_SCALE * LOG2E) for c in cols]
        kb = [kwin_ref[:, c] for c in cols]
        vb = [vwin_ref[:, c] for c in cols]
        first = _iota((ATT_QROWS, PAIR), 1) < HEAD_DIM
        first_w = _iota((ATT_WIN, PAIR), 1) < HEAD_DIM
        qh = [jnp.where(first, qn[p], 0.0) if h == 0 else jnp.where(first, 0.0, qn[p]) for p, h in chains]
        s = [_dot(qh[i], kb[p], NT) + bias_ref[i] for i, (p, h) in enumerate(chains)]
        if mask_start:
            started = _iota(shape, 1) >= ATT_REACH - qb * ATT_QROWS
            s = [jnp.where(started, x, -jnp.inf) for x in s]
        m = [jnp.max(x, axis=-1, keepdims=True) for x in s]
        pr = [jnp.exp2(x - mm) for x, mm in zip(s, m)]
        one = jnp.ones((), BF16)
        v_aug = [jnp.where(first_w, vb[p], one) if h == 0 else jnp.where(first_w, one, vb[p]) for p, h in chains]
        o = [_dot(pr[i], v_aug[i]) for i in range(len(chains))]
        o = [x / pltpu.roll(x, HEAD_DIM, 1) for x in o]
        for p in pairs:
            o_ref[:, cols[p]] = jnp.where(first, o[2 * p], o[2 * p + 1]).astype(BF16)

    full_window_from = ATT_REACH // ATT_QROWS
    pl.when(qb < full_window_from)(lambda: attend(True))
    pl.when(qb >= full_window_from)(lambda: attend(False))


def _attn_prompt(z3, q_gain, k_gain, u):
    nb, t, _ = z3.shape
    npairs = N_ATT_HEADS // 2
    npp = ATT_PAIRS
    width = npp * PAIR
    ngroups = npairs // npp
    nq = t // ATT_QROWS
    kcol = D_ATT // width
    vcol = 2 * D_ATT // width
    keep_blocks = ATT_REACH // ATT_QROWS
    blk = (None, ATT_QROWS, width)

    keep_spec = pl.BlockSpec(blk, lambda g, b, q: (b, jnp.maximum(q - (nq - keep_blocks), 0), g))
    att, kn, vk = pl.pallas_call(
        _attn_prompt_kernel,
        grid=(ngroups, nb, nq),
        in_specs=[
            pl.BlockSpec(blk, lambda g, b, q: (b, q, g)),
            pl.BlockSpec(blk, lambda g, b, q: (b, q, kcol + g)),
            pl.BlockSpec(blk, lambda g, b, q: (b, q, vcol + g)),
            pl.BlockSpec((1, PAIR), lambda g, b, q: (0, 0)),
            pl.BlockSpec((1, PAIR), lambda g, b, q: (0, 0)),
            pl.BlockSpec((npp, 2, BIAS_LEN), lambda g, b, q: (g, 0, 0)),
        ],
        out_specs=[
            pl.BlockSpec(blk, lambda g, b, q: (b, q, g)),
            keep_spec, keep_spec,
        ],
        out_shape=[
            jax.ShapeDtypeStruct((nb, t, D_ATT), BF16),
            jax.ShapeDtypeStruct((nb, ATT_REACH, D_ATT), F32),
            jax.ShapeDtypeStruct((nb, ATT_REACH, D_ATT), F32),
        ],
        scratch_shapes=[pltpu.VMEM((2 * npp, ATT_QROWS, ATT_WIN), F32),
                        pltpu.VMEM((ATT_WIN, width), BF16), pltpu.VMEM((ATT_WIN, width), BF16)],
        compiler_params=_cparams(("arbitrary", "arbitrary", "arbitrary")),
        name="attn_prompt",
    )(z3, z3, z3, jnp.tile(q_gain, 2).reshape(1, PAIR),
      jnp.tile(k_gain, 2).reshape(1, PAIR), u.reshape(npairs, 2, BIAS_LEN))
    return att, kn, vk


def _attn_sample_kernel(q_ref, k_ref, v_ref, kp_ref, vp_ref, qg_ref, kg_ref, u_ref, o_ref, kn_ref):
    t = q_ref.shape[0]
    reach = kp_ref.shape[0]
    first = _iota((t, PAIR), 1) < HEAD_DIM
    pairs = range(N_ATT_HEADS // 2)
    cols = [slice(p * PAIR, (p + 1) * PAIR) for p in pairs]
    chains = [(p, h) for p in pairs for h in range(2)]
    qn = [_pair_rms(q_ref[:, c], qg_ref[...]) * ATT_SCALE for c in cols]
    kn = [_pair_rms(k_ref[:, c], kg_ref[...]) for c in cols]
    for p in pairs:
        kn_ref[:, cols[p]] = kn[p]
    kpast = [kp_ref[:, c].astype(BF16) for c in cols]
    vpast = [vp_ref[:, c].astype(BF16) for c in cols]
    vnew = [v_ref[:, c].astype(BF16) for c in cols]
    qh = [jnp.where(first, qn[p], 0.0) if h == 0 else jnp.where(first, 0.0, qn[p]) for p, h in chains]
    bias = [_toeplitz(u_ref[p, h:h + 1, :], t) for p, h in chains]
    s_past = [_dot(qh[i], kpast[p], NT) + bias[i][:, :reach] for i, (p, h) in enumerate(chains)]
    s_new = [_dot(qh[i], kn[p], NT) + bias[i][:, reach:reach + t] for i, (p, h) in enumerate(chains)]
    m = [jnp.maximum(jnp.max(a, axis=-1, keepdims=True), jnp.max(b, axis=-1, keepdims=True))
         for a, b in zip(s_past, s_new)]
    p_past = [jnp.exp(a - mm) for a, mm in zip(s_past, m)]
    p_new = [jnp.exp(b - mm) for b, mm in zip(s_new, m)]
    l = [jnp.sum(a, axis=-1, keepdims=True) + jnp.sum(b, axis=-1, keepdims=True)
         for a, b in zip(p_past, p_new)]
    o = [(_dot(p_past[i], vpast[p]) + _dot(p_new[i], vnew[p])) / l[i] for i, (p, h) in enumerate(chains)]
    for p in pairs:
        o_ref[:, cols[p]] = jnp.where(first, o[2 * p], o[2 * p + 1]).astype(BF16)


def _attn_sample(z3, k_past, v_past, q_gain, k_gain, u):
    nb, t, _ = z3.shape
    reach = k_past.shape[1]
    npairs = N_ATT_HEADS // 2
    att, kn = pl.pallas_call(
        _attn_sample_kernel,
        grid=(nb,),
        in_specs=[
            pl.BlockSpec((None, t, D_ATT), lambda b: (b, 0, 0)),
            pl.BlockSpec((None, t, D_ATT), lambda b: (b, 0, 1)),
            pl.BlockSpec((None, t, D_ATT), lambda b: (b, 0, 2)),
            pl.BlockSpec((None, reach, D_ATT), lambda b: (b, 0, 0)),
            pl.BlockSpec((None, reach, D_ATT), lambda b: (b, 0, 0)),
            pl.BlockSpec((1, PAIR), lambda b: (0, 0)),
            pl.BlockSpec((1, PAIR), lambda b: (0, 0)),
            pl.BlockSpec((npairs, 2, BIAS_LEN), lambda b: (0, 0, 0)),
        ],
        out_specs=[
            pl.BlockSpec((None, t, D_ATT), lambda b: (b, 0, 0)),
            pl.BlockSpec((None, t, D_ATT), lambda b: (b, 0, 0)),
        ],
        out_shape=[
            jax.ShapeDtypeStruct((nb, t, D_ATT), BF16),
            jax.ShapeDtypeStruct((nb, t, D_ATT), F32),
        ],
        compiler_params=_cparams(("arbitrary",)),
        name="attn_sample",
    )(z3, z3, z3, k_past.reshape(nb, reach, D_ATT), v_past.reshape(nb, reach, D_ATT),
      jnp.tile(q_gain, 2).reshape(1, PAIR), jnp.tile(k_gain, 2).reshape(1, PAIR),
      u.reshape(npairs, 2, BIAS_LEN))
    return att, kn


def _tri_inverse(l_mats, c):
    n = l_mats[0].shape[0]
    eye = jnp.where(_iota((n, n), 0) == _iota((n, n), 1), 1.0, 0.0).astype(F32)
    a_s = [(eye + l).astype(BF16) for l in l_mats]
    t_s = [eye - l for l in l_mats]
    for _ in range(c.bit_length() - 2):
        r_s = [eye - _dot(a, t) for a, t in zip(a_s, t_s)]
        t_s = [t + _dot(t, r) for t, r in zip(t_s, r_s)]
    return t_s


def _rwkv_kernel(c, r_ref, k_ref, v_ref, lo_ref, sr_ref, sk_ref, sv_ref, slo_ref, s0_ref,
                 mur_ref, muk_ref, muv_ref, mulo_ref, w0_ref, a0_ref, kkg_ref, ka_ref, rk_ref,
                 lnw_ref, lnb_ref, w2_ref, a2_ref, g2_ref,
                 o_ref, sT_ref, s_ref, cr_ref, ck_ref, cv_ref, clo_ref):
    tb = pl.program_id(2)
    rows, width = r_ref.shape
    npp = width // PAIR
    nchunks = rows // c
    h0 = _iota((rows, PAIR), 1) < HEAD_DIM
    bd = _blk(_iota((PAIR, PAIR), 0), HEAD_DIM) == _blk(_iota((PAIR, PAIR), 1), HEAD_DIM)

    @pl.when(tb == 0)
    def _():
        s_ref[...] = jnp.zeros(s_ref.shape, F32)
        for pp in range(npp):
            s_ref[pp, 0:HEAD_DIM, 0:HEAD_DIM] = s0_ref[2 * pp]
            s_ref[pp, HEAD_DIM:PAIR, HEAD_DIM:PAIR] = s0_ref[2 * pp + 1]
        cr_ref[...] = sr_ref[...]
        ck_ref[...] = sk_ref[...]
        cv_ref[...] = sv_ref[...]
        clo_ref[...] = slo_ref[...]

    def shifted(x_ref, carry_ref, mu_ref):
        x = x_ref[...]
        prev = jnp.where(_iota(x.shape, 0) == 0, carry_ref[...], pltpu.roll(x, 1, 0))
        carry_ref[...] = x[rows - 1:rows]
        return x + (prev - x) * mu_ref[...]

    r = shifted(r_ref, cr_ref, mur_ref)
    k = shifted(k_ref, ck_ref, muk_ref)
    v = shifted(v_ref, cv_ref, muv_ref)
    lo = shifted(lo_ref, clo_ref, mulo_ref)

    zeros_w = jnp.zeros((RANK_W, width), F32)
    w2p = jnp.concatenate([w2_ref[...], zeros_w], axis=0)
    a2p = jnp.concatenate([zeros_w, a2_ref[...]], axis=0)
    lo_wa = lo[:, 0:RANK_W + RANK_A]
    u = w0_ref[...] + _dot(jnp.tanh(lo_wa), w2p)
    lw = -DECAY_SCALE * _sigmoid(u)
    a = _sigmoid(a0_ref[...] + _dot(lo_wa, a2p))
    g = _dot(_sigmoid(lo[:, RANK_W + RANK_A:]), g2_ref[...])

    kk = k * kkg_ref[...]
    kk = kk * lax.rsqrt(jnp.maximum(_head_sums(kk * kk), KK_EPS * KK_EPS))
    k = k * (1.0 + (a - 1.0) * ka_ref[...])
    b = kk * a
    bonus = _head_sums(r * k * rk_ref[...]) * v

    tr = _iota((rows, rows), 0)
    tc = _iota((rows, rows), 1)
    same_chunk = _blk(tr, c) == _blk(tc, c)
    strict = same_chunk & (tr > tc)
    incl = same_chunk & (tr >= tc)
    lw_hi, lw_lo = _split2(lw)
    tril_ones = jnp.where(incl, 1.0, 0.0).astype(BF16)
    lp = jnp.dot(tril_ones, lw_hi, preferred_element_type=F32) + \
        jnp.dot(tril_ones, lw_lo, preferred_element_type=F32)
    decay_end = [jnp.exp(lp[(ci + 1) * c - 1:(ci + 1) * c]) for ci in range(nchunks)]

    alpha_w = kk * jnp.exp(lp - lw)
    inv_p = jnp.exp(-lp)
    beta_w = b * inv_p
    kappa_w = k * inv_p
    rho_w = r * jnp.exp(lp)
    to_end = [inv_p[ci * c:(ci + 1) * c] * decay_end[ci] for ci in range(nchunks)]
    to_end = to_end[0] if nchunks == 1 else jnp.concatenate(to_end, axis=0)
    beta_ew = b * to_end
    kappa_ew = k * to_end

    wide = (rows, nchunks * PAIR)
    col_chunk = _blk(_iota(wide, 1), PAIR) == _blk(_iota(wide, 0), c)
    spread = lambda m: jnp.where(col_chunk, jnp.tile(m, (1, nchunks)), 0.0)
    eye_p = _iota((PAIR, PAIR), 0) == _iota((PAIR, PAIR), 1)

    pairs = range(npp)
    lanes = [slice(pp * PAIR, (pp + 1) * PAIR) for pp in pairs]
    alpha = [alpha_w[:, l] for l in lanes]
    rho = [rho_w[:, l] for l in lanes]
    vv = [v[:, l] for l in lanes]
    head_mask = [h0, jnp.logical_not(h0)]
    beta = [beta_w[:, l] for l in lanes]
    kappa = [kappa_w[:, l] for l in lanes]
    a_b, a_k, r_b, r_k = [], [], [], []
    if rows == 2 * PAIR:
        halves = [slice(0, PAIR), slice(PAIR, rows)]
        swap = lambda m: jnp.concatenate([m[:, PAIR:], m[:, :PAIR]], axis=1)
        half_h0 = _iota((PAIR, PAIR), 1) < HEAD_DIM
        for pp in pairs:
            keys = [jnp.concatenate([beta[pp][halves[0]], kappa[pp][halves[0]]], axis=0).astype(BF16),
                    jnp.concatenate([kappa[pp][halves[1]], beta[pp][halves[1]]], axis=0).astype(BF16)]
            for hm in (half_h0, jnp.logical_not(half_h0)):
                out = [_dot(jnp.concatenate([jnp.where(hm, alpha[pp][sl], 0.0),
                                             jnp.where(hm, rho[pp][sl], 0.0)], axis=0), keys[g], NT)
                       for g, sl in enumerate(halves)]
                a_rows = jnp.concatenate([out[0][:PAIR], out[1][:PAIR]], axis=0)
                r_rows = jnp.concatenate([out[0][PAIR:], out[1][PAIR:]], axis=0)
                a_b.append(a_rows)
                a_k.append(swap(a_rows))
                r_b.append(r_rows)
                r_k.append(swap(r_rows))
    else:
        for pp in pairs:
            keys = jnp.concatenate([beta[pp], kappa[pp]], axis=0).astype(BF16)
            for hm in head_mask:
                out = _dot(jnp.concatenate([jnp.where(hm, alpha[pp], 0.0), jnp.where(hm, rho[pp], 0.0)], axis=0),
                           keys, NT)
                a_b.append(out[:rows, :rows])
                a_k.append(out[:rows, rows:])
                r_b.append(out[rows:, :rows])
                r_k.append(out[rows:, rows:])
    t_inv = _tri_inverse([jnp.where(strict, m, 0.0) for m in a_b], c)
    x = [[_dot(jnp.where(strict, a_k[2 * pp + h], 0.0), vv[pp]) for h in range(2)] for pp in pairs]
    ws = [[_dot(t_inv[2 * pp + h], jnp.concatenate([alpha[pp], x[pp][h]], axis=1)) for h in range(2)]
          for pp in pairs]
    w12 = [jnp.concatenate([jnp.where(h0, ws[pp][0][:, :PAIR], ws[pp][1][:, :PAIR]),
                            jnp.where(h0, ws[pp][0][:, PAIR:], ws[pp][1][:, PAIR:])], axis=1) for pp in pairs]
    q = [[_dot(jnp.where(incl, r_b[2 * pp + h], 0.0), w12[pp]) for h in range(2)] for pp in pairs]
    qk = [[_dot(jnp.where(incl, r_k[2 * pp + h], 0.0), vv[pp]) for h in range(2)] for pp in pairs]
    rp = [rho[pp] - jnp.where(h0, q[pp][0][:, :PAIR], q[pp][1][:, :PAIR]) for pp in pairs]
    y0 = [jnp.where(h0, qk[pp][0] - q[pp][0][:, PAIR:], qk[pp][1] - q[pp][1][:, PAIR:]) for pp in pairs]
    wtb = [_dot(w12[pp], spread(beta_ew[:, lanes[pp]]), TN) for pp in pairs]
    vtk = [_dot(vv[pp], spread(kappa_ew[:, lanes[pp]]), TN) for pp in pairs]

    s_cur = [s_ref[pp] for pp in pairs]
    ys = [[] for _ in pairs]
    for ci in range(nchunks):
        sl = slice(ci * c, (ci + 1) * c)
        cols = slice(ci * PAIR, (ci + 1) * PAIR)
        for pp in pairs:
            p_end = decay_end[ci][:, lanes[pp]]
            gmat = jnp.where(eye_p, jnp.broadcast_to(p_end, (PAIR, PAIR)), 0.0) \
                - jnp.where(bd, wtb[pp][:PAIR, cols], 0.0)
            hmat = jnp.where(bd, vtk[pp][:, cols] - wtb[pp][PAIR:, cols], 0.0)
            ys[pp].append(_dot(rp[pp][sl], s_cur[pp], NT) + y0[pp][sl])
            s_cur[pp] = _dot(s_cur[pp], gmat) + hmat
    for pp in pairs:
        s_ref[pp] = s_cur[pp]
    y_pairs = [ys[pp][0] if nchunks == 1 else jnp.concatenate(ys[pp], axis=0) for pp in pairs]

    @pl.when(tb == pl.num_programs(2) - 1)
    def _():
        for pp in range(npp):
            sT_ref[2 * pp] = s_ref[pp, 0:HEAD_DIM, 0:HEAD_DIM]
            sT_ref[2 * pp + 1] = s_ref[pp, HEAD_DIM:PAIR, HEAD_DIM:PAIR]

    y = y_pairs[0] if npp == 1 else jnp.concatenate(y_pairs, axis=1)
    mu = _head_sums(y) * (1.0 / HEAD_DIM)
    d = y - mu
    var = _head_sums(d * d) * (1.0 / HEAD_DIM)
    yn = d * lax.rsqrt(var + GN_EPS) * lnw_ref[...] + lnb_ref[...]
    o_ref[...] = ((yn + bonus) * g).astype(BF16)


def _rwkv(z3, zlo3, shift_prev, s0, p, rows, c, npp):
    nb, t, _ = z3.shape
    width = npp * PAIR
    ngroups = D_RWKV // width
    col0 = 3 * D_ATT // width
    sp = shift_prev.reshape(nb, 1, D_SHIFT)

    def zspec(off):
        return pl.BlockSpec((None, rows, width), lambda b, q, s: (b, s, col0 + off * ngroups + q))

    def sspec(off):
        return pl.BlockSpec((None, 1, width), lambda b, q, s: (b, 0, off * ngroups + q))

    def vec(off=0):
        return pl.BlockSpec((1, width), lambda b, q, s: (0, off * ngroups + q))

    def row2(x):
        return x.reshape(1, -1)

    out, s_fin = pl.pallas_call(
        functools.partial(_rwkv_kernel, c),
        grid=(nb, ngroups, t // rows),
        in_specs=[
            zspec(0), zspec(1), zspec(2),
            pl.BlockSpec((None, rows, D_LORA), lambda b, q, s: (b, s, 0)),
            sspec(0), sspec(1), sspec(2),
            pl.BlockSpec((None, 1, D_LORA), lambda b, q, s: (b, 0, 3 * D_RWKV // D_LORA)),
            pl.BlockSpec((None, 2 * npp, HEAD_DIM, HEAD_DIM), lambda b, q, s: (b, q, 0, 0)),
            vec(0), vec(1), vec(2),
            pl.BlockSpec((1, D_LORA), lambda b, q, s: (0, 3 * D_RWKV // D_LORA)),
            vec(), vec(), vec(), vec(), vec(), vec(), vec(),
            pl.BlockSpec((RANK_W, width), lambda b, q, s: (0, q)),
            pl.BlockSpec((RANK_A, width), lambda b, q, s: (0, q)),
            pl.BlockSpec((RANK_G, width), lambda b, q, s: (0, q)),
        ],
        out_specs=[
            pl.BlockSpec((None, rows, width), lambda b, q, s: (b, s, q)),
            pl.BlockSpec((None, 2 * npp, HEAD_DIM, HEAD_DIM), lambda b, q, s: (b, q, 0, 0)),
        ],
        out_shape=[
            jax.ShapeDtypeStruct((nb, t, D_RWKV), BF16),
            jax.ShapeDtypeStruct((nb, N_RWKV_HEADS, HEAD_DIM, HEAD_DIM), F32),
        ],
        scratch_shapes=[
            pltpu.VMEM((npp, PAIR, PAIR), F32),
            pltpu.VMEM((1, width), F32), pltpu.VMEM((1, width), F32), pltpu.VMEM((1, width), F32),
            pltpu.VMEM((1, D_LORA), F32),
        ],
        compiler_params=_cparams(("arbitrary", "arbitrary", "arbitrary")),
        name="rwkv7_mix",
    )(z3, z3, z3, zlo3, sp, sp, sp, sp, s0,
      row2(p['mu_shift']), row2(p['mu_shift']), row2(p['mu_shift']), row2(p['mu_shift']),
      row2(p['w0']), row2(p['a0']), row2(p['k_k']), row2(p['k_a']), row2(p['r_k']),
      row2(p['ln_x_w']), row2(p['ln_x_b']), p['w2'], p['a2'], p['g2'])
    return out, s_fin


def _layer(xp3, xs3, mod_p, mod_s, p, u, k_past, v_past, s0_p, s0_s, shift_p, shift_s, conv_p, conv_s):
    bp, tp, d = xp3.shape
    bs, ts, _ = xs3.shape
    mp, msr = bp * tp, bs * ts
    xp = xp3.reshape(mp, d)
    xs = xs3.reshape(msr, d)
    zp, zp_lo, zs, zs_lo = _norm_proj(xp, xs, p['norm_att_g'], mod_p, mod_s, 1, 0, p['w_in'], IN_COLS,
                                      "in_proj", min(IN_ROW_TILE, tp))
    assert zp.shape[1] == 3 * D_ATT + 3 * D_RWKV and zp_lo.shape[1] == D_LORA
    zp3, zp_lo3 = zp.reshape(bp, tp, -1), zp_lo.reshape(bp, tp, -1)
    zs3, zs_lo3 = zs.reshape(bs, ts, -1), zs_lo.reshape(bs, ts, -1)
    att_p, k_keep_p, v_keep_p = _attn_prompt(zp3, p['q_norm_g'], p['k_norm_g'], u)
    rw_p, s_fin_p = _rwkv(zp3, zp_lo3, shift_p, s0_p, p, RWKV_ROWS, CHUNK, RWKV_PAIRS_PROMPT)
    att_s, k_keep_s = _attn_sample(zs3, k_past, v_past, p['q_norm_g'], p['k_norm_g'], u)
    v_keep_s = zs3[:, :, 2 * D_ATT:3 * D_ATT]
    rw_s, s_fin_s = _rwkv(zs3, zs_lo3, shift_s, s0_s, p, ts, ts, RWKV_PAIRS_SAMPLE)
    shift_last_p = jnp.concatenate([zp3[:, tp - 1, 3 * D_ATT:], zp_lo3[:, tp - 1]], axis=-1)
    shift_last_s = jnp.concatenate([zs3[:, ts - 1, 3 * D_ATT:], zs_lo3[:, ts - 1]], axis=-1)
    x1p, h2p = _out_proj(att_p.reshape(mp, D_ATT), rw_p.reshape(mp, D_RWKV), p['w_out'], xp, mod_p,
                         p['norm_ffn_g'], OUT_ROW_TILE)
    x1s, h2s = _out_proj(att_s.reshape(msr, D_ATT), rw_s.reshape(msr, D_RWKV), p['w_out'], xs, mod_s,
                         p['norm_ffn_g'], OUT_ROW_TILE)
    act_p, conv_last_p, gate_s, val_s, w_down = _ffn_up_fused(h2p, h2s, p['w_up'], p['w_down'], conv_p,
                                                              p['dw_conv'], p['dw_bias'], tp)
    act_s = _act_sample(gate_s, val_s, conv_s, p['dw_conv'], p['dw_bias'], bs, ts)
    conv_last_s = gate_s.reshape(bs, ts, -1)[:, ts - (CONV_W - 1):]
    x2p, x2s = _proj_resid(act_p, act_s, w_down, x1p, x1s, mod_p, mod_s, 5, DOWN_COLS, "ffn_down")
    heads = lambda a: a.reshape(a.shape[0], a.shape[1], N_ATT_HEADS, HEAD_DIM)
    out_p = (x2p.reshape(bp, tp, d), heads(k_keep_p), heads(v_keep_p), s_fin_p, shift_last_p, conv_last_p)
    out_s = (x2s.reshape(bs, ts, d), heads(k_keep_s), heads(v_keep_s), s_fin_s, shift_last_s, conv_last_s)
    return out_p, out_s


def kernel(x_prompt, x_sample, c_prompt, c_sample, cache_att_k, cache_att_v, state_rwkv, state_shift, state_ffn_conv, norm_att_g, norm_ffn_g, w_ada, b_ada, w_in, q_norm_g, k_norm_g, rel_bias, mu_shift, w0, w2, a0, a2, g2, k_k, k_a, r_k, ln_x_w, ln_x_b, w_out, w_up, dw_conv, dw_bias, w_down):
    depth = w_in.shape[0]
    bp, tp, d = x_prompt.shape
    bs, ts, _ = x_sample.shape
    d_ff = w_down.shape[1]
    hp, hs = x_prompt, x_sample
    outs_p = [[] for _ in range(5)]
    outs_s = [[] for _ in range(5)]
    for l in range(depth):
        p = dict(norm_att_g=norm_att_g[l], norm_ffn_g=norm_ffn_g[l], w_in=w_in[l], q_norm_g=q_norm_g[l],
                 k_norm_g=k_norm_g[l], mu_shift=mu_shift[l], w0=w0[l], w2=w2[l], a0=a0[l], a2=a2[l],
                 g2=g2[l], k_k=k_k[l], k_a=k_a[l], r_k=r_k[l], ln_x_w=ln_x_w[l], ln_x_b=ln_x_b[l],
                 w_out=w_out[l], w_up=w_up[l], dw_conv=dw_conv[l], dw_bias=dw_bias[l], w_down=w_down[l])
        n_c = bp + bs
        pad = (-n_c) % 8
        c_all = jnp.concatenate([c_prompt, c_sample, jnp.zeros((pad, d), F32)], axis=0)
        mod = _ada(c_all, w_ada[l], b_ada[l])
        mod_p = _Mod(mod.reshape(n_c + pad, 6, 1, d), False, rows_per_batch=tp)
        mod_s = _Mod(jnp.repeat(mod[bp:bp + bs], ts, axis=0), True)
        u = _bias_rows(rel_bias[l])

        res_p, res_s = _layer(hp, hs, mod_p, mod_s, p, u, cache_att_k[l], cache_att_v[l],
                              jnp.zeros((bp, N_RWKV_HEADS, HEAD_DIM, HEAD_DIM), F32), state_rwkv[l],
                              jnp.zeros((bp, D_SHIFT), F32), state_shift[l],
                              jnp.zeros((bp, CONV_W - 1, d_ff), F32), state_ffn_conv[l])
        hp, hs = res_p[0], res_s[0]
        for lst, val in zip(outs_p, res_p[1:]):
            lst.append(val)
        for lst, val in zip(outs_s, res_s[1:]):
            lst.append(val)
    st = lambda lst: jnp.stack(lst)
    return (hp, hs, *[st(x) for x in outs_p], *[st(x) for x in outs_s])
```

```python
import functools

import jax
import jax.numpy as jnp
from jax import lax
from jax.experimental import pallas as pl
from jax.experimental.pallas import tpu as pltpu

F32 = jnp.float32
BF16 = jnp.bfloat16

CHUNK = 64
N_PREV_CHUNKS = 8
ATT_REACH = N_PREV_CHUNKS * CHUNK
HEAD_DIM = 64
N_ATT_HEADS = 16
N_RWKV_HEADS = 16
D_ATT = N_ATT_HEADS * HEAD_DIM
D_RWKV = N_RWKV_HEADS * HEAD_DIM
REL_CLIP = 128
RANK_W = 64
RANK_A = 64
RANK_G = 128
D_LORA = RANK_W + RANK_A + RANK_G
D_SHIFT = 3 * D_RWKV + D_LORA
D_IN = 3 * D_ATT + D_SHIFT
CONV_W = 3
RMS_EPS = 1e-6
GN_EPS = 64e-5
KK_EPS = 1e-12
ATT_SCALE = HEAD_DIM ** -0.5
LOG2E = 1.4426950408889634
DECAY_SCALE = 0.6065306597126334

PAIR = 2 * HEAD_DIM
MXU_DIM = 256
VMEM_LIMIT = 60 * 1024 * 1024

ROW_TILE = 1024
IN_ROW_TILE = 2048
ADA_COLS = 1024
IN_COLS = 512
OUT_ROW_TILE = 512
UP_COLS = 512
DOWN_COLS = 512
ATT_QROWS = 256
ATT_WIN = ATT_QROWS + ATT_REACH
ATT_PAIRS = 4
BIAS_LEN = 1024
RWKV_ROWS = 256
RWKV_PAIRS_PROMPT = 8
RWKV_PAIRS_SAMPLE = 8


def _cparams(sem):
    return pltpu.CompilerParams(dimension_semantics=sem, vmem_limit_bytes=VMEM_LIMIT)


def _dot(a, b, dims=(((1,), (0,)), ((), ()))):
    return lax.dot_general(a.astype(BF16), b.astype(BF16), dims, preferred_element_type=F32)


def _split2(x):
    hi = x.astype(BF16)
    lo = (x - hi.astype(F32)).astype(BF16)
    return hi, lo


NT = (((1,), (1,)), ((), ()))
TN = (((0,), (0,)), ((), ()))


def _iota(shape, dim):
    return lax.broadcasted_iota(jnp.int32, shape, dim)


def _blk(x, size):
    return jnp.right_shift(x, size.bit_length() - 1)


def _head_ones(n):
    r = _blk(_iota((n, n), 0), HEAD_DIM)
    c = _blk(_iota((n, n), 1), HEAD_DIM)
    return jnp.where(r == c, 1.0, 0.0).astype(BF16)


def _head_sums(x):
    lanes = x.shape[1]
    group = min(lanes, MXU_DIM)
    ones = _head_ones(group)
    parts = [_dot(x[:, i:i + group], ones) for i in range(0, lanes, group)]
    return parts[0] if len(parts) == 1 else jnp.concatenate(parts, axis=1)


def _sigmoid(x):
    return 1.0 / (1.0 + jnp.exp(-x))


def _ada_kernel(c_ref, w_ref, b_ref, o_ref):
    c = c_ref[...]
    s = c * _sigmoid(c)
    o_ref[...] = _dot(s, w_ref[...]) + b_ref[...]


def _ada(c_all, w_ada, b_ada):
    rows, d = c_all.shape
    n = w_ada.shape[1]
    return pl.pallas_call(
        _ada_kernel,
        grid=(n // ADA_COLS,),
        in_specs=[
            pl.BlockSpec((rows, d), lambda j: (0, 0)),
            pl.BlockSpec((d, ADA_COLS), lambda j: (0, j)),
            pl.BlockSpec((1, ADA_COLS), lambda j: (0, j)),
        ],
        out_specs=pl.BlockSpec((rows, ADA_COLS), lambda j: (0, j)),
        out_shape=jax.ShapeDtypeStruct((rows, n), F32),
        compiler_params=_cparams(("arbitrary",)),
        name="ada_mod",
    )(c_all, w_ada, b_ada.reshape(1, n))


class _Mod:
    def __init__(self, arr, per_row, rows_per_batch=None):
        self.arr = arr
        self.per_row = per_row
        self.rows_per_batch = rows_per_batch

    def spec(self, idx, cols, col_of, row_tile):
        if self.per_row:
            m = self.arr.shape[0]
            d = self.arr.shape[1] // 6
            nblk = d // cols
            return pl.BlockSpec((m, cols), lambda i, j: (0, idx * nblk + col_of(j)))
        tiles_per_batch = self.rows_per_batch // row_tile
        return pl.BlockSpec((None, None, 1, cols),
                            lambda i, j: (i // tiles_per_batch, idx, 0, col_of(j)))

    def rider_spec(self, idx, cols, nj):
        m = self.arr.shape[0]
        nblk = self.arr.shape[1] // 6 // cols
        return pl.BlockSpec((m, cols), lambda i, j: (0, idx * nblk + _rider_col(i, j, nj)))


def _rider_col(i, j, nj):
    return jnp.where(i == 0, j, nj - 1)


NORM_ROWS = 128


def _store_normed(h_ref, x_ref, g_ref, sc_ref, sh_ref):
    rows = x_ref.shape[0]
    step = min(NORM_ROWS, rows)
    per_row = sc_ref.shape[0] == rows

    def body(r, carry):
        sl = pl.ds(pl.multiple_of(r * step, step), step)
        x = x_ref[sl, :]
        ms = jnp.mean(x * x, axis=-1, keepdims=True)
        xn = x * lax.rsqrt(ms + RMS_EPS) * g_ref[...]
        sc = sc_ref[sl, :] if per_row else sc_ref[...]
        sh = sh_ref[sl, :] if per_row else sh_ref[...]
        h_ref[sl, :] = (xn * (1.0 + sc) + sh).astype(BF16)
        return carry

    lax.fori_loop(0, rows // step, body, 0)


def _norm_proj_kernel(nj, x_ref, g_ref, sc_ref, sh_ref, w_ref, wt_ref, xs_ref, scs_ref, shs_ref,
                      o_ref, ot_ref, os_ref, ost_ref, h_ref):
    i = pl.program_id(0)
    j = pl.program_id(1)
    tm = x_ref.shape[0]
    host = h_ref.at[0:tm]
    riders = h_ref.at[tm:]

    @pl.when(j == 0)
    def _():
        _store_normed(host, x_ref, g_ref, sc_ref, sh_ref)

    @pl.when((i == 0) & (j == 0))
    def _():
        _store_normed(riders, xs_ref, g_ref, scs_ref, shs_ref)

    def project(weight_ref, out_ref, rider_out_ref):
        @pl.when(i == 0)
        def _():
            both = jnp.dot(h_ref[...], weight_ref[...].astype(BF16), preferred_element_type=F32)
            out_ref[...] = both[0:tm]
            rider_out_ref[...] = both[tm:]

        @pl.when(i > 0)
        def _():
            out_ref[...] = jnp.dot(host[...], weight_ref[...].astype(BF16), preferred_element_type=F32)

    pl.when(j < nj)(lambda: project(w_ref, o_ref, os_ref))
    pl.when(j == nj)(lambda: project(wt_ref, ot_ref, ost_ref))


def _norm_proj(x, xs, gain, mod, mod_s, sc_idx, sh_idx, w, cols, name, row_tile):
    m, d = x.shape
    ms = xs.shape[0]
    n = w.shape[1]
    tm = min(row_tile, m)
    nj = n // cols
    n_main = nj * cols
    tail = n - n_main
    whole = lambda j: 0
    main_col = lambda j: jnp.minimum(j, nj - 1)
    x_mode = dict(pipeline_mode=pl.Buffered(1)) if tm > ROW_TILE else {}
    return pl.pallas_call(
        functools.partial(_norm_proj_kernel, nj),
        grid=(m // tm, nj + 1),
        in_specs=[
            pl.BlockSpec((tm, d), lambda i, j: (i, 0), **x_mode),
            pl.BlockSpec((1, d), lambda i, j: (0, 0)),
            mod.spec(sc_idx, d, whole, tm),
            mod.spec(sh_idx, d, whole, tm),
            pl.BlockSpec((d, cols), lambda i, j: (0, main_col(j))),
            pl.BlockSpec((d, tail), lambda i, j: (0, n_main // tail)),
            pl.BlockSpec((ms, d), lambda i, j: (0, 0)),
            mod_s.spec(sc_idx, d, whole, ms),
            mod_s.spec(sh_idx, d, whole, ms),
        ],
        out_specs=[
            pl.BlockSpec((tm, cols), lambda i, j: (i, main_col(j))),
            pl.BlockSpec((tm, tail), lambda i, j: (i, 0)),
            pl.BlockSpec((ms, cols), lambda i, j: (0, _rider_col(i, main_col(j), nj))),
            pl.BlockSpec((ms, tail), lambda i, j: (0, 0)),
        ],
        out_shape=[jax.ShapeDtypeStruct((m, n_main), F32), jax.ShapeDtypeStruct((m, tail), F32),
                   jax.ShapeDtypeStruct((ms, n_main), F32), jax.ShapeDtypeStruct((ms, tail), F32)],
        scratch_shapes=[pltpu.VMEM((tm + ms, d), BF16)],
        compiler_params=_cparams(("arbitrary", "arbitrary")),
        name=name,
    )(x, gain.reshape(1, d), mod.arr, mod.arr, w, w, xs, mod_s.arr, mod_s.arr)


def _proj_resid_kernel(a_ref, w_ref, x_ref, g_ref, as_ref, xs_ref, gs_ref, o_ref, os_ref):
    @pl.when(pl.program_id(0) == 0)
    def _():
        acc = jnp.dot(as_ref[...], w_ref[...], preferred_element_type=F32)
        os_ref[...] = xs_ref[...] + gs_ref[...] * acc

    acc = jnp.dot(a_ref[...], w_ref[...], preferred_element_type=F32)
    o_ref[...] = x_ref[...] + g_ref[...] * acc


def _proj_resid(a, a_s, w, x, xs, mod, mod_s, g_idx, cols, name):
    m, n = x.shape
    ms = xs.shape[0]
    kdim = a.shape[1]
    tm = min(ROW_TILE, m)
    nj = n // cols
    rider_block = pl.BlockSpec((ms, cols), lambda i, j: (0, _rider_col(i, j, nj)))
    return pl.pallas_call(
        _proj_resid_kernel,
        grid=(m // tm, nj),
        in_specs=[
            pl.BlockSpec((tm, kdim), lambda i, j: (i, 0)),
            pl.BlockSpec((kdim, cols), lambda i, j: (0, j)),
            pl.BlockSpec((tm, cols), lambda i, j: (i, j)),
            mod.spec(g_idx, cols, lambda j: j, tm),
            pl.BlockSpec((ms, kdim), lambda i, j: (0, 0)),
            rider_block,
            mod_s.rider_spec(g_idx, cols, nj),
        ],
        out_specs=[pl.BlockSpec((tm, cols), lambda i, j: (i, j)), rider_block],
        out_shape=[jax.ShapeDtypeStruct((m, n), F32), jax.ShapeDtypeStruct((ms, n), F32)],
        compiler_params=_cparams(("arbitrary", "arbitrary")),
        name=name,
    )(a, w, x, mod.arr, a_s, xs, mod_s.arr)


def _out_proj_kernel(a1_ref, a2_ref, w_ref, x_ref, g_ref, gain_ref, sc_ref, sh_ref, o_ref, h_ref, wb_ref):
    @pl.when(pl.program_id(0) == 0)
    def _():
        step = MXU_DIM

        def cast_rows(r, carry):
            sl = pl.ds(pl.multiple_of(r * step, step), step)
            wb_ref[sl, :] = w_ref[sl, :].astype(BF16)
            return carry

        lax.fori_loop(0, w_ref.shape[0] // step, cast_rows, 0)

    k1 = a1_ref.shape[1]
    rows = x_ref.shape[0]
    step = min(MXU_DIM, rows)
    per_row = sc_ref.shape[0] == rows
    pieces = [slice(r0, r0 + step) for r0 in range(0, rows, step)]
    x1s = []
    for sl in pieces:
        acc = jnp.dot(a1_ref[sl, :], wb_ref[0:k1], preferred_element_type=F32) \
            + jnp.dot(a2_ref[sl, :], wb_ref[k1:], preferred_element_type=F32)
        g = g_ref[sl, :] if per_row else g_ref[...]
        x1 = x_ref[sl, :] + g * acc
        o_ref[sl, :] = x1
        x1s.append(x1)
    for sl, x1 in zip(pieces, x1s):
        ms = jnp.mean(x1 * x1, axis=-1, keepdims=True)
        xn = x1 * lax.rsqrt(ms + RMS_EPS) * gain_ref[...]
        sc = sc_ref[sl, :] if per_row else sc_ref[...]
        sh = sh_ref[sl, :] if per_row else sh_ref[...]
        h_ref[sl, :] = (xn * (1.0 + sc) + sh).astype(BF16)


def _out_proj(a1, a2, w, x, mod, gain, row_tile):
    m, d = x.shape
    tm = min(row_tile, m)
    whole = lambda j: 0
    row = lambda kdim: pl.BlockSpec((tm, kdim), lambda i, j: (i, 0))
    return pl.pallas_call(
        _out_proj_kernel,
        grid=(m // tm, 1),
        in_specs=[
            row(a1.shape[1]), row(a2.shape[1]),
            pl.BlockSpec(w.shape, lambda i, j: (0, 0), pipeline_mode=pl.Buffered(1)),
            row(d),
            mod.spec(2, d, whole, tm),
            pl.BlockSpec((1, d), lambda i, j: (0, 0)),
            mod.spec(4, d, whole, tm),
            mod.spec(3, d, whole, tm),
        ],
        out_specs=[row(d), row(d)],
        out_shape=[jax.ShapeDtypeStruct((m, d), F32), jax.ShapeDtypeStruct((m, d), BF16)],
        scratch_shapes=[pltpu.VMEM(w.shape, BF16)],
        compiler_params=_cparams(("arbitrary", "arbitrary")),
        name="out_proj",
    )(a1, a2, w, x, mod.arr, gain.reshape(1, d), mod.arr, mod.arr)


def _gelu(x):
    return 0.5 * x * (1.0 + lax.erf(x * (2.0 ** -0.5)))


def _ffn_up_kernel(tiles_per_batch, h_ref, wg_ref, wv_ref, hist_ref, cw_ref, cb_ref, hs_ref, wd_ref,
                   act_ref, last_ref, gs_ref, vs_ref, wdb_ref, carry_ref):
    i = pl.program_id(0)
    j = pl.program_id(1)

    @pl.when(i == 0)
    def _():
        hs = hs_ref[...]
        gs_ref[...] = jnp.dot(hs, wg_ref[...].astype(BF16), preferred_element_type=F32)
        vs_ref[...] = jnp.dot(hs, wv_ref[...].astype(BF16), preferred_element_type=F32)
        wdb_ref[...] = wd_ref[...].astype(BF16)

    @pl.when((i % tiles_per_batch) == 0)
    def _():
        carry_ref[j] = hist_ref[...]

    h = h_ref[...]
    gate = jnp.dot(h, wg_ref[...].astype(BF16), preferred_element_type=F32)
    val = jnp.dot(h, wv_ref[...].astype(BF16), preferred_element_type=F32)
    tm = gate.shape[0]
    prev = carry_ref[j]
    row = _iota(gate.shape, 0)
    g1 = pltpu.roll(gate, 1, 0)
    g2 = pltpu.roll(gate, 2, 0)
    g1 = jnp.where(row == 0, prev[1:2], g1)
    g2 = jnp.where(row == 0, prev[0:1], jnp.where(row == 1, prev[1:2], g2))
    cw = cw_ref[...]
    conv = cb_ref[...] + g2 * cw[0:1] + g1 * cw[1:2] + gate * cw[2:3]
    act_ref[...] = (_gelu(conv) * val).astype(BF16)
    tail = gate[tm - 2:tm]
    carry_ref[j] = tail
    last_ref[...] = tail


def _ffn_up_fused(h, hs, w_up, w_down, hist, conv_w, conv_b, rows_per_batch):
    m, d = h.shape
    ms = hs.shape[0]
    f = w_up.shape[1] // 2
    tm = min(ROW_TILE, rows_per_batch)
    cols = UP_COLS
    nj = f // cols
    tiles_per_batch = rows_per_batch // tm
    rider_block = pl.BlockSpec((ms, cols), lambda i, j: (0, _rider_col(i, j, nj)))
    wd_block = pl.BlockSpec((f // nj, w_down.shape[1]), lambda i, j: (_rider_col(i, j, nj), 0))
    act, tile_tails, gate_s, val_s, w_down_bf16 = pl.pallas_call(
        functools.partial(_ffn_up_kernel, tiles_per_batch),
        grid=(m // tm, nj),
        in_specs=[
            pl.BlockSpec((tm, d), lambda i, j: (i, 0)),
            pl.BlockSpec((d, cols), lambda i, j: (0, j)),
            pl.BlockSpec((d, cols), lambda i, j: (0, nj + j)),
            pl.BlockSpec((None, CONV_W - 1, cols), lambda i, j: (i // tiles_per_batch, 0, j)),
            pl.BlockSpec((CONV_W, cols), lambda i, j: (0, j)),
            pl.BlockSpec((1, cols), lambda i, j: (0, j)),
            pl.BlockSpec((ms, d), lambda i, j: (0, 0)),
            wd_block,
        ],
        out_specs=[
            pl.BlockSpec((tm, cols), lambda i, j: (i, j)),
            pl.BlockSpec((None, CONV_W - 1, cols), lambda i, j: (i, 0, j)),
            rider_block, rider_block, wd_block,
        ],
        out_shape=[
            jax.ShapeDtypeStruct((m, f), BF16),
            jax.ShapeDtypeStruct((m // tm, CONV_W - 1, f), F32),
            jax.ShapeDtypeStruct((ms, f), F32),
            jax.ShapeDtypeStruct((ms, f), F32),
            jax.ShapeDtypeStruct(w_down.shape, BF16),
        ],
        scratch_shapes=[pltpu.VMEM((nj, CONV_W - 1, cols), F32)],
        compiler_params=_cparams(("arbitrary", "arbitrary")),
        name="ffn_up_prompt",
    )(h, w_up, w_up, hist, conv_w, conv_b.reshape(1, f), hs, w_down)
    return act, tile_tails[tiles_per_batch - 1::tiles_per_batch], gate_s, val_s, w_down_bf16


def _act_sample_kernel(gate_ref, val_ref, hist_ref, cw_ref, cb_ref, act_ref):
    gate = gate_ref[...]
    hist = hist_ref[...]
    t = _iota(gate.shape, 1)
    g1 = jnp.where(t == 0, hist[:, 1:2], pltpu.roll(gate, 1, 1))
    g2 = jnp.where(t == 0, hist[:, 0:1], jnp.where(t == 1, hist[:, 1:2], pltpu.roll(gate, 2, 1)))
    cw = cw_ref[...]
    conv = cb_ref[...] + g2 * cw[0:1] + g1 * cw[1:2] + gate * cw[2:3]
    act_ref[...] = (_gelu(conv) * val_ref[...]).astype(BF16)


def _act_sample(gate, val, hist, conv_w, conv_b, nb, t):
    f = gate.shape[1]
    cols = UP_COLS
    nj = f // cols
    gate3 = gate.reshape(nb, t, f)
    val3 = val.reshape(nb, t, f)
    act = pl.pallas_call(
        _act_sample_kernel,
        grid=(nj,),
        in_specs=[
            pl.BlockSpec((nb, t, cols), lambda j: (0, 0, j)),
            pl.BlockSpec((nb, t, cols), lambda j: (0, 0, j)),
            pl.BlockSpec((nb, CONV_W - 1, cols), lambda j: (0, 0, j)),
            pl.BlockSpec((CONV_W, cols), lambda j: (0, j)),
            pl.BlockSpec((1, cols), lambda j: (0, j)),
        ],
        out_specs=pl.BlockSpec((nb, t, cols), lambda j: (0, 0, j)),
        out_shape=jax.ShapeDtypeStruct((nb, t, f), BF16),
        compiler_params=_cparams(("arbitrary",)),
        name="ffn_act_sample",
    )(gate3, val3, hist, conv_w, conv_b.reshape(1, f))
    return act.reshape(nb * t, f)


def _pair_rms(x, gain):
    x2 = x * x
    first = _iota(x.shape, 1) < HEAD_DIM
    s0 = jnp.sum(jnp.where(first, x2, 0.0), axis=-1, keepdims=True)
    s1 = jnp.sum(jnp.where(first, 0.0, x2), axis=-1, keepdims=True)
    ms = jnp.where(first, s0, s1) * (1.0 / HEAD_DIM)
    return x * lax.rsqrt(ms + RMS_EPS) * gain


def _bias_rows(table):
    h = table.shape[0]
    far = jnp.broadcast_to(table[:, 2 * REL_CLIP:], (h, ATT_REACH - REL_CLIP))
    mid = table[:, ::-1]
    near_len = BIAS_LEN - ATT_QROWS - (ATT_REACH - REL_CLIP) - (2 * REL_CLIP + 1)
    near = jnp.broadcast_to(table[:, 0:1], (h, near_len))
    wrap = jnp.broadcast_to(table[:, 2 * REL_CLIP:], (h, ATT_QROWS))
    return jnp.concatenate([far, mid, near, wrap], axis=1)


def _toeplitz(u_row, rows):
    return pltpu.roll(jnp.broadcast_to(u_row, (rows, BIAS_LEN)), 0, 1, stride=1, stride_axis=0)


def _attn_prompt_kernel(q_ref, k_ref, v_ref, qg_ref, kg_ref, u_ref, o_ref, kn_ref, vk_ref,
                        bias_ref, kwin_ref, vwin_ref):
    b = pl.program_id(1)
    qb = pl.program_id(2)
    shape = (ATT_QROWS, ATT_WIN)
    pairs = range(q_ref.shape[1] // PAIR)
    cols = [slice(p * PAIR, (p + 1) * PAIR) for p in pairs]
    chains = [(p, h) for p in pairs for h in range(2)]

    @pl.when((b == 0) & (qb == 0))
    def _():
        r = _iota(shape, 0)
        w = _iota(shape, 1)
        chunk_lo = _blk(r, CHUNK) * CHUNK
        in_band = (w >= chunk_lo) & (w < chunk_lo + (ATT_REACH + CHUNK))
        for i, (p, h) in enumerate(chains):
            bias = _toeplitz(u_ref[p, h:h + 1, :], ATT_QROWS)[:, :ATT_WIN]
            bias_ref[i] = jnp.where(in_band, bias * LOG2E, -jnp.inf)

    @pl.when(qb == 0)
    def _():
        kwin_ref[0:ATT_REACH] = jnp.zeros((ATT_REACH, kwin_ref.shape[1]), BF16)
        vwin_ref[0:ATT_REACH] = jnp.zeros((ATT_REACH, vwin_ref.shape[1]), BF16)

    @pl.when(qb > 0)
    def _():
        kwin_ref[0:ATT_REACH] = kwin_ref[ATT_QROWS:ATT_WIN]
        vwin_ref[0:ATT_REACH] = vwin_ref[ATT_QROWS:ATT_WIN]

    kn = [_pair_rms(k_ref[:, c], kg_ref[...]) for c in cols]
    for p in pairs:
        kn_ref[:, cols[p]] = kn[p]
        kwin_ref[ATT_REACH:ATT_WIN, cols[p]] = kn[p].astype(BF16)
    v_new = v_ref[...]
    vk_ref[...] = v_new
    vwin_ref[ATT_REACH:ATT_WIN] = v_new.astype(BF16)

    def attend(mask_start):
        qn = [_pair_rms(q_ref[:, c], qg_ref[...]) * (ATT_SCALE * LOG2E) for c in cols]
        kb = [kwin_ref[:, c] for c in cols]
        vb = [vwin_ref[:, c] for c in cols]
        first = _iota((ATT_QROWS, PAIR), 1) < HEAD_DIM
        first_w = _iota((ATT_WIN, PAIR), 1) < HEAD_DIM
        qh = [jnp.where(first, qn[p], 0.0) if h == 0 else jnp.where(first, 0.0, qn[p]) for p, h in chains]
        s = [_dot(qh[i], kb[p], NT) + bias_ref[i] for i, (p, h) in enumerate(chains)]
        if mask_start:
            started = _iota(shape, 1) >= ATT_REACH - qb * ATT_QROWS
            s = [jnp.where(started, x, -jnp.inf) for x in s]
        m = [jnp.max(x, axis=-1, keepdims=True) for x in s]
        pr = [jnp.exp2(x - mm) for x, mm in zip(s, m)]
        one = jnp.ones((), BF16)
        v_aug = [jnp.where(first_w, vb[p], one) if h == 0 else jnp.where(first_w, one, vb[p]) for p, h in chains]
        o = [_dot(pr[i], v_aug[i]) for i in range(len(chains))]
        o = [x / pltpu.roll(x, HEAD_DIM, 1) for x in o]
        for p in pairs:
            o_ref[:, cols[p]] = jnp.where(first, o[2 * p], o[2 * p + 1]).astype(BF16)

    full_window_from = ATT_REACH // ATT_QROWS
    pl.when(qb < full_window_from)(lambda: attend(True))
    pl.when(qb >= full_window_from)(lambda: attend(False))


def _attn_prompt(z3, q_gain, k_gain, u):
    nb, t, _ = z3.shape
    npairs = N_ATT_HEADS // 2
    npp = ATT_PAIRS
    width = npp * PAIR
    ngroups = npairs // npp
    nq = t // ATT_QROWS
    kcol = D_ATT // width
    vcol = 2 * D_ATT // width
    keep_blocks = ATT_REACH // ATT_QROWS
    blk = (None, ATT_QROWS, width)

    keep_spec = pl.BlockSpec(blk, lambda g, b, q: (b, jnp.maximum(q - (nq - keep_blocks), 0), g))
    att, kn, vk = pl.pallas_call(
        _attn_prompt_kernel,
        grid=(ngroups, nb, nq),
        in_specs=[
            pl.BlockSpec(blk, lambda g, b, q: (b, q, g)),
            pl.BlockSpec(blk, lambda g, b, q: (b, q, kcol + g)),
            pl.BlockSpec(blk, lambda g, b, q: (b, q, vcol + g)),
            pl.BlockSpec((1, PAIR), lambda g, b, q: (0, 0)),
            pl.BlockSpec((1, PAIR), lambda g, b, q: (0, 0)),
            pl.BlockSpec((npp, 2, BIAS_LEN), lambda g, b, q: (g, 0, 0)),
        ],
        out_specs=[
            pl.BlockSpec(blk, lambda g, b, q: (b, q, g)),
            keep_spec, keep_spec,
        ],
        out_shape=[
            jax.ShapeDtypeStruct((nb, t, D_ATT), BF16),
            jax.ShapeDtypeStruct((nb, ATT_REACH, D_ATT), F32),
            jax.ShapeDtypeStruct((nb, ATT_REACH, D_ATT), F32),
        ],
        scratch_shapes=[pltpu.VMEM((2 * npp, ATT_QROWS, ATT_WIN), F32),
                        pltpu.VMEM((ATT_WIN, width), BF16), pltpu.VMEM((ATT_WIN, width), BF16)],
        compiler_params=_cparams(("arbitrary", "arbitrary", "arbitrary")),
        name="attn_prompt",
    )(z3, z3, z3, jnp.tile(q_gain, 2).reshape(1, PAIR),
      jnp.tile(k_gain, 2).reshape(1, PAIR), u.reshape(npairs, 2, BIAS_LEN))
    return att, kn, vk


def _attn_sample_kernel(q_ref, k_ref, v_ref, kp_ref, vp_ref, qg_ref, kg_ref, u_ref, o_ref, kn_ref):
    t = q_ref.shape[0]
    reach = kp_ref.shape[0]
    first = _iota((t, PAIR), 1) < HEAD_DIM
    pairs = range(N_ATT_HEADS // 2)
    cols = [slice(p * PAIR, (p + 1) * PAIR) for p in pairs]
    chains = [(p, h) for p in pairs for h in range(2)]
    qn = [_pair_rms(q_ref[:, c], qg_ref[...]) * ATT_SCALE for c in cols]
    kn = [_pair_rms(k_ref[:, c], kg_ref[...]) for c in cols]
    for p in pairs:
        kn_ref[:, cols[p]] = kn[p]
    kpast = [kp_ref[:, c].astype(BF16) for c in cols]
    vpast = [vp_ref[:, c].astype(BF16) for c in cols]
    vnew = [v_ref[:, c].astype(BF16) for c in cols]
    qh = [jnp.where(first, qn[p], 0.0) if h == 0 else jnp.where(first, 0.0, qn[p]) for p, h in chains]
    bias = [_toeplitz(u_ref[p, h:h + 1, :], t) for p, h in chains]
    s_past = [_dot(qh[i], kpast[p], NT) + bias[i][:, :reach] for i, (p, h) in enumerate(chains)]
    s_new = [_dot(qh[i], kn[p], NT) + bias[i][:, reach:reach + t] for i, (p, h) in enumerate(chains)]
    m = [jnp.maximum(jnp.max(a, axis=-1, keepdims=True), jnp.max(b, axis=-1, keepdims=True))
         for a, b in zip(s_past, s_new)]
    p_past = [jnp.exp(a - mm) for a, mm in zip(s_past, m)]
    p_new = [jnp.exp(b - mm) for b, mm in zip(s_new, m)]
    l = [jnp.sum(a, axis=-1, keepdims=True) + jnp.sum(b, axis=-1, keepdims=True)
         for a, b in zip(p_past, p_new)]
    o = [(_dot(p_past[i], vpast[p]) + _dot(p_new[i], vnew[p])) / l[i] for i, (p, h) in enumerate(chains)]
    for p in pairs:
        o_ref[:, cols[p]] = jnp.where(first, o[2 * p], o[2 * p + 1]).astype(BF16)


def _attn_sample(z3, k_past, v_past, q_gain, k_gain, u):
    nb, t, _ = z3.shape
    reach = k_past.shape[1]
    npairs = N_ATT_HEADS // 2
    att, kn = pl.pallas_call(
        _attn_sample_kernel,
        grid=(nb,),
        in_specs=[
            pl.BlockSpec((None, t, D_ATT), lambda b: (b, 0, 0)),
            pl.BlockSpec((None, t, D_ATT), lambda b: (b, 0, 1)),
            pl.BlockSpec((None, t, D_ATT), lambda b: (b, 0, 2)),
            pl.BlockSpec((None, reach, D_ATT), lambda b: (b, 0, 0)),
            pl.BlockSpec((None, reach, D_ATT), lambda b: (b, 0, 0)),
            pl.BlockSpec((1, PAIR), lambda b: (0, 0)),
            pl.BlockSpec((1, PAIR), lambda b: (0, 0)),
            pl.BlockSpec((npairs, 2, BIAS_LEN), lambda b: (0, 0, 0)),
        ],
        out_specs=[
            pl.BlockSpec((None, t, D_ATT), lambda b: (b, 0, 0)),
            pl.BlockSpec((None, t, D_ATT), lambda b: (b, 0, 0)),
        ],
        out_shape=[
            jax.ShapeDtypeStruct((nb, t, D_ATT), BF16),
            jax.ShapeDtypeStruct((nb, t, D_ATT), F32),
        ],
        compiler_params=_cparams(("arbitrary",)),
        name="attn_sample",
    )(z3, z3, z3, k_past.reshape(nb, reach, D_ATT), v_past.reshape(nb, reach, D_ATT),
      jnp.tile(q_gain, 2).reshape(1, PAIR), jnp.tile(k_gain, 2).reshape(1, PAIR),
      u.reshape(npairs, 2, BIAS_LEN))
    return att, kn


def _tri_inverse(l_mats, c):
    n = l_mats[0].shape[0]
    nb = n // c
    row, col = _iota((n, n), 0), _iota((n, n), 1)
    eye = jnp.where(row == col, 1.0, 0.0).astype(F32)
    same_block = _blk(row, c) == _blk(col, c)

    def side_by_side(m):
        parts = [m[i * c:(i + 1) * c] for i in range(nb)]
        return functools.reduce(lambda x, y: x + y, parts)

    def block_diag(m):
        return m if nb == 1 else jnp.where(same_block, jnp.tile(m, (nb, 1)), 0.0)

    eye_blocks = side_by_side(eye)
    a_s = [side_by_side(eye + l).astype(BF16) for l in l_mats]
    t_s = [side_by_side(eye - l) for l in l_mats]
    t_bd = [eye - l for l in l_mats]
    for _ in range(c.bit_length() - 2):
        r_s = [eye_blocks - _dot(a, t) for a, t in zip(a_s, t_bd)]
        t_s = [t + _dot(t, block_diag(r)) for t, r in zip(t_s, r_s)]
        t_bd = [block_diag(t) for t in t_s]
    return t_bd


def _rwkv_kernel(c, r_ref, k_ref, v_ref, lo_ref, sr_ref, sk_ref, sv_ref, slo_ref, s0_ref,
                 mur_ref, muk_ref, muv_ref, mulo_ref, w0_ref, a0_ref, kkg_ref, ka_ref, rk_ref,
                 lnw_ref, lnb_ref, w2_ref, a2_ref, g2_ref,
                 o_ref, sT_ref, s_ref, cr_ref, ck_ref, cv_ref, clo_ref):
    tb = pl.program_id(2)
    rows, width = r_ref.shape
    npp = width // PAIR
    nchunks = rows // c
    h0 = _iota((rows, PAIR), 1) < HEAD_DIM
    bd = _blk(_iota((PAIR, PAIR), 0), HEAD_DIM) == _blk(_iota((PAIR, PAIR), 1), HEAD_DIM)

    @pl.when(tb == 0)
    def _():
        s_ref[...] = jnp.zeros(s_ref.shape, F32)
        for pp in range(npp):
            s_ref[pp, 0:HEAD_DIM, 0:HEAD_DIM] = s0_ref[2 * pp]
            s_ref[pp, HEAD_DIM:PAIR, HEAD_DIM:PAIR] = s0_ref[2 * pp + 1]
        cr_ref[...] = sr_ref[...]
        ck_ref[...] = sk_ref[...]
        cv_ref[...] = sv_ref[...]
        clo_ref[...] = slo_ref[...]

    def shifted(x_ref, carry_ref, mu_ref):
        x = x_ref[...]
        prev = jnp.where(_iota(x.shape, 0) == 0, carry_ref[...], pltpu.roll(x, 1, 0))
        carry_ref[...] = x[rows - 1:rows]
        return x + (prev - x) * mu_ref[...]

    r = shifted(r_ref, cr_ref, mur_ref)
    k = shifted(k_ref, ck_ref, muk_ref)
    v = shifted(v_ref, cv_ref, muv_ref)
    lo = shifted(lo_ref, clo_ref, mulo_ref)

    zeros_w = jnp.zeros((RANK_W, width), F32)
    w2p = jnp.concatenate([w2_ref[...], zeros_w], axis=0)
    a2p = jnp.concatenate([zeros_w, a2_ref[...]], axis=0)
    lo_wa = lo[:, 0:RANK_W + RANK_A]
    u = w0_ref[...] + _dot(jnp.tanh(lo_wa), w2p)
    lw = -DECAY_SCALE * _sigmoid(u)
    a = _sigmoid(a0_ref[...] + _dot(lo_wa, a2p))
    g = _dot(_sigmoid(lo[:, RANK_W + RANK_A:]), g2_ref[...])

    kk = k * kkg_ref[...]
    kk = kk * lax.rsqrt(jnp.maximum(_head_sums(kk * kk), KK_EPS * KK_EPS))
    k = k * (1.0 + (a - 1.0) * ka_ref[...])
    b = kk * a
    bonus = _head_sums(r * k * rk_ref[...]) * v

    tr = _iota((rows, rows), 0)
    tc = _iota((rows, rows), 1)
    same_chunk = _blk(tr, c) == _blk(tc, c)
    strict = same_chunk & (tr > tc)
    incl = same_chunk & (tr >= tc)
    lw_hi, lw_lo = _split2(lw)
    tril_ones = jnp.where(incl, 1.0, 0.0).astype(BF16)
    lp = jnp.dot(tril_ones, lw_hi, preferred_element_type=F32) + \
        jnp.dot(tril_ones, lw_lo, preferred_element_type=F32)
    decay_end = [jnp.exp(lp[(ci + 1) * c - 1:(ci + 1) * c]) for ci in range(nchunks)]

    alpha_w = kk * jnp.exp(lp - lw)
    inv_p = jnp.exp(-lp)
    beta_w = b * inv_p
    kappa_w = k * inv_p
    rho_w = r * jnp.exp(lp)
    to_end = [inv_p[ci * c:(ci + 1) * c] * decay_end[ci] for ci in range(nchunks)]
    to_end = to_end[0] if nchunks == 1 else jnp.concatenate(to_end, axis=0)
    beta_ew = b * to_end
    kappa_ew = k * to_end

    wide = (rows, nchunks * PAIR)
    col_chunk = _blk(_iota(wide, 1), PAIR) == _blk(_iota(wide, 0), c)
    spread = lambda m: jnp.where(col_chunk, jnp.tile(m, (1, nchunks)), 0.0)
    eye_p = _iota((PAIR, PAIR), 0) == _iota((PAIR, PAIR), 1)

    pairs = range(npp)
    lanes = [slice(pp * PAIR, (pp + 1) * PAIR) for pp in pairs]
    alpha = [alpha_w[:, l] for l in lanes]
    rho = [rho_w[:, l] for l in lanes]
    vv = [v[:, l] for l in lanes]
    head_mask = [h0, jnp.logical_not(h0)]
    beta = [beta_w[:, l] for l in lanes]
    kappa = [kappa_w[:, l] for l in lanes]
    a_b, a_k, r_b, r_k = [], [], [], []
    if rows == 2 * PAIR:
        halves = [slice(0, PAIR), slice(PAIR, rows)]
        swap = lambda m: jnp.concatenate([m[:, PAIR:], m[:, :PAIR]], axis=1)
        half_h0 = _iota((PAIR, PAIR), 1) < HEAD_DIM
        for pp in pairs:
            keys = [jnp.concatenate([beta[pp][halves[0]], kappa[pp][halves[0]]], axis=0).astype(BF16),
                    jnp.concatenate([kappa[pp][halves[1]], beta[pp][halves[1]]], axis=0).astype(BF16)]
            for hm in (half_h0, jnp.logical_not(half_h0)):
                out = [_dot(jnp.concatenate([jnp.where(hm, alpha[pp][sl], 0.0),
                                             jnp.where(hm, rho[pp][sl], 0.0)], axis=0), keys[g], NT)
                       for g, sl in enumerate(halves)]
                a_rows = jnp.concatenate([out[0][:PAIR], out[1][:PAIR]], axis=0)
                r_rows = jnp.concatenate([out[0][PAIR:], out[1][PAIR:]], axis=0)
                a_b.append(a_rows)
                a_k.append(swap(a_rows))
                r_b.append(r_rows)
                r_k.append(swap(r_rows))
    else:
        for pp in pairs:
            keys = jnp.concatenate([beta[pp], kappa[pp]], axis=0).astype(BF16)
            for hm in head_mask:
                out = _dot(jnp.concatenate([jnp.where(hm, alpha[pp], 0.0), jnp.where(hm, rho[pp], 0.0)], axis=0),
                           keys, NT)
                a_b.append(out[:rows, :rows])
                a_k.append(out[:rows, rows:])
                r_b.append(out[rows:, :rows])
                r_k.append(out[rows:, rows:])
    t_inv = _tri_inverse([jnp.where(strict, m, 0.0) for m in a_b], c)
    x = [[_dot(jnp.where(strict, a_k[2 * pp + h], 0.0), vv[pp]) for h in range(2)] for pp in pairs]
    ws = [[_dot(t_inv[2 * pp + h], jnp.concatenate([alpha[pp], x[pp][h]], axis=1)) for h in range(2)]
          for pp in pairs]
    w12 = [jnp.concatenate([jnp.where(h0, ws[pp][0][:, :PAIR], ws[pp][1][:, :PAIR]),
                            jnp.where(h0, ws[pp][0][:, PAIR:], ws[pp][1][:, PAIR:])], axis=1) for pp in pairs]
    q = [[_dot(jnp.where(incl, r_b[2 * pp + h], 0.0), w12[pp]) for h in range(2)] for pp in pairs]
    qk = [[_dot(jnp.where(incl, r_k[2 * pp + h], 0.0), vv[pp]) for h in range(2)] for pp in pairs]
    rp = [rho[pp] - jnp.where(h0, q[pp][0][:, :PAIR], q[pp][1][:, :PAIR]) for pp in pairs]
    y0 = [jnp.where(h0, qk[pp][0] - q[pp][0][:, PAIR:], qk[pp][1] - q[pp][1][:, PAIR:]) for pp in pairs]
    wtb = [_dot(w12[pp], spread(beta_ew[:, lanes[pp]]), TN) for pp in pairs]
    vtk = [_dot(vv[pp], spread(kappa_ew[:, lanes[pp]]), TN) for pp in pairs]

    s_cur = [s_ref[pp] for pp in pairs]
    ys = [[] for _ in pairs]
    for ci in range(nchunks):
        sl = slice(ci * c, (ci + 1) * c)
        cols = slice(ci * PAIR, (ci + 1) * PAIR)
        for pp in pairs:
            p_end = decay_end[ci][:, lanes[pp]]
            gmat = jnp.where(eye_p, jnp.broadcast_to(p_end, (PAIR, PAIR)), 0.0) \
                - jnp.where(bd, wtb[pp][:PAIR, cols], 0.0)
            hmat = jnp.where(bd, vtk[pp][:, cols] - wtb[pp][PAIR:, cols], 0.0)
            ys[pp].append(_dot(rp[pp][sl], s_cur[pp], NT) + y0[pp][sl])
            s_cur[pp] = _dot(s_cur[pp], gmat) + hmat
    for pp in pairs:
        s_ref[pp] = s_cur[pp]
    y_pairs = [ys[pp][0] if nchunks == 1 else jnp.concatenate(ys[pp], axis=0) for pp in pairs]

    @pl.when(tb == pl.num_programs(2) - 1)
    def _():
        for pp in range(npp):
            sT_ref[2 * pp] = s_ref[pp, 0:HEAD_DIM, 0:HEAD_DIM]
            sT_ref[2 * pp + 1] = s_ref[pp, HEAD_DIM:PAIR, HEAD_DIM:PAIR]

    y = y_pairs[0] if npp == 1 else jnp.concatenate(y_pairs, axis=1)
    mu = _head_sums(y) * (1.0 / HEAD_DIM)
    d = y - mu
    var = _head_sums(d * d) * (1.0 / HEAD_DIM)
    yn = d * lax.rsqrt(var + GN_EPS) * lnw_ref[...] + lnb_ref[...]
    o_ref[...] = ((yn + bonus) * g).astype(BF16)


def _rwkv(z3, zlo3, shift_prev, s0, p, rows, c, npp):
    nb, t, _ = z3.shape
    width = npp * PAIR
    ngroups = D_RWKV // width
    col0 = 3 * D_ATT // width
    sp = shift_prev.reshape(nb, 1, D_SHIFT)

    def zspec(off):
        return pl.BlockSpec((None, rows, width), lambda b, q, s: (b, s, col0 + off * ngroups + q))

    def sspec(off):
        return pl.BlockSpec((None, 1, width), lambda b, q, s: (b, 0, off * ngroups + q))

    def vec(off=0):
        return pl.BlockSpec((1, width), lambda b, q, s: (0, off * ngroups + q))

    def row2(x):
        return x.reshape(1, -1)

    out, s_fin = pl.pallas_call(
        functools.partial(_rwkv_kernel, c),
        grid=(nb, ngroups, t // rows),
        in_specs=[
            zspec(0), zspec(1), zspec(2),
            pl.BlockSpec((None, rows, D_LORA), lambda b, q, s: (b, s, 0)),
            sspec(0), sspec(1), sspec(2),
            pl.BlockSpec((None, 1, D_LORA), lambda b, q, s: (b, 0, 3 * D_RWKV // D_LORA)),
            pl.BlockSpec((None, 2 * npp, HEAD_DIM, HEAD_DIM), lambda b, q, s: (b, q, 0, 0)),
            vec(0), vec(1), vec(2),
            pl.BlockSpec((1, D_LORA), lambda b, q, s: (0, 3 * D_RWKV // D_LORA)),
            vec(), vec(), vec(), vec(), vec(), vec(), vec(),
            pl.BlockSpec((RANK_W, width), lambda b, q, s: (0, q)),
            pl.BlockSpec((RANK_A, width), lambda b, q, s: (0, q)),
            pl.BlockSpec((RANK_G, width), lambda b, q, s: (0, q)),
        ],
        out_specs=[
            pl.BlockSpec((None, rows, width), lambda b, q, s: (b, s, q)),
            pl.BlockSpec((None, 2 * npp, HEAD_DIM, HEAD_DIM), lambda b, q, s: (b, q, 0, 0)),
        ],
        out_shape=[
            jax.ShapeDtypeStruct((nb, t, D_RWKV), BF16),
            jax.ShapeDtypeStruct((nb, N_RWKV_HEADS, HEAD_DIM, HEAD_DIM), F32),
        ],
        scratch_shapes=[
            pltpu.VMEM((npp, PAIR, PAIR), F32),
            pltpu.VMEM((1, width), F32), pltpu.VMEM((1, width), F32), pltpu.VMEM((1, width), F32),
            pltpu.VMEM((1, D_LORA), F32),
        ],
        compiler_params=_cparams(("arbitrary", "arbitrary", "arbitrary")),
        name="rwkv7_mix",
    )(z3, z3, z3, zlo3, sp, sp, sp, sp, s0,
      row2(p['mu_shift']), row2(p['mu_shift']), row2(p['mu_shift']), row2(p['mu_shift']),
      row2(p['w0']), row2(p['a0']), row2(p['k_k']), row2(p['k_a']), row2(p['r_k']),
      row2(p['ln_x_w']), row2(p['ln_x_b']), p['w2'], p['a2'], p['g2'])
    return out, s_fin


def _layer(xp3, xs3, mod_p, mod_s, p, u, k_past, v_past, s0_p, s0_s, shift_p, shift_s, conv_p, conv_s):
    bp, tp, d = xp3.shape
    bs, ts, _ = xs3.shape
    mp, msr = bp * tp, bs * ts
    xp = xp3.reshape(mp, d)
    xs = xs3.reshape(msr, d)
    zp, zp_lo, zs, zs_lo = _norm_proj(xp, xs, p['norm_att_g'], mod_p, mod_s, 1, 0, p['w_in'], IN_COLS,
                                      "in_proj", min(IN_ROW_TILE, tp))
    assert zp.shape[1] == 3 * D_ATT + 3 * D_RWKV and zp_lo.shape[1] == D_LORA
    zp3, zp_lo3 = zp.reshape(bp, tp, -1), zp_lo.reshape(bp, tp, -1)
    zs3, zs_lo3 = zs.reshape(bs, ts, -1), zs_lo.reshape(bs, ts, -1)
    att_p, k_keep_p, v_keep_p = _attn_prompt(zp3, p['q_norm_g'], p['k_norm_g'], u)
    rw_p, s_fin_p = _rwkv(zp3, zp_lo3, shift_p, s0_p, p, RWKV_ROWS, CHUNK, RWKV_PAIRS_PROMPT)
    att_s, k_keep_s = _attn_sample(zs3, k_past, v_past, p['q_norm_g'], p['k_norm_g'], u)
    v_keep_s = zs3[:, :, 2 * D_ATT:3 * D_ATT]
    rw_s, s_fin_s = _rwkv(zs3, zs_lo3, shift_s, s0_s, p, ts, ts, RWKV_PAIRS_SAMPLE)
    shift_last_p = jnp.concatenate([zp3[:, tp - 1, 3 * D_ATT:], zp_lo3[:, tp - 1]], axis=-1)
    shift_last_s = jnp.concatenate([zs3[:, ts - 1, 3 * D_ATT:], zs_lo3[:, ts - 1]], axis=-1)
    x1p, h2p = _out_proj(att_p.reshape(mp, D_ATT), rw_p.reshape(mp, D_RWKV), p['w_out'], xp, mod_p,
                         p['norm_ffn_g'], OUT_ROW_TILE)
    x1s, h2s = _out_proj(att_s.reshape(msr, D_ATT), rw_s.reshape(msr, D_RWKV), p['w_out'], xs, mod_s,
                         p['norm_ffn_g'], OUT_ROW_TILE)
    act_p, conv_last_p, gate_s, val_s, w_down = _ffn_up_fused(h2p, h2s, p['w_up'], p['w_down'], conv_p,
                                                              p['dw_conv'], p['dw_bias'], tp)
    act_s = _act_sample(gate_s, val_s, conv_s, p['dw_conv'], p['dw_bias'], bs, ts)
    conv_last_s = gate_s.reshape(bs, ts, -1)[:, ts - (CONV_W - 1):]
    x2p, x2s = _proj_resid(act_p, act_s, w_down, x1p, x1s, mod_p, mod_s, 5, DOWN_COLS, "ffn_down")
    heads = lambda a: a.reshape(a.shape[0], a.shape[1], N_ATT_HEADS, HEAD_DIM)
    out_p = (x2p.reshape(bp, tp, d), heads(k_keep_p), heads(v_keep_p), s_fin_p, shift_last_p, conv_last_p)
    out_s = (x2s.reshape(bs, ts, d), heads(k_keep_s), heads(v_keep_s), s_fin_s, shift_last_s, conv_last_s)
    return out_p, out_s


def kernel(x_prompt, x_sample, c_prompt, c_sample, cache_att_k, cache_att_v, state_rwkv, state_shift, state_ffn_conv, norm_att_g, norm_ffn_g, w_ada, b_ada, w_in, q_norm_g, k_norm_g, rel_bias, mu_shift, w0, w2, a0, a2, g2, k_k, k_a, r_k, ln_x_w, ln_x_b, w_out, w_up, dw_conv, dw_bias, w_down):
    depth = w_in.shape[0]
    bp, tp, d = x_prompt.shape
    bs, ts, _ = x_sample.shape
    d_ff = w_down.shape[1]
    hp, hs = x_prompt, x_sample
    outs_p = [[] for _ in range(5)]
    outs_s = [[] for _ in range(5)]
    for l in range(depth):
        p = dict(norm_att_g=norm_att_g[l], norm_ffn_g=norm_ffn_g[l], w_in=w_in[l], q_norm_g=q_norm_g[l],
                 k_norm_g=k_norm_g[l], mu_shift=mu_shift[l], w0=w0[l], w2=w2[l], a0=a0[l], a2=a2[l],
                 g2=g2[l], k_k=k_k[l], k_a=k_a[l], r_k=r_k[l], ln_x_w=ln_x_w[l], ln_x_b=ln_x_b[l],
                 w_out=w_out[l], w_up=w_up[l], dw_conv=dw_conv[l], dw_bias=dw_bias[l], w_down=w_down[l])
        n_c = bp + bs
        pad = (-n_c) % 8
        c_all = jnp.concatenate([c_prompt, c_sample, jnp.zeros((pad, d), F32)], axis=0)
        mod = _ada(c_all, w_ada[l], b_ada[l])
        mod_p = _Mod(mod.reshape(n_c + pad, 6, 1, d), False, rows_per_batch=tp)
        mod_s = _Mod(jnp.repeat(mod[bp:bp + bs], ts, axis=0), True)
        u = _bias_rows(rel_bias[l])

        res_p, res_s = _layer(hp, hs, mod_p, mod_s, p, u, cache_att_k[l], cache_att_v[l],
                              jnp.zeros((bp, N_RWKV_HEADS, HEAD_DIM, HEAD_DIM), F32), state_rwkv[l],
                              jnp.zeros((bp, D_SHIFT), F32), state_shift[l],
                              jnp.zeros((bp, CONV_W - 1, d_ff), F32), state_ffn_conv[l])
        hp, hs = res_p[0], res_s[0]
        for lst, val in zip(outs_p, res_p[1:]):
            lst.append(val)
        for lst, val in zip(outs_s, res_s[1:]):
            lst.append(val)
    st = lambda lst: jnp.stack(lst)
    return (hp, hs, *[st(x) for x in outs_p], *[st(x) for x in outs_s])
```

```python
import functools

import jax
import jax.numpy as jnp
from jax import lax
from jax.experimental import pallas as pl
from jax.experimental.pallas import tpu as pltpu

F32 = jnp.float32
BF16 = jnp.bfloat16

CHUNK = 64
N_PREV_CHUNKS = 8
ATT_REACH = N_PREV_CHUNKS * CHUNK
HEAD_DIM = 64
N_ATT_HEADS = 16
N_RWKV_HEADS = 16
D_ATT = N_ATT_HEADS * HEAD_DIM
D_RWKV = N_RWKV_HEADS * HEAD_DIM
REL_CLIP = 128
RANK_W = 64
RANK_A = 64
RANK_G = 128
D_LORA = RANK_W + RANK_A + RANK_G
D_SHIFT = 3 * D_RWKV + D_LORA
D_IN = 3 * D_ATT + D_SHIFT
CONV_W = 3
RMS_EPS = 1e-6
GN_EPS = 64e-5
KK_EPS = 1e-12
ATT_SCALE = HEAD_DIM ** -0.5
LOG2E = 1.4426950408889634
DECAY_SCALE = 0.6065306597126334

PAIR = 2 * HEAD_DIM
MXU_DIM = 256
VMEM_LIMIT = 60 * 1024 * 1024

ROW_TILE = 1024
IN_ROW_TILE = 2048
ADA_COLS = 1024
IN_COLS = 512
OUT_ROW_TILE = 512
UP_COLS = 512
DOWN_COLS = 512
ATT_QROWS = 256
ATT_WIN = ATT_QROWS + ATT_REACH
ATT_PAIRS = 4
BIAS_LEN = 1024
RWKV_ROWS = 256
RWKV_PAIRS_PROMPT = 8
RWKV_PAIRS_SAMPLE = 8


def _cparams(sem):
    return pltpu.CompilerParams(dimension_semantics=sem, vmem_limit_bytes=VMEM_LIMIT)


def _dot(a, b, dims=(((1,), (0,)), ((), ()))):
    return lax.dot_general(a.astype(BF16), b.astype(BF16), dims, preferred_element_type=F32)


def _split2(x):
    hi = x.astype(BF16)
    lo = (x - hi.astype(F32)).astype(BF16)
    return hi, lo


NT = (((1,), (1,)), ((), ()))
TN = (((0,), (0,)), ((), ()))


def _iota(shape, dim):
    return lax.broadcasted_iota(jnp.int32, shape, dim)


def _blk(x, size):
    return jnp.right_shift(x, size.bit_length() - 1)


def _head_ones(n):
    r = _blk(_iota((n, n), 0), HEAD_DIM)
    c = _blk(_iota((n, n), 1), HEAD_DIM)
    return jnp.where(r == c, 1.0, 0.0).astype(BF16)


def _head_sums(x):
    lanes = x.shape[1]
    group = min(lanes, MXU_DIM)
    ones = _head_ones(group)
    parts = [_dot(x[:, i:i + group], ones) for i in range(0, lanes, group)]
    return parts[0] if len(parts) == 1 else jnp.concatenate(parts, axis=1)


def _sigmoid(x):
    return 1.0 / (1.0 + jnp.exp(-x))


def _ada_kernel(c_ref, w_ref, b_ref, o_ref):
    c = c_ref[...]
    s = c * _sigmoid(c)
    o_ref[...] = _dot(s, w_ref[...]) + b_ref[...]


def _ada(c_all, w_ada, b_ada):
    rows, d = c_all.shape
    n = w_ada.shape[1]
    return pl.pallas_call(
        _ada_kernel,
        grid=(n // ADA_COLS,),
        in_specs=[
            pl.BlockSpec((rows, d), lambda j: (0, 0)),
            pl.BlockSpec((d, ADA_COLS), lambda j: (0, j)),
            pl.BlockSpec((1, ADA_COLS), lambda j: (0, j)),
        ],
        out_specs=pl.BlockSpec((rows, ADA_COLS), lambda j: (0, j)),
        out_shape=jax.ShapeDtypeStruct((rows, n), F32),
        compiler_params=_cparams(("arbitrary",)),
        name="ada_mod",
    )(c_all, w_ada, b_ada.reshape(1, n))


class _Mod:
    def __init__(self, arr, per_row, rows_per_batch=None):
        self.arr = arr
        self.per_row = per_row
        self.rows_per_batch = rows_per_batch

    def spec(self, idx, cols, col_of, row_tile):
        if self.per_row:
            m = self.arr.shape[0]
            d = self.arr.shape[1] // 6
            nblk = d // cols
            return pl.BlockSpec((m, cols), lambda i, j: (0, idx * nblk + col_of(j)))
        tiles_per_batch = self.rows_per_batch // row_tile
        return pl.BlockSpec((None, None, 1, cols),
                            lambda i, j: (i // tiles_per_batch, idx, 0, col_of(j)))

    def rider_spec(self, idx, cols, nj):
        m = self.arr.shape[0]
        nblk = self.arr.shape[1] // 6 // cols
        return pl.BlockSpec((m, cols), lambda i, j: (0, idx * nblk + _rider_col(i, j, nj)))


def _rider_col(i, j, nj):
    return jnp.where(i == 0, j, nj - 1)


NORM_ROWS = 128


def _store_normed(h_ref, x_ref, g_ref, sc_ref, sh_ref):
    rows = x_ref.shape[0]
    step = min(NORM_ROWS, rows)
    per_row = sc_ref.shape[0] == rows

    def body(r, carry):
        sl = pl.ds(pl.multiple_of(r * step, step), step)
        x = x_ref[sl, :]
        ms = jnp.mean(x * x, axis=-1, keepdims=True)
        xn = x * lax.rsqrt(ms + RMS_EPS) * g_ref[...]
        sc = sc_ref[sl, :] if per_row else sc_ref[...]
        sh = sh_ref[sl, :] if per_row else sh_ref[...]
        h_ref[sl, :] = (xn * (1.0 + sc) + sh).astype(BF16)
        return carry

    lax.fori_loop(0, rows // step, body, 0)


def _norm_proj_kernel(nj, x_ref, g_ref, sc_ref, sh_ref, w_ref, wt_ref, xs_ref, scs_ref, shs_ref,
                      o_ref, ot_ref, os_ref, ost_ref, h_ref):
    i = pl.program_id(0)
    j = pl.program_id(1)
    tm = x_ref.shape[0]
    host = h_ref.at[0:tm]
    riders = h_ref.at[tm:]

    @pl.when(j == 0)
    def _():
        _store_normed(host, x_ref, g_ref, sc_ref, sh_ref)

    @pl.when((i == 0) & (j == 0))
    def _():
        _store_normed(riders, xs_ref, g_ref, scs_ref, shs_ref)

    def project(weight_ref, out_ref, rider_out_ref):
        @pl.when(i == 0)
        def _():
            both = jnp.dot(h_ref[...], weight_ref[...].astype(BF16), preferred_element_type=F32)
            out_ref[...] = both[0:tm]
            rider_out_ref[...] = both[tm:]

        @pl.when(i > 0)
        def _():
            out_ref[...] = jnp.dot(host[...], weight_ref[...].astype(BF16), preferred_element_type=F32)

    pl.when(j < nj)(lambda: project(w_ref, o_ref, os_ref))
    pl.when(j == nj)(lambda: project(wt_ref, ot_ref, ost_ref))


def _norm_proj(x, xs, gain, mod, mod_s, sc_idx, sh_idx, w, cols, name, row_tile):
    m, d = x.shape
    ms = xs.shape[0]
    n = w.shape[1]
    tm = min(row_tile, m)
    nj = n // cols
    n_main = nj * cols
    tail = n - n_main
    whole = lambda j: 0
    main_col = lambda j: jnp.minimum(j, nj - 1)
    x_mode = dict(pipeline_mode=pl.Buffered(1)) if tm > ROW_TILE else {}
    return pl.pallas_call(
        functools.partial(_norm_proj_kernel, nj),
        grid=(m // tm, nj + 1),
        in_specs=[
            pl.BlockSpec((tm, d), lambda i, j: (i, 0), **x_mode),
            pl.BlockSpec((1, d), lambda i, j: (0, 0)),
            mod.spec(sc_idx, d, whole, tm),
            mod.spec(sh_idx, d, whole, tm),
            pl.BlockSpec((d, cols), lambda i, j: (0, main_col(j))),
            pl.BlockSpec((d, tail), lambda i, j: (0, n_main // tail)),
            pl.BlockSpec((ms, d), lambda i, j: (0, 0)),
            mod_s.spec(sc_idx, d, whole, ms),
            mod_s.spec(sh_idx, d, whole, ms),
        ],
        out_specs=[
            pl.BlockSpec((tm, cols), lambda i, j: (i, main_col(j))),
            pl.BlockSpec((tm, tail), lambda i, j: (i, 0)),
            pl.BlockSpec((ms, cols), lambda i, j: (0, _rider_col(i, main_col(j), nj))),
            pl.BlockSpec((ms, tail), lambda i, j: (0, 0)),
        ],
        out_shape=[jax.ShapeDtypeStruct((m, n_main), F32), jax.ShapeDtypeStruct((m, tail), F32),
                   jax.ShapeDtypeStruct((ms, n_main), F32), jax.ShapeDtypeStruct((ms, tail), F32)],
        scratch_shapes=[pltpu.VMEM((tm + ms, d), BF16)],
        compiler_params=_cparams(("arbitrary", "arbitrary")),
        name=name,
    )(x, gain.reshape(1, d), mod.arr, mod.arr, w, w, xs, mod_s.arr, mod_s.arr)


W_SLOTS = 3


def _proj_resid_kernel(nj, a_ref, w_hbm, x_ref, g_ref, as_ref, xs_ref, gs_ref, o_ref, os_ref, wbuf, sem):
    nsteps = pl.num_programs(0) * nj
    s = pl.program_id(0) * nj + pl.program_id(1)
    cols = wbuf.shape[2]

    def w_copy(t):
        col = (t % nj) * cols
        if not isinstance(col, int):
            col = pl.multiple_of(col, cols)
        slot = t % W_SLOTS
        return pltpu.make_async_copy(w_hbm.at[:, pl.ds(col, cols)], wbuf.at[slot], sem.at[slot])

    @pl.when(s == 0)
    def _():
        for t in range(W_SLOTS - 1):
            w_copy(t).start()

    @pl.when(s + (W_SLOTS - 1) < nsteps)
    def _():
        w_copy(s + (W_SLOTS - 1)).start()

    w_copy(s).wait()
    slot = s % W_SLOTS

    @pl.when(pl.program_id(0) == 0)
    def _():
        acc = jnp.dot(as_ref[...], wbuf[slot], preferred_element_type=F32)
        os_ref[...] = xs_ref[...] + gs_ref[...] * acc

    acc = jnp.dot(a_ref[...], wbuf[slot], preferred_element_type=F32)
    o_ref[...] = x_ref[...] + g_ref[...] * acc


def _proj_resid(a, a_s, w, x, xs, mod, mod_s, g_idx, cols, name):
    m, n = x.shape
    ms = xs.shape[0]
    kdim = a.shape[1]
    tm = min(ROW_TILE, m)
    nj = n // cols
    assert (m // tm) * nj >= W_SLOTS - 1 and w.dtype == BF16
    rider_block = pl.BlockSpec((ms, cols), lambda i, j: (0, _rider_col(i, j, nj)))
    return pl.pallas_call(
        functools.partial(_proj_resid_kernel, nj),
        grid=(m // tm, nj),
        in_specs=[
            pl.BlockSpec((tm, kdim), lambda i, j: (i, 0)),
            pl.BlockSpec(memory_space=pl.ANY),
            pl.BlockSpec((tm, cols), lambda i, j: (i, j)),
            mod.spec(g_idx, cols, lambda j: j, tm),
            pl.BlockSpec((ms, kdim), lambda i, j: (0, 0)),
            rider_block,
            mod_s.rider_spec(g_idx, cols, nj),
        ],
        out_specs=[pl.BlockSpec((tm, cols), lambda i, j: (i, j)), rider_block],
        out_shape=[jax.ShapeDtypeStruct((m, n), F32), jax.ShapeDtypeStruct((ms, n), F32)],
        scratch_shapes=[pltpu.VMEM((W_SLOTS, kdim, cols), BF16), pltpu.SemaphoreType.DMA((W_SLOTS,))],
        compiler_params=_cparams(("arbitrary", "arbitrary")),
        name=name,
    )(a, w, x, mod.arr, a_s, xs, mod_s.arr)


def _out_proj_kernel(a1_ref, a2_ref, w_ref, x_ref, g_ref, gain_ref, sc_ref, sh_ref, o_ref, h_ref, wb_ref):
    @pl.when(pl.program_id(0) == 0)
    def _():
        step = MXU_DIM

        def cast_rows(r, carry):
            sl = pl.ds(pl.multiple_of(r * step, step), step)
            wb_ref[sl, :] = w_ref[sl, :].astype(BF16)
            return carry

        lax.fori_loop(0, w_ref.shape[0] // step, cast_rows, 0)

    k1 = a1_ref.shape[1]
    rows = x_ref.shape[0]
    step = min(MXU_DIM, rows)
    per_row = sc_ref.shape[0] == rows
    pieces = [slice(r0, r0 + step) for r0 in range(0, rows, step)]
    x1s = []
    for sl in pieces:
        acc = jnp.dot(a1_ref[sl, :], wb_ref[0:k1], preferred_element_type=F32) \
            + jnp.dot(a2_ref[sl, :], wb_ref[k1:], preferred_element_type=F32)
        g = g_ref[sl, :] if per_row else g_ref[...]
        x1 = x_ref[sl, :] + g * acc
        o_ref[sl, :] = x1
        x1s.append(x1)
    for sl, x1 in zip(pieces, x1s):
        ms = jnp.mean(x1 * x1, axis=-1, keepdims=True)
        xn = x1 * lax.rsqrt(ms + RMS_EPS) * gain_ref[...]
        sc = sc_ref[sl, :] if per_row else sc_ref[...]
        sh = sh_ref[sl, :] if per_row else sh_ref[...]
        h_ref[sl, :] = (xn * (1.0 + sc) + sh).astype(BF16)


def _out_proj(a1, a2, w, x, mod, gain, row_tile):
    m, d = x.shape
    tm = min(row_tile, m)
    whole = lambda j: 0
    row = lambda kdim: pl.BlockSpec((tm, kdim), lambda i, j: (i, 0))
    return pl.pallas_call(
        _out_proj_kernel,
        grid=(m // tm, 1),
        in_specs=[
            row(a1.shape[1]), row(a2.shape[1]),
            pl.BlockSpec(w.shape, lambda i, j: (0, 0), pipeline_mode=pl.Buffered(1)),
            row(d),
            mod.spec(2, d, whole, tm),
            pl.BlockSpec((1, d), lambda i, j: (0, 0)),
            mod.spec(4, d, whole, tm),
            mod.spec(3, d, whole, tm),
        ],
        out_specs=[row(d), row(d)],
        out_shape=[jax.ShapeDtypeStruct((m, d), F32), jax.ShapeDtypeStruct((m, d), BF16)],
        scratch_shapes=[pltpu.VMEM(w.shape, BF16)],
        compiler_params=_cparams(("arbitrary", "arbitrary")),
        name="out_proj",
    )(a1, a2, w, x, mod.arr, gain.reshape(1, d), mod.arr, mod.arr)


def _gelu(x):
    return 0.5 * x * (1.0 + lax.erf(x * (2.0 ** -0.5)))


def _ffn_up_kernel(tiles_per_batch, h_ref, wg_ref, wv_ref, hist_ref, cw_ref, cb_ref, hs_ref, wd_ref,
                   act_ref, last_ref, gs_ref, vs_ref, wdb_ref, carry_ref):
    i = pl.program_id(0)
    j = pl.program_id(1)

    @pl.when(i == 0)
    def _():
        hs = hs_ref[...]
        gs_ref[...] = jnp.dot(hs, wg_ref[...].astype(BF16), preferred_element_type=F32)
        vs_ref[...] = jnp.dot(hs, wv_ref[...].astype(BF16), preferred_element_type=F32)
        wdb_ref[...] = wd_ref[...].astype(BF16)

    @pl.when((i % tiles_per_batch) == 0)
    def _():
        carry_ref[j] = hist_ref[...]

    h = h_ref[...]
    gate = jnp.dot(h, wg_ref[...].astype(BF16), preferred_element_type=F32)
    val = jnp.dot(h, wv_ref[...].astype(BF16), preferred_element_type=F32)
    tm = gate.shape[0]
    prev = carry_ref[j]
    row = _iota(gate.shape, 0)
    g1 = pltpu.roll(gate, 1, 0)
    g2 = pltpu.roll(gate, 2, 0)
    g1 = jnp.where(row == 0, prev[1:2], g1)
    g2 = jnp.where(row == 0, prev[0:1], jnp.where(row == 1, prev[1:2], g2))
    cw = cw_ref[...]
    conv = cb_ref[...] + g2 * cw[0:1] + g1 * cw[1:2] + gate * cw[2:3]
    act_ref[...] = (_gelu(conv) * val).astype(BF16)
    tail = gate[tm - 2:tm]
    carry_ref[j] = tail
    last_ref[...] = tail


def _ffn_up_fused(h, hs, w_up, w_down, hist, conv_w, conv_b, rows_per_batch):
    m, d = h.shape
    ms = hs.shape[0]
    f = w_up.shape[1] // 2
    tm = min(ROW_TILE, rows_per_batch)
    cols = UP_COLS
    nj = f // cols
    tiles_per_batch = rows_per_batch // tm
    rider_block = pl.BlockSpec((ms, cols), lambda i, j: (0, _rider_col(i, j, nj)))
    wd_block = pl.BlockSpec((f // nj, w_down.shape[1]), lambda i, j: (_rider_col(i, j, nj), 0))
    act, tile_tails, gate_s, val_s, w_down_bf16 = pl.pallas_call(
        functools.partial(_ffn_up_kernel, tiles_per_batch),
        grid=(m // tm, nj),
        in_specs=[
            pl.BlockSpec((tm, d), lambda i, j: (i, 0)),
            pl.BlockSpec((d, cols), lambda i, j: (0, j)),
            pl.BlockSpec((d, cols), lambda i, j: (0, nj + j)),
            pl.BlockSpec((None, CONV_W - 1, cols), lambda i, j: (i // tiles_per_batch, 0, j)),
            pl.BlockSpec((CONV_W, cols), lambda i, j: (0, j)),
            pl.BlockSpec((1, cols), lambda i, j: (0, j)),
            pl.BlockSpec((ms, d), lambda i, j: (0, 0)),
            wd_block,
        ],
        out_specs=[
            pl.BlockSpec((tm, cols), lambda i, j: (i, j)),
            pl.BlockSpec((None, CONV_W - 1, cols), lambda i, j: (i, 0, j)),
            rider_block, rider_block, wd_block,
        ],
        out_shape=[
            jax.ShapeDtypeStruct((m, f), BF16),
            jax.ShapeDtypeStruct((m // tm, CONV_W - 1, f), F32),
            jax.ShapeDtypeStruct((ms, f), F32),
            jax.ShapeDtypeStruct((ms, f), F32),
            jax.ShapeDtypeStruct(w_down.shape, BF16),
        ],
        scratch_shapes=[pltpu.VMEM((nj, CONV_W - 1, cols), F32)],
        compiler_params=_cparams(("arbitrary", "arbitrary")),
        name="ffn_up_prompt",
    )(h, w_up, w_up, hist, conv_w, conv_b.reshape(1, f), hs, w_down)
    return act, tile_tails[tiles_per_batch - 1::tiles_per_batch], gate_s, val_s, w_down_bf16


def _act_sample_kernel(gate_ref, val_ref, hist_ref, cw_ref, cb_ref, act_ref):
    gate = gate_ref[...]
    hist = hist_ref[...]
    t = _iota(gate.shape, 1)
    g1 = jnp.where(t == 0, hist[:, 1:2], pltpu.roll(gate, 1, 1))
    g2 = jnp.where(t == 0, hist[:, 0:1], jnp.where(t == 1, hist[:, 1:2], pltpu.roll(gate, 2, 1)))
    cw = cw_ref[...]
    conv = cb_ref[...] + g2 * cw[0:1] + g1 * cw[1:2] + gate * cw[2:3]
    act_ref[...] = (_gelu(conv) * val_ref[...]).astype(BF16)


def _act_sample(gate, val, hist, conv_w, conv_b, nb, t):
    f = gate.shape[1]
    cols = UP_COLS
    nj = f // cols
    gate3 = gate.reshape(nb, t, f)
    val3 = val.reshape(nb, t, f)
    act = pl.pallas_call(
        _act_sample_kernel,
        grid=(nj,),
        in_specs=[
            pl.BlockSpec((nb, t, cols), lambda j: (0, 0, j)),
            pl.BlockSpec((nb, t, cols), lambda j: (0, 0, j)),
            pl.BlockSpec((nb, CONV_W - 1, cols), lambda j: (0, 0, j)),
            pl.BlockSpec((CONV_W, cols), lambda j: (0, j)),
            pl.BlockSpec((1, cols), lambda j: (0, j)),
        ],
        out_specs=pl.BlockSpec((nb, t, cols), lambda j: (0, 0, j)),
        out_shape=jax.ShapeDtypeStruct((nb, t, f), BF16),
        compiler_params=_cparams(("arbitrary",)),
        name="ffn_act_sample",
    )(gate3, val3, hist, conv_w, conv_b.reshape(1, f))
    return act.reshape(nb * t, f)


def _pair_rms(x, gain):
    x2 = x * x
    first = _iota(x.shape, 1) < HEAD_DIM
    s0 = jnp.sum(jnp.where(first, x2, 0.0), axis=-1, keepdims=True)
    s1 = jnp.sum(jnp.where(first, 0.0, x2), axis=-1, keepdims=True)
    ms = jnp.where(first, s0, s1) * (1.0 / HEAD_DIM)
    return x * lax.rsqrt(ms + RMS_EPS) * gain


def _bias_rows(table):
    h = table.shape[0]
    far = jnp.broadcast_to(table[:, 2 * REL_CLIP:], (h, ATT_REACH - REL_CLIP))
    mid = table[:, ::-1]
    near_len = BIAS_LEN - ATT_QROWS - (ATT_REACH - REL_CLIP) - (2 * REL_CLIP + 1)
    near = jnp.broadcast_to(table[:, 0:1], (h, near_len))
    wrap = jnp.broadcast_to(table[:, 2 * REL_CLIP:], (h, ATT_QROWS))
    return jnp.concatenate([far, mid, near, wrap], axis=1)


def _toeplitz(u_row, rows):
    return pltpu.roll(jnp.broadcast_to(u_row, (rows, BIAS_LEN)), 0, 1, stride=1, stride_axis=0)


def _attn_prompt_kernel(q_ref, k_ref, v_ref, qg_ref, kg_ref, u_ref, o_ref, kn_ref, vk_ref,
                        bias_ref, kwin_ref, vwin_ref):
    b = pl.program_id(1)
    qb = pl.program_id(2)
    shape = (ATT_QROWS, ATT_WIN)
    pairs = range(q_ref.shape[1] // PAIR)
    cols = [slice(p * PAIR, (p + 1) * PAIR) for p in pairs]
    chains = [(p, h) for p in pairs for h in range(2)]

    @pl.when((b == 0) & (qb == 0))
    def _():
        r = _iota(shape, 0)
        w = _iota(shape, 1)
        chunk_lo = _blk(r, CHUNK) * CHUNK
        in_band = (w >= chunk_lo) & (w < chunk_lo + (ATT_REACH + CHUNK))
        for i, (p, h) in enumerate(chains):
            bias = _toeplitz(u_ref[p, h:h + 1, :], ATT_QROWS)[:, :ATT_WIN]
            bias_ref[i] = jnp.where(in_band, bias * LOG2E, -jnp.inf)

    @pl.when(qb == 0)
    def _():
        kwin_ref[0:ATT_REACH] = jnp.zeros((ATT_REACH, kwin_ref.shape[1]), BF16)
        vwin_ref[0:ATT_REACH] = jnp.zeros((ATT_REACH, vwin_ref.shape[1]), BF16)

    @pl.when(qb > 0)
    def _():
        kwin_ref[0:ATT_REACH] = kwin_ref[ATT_QROWS:ATT_WIN]
        vwin_ref[0:ATT_REACH] = vwin_ref[ATT_QROWS:ATT_WIN]

    kn = [_pair_rms(k_ref[:, c], kg_ref[...]) for c in cols]
    for p in pairs:
        kn_ref[:, cols[p]] = kn[p]
        kwin_ref[ATT_REACH:ATT_WIN, cols[p]] = kn[p].astype(BF16)
    v_new = v_ref[...]
    vk_ref[...] = v_new
    vwin_ref[ATT_REACH:ATT_WIN] = v_new.astype(BF16)

    def attend(mask_start):
        qn = [_pair_rms(q_ref[:, c], qg_ref[...]) * (ATT_SCALE * LOG2E) for c in cols]
        kb = [kwin_ref[:, c] for c in cols]
        vb = [vwin_ref[:, c] for c in cols]
        first = _iota((ATT_QROWS, PAIR), 1) < HEAD_DIM
        first_w = _iota((ATT_WIN, PAIR), 1) < HEAD_DIM
        qh = [jnp.where(first, qn[p], 0.0) if h == 0 else jnp.where(first, 0.0, qn[p]) for p, h in chains]
        s = [_dot(qh[i], kb[p], NT) + bias_ref[i] for i, (p, h) in enumerate(chains)]
        if mask_start:
            started = _iota(shape, 1) >= ATT_REACH - qb * ATT_QROWS
            s = [jnp.where(started, x, -jnp.inf) for x in s]
        m = [jnp.max(x, axis=-1, keepdims=True) for x in s]
        pr = [jnp.exp2(x - mm) for x, mm in zip(s, m)]
        one = jnp.ones((), BF16)
        v_aug = [jnp.where(first_w, vb[p], one) if h == 0 else jnp.where(first_w, one, vb[p]) for p, h in chains]
        o = [_dot(pr[i], v_aug[i]) for i in range(len(chains))]
        o = [x / pltpu.roll(x, HEAD_DIM, 1) for x in o]
        for p in pairs:
            o_ref[:, cols[p]] = jnp.where(first, o[2 * p], o[2 * p + 1]).astype(BF16)

    full_window_from = ATT_REACH // ATT_QROWS
    pl.when(qb < full_window_from)(lambda: attend(True))
    pl.when(qb >= full_window_from)(lambda: attend(False))


def _attn_prompt(z3, q_gain, k_gain, u):
    nb, t, _ = z3.shape
    npairs = N_ATT_HEADS // 2
    npp = ATT_PAIRS
    width = npp * PAIR
    ngroups = npairs // npp
    nq = t // ATT_QROWS
    kcol = D_ATT // width
    vcol = 2 * D_ATT // width
    keep_blocks = ATT_REACH // ATT_QROWS
    blk = (None, ATT_QROWS, width)

    keep_spec = pl.BlockSpec(blk, lambda g, b, q: (b, jnp.maximum(q - (nq - keep_blocks), 0), g))
    att, kn, vk = pl.pallas_call(
        _attn_prompt_kernel,
        grid=(ngroups, nb, nq),
        in_specs=[
            pl.BlockSpec(blk, lambda g, b, q: (b, q, g)),
            pl.BlockSpec(blk, lambda g, b, q: (b, q, kcol + g)),
            pl.BlockSpec(blk, lambda g, b, q: (b, q, vcol + g)),
            pl.BlockSpec((1, PAIR), lambda g, b, q: (0, 0)),
            pl.BlockSpec((1, PAIR), lambda g, b, q: (0, 0)),
            pl.BlockSpec((npp, 2, BIAS_LEN), lambda g, b, q: (g, 0, 0)),
        ],
        out_specs=[
            pl.BlockSpec(blk, lambda g, b, q: (b, q, g)),
            keep_spec, keep_spec,
        ],
        out_shape=[
            jax.ShapeDtypeStruct((nb, t, D_ATT), BF16),
            jax.ShapeDtypeStruct((nb, ATT_REACH, D_ATT), F32),
            jax.ShapeDtypeStruct((nb, ATT_REACH, D_ATT), F32),
        ],
        scratch_shapes=[pltpu.VMEM((2 * npp, ATT_QROWS, ATT_WIN), F32),
                        pltpu.VMEM((ATT_WIN, width), BF16), pltpu.VMEM((ATT_WIN, width), BF16)],
        compiler_params=_cparams(("arbitrary", "arbitrary", "arbitrary")),
        name="attn_prompt",
    )(z3, z3, z3, jnp.tile(q_gain, 2).reshape(1, PAIR),
      jnp.tile(k_gain, 2).reshape(1, PAIR), u.reshape(npairs, 2, BIAS_LEN))
    return att, kn, vk


def _attn_sample_kernel(q_ref, k_ref, v_ref, kp_ref, vp_ref, qg_ref, kg_ref, u_ref, o_ref, kn_ref):
    t = q_ref.shape[0]
    reach = kp_ref.shape[0]
    first = _iota((t, PAIR), 1) < HEAD_DIM
    pairs = range(N_ATT_HEADS // 2)
    cols = [slice(p * PAIR, (p + 1) * PAIR) for p in pairs]
    chains = [(p, h) for p in pairs for h in range(2)]
    qn = [_pair_rms(q_ref[:, c], qg_ref[...]) * ATT_SCALE for c in cols]
    kn = [_pair_rms(k_ref[:, c], kg_ref[...]) for c in cols]
    for p in pairs:
        kn_ref[:, cols[p]] = kn[p]
    kpast = [kp_ref[:, c].astype(BF16) for c in cols]
    vpast = [vp_ref[:, c].astype(BF16) for c in cols]
    vnew = [v_ref[:, c].astype(BF16) for c in cols]
    qh = [jnp.where(first, qn[p], 0.0) if h == 0 else jnp.where(first, 0.0, qn[p]) for p, h in chains]
    bias = [_toeplitz(u_ref[p, h:h + 1, :], t) for p, h in chains]
    s_past = [_dot(qh[i], kpast[p], NT) + bias[i][:, :reach] for i, (p, h) in enumerate(chains)]
    s_new = [_dot(qh[i], kn[p], NT) + bias[i][:, reach:reach + t] for i, (p, h) in enumerate(chains)]
    m = [jnp.maximum(jnp.max(a, axis=-1, keepdims=True), jnp.max(b, axis=-1, keepdims=True))
         for a, b in zip(s_past, s_new)]
    p_past = [jnp.exp(a - mm) for a, mm in zip(s_past, m)]
    p_new = [jnp.exp(b - mm) for b, mm in zip(s_new, m)]
    l = [jnp.sum(a, axis=-1, keepdims=True) + jnp.sum(b, axis=-1, keepdims=True)
         for a, b in zip(p_past, p_new)]
    o = [(_dot(p_past[i], vpast[p]) + _dot(p_new[i], vnew[p])) / l[i] for i, (p, h) in enumerate(chains)]
    for p in pairs:
        o_ref[:, cols[p]] = jnp.where(first, o[2 * p], o[2 * p + 1]).astype(BF16)


def _attn_sample(z3, k_past, v_past, q_gain, k_gain, u):
    nb, t, _ = z3.shape
    reach = k_past.shape[1]
    npairs = N_ATT_HEADS // 2
    att, kn = pl.pallas_call(
        _attn_sample_kernel,
        grid=(nb,),
        in_specs=[
            pl.BlockSpec((None, t, D_ATT), lambda b: (b, 0, 0)),
            pl.BlockSpec((None, t, D_ATT), lambda b: (b, 0, 1)),
            pl.BlockSpec((None, t, D_ATT), lambda b: (b, 0, 2)),
            pl.BlockSpec((None, reach, D_ATT), lambda b: (b, 0, 0)),
            pl.BlockSpec((None, reach, D_ATT), lambda b: (b, 0, 0)),
            pl.BlockSpec((1, PAIR), lambda b: (0, 0)),
            pl.BlockSpec((1, PAIR), lambda b: (0, 0)),
            pl.BlockSpec((npairs, 2, BIAS_LEN), lambda b: (0, 0, 0)),
        ],
        out_specs=[
            pl.BlockSpec((None, t, D_ATT), lambda b: (b, 0, 0)),
            pl.BlockSpec((None, t, D_ATT), lambda b: (b, 0, 0)),
        ],
        out_shape=[
            jax.ShapeDtypeStruct((nb, t, D_ATT), BF16),
            jax.ShapeDtypeStruct((nb, t, D_ATT), F32),
        ],
        compiler_params=_cparams(("arbitrary",)),
        name="attn_sample",
    )(z3, z3, z3, k_past.reshape(nb, reach, D_ATT), v_past.reshape(nb, reach, D_ATT),
      jnp.tile(q_gain, 2).reshape(1, PAIR), jnp.tile(k_gain, 2).reshape(1, PAIR),
      u.reshape(npairs, 2, BIAS_LEN))
    return att, kn


def _tri_inverse(l_mats, c):
    n = l_mats[0].shape[0]
    nb = n // c
    row, col = _iota((n, n), 0), _iota((n, n), 1)
    eye = jnp.where(row == col, 1.0, 0.0).astype(F32)
    same_block = _blk(row, c) == _blk(col, c)

    def side_by_side(m):
        parts = [m[i * c:(i + 1) * c] for i in range(nb)]
        return functools.reduce(lambda x, y: x + y, parts)

    def block_diag(m):
        return m if nb == 1 else jnp.where(same_block, jnp.tile(m, (nb, 1)), 0.0)

    eye_blocks = side_by_side(eye)
    a_s = [side_by_side(eye + l).astype(BF16) for l in l_mats]
    t_s = [side_by_side(eye - l) for l in l_mats]
    t_bd = [eye - l for l in l_mats]
    for _ in range(c.bit_length() - 2):
        r_s = [eye_blocks - _dot(a, t) for a, t in zip(a_s, t_bd)]
        t_s = [t + _dot(t, block_diag(r)) for t, r in zip(t_s, r_s)]
        t_bd = [block_diag(t) for t in t_s]
    return t_bd


def _rwkv_kernel(c, r_ref, k_ref, v_ref, lo_ref, sr_ref, sk_ref, sv_ref, slo_ref, s0_ref,
                 mur_ref, muk_ref, muv_ref, mulo_ref, w0_ref, a0_ref, kkg_ref, ka_ref, rk_ref,
                 lnw_ref, lnb_ref, w2_ref, a2_ref, g2_ref,
                 o_ref, sT_ref, s_ref, cr_ref, ck_ref, cv_ref, clo_ref):
    tb = pl.program_id(2)
    rows, width = r_ref.shape
    npp = width // PAIR
    nchunks = rows // c
    h0 = _iota((rows, PAIR), 1) < HEAD_DIM
    bd = _blk(_iota((PAIR, PAIR), 0), HEAD_DIM) == _blk(_iota((PAIR, PAIR), 1), HEAD_DIM)

    @pl.when(tb == 0)
    def _():
        s_ref[...] = jnp.zeros(s_ref.shape, F32)
        for pp in range(npp):
            s_ref[pp, 0:HEAD_DIM, 0:HEAD_DIM] = s0_ref[2 * pp]
            s_ref[pp, HEAD_DIM:PAIR, HEAD_DIM:PAIR] = s0_ref[2 * pp + 1]
        cr_ref[...] = sr_ref[...]
        ck_ref[...] = sk_ref[...]
        cv_ref[...] = sv_ref[...]
        clo_ref[...] = slo_ref[...]

    def shifted(x_ref, carry_ref, mu_ref):
        x = x_ref[...]
        prev = jnp.where(_iota(x.shape, 0) == 0, carry_ref[...], pltpu.roll(x, 1, 0))
        carry_ref[...] = x[rows - 1:rows]
        return x + (prev - x) * mu_ref[...]

    r = shifted(r_ref, cr_ref, mur_ref)
    k = shifted(k_ref, ck_ref, muk_ref)
    v = shifted(v_ref, cv_ref, muv_ref)
    lo = shifted(lo_ref, clo_ref, mulo_ref)

    zeros_w = jnp.zeros((RANK_W, width), F32)
    w2p = jnp.concatenate([w2_ref[...], zeros_w], axis=0)
    a2p = jnp.concatenate([zeros_w, a2_ref[...]], axis=0)
    lo_wa = lo[:, 0:RANK_W + RANK_A]
    u = w0_ref[...] + _dot(jnp.tanh(lo_wa), w2p)
    lw = -DECAY_SCALE * _sigmoid(u)
    a = _sigmoid(a0_ref[...] + _dot(lo_wa, a2p))
    g = _dot(_sigmoid(lo[:, RANK_W + RANK_A:]), g2_ref[...])

    kk = k * kkg_ref[...]
    kk = kk * lax.rsqrt(jnp.maximum(_head_sums(kk * kk), KK_EPS * KK_EPS))
    k = k * (1.0 + (a - 1.0) * ka_ref[...])
    b = kk * a
    bonus = _head_sums(r * k * rk_ref[...]) * v

    tr = _iota((rows, rows), 0)
    tc = _iota((rows, rows), 1)
    same_chunk = _blk(tr, c) == _blk(tc, c)
    strict = same_chunk & (tr > tc)
    incl = same_chunk & (tr >= tc)
    lw_hi, lw_lo = _split2(lw)
    tril_ones = jnp.where(incl, 1.0, 0.0).astype(BF16)
    lp = jnp.dot(tril_ones, lw_hi, preferred_element_type=F32) + \
        jnp.dot(tril_ones, lw_lo, preferred_element_type=F32)
    decay_end = [jnp.exp(lp[(ci + 1) * c - 1:(ci + 1) * c]) for ci in range(nchunks)]

    alpha_w = kk * jnp.exp(lp - lw)
    inv_p = jnp.exp(-lp)
    beta_w = b * inv_p
    kappa_w = k * inv_p
    rho_w = r * jnp.exp(lp)
    to_end = [inv_p[ci * c:(ci + 1) * c] * decay_end[ci] for ci in range(nchunks)]
    to_end = to_end[0] if nchunks == 1 else jnp.concatenate(to_end, axis=0)
    beta_ew = b * to_end
    kappa_ew = k * to_end

    wide = (rows, nchunks * PAIR)
    col_chunk = _blk(_iota(wide, 1), PAIR) == _blk(_iota(wide, 0), c)
    spread = lambda m: jnp.where(col_chunk, jnp.tile(m, (1, nchunks)), 0.0)
    eye_p = _iota((PAIR, PAIR), 0) == _iota((PAIR, PAIR), 1)

    pairs = range(npp)
    lanes = [slice(pp * PAIR, (pp + 1) * PAIR) for pp in pairs]
    alpha = [alpha_w[:, l] for l in lanes]
    rho = [rho_w[:, l] for l in lanes]
    vv = [v[:, l] for l in lanes]
    head_mask = [h0, jnp.logical_not(h0)]
    beta = [beta_w[:, l] for l in lanes]
    kappa = [kappa_w[:, l] for l in lanes]
    a_b, a_k, r_b, r_k = [], [], [], []
    if rows == 2 * PAIR:
        halves = [slice(0, PAIR), slice(PAIR, rows)]
        swap = lambda m: jnp.concatenate([m[:, PAIR:], m[:, :PAIR]], axis=1)
        half_h0 = _iota((PAIR, PAIR), 1) < HEAD_DIM
        for pp in pairs:
            keys = [jnp.concatenate([beta[pp][halves[0]], kappa[pp][halves[0]]], axis=0).astype(BF16),
                    jnp.concatenate([kappa[pp][halves[1]], beta[pp][halves[1]]], axis=0).astype(BF16)]
            for hm in (half_h0, jnp.logical_not(half_h0)):
                out = [_dot(jnp.concatenate([jnp.where(hm, alpha[pp][sl], 0.0),
                                             jnp.where(hm, rho[pp][sl], 0.0)], axis=0), keys[g], NT)
                       for g, sl in enumerate(halves)]
                a_rows = jnp.concatenate([out[0][:PAIR], out[1][:PAIR]], axis=0)
                r_rows = jnp.concatenate([out[0][PAIR:], out[1][PAIR:]], axis=0)
                a_b.append(a_rows)
                a_k.append(swap(a_rows))
                r_b.append(r_rows)
                r_k.append(swap(r_rows))
    else:
        for pp in pairs:
            keys = jnp.concatenate([beta[pp], kappa[pp]], axis=0).astype(BF16)
            for hm in head_mask:
                out = _dot(jnp.concatenate([jnp.where(hm, alpha[pp], 0.0), jnp.where(hm, rho[pp], 0.0)], axis=0),
                           keys, NT)
                a_b.append(out[:rows, :rows])
                a_k.append(out[:rows, rows:])
                r_b.append(out[rows:, :rows])
                r_k.append(out[rows:, rows:])
    t_inv = _tri_inverse([jnp.where(strict, m, 0.0) for m in a_b], c)
    x = [[_dot(jnp.where(strict, a_k[2 * pp + h], 0.0), vv[pp]) for h in range(2)] for pp in pairs]
    ws = [[_dot(t_inv[2 * pp + h], jnp.concatenate([alpha[pp], x[pp][h]], axis=1)) for h in range(2)]
          for pp in pairs]
    w12 = [jnp.concatenate([jnp.where(h0, ws[pp][0][:, :PAIR], ws[pp][1][:, :PAIR]),
                            jnp.where(h0, ws[pp][0][:, PAIR:], ws[pp][1][:, PAIR:])], axis=1) for pp in pairs]
    q = [[_dot(jnp.where(incl, r_b[2 * pp + h], 0.0), w12[pp]) for h in range(2)] for pp in pairs]
    qk = [[_dot(jnp.where(incl, r_k[2 * pp + h], 0.0), vv[pp]) for h in range(2)] for pp in pairs]
    rp = [rho[pp] - jnp.where(h0, q[pp][0][:, :PAIR], q[pp][1][:, :PAIR]) for pp in pairs]
    y0 = [jnp.where(h0, qk[pp][0] - q[pp][0][:, PAIR:], qk[pp][1] - q[pp][1][:, PAIR:]) for pp in pairs]
    wtb = [_dot(w12[pp], spread(beta_ew[:, lanes[pp]]), TN) for pp in pairs]
    vtk = [_dot(vv[pp], spread(kappa_ew[:, lanes[pp]]), TN) for pp in pairs]

    s_cur = [s_ref[pp] for pp in pairs]
    ys = [[] for _ in pairs]
    for ci in range(nchunks):
        sl = slice(ci * c, (ci + 1) * c)
        cols = slice(ci * PAIR, (ci + 1) * PAIR)
        for pp in pairs:
            p_end = decay_end[ci][:, lanes[pp]]
            gmat = jnp.where(eye_p, jnp.broadcast_to(p_end, (PAIR, PAIR)), 0.0) \
                - jnp.where(bd, wtb[pp][:PAIR, cols], 0.0)
            hmat = jnp.where(bd, vtk[pp][:, cols] - wtb[pp][PAIR:, cols], 0.0)
            ys[pp].append(_dot(rp[pp][sl], s_cur[pp], NT) + y0[pp][sl])
            s_cur[pp] = _dot(s_cur[pp], gmat) + hmat
    for pp in pairs:
        s_ref[pp] = s_cur[pp]
    y_pairs = [ys[pp][0] if nchunks == 1 else jnp.concatenate(ys[pp], axis=0) for pp in pairs]

    @pl.when(tb == pl.num_programs(2) - 1)
    def _():
        for pp in range(npp):
            sT_ref[2 * pp] = s_ref[pp, 0:HEAD_DIM, 0:HEAD_DIM]
            sT_ref[2 * pp + 1] = s_ref[pp, HEAD_DIM:PAIR, HEAD_DIM:PAIR]

    y = y_pairs[0] if npp == 1 else jnp.concatenate(y_pairs, axis=1)
    mu = _head_sums(y) * (1.0 / HEAD_DIM)
    d = y - mu
    var = _head_sums(d * d) * (1.0 / HEAD_DIM)
    yn = d * lax.rsqrt(var + GN_EPS) * lnw_ref[...] + lnb_ref[...]
    o_ref[...] = ((yn + bonus) * g).astype(BF16)


def _rwkv(z3, zlo3, shift_prev, s0, p, rows, c, npp):
    nb, t, _ = z3.shape
    width = npp * PAIR
    ngroups = D_RWKV // width
    col0 = 3 * D_ATT // width
    sp = shift_prev.reshape(nb, 1, D_SHIFT)

    def zspec(off):
        return pl.BlockSpec((None, rows, width), lambda b, q, s: (b, s, col0 + off * ngroups + q))

    def sspec(off):
        return pl.BlockSpec((None, 1, width), lambda b, q, s: (b, 0, off * ngroups + q))

    def vec(off=0):
        return pl.BlockSpec((1, width), lambda b, q, s: (0, off * ngroups + q))

    def row2(x):
        return x.reshape(1, -1)

    out, s_fin = pl.pallas_call(
        functools.partial(_rwkv_kernel, c),
        grid=(nb, ngroups, t // rows),
        in_specs=[
            zspec(0), zspec(1), zspec(2),
            pl.BlockSpec((None, rows, D_LORA), lambda b, q, s: (b, s, 0)),
            sspec(0), sspec(1), sspec(2),
            pl.BlockSpec((None, 1, D_LORA), lambda b, q, s: (b, 0, 3 * D_RWKV // D_LORA)),
            pl.BlockSpec((None, 2 * npp, HEAD_DIM, HEAD_DIM), lambda b, q, s: (b, q, 0, 0)),
            vec(0), vec(1), vec(2),
            pl.BlockSpec((1, D_LORA), lambda b, q, s: (0, 3 * D_RWKV // D_LORA)),
            vec(), vec(), vec(), vec(), vec(), vec(), vec(),
            pl.BlockSpec((RANK_W, width), lambda b, q, s: (0, q)),
            pl.BlockSpec((RANK_A, width), lambda b, q, s: (0, q)),
            pl.BlockSpec((RANK_G, width), lambda b, q, s: (0, q)),
        ],
        out_specs=[
            pl.BlockSpec((None, rows, width), lambda b, q, s: (b, s, q)),
            pl.BlockSpec((None, 2 * npp, HEAD_DIM, HEAD_DIM), lambda b, q, s: (b, q, 0, 0)),
        ],
        out_shape=[
            jax.ShapeDtypeStruct((nb, t, D_RWKV), BF16),
            jax.ShapeDtypeStruct((nb, N_RWKV_HEADS, HEAD_DIM, HEAD_DIM), F32),
        ],
        scratch_shapes=[
            pltpu.VMEM((npp, PAIR, PAIR), F32),
            pltpu.VMEM((1, width), F32), pltpu.VMEM((1, width), F32), pltpu.VMEM((1, width), F32),
            pltpu.VMEM((1, D_LORA), F32),
        ],
        compiler_params=_cparams(("arbitrary", "arbitrary", "arbitrary")),
        name="rwkv7_mix",
    )(z3, z3, z3, zlo3, sp, sp, sp, sp, s0,
      row2(p['mu_shift']), row2(p['mu_shift']), row2(p['mu_shift']), row2(p['mu_shift']),
      row2(p['w0']), row2(p['a0']), row2(p['k_k']), row2(p['k_a']), row2(p['r_k']),
      row2(p['ln_x_w']), row2(p['ln_x_b']), p['w2'], p['a2'], p['g2'])
    return out, s_fin


def _layer(xp3, xs3, mod_p, mod_s, p, u, k_past, v_past, s0_p, s0_s, shift_p, shift_s, conv_p, conv_s):
    bp, tp, d = xp3.shape
    bs, ts, _ = xs3.shape
    mp, msr = bp * tp, bs * ts
    xp = xp3.reshape(mp, d)
    xs = xs3.reshape(msr, d)
    zp, zp_lo, zs, zs_lo = _norm_proj(xp, xs, p['norm_att_g'], mod_p, mod_s, 1, 0, p['w_in'], IN_COLS,
                                      "in_proj", min(IN_ROW_TILE, tp))
    assert zp.shape[1] == 3 * D_ATT + 3 * D_RWKV and zp_lo.shape[1] == D_LORA
    zp3, zp_lo3 = zp.reshape(bp, tp, -1), zp_lo.reshape(bp, tp, -1)
    zs3, zs_lo3 = zs.reshape(bs, ts, -1), zs_lo.reshape(bs, ts, -1)
    att_p, k_keep_p, v_keep_p = _attn_prompt(zp3, p['q_norm_g'], p['k_norm_g'], u)
    rw_p, s_fin_p = _rwkv(zp3, zp_lo3, shift_p, s0_p, p, RWKV_ROWS, CHUNK, RWKV_PAIRS_PROMPT)
    att_s, k_keep_s = _attn_sample(zs3, k_past, v_past, p['q_norm_g'], p['k_norm_g'], u)
    v_keep_s = zs3[:, :, 2 * D_ATT:3 * D_ATT]
    rw_s, s_fin_s = _rwkv(zs3, zs_lo3, shift_s, s0_s, p, ts, ts, RWKV_PAIRS_SAMPLE)
    shift_last_p = jnp.concatenate([zp3[:, tp - 1, 3 * D_ATT:], zp_lo3[:, tp - 1]], axis=-1)
    shift_last_s = jnp.concatenate([zs3[:, ts - 1, 3 * D_ATT:], zs_lo3[:, ts - 1]], axis=-1)
    x1p, h2p = _out_proj(att_p.reshape(mp, D_ATT), rw_p.reshape(mp, D_RWKV), p['w_out'], xp, mod_p,
                         p['norm_ffn_g'], OUT_ROW_TILE)
    x1s, h2s = _out_proj(att_s.reshape(msr, D_ATT), rw_s.reshape(msr, D_RWKV), p['w_out'], xs, mod_s,
                         p['norm_ffn_g'], OUT_ROW_TILE)
    act_p, conv_last_p, gate_s, val_s, w_down = _ffn_up_fused(h2p, h2s, p['w_up'], p['w_down'], conv_p,
                                                              p['dw_conv'], p['dw_bias'], tp)
    act_s = _act_sample(gate_s, val_s, conv_s, p['dw_conv'], p['dw_bias'], bs, ts)
    conv_last_s = gate_s.reshape(bs, ts, -1)[:, ts - (CONV_W - 1):]
    x2p, x2s = _proj_resid(act_p, act_s, w_down, x1p, x1s, mod_p, mod_s, 5, DOWN_COLS, "ffn_down")
    heads = lambda a: a.reshape(a.shape[0], a.shape[1], N_ATT_HEADS, HEAD_DIM)
    out_p = (x2p.reshape(bp, tp, d), heads(k_keep_p), heads(v_keep_p), s_fin_p, shift_last_p, conv_last_p)
    out_s = (x2s.reshape(bs, ts, d), heads(k_keep_s), heads(v_keep_s), s_fin_s, shift_last_s, conv_last_s)
    return out_p, out_s


def kernel(x_prompt, x_sample, c_prompt, c_sample, cache_att_k, cache_att_v, state_rwkv, state_shift, state_ffn_conv, norm_att_g, norm_ffn_g, w_ada, b_ada, w_in, q_norm_g, k_norm_g, rel_bias, mu_shift, w0, w2, a0, a2, g2, k_k, k_a, r_k, ln_x_w, ln_x_b, w_out, w_up, dw_conv, dw_bias, w_down):
    depth = w_in.shape[0]
    bp, tp, d = x_prompt.shape
    bs, ts, _ = x_sample.shape
    d_ff = w_down.shape[1]
    hp, hs = x_prompt, x_sample
    outs_p = [[] for _ in range(5)]
    outs_s = [[] for _ in range(5)]
    for l in range(depth):
        p = dict(norm_att_g=norm_att_g[l], norm_ffn_g=norm_ffn_g[l], w_in=w_in[l], q_norm_g=q_norm_g[l],
                 k_norm_g=k_norm_g[l], mu_shift=mu_shift[l], w0=w0[l], w2=w2[l], a0=a0[l], a2=a2[l],
                 g2=g2[l], k_k=k_k[l], k_a=k_a[l], r_k=r_k[l], ln_x_w=ln_x_w[l], ln_x_b=ln_x_b[l],
                 w_out=w_out[l], w_up=w_up[l], dw_conv=dw_conv[l], dw_bias=dw_bias[l], w_down=w_down[l])
        n_c = bp + bs
        pad = (-n_c) % 8
        c_all = jnp.concatenate([c_prompt, c_sample, jnp.zeros((pad, d), F32)], axis=0)
        mod = _ada(c_all, w_ada[l], b_ada[l])
        mod_p = _Mod(mod.reshape(n_c + pad, 6, 1, d), False, rows_per_batch=tp)
        mod_s = _Mod(jnp.repeat(mod[bp:bp + bs], ts, axis=0), True)
        u = _bias_rows(rel_bias[l])

        res_p, res_s = _layer(hp, hs, mod_p, mod_s, p, u, cache_att_k[l], cache_att_v[l],
                              jnp.zeros((bp, N_RWKV_HEADS, HEAD_DIM, HEAD_DIM), F32), state_rwkv[l],
                              jnp.zeros((bp, D_SHIFT), F32), state_shift[l],
                              jnp.zeros((bp, CONV_W - 1, d_ff), F32), state_ffn_conv[l])
        hp, hs = res_p[0], res_s[0]
        for lst, val in zip(outs_p, res_p[1:]):
            lst.append(val)
        for lst, val in zip(outs_s, res_s[1:]):
            lst.append(val)
    st = lambda lst: jnp.stack(lst)
    return (hp, hs, *[st(x) for x in outs_p], *[st(x) for x in outs_s])
```

```python
import functools

import jax
import jax.numpy as jnp
from jax import lax
from jax.experimental import pallas as pl
from jax.experimental.pallas import tpu as pltpu

F32 = jnp.float32
BF16 = jnp.bfloat16

CHUNK = 64
N_PREV_CHUNKS = 8
ATT_REACH = N_PREV_CHUNKS * CHUNK
HEAD_DIM = 64
N_ATT_HEADS = 16
N_RWKV_HEADS = 16
D_ATT = N_ATT_HEADS * HEAD_DIM
D_RWKV = N_RWKV_HEADS * HEAD_DIM
REL_CLIP = 128
RANK_W = 64
RANK_A = 64
RANK_G = 128
D_LORA = RANK_W + RANK_A + RANK_G
D_SHIFT = 3 * D_RWKV + D_LORA
D_IN = 3 * D_ATT + D_SHIFT
CONV_W = 3
RMS_EPS = 1e-6
GN_EPS = 64e-5
KK_EPS = 1e-12
ATT_SCALE = HEAD_DIM ** -0.5
LOG2E = 1.4426950408889634
DECAY_SCALE = 0.6065306597126334

PAIR = 2 * HEAD_DIM
MXU_DIM = 256
VMEM_LIMIT = 60 * 1024 * 1024

ROW_TILE = 1024
IN_ROW_TILE = 2048
ADA_COLS = 1024
IN_COLS = 512
OUT_ROW_TILE = 512
UP_COLS = 512
DOWN_COLS = 512
ATT_QROWS = 256
ATT_WIN = ATT_QROWS + ATT_REACH
ATT_PAIRS = 4
BIAS_LEN = 1024
RWKV_ROWS = 256
RWKV_PAIRS_PROMPT = 8
RWKV_PAIRS_SAMPLE = 8


def _cparams(sem):
    return pltpu.CompilerParams(dimension_semantics=sem, vmem_limit_bytes=VMEM_LIMIT)


def _dot(a, b, dims=(((1,), (0,)), ((), ()))):
    return lax.dot_general(a.astype(BF16), b.astype(BF16), dims, preferred_element_type=F32)


def _split2(x):
    hi = x.astype(BF16)
    lo = (x - hi.astype(F32)).astype(BF16)
    return hi, lo


NT = (((1,), (1,)), ((), ()))
TN = (((0,), (0,)), ((), ()))


def _iota(shape, dim):
    return lax.broadcasted_iota(jnp.int32, shape, dim)


def _blk(x, size):
    return jnp.right_shift(x, size.bit_length() - 1)


def _head_ones(n):
    r = _blk(_iota((n, n), 0), HEAD_DIM)
    c = _blk(_iota((n, n), 1), HEAD_DIM)
    return jnp.where(r == c, 1.0, 0.0).astype(BF16)


def _head_sums(x):
    lanes = x.shape[1]
    group = min(lanes, MXU_DIM)
    ones = _head_ones(group)
    parts = [_dot(x[:, i:i + group], ones) for i in range(0, lanes, group)]
    return parts[0] if len(parts) == 1 else jnp.concatenate(parts, axis=1)


def _sigmoid(x):
    return 1.0 / (1.0 + jnp.exp(-x))


def _ada_kernel(c_ref, w_ref, b_ref, o_ref):
    c = c_ref[...]
    s = c * _sigmoid(c)
    o_ref[...] = _dot(s, w_ref[...]) + b_ref[...]


def _ada(c_all, w_ada, b_ada):
    rows, d = c_all.shape
    n = w_ada.shape[1]
    return pl.pallas_call(
        _ada_kernel,
        grid=(n // ADA_COLS,),
        in_specs=[
            pl.BlockSpec((rows, d), lambda j: (0, 0)),
            pl.BlockSpec((d, ADA_COLS), lambda j: (0, j)),
            pl.BlockSpec((1, ADA_COLS), lambda j: (0, j)),
        ],
        out_specs=pl.BlockSpec((rows, ADA_COLS), lambda j: (0, j)),
        out_shape=jax.ShapeDtypeStruct((rows, n), F32),
        compiler_params=_cparams(("arbitrary",)),
        name="ada_mod",
    )(c_all, w_ada, b_ada.reshape(1, n))


class _Mod:
    def __init__(self, arr, per_row, rows_per_batch=None):
        self.arr = arr
        self.per_row = per_row
        self.rows_per_batch = rows_per_batch

    def spec(self, idx, cols, col_of, row_tile):
        if self.per_row:
            m = self.arr.shape[0]
            d = self.arr.shape[1] // 6
            nblk = d // cols
            return pl.BlockSpec((m, cols), lambda i, j: (0, idx * nblk + col_of(j)))
        tiles_per_batch = self.rows_per_batch // row_tile
        return pl.BlockSpec((None, None, 1, cols),
                            lambda i, j: (i // tiles_per_batch, idx, 0, col_of(j)))

    def rider_spec(self, idx, cols, nj):
        m = self.arr.shape[0]
        nblk = self.arr.shape[1] // 6 // cols
        return pl.BlockSpec((m, cols), lambda i, j: (0, idx * nblk + _rider_col(i, j, nj)))


def _rider_col(i, j, nj):
    return jnp.where(i == 0, j, nj - 1)


NORM_ROWS = 128


def _store_normed(h_ref, x_ref, g_ref, sc_ref, sh_ref):
    rows = x_ref.shape[0]
    step = min(NORM_ROWS, rows)
    per_row = sc_ref.shape[0] == rows

    def body(r, carry):
        sl = pl.ds(pl.multiple_of(r * step, step), step)
        x = x_ref[sl, :]
        ms = jnp.mean(x * x, axis=-1, keepdims=True)
        xn = x * lax.rsqrt(ms + RMS_EPS) * g_ref[...]
        sc = sc_ref[sl, :] if per_row else sc_ref[...]
        sh = sh_ref[sl, :] if per_row else sh_ref[...]
        h_ref[sl, :] = (xn * (1.0 + sc) + sh).astype(BF16)
        return carry

    lax.fori_loop(0, rows // step, body, 0)


def _norm_proj_kernel(nj, x_hbm, g_ref, sc_ref, sh_ref, w_ref, wt_ref, xs_ref, scs_ref, shs_ref,
                      o_ref, ot_ref, os_ref, ost_ref, h_ref, x_ref, x_sem):
    i = pl.program_id(0)
    j = pl.program_id(1)
    tm = x_ref.shape[0]
    host = h_ref.at[0:tm]
    riders = h_ref.at[tm:]

    def x_copy(tile):
        row = tile * tm
        if not isinstance(row, int):
            row = pl.multiple_of(row, tm)
        return pltpu.make_async_copy(x_hbm.at[pl.ds(row, tm), :], x_ref, x_sem.at[0])

    @pl.when((i == 0) & (j == 0))
    def _():
        x_copy(0).start()

    @pl.when(j == 0)
    def _():
        x_copy(i).wait()
        _store_normed(host, x_ref, g_ref, sc_ref, sh_ref)

    @pl.when((j == 1) & (i + 1 < pl.num_programs(0)))
    def _():
        x_copy(i + 1).start()

    @pl.when((i == 0) & (j == 0))
    def _():
        _store_normed(riders, xs_ref, g_ref, scs_ref, shs_ref)

    def project(weight_ref, out_ref, rider_out_ref):
        @pl.when(i == 0)
        def _():
            both = jnp.dot(h_ref[...], weight_ref[...].astype(BF16), preferred_element_type=F32)
            out_ref[...] = both[0:tm]
            rider_out_ref[...] = both[tm:]

        @pl.when(i > 0)
        def _():
            out_ref[...] = jnp.dot(host[...], weight_ref[...].astype(BF16), preferred_element_type=F32)

    pl.when(j < nj)(lambda: project(w_ref, o_ref, os_ref))
    pl.when(j == nj)(lambda: project(wt_ref, ot_ref, ost_ref))


def _norm_proj(x, xs, gain, mod, mod_s, sc_idx, sh_idx, w, cols, name, row_tile):
    m, d = x.shape
    ms = xs.shape[0]
    n = w.shape[1]
    tm = min(row_tile, m)
    nj = n // cols
    n_main = nj * cols
    tail = n - n_main
    whole = lambda j: 0
    main_col = lambda j: jnp.minimum(j, nj - 1)
    return pl.pallas_call(
        functools.partial(_norm_proj_kernel, nj),
        grid=(m // tm, nj + 1),
        in_specs=[
            pl.BlockSpec(memory_space=pl.ANY),
            pl.BlockSpec((1, d), lambda i, j: (0, 0)),
            mod.spec(sc_idx, d, whole, tm),
            mod.spec(sh_idx, d, whole, tm),
            pl.BlockSpec((d, cols), lambda i, j: (0, main_col(j))),
            pl.BlockSpec((d, tail), lambda i, j: (0, n_main // tail)),
            pl.BlockSpec((ms, d), lambda i, j: (0, 0)),
            mod_s.spec(sc_idx, d, whole, ms),
            mod_s.spec(sh_idx, d, whole, ms),
        ],
        out_specs=[
            pl.BlockSpec((tm, cols), lambda i, j: (i, main_col(j))),
            pl.BlockSpec((tm, tail), lambda i, j: (i, 0)),
            pl.BlockSpec((ms, cols), lambda i, j: (0, _rider_col(i, main_col(j), nj))),
            pl.BlockSpec((ms, tail), lambda i, j: (0, 0)),
        ],
        out_shape=[jax.ShapeDtypeStruct((m, n_main), F32), jax.ShapeDtypeStruct((m, tail), F32),
                   jax.ShapeDtypeStruct((ms, n_main), F32), jax.ShapeDtypeStruct((ms, tail), F32)],
        scratch_shapes=[pltpu.VMEM((tm + ms, d), BF16), pltpu.VMEM((tm, d), F32), pltpu.SemaphoreType.DMA((1,))],
        compiler_params=_cparams(("arbitrary", "arbitrary")),
        name=name,
    )(x, gain.reshape(1, d), mod.arr, mod.arr, w, w, xs, mod_s.arr, mod_s.arr)


W_SLOTS = 3


def _proj_resid_kernel(nj, a_ref, w_hbm, x_ref, g_ref, as_ref, xs_ref, gs_ref, o_ref, os_ref, wbuf, sem):
    nsteps = pl.num_programs(0) * nj
    s = pl.program_id(0) * nj + pl.program_id(1)
    cols = wbuf.shape[2]

    def w_copy(t):
        col = (t % nj) * cols
        if not isinstance(col, int):
            col = pl.multiple_of(col, cols)
        slot = t % W_SLOTS
        return pltpu.make_async_copy(w_hbm.at[:, pl.ds(col, cols)], wbuf.at[slot], sem.at[slot])

    @pl.when(s == 0)
    def _():
        for t in range(W_SLOTS - 1):
            w_copy(t).start()

    @pl.when(s + (W_SLOTS - 1) < nsteps)
    def _():
        w_copy(s + (W_SLOTS - 1)).start()

    w_copy(s).wait()
    slot = s % W_SLOTS

    @pl.when(pl.program_id(0) == 0)
    def _():
        acc = jnp.dot(as_ref[...], wbuf[slot], preferred_element_type=F32)
        os_ref[...] = xs_ref[...] + gs_ref[...] * acc

    acc = jnp.dot(a_ref[...], wbuf[slot], preferred_element_type=F32)
    o_ref[...] = x_ref[...] + g_ref[...] * acc


def _proj_resid(a, a_s, w, x, xs, mod, mod_s, g_idx, cols, name):
    m, n = x.shape
    ms = xs.shape[0]
    kdim = a.shape[1]
    tm = min(ROW_TILE, m)
    nj = n // cols
    assert (m // tm) * nj >= W_SLOTS - 1 and w.dtype == BF16
    rider_block = pl.BlockSpec((ms, cols), lambda i, j: (0, _rider_col(i, j, nj)))
    return pl.pallas_call(
        functools.partial(_proj_resid_kernel, nj),
        grid=(m // tm, nj),
        in_specs=[
            pl.BlockSpec((tm, kdim), lambda i, j: (i, 0)),
            pl.BlockSpec(memory_space=pl.ANY),
            pl.BlockSpec((tm, cols), lambda i, j: (i, j)),
            mod.spec(g_idx, cols, lambda j: j, tm),
            pl.BlockSpec((ms, kdim), lambda i, j: (0, 0)),
            rider_block,
            mod_s.rider_spec(g_idx, cols, nj),
        ],
        out_specs=[pl.BlockSpec((tm, cols), lambda i, j: (i, j)), rider_block],
        out_shape=[jax.ShapeDtypeStruct((m, n), F32), jax.ShapeDtypeStruct((ms, n), F32)],
        scratch_shapes=[pltpu.VMEM((W_SLOTS, kdim, cols), BF16), pltpu.SemaphoreType.DMA((W_SLOTS,))],
        compiler_params=_cparams(("arbitrary", "arbitrary")),
        name=name,
    )(a, w, x, mod.arr, a_s, xs, mod_s.arr)


def _out_proj_kernel(a1_ref, a2_ref, w_ref, x_ref, g_ref, gain_ref, sc_ref, sh_ref, o_ref, h_ref, wb_ref):
    @pl.when(pl.program_id(0) == 0)
    def _():
        step = MXU_DIM

        def cast_rows(r, carry):
            sl = pl.ds(pl.multiple_of(r * step, step), step)
            wb_ref[sl, :] = w_ref[sl, :].astype(BF16)
            return carry

        lax.fori_loop(0, w_ref.shape[0] // step, cast_rows, 0)

    k1 = a1_ref.shape[1]
    rows = x_ref.shape[0]
    step = min(MXU_DIM, rows)
    per_row = sc_ref.shape[0] == rows
    pieces = [slice(r0, r0 + step) for r0 in range(0, rows, step)]
    x1s = []
    for sl in pieces:
        acc = jnp.dot(a1_ref[sl, :], wb_ref[0:k1], preferred_element_type=F32) \
            + jnp.dot(a2_ref[sl, :], wb_ref[k1:], preferred_element_type=F32)
        g = g_ref[sl, :] if per_row else g_ref[...]
        x1 = x_ref[sl, :] + g * acc
        o_ref[sl, :] = x1
        x1s.append(x1)
    for sl, x1 in zip(pieces, x1s):
        ms = jnp.mean(x1 * x1, axis=-1, keepdims=True)
        xn = x1 * lax.rsqrt(ms + RMS_EPS) * gain_ref[...]
        sc = sc_ref[sl, :] if per_row else sc_ref[...]
        sh = sh_ref[sl, :] if per_row else sh_ref[...]
        h_ref[sl, :] = (xn * (1.0 + sc) + sh).astype(BF16)


def _out_proj(a1, a2, w, x, mod, gain, row_tile):
    m, d = x.shape
    tm = min(row_tile, m)
    whole = lambda j: 0
    row = lambda kdim: pl.BlockSpec((tm, kdim), lambda i, j: (i, 0))
    return pl.pallas_call(
        _out_proj_kernel,
        grid=(m // tm, 1),
        in_specs=[
            row(a1.shape[1]), row(a2.shape[1]),
            pl.BlockSpec(w.shape, lambda i, j: (0, 0), pipeline_mode=pl.Buffered(1)),
            row(d),
            mod.spec(2, d, whole, tm),
            pl.BlockSpec((1, d), lambda i, j: (0, 0)),
            mod.spec(4, d, whole, tm),
            mod.spec(3, d, whole, tm),
        ],
        out_specs=[row(d), row(d)],
        out_shape=[jax.ShapeDtypeStruct((m, d), F32), jax.ShapeDtypeStruct((m, d), BF16)],
        scratch_shapes=[pltpu.VMEM(w.shape, BF16)],
        compiler_params=_cparams(("arbitrary", "arbitrary")),
        name="out_proj",
    )(a1, a2, w, x, mod.arr, gain.reshape(1, d), mod.arr, mod.arr)


def _gelu(x):
    return 0.5 * x * (1.0 + lax.erf(x * (2.0 ** -0.5)))


def _ffn_up_kernel(tiles_per_batch, h_ref, wg_ref, wv_ref, hist_ref, cw_ref, cb_ref, hs_ref, wd_ref,
                   act_ref, last_ref, gs_ref, vs_ref, wdb_ref, carry_ref):
    i = pl.program_id(0)
    j = pl.program_id(1)

    @pl.when(i == 0)
    def _():
        hs = hs_ref[...]
        gs_ref[...] = jnp.dot(hs, wg_ref[...].astype(BF16), preferred_element_type=F32)
        vs_ref[...] = jnp.dot(hs, wv_ref[...].astype(BF16), preferred_element_type=F32)
        wdb_ref[...] = wd_ref[...].astype(BF16)

    @pl.when((i % tiles_per_batch) == 0)
    def _():
        carry_ref[j] = hist_ref[...]

    h = h_ref[...]
    gate = jnp.dot(h, wg_ref[...].astype(BF16), preferred_element_type=F32)
    val = jnp.dot(h, wv_ref[...].astype(BF16), preferred_element_type=F32)
    tm = gate.shape[0]
    prev = carry_ref[j]
    row = _iota(gate.shape, 0)
    g1 = pltpu.roll(gate, 1, 0)
    g2 = pltpu.roll(gate, 2, 0)
    g1 = jnp.where(row == 0, prev[1:2], g1)
    g2 = jnp.where(row == 0, prev[0:1], jnp.where(row == 1, prev[1:2], g2))
    cw = cw_ref[...]
    conv = cb_ref[...] + g2 * cw[0:1] + g1 * cw[1:2] + gate * cw[2:3]
    act_ref[...] = (_gelu(conv) * val).astype(BF16)
    tail = gate[tm - 2:tm]
    carry_ref[j] = tail
    last_ref[...] = tail


def _ffn_up_fused(h, hs, w_up, w_down, hist, conv_w, conv_b, rows_per_batch):
    m, d = h.shape
    ms = hs.shape[0]
    f = w_up.shape[1] // 2
    tm = min(ROW_TILE, rows_per_batch)
    cols = UP_COLS
    nj = f // cols
    tiles_per_batch = rows_per_batch // tm
    rider_block = pl.BlockSpec((ms, cols), lambda i, j: (0, _rider_col(i, j, nj)))
    wd_block = pl.BlockSpec((f // nj, w_down.shape[1]), lambda i, j: (_rider_col(i, j, nj), 0))
    act, tile_tails, gate_s, val_s, w_down_bf16 = pl.pallas_call(
        functools.partial(_ffn_up_kernel, tiles_per_batch),
        grid=(m // tm, nj),
        in_specs=[
            pl.BlockSpec((tm, d), lambda i, j: (i, 0)),
            pl.BlockSpec((d, cols), lambda i, j: (0, j)),
            pl.BlockSpec((d, cols), lambda i, j: (0, nj + j)),
            pl.BlockSpec((None, CONV_W - 1, cols), lambda i, j: (i // tiles_per_batch, 0, j)),
            pl.BlockSpec((CONV_W, cols), lambda i, j: (0, j)),
            pl.BlockSpec((1, cols), lambda i, j: (0, j)),
            pl.BlockSpec((ms, d), lambda i, j: (0, 0)),
            wd_block,
        ],
        out_specs=[
            pl.BlockSpec((tm, cols), lambda i, j: (i, j)),
            pl.BlockSpec((None, CONV_W - 1, cols), lambda i, j: (i, 0, j)),
            rider_block, rider_block, wd_block,
        ],
        out_shape=[
            jax.ShapeDtypeStruct((m, f), BF16),
            jax.ShapeDtypeStruct((m // tm, CONV_W - 1, f), F32),
            jax.ShapeDtypeStruct((ms, f), F32),
            jax.ShapeDtypeStruct((ms, f), F32),
            jax.ShapeDtypeStruct(w_down.shape, BF16),
        ],
        scratch_shapes=[pltpu.VMEM((nj, CONV_W - 1, cols), F32)],
        compiler_params=_cparams(("arbitrary", "arbitrary")),
        name="ffn_up_prompt",
    )(h, w_up, w_up, hist, conv_w, conv_b.reshape(1, f), hs, w_down)
    return act, tile_tails[tiles_per_batch - 1::tiles_per_batch], gate_s, val_s, w_down_bf16


def _act_sample_kernel(gate_ref, val_ref, hist_ref, cw_ref, cb_ref, act_ref):
    gate = gate_ref[...]
    hist = hist_ref[...]
    t = _iota(gate.shape, 1)
    g1 = jnp.where(t == 0, hist[:, 1:2], pltpu.roll(gate, 1, 1))
    g2 = jnp.where(t == 0, hist[:, 0:1], jnp.where(t == 1, hist[:, 1:2], pltpu.roll(gate, 2, 1)))
    cw = cw_ref[...]
    conv = cb_ref[...] + g2 * cw[0:1] + g1 * cw[1:2] + gate * cw[2:3]
    act_ref[...] = (_gelu(conv) * val_ref[...]).astype(BF16)


def _act_sample(gate, val, hist, conv_w, conv_b, nb, t):
    f = gate.shape[1]
    cols = UP_COLS
    nj = f // cols
    gate3 = gate.reshape(nb, t, f)
    val3 = val.reshape(nb, t, f)
    act = pl.pallas_call(
        _act_sample_kernel,
        grid=(nj,),
        in_specs=[
            pl.BlockSpec((nb, t, cols), lambda j: (0, 0, j)),
            pl.BlockSpec((nb, t, cols), lambda j: (0, 0, j)),
            pl.BlockSpec((nb, CONV_W - 1, cols), lambda j: (0, 0, j)),
            pl.BlockSpec((CONV_W, cols), lambda j: (0, j)),
            pl.BlockSpec((1, cols), lambda j: (0, j)),
        ],
        out_specs=pl.BlockSpec((nb, t, cols), lambda j: (0, 0, j)),
        out_shape=jax.ShapeDtypeStruct((nb, t, f), BF16),
        compiler_params=_cparams(("arbitrary",)),
        name="ffn_act_sample",
    )(gate3, val3, hist, conv_w, conv_b.reshape(1, f))
    return act.reshape(nb * t, f)


def _pair_rms(x, gain):
    x2 = x * x
    first = _iota(x.shape, 1) < HEAD_DIM
    s0 = jnp.sum(jnp.where(first, x2, 0.0), axis=-1, keepdims=True)
    s1 = jnp.sum(jnp.where(first, 0.0, x2), axis=-1, keepdims=True)
    ms = jnp.where(first, s0, s1) * (1.0 / HEAD_DIM)
    return x * lax.rsqrt(ms + RMS_EPS) * gain


def _bias_rows(table):
    h = table.shape[0]
    far = jnp.broadcast_to(table[:, 2 * REL_CLIP:], (h, ATT_REACH - REL_CLIP))
    mid = table[:, ::-1]
    near_len = BIAS_LEN - ATT_QROWS - (ATT_REACH - REL_CLIP) - (2 * REL_CLIP + 1)
    near = jnp.broadcast_to(table[:, 0:1], (h, near_len))
    wrap = jnp.broadcast_to(table[:, 2 * REL_CLIP:], (h, ATT_QROWS))
    return jnp.concatenate([far, mid, near, wrap], axis=1)


def _toeplitz(u_row, rows):
    return pltpu.roll(jnp.broadcast_to(u_row, (rows, BIAS_LEN)), 0, 1, stride=1, stride_axis=0)


def _attn_prompt_kernel(q_ref, k_ref, v_ref, qg_ref, kg_ref, u_ref, o_ref, kn_ref, vk_ref,
                        bias_ref, kwin_ref, vwin_ref):
    b = pl.program_id(1)
    qb = pl.program_id(2)
    shape = (ATT_QROWS, ATT_WIN)
    pairs = range(q_ref.shape[1] // PAIR)
    cols = [slice(p * PAIR, (p + 1) * PAIR) for p in pairs]
    chains = [(p, h) for p in pairs for h in range(2)]

    @pl.when((b == 0) & (qb == 0))
    def _():
        r = _iota(shape, 0)
        w = _iota(shape, 1)
        chunk_lo = _blk(r, CHUNK) * CHUNK
        in_band = (w >= chunk_lo) & (w < chunk_lo + (ATT_REACH + CHUNK))
        for i, (p, h) in enumerate(chains):
            bias = _toeplitz(u_ref[p, h:h + 1, :], ATT_QROWS)[:, :ATT_WIN]
            bias_ref[i] = jnp.where(in_band, bias * LOG2E, -jnp.inf)

    @pl.when(qb == 0)
    def _():
        kwin_ref[0:ATT_REACH] = jnp.zeros((ATT_REACH, kwin_ref.shape[1]), BF16)
        vwin_ref[0:ATT_REACH] = jnp.zeros((ATT_REACH, vwin_ref.shape[1]), BF16)

    @pl.when(qb > 0)
    def _():
        kwin_ref[0:ATT_REACH] = kwin_ref[ATT_QROWS:ATT_WIN]
        vwin_ref[0:ATT_REACH] = vwin_ref[ATT_QROWS:ATT_WIN]

    kn = [_pair_rms(k_ref[:, c], kg_ref[...]) for c in cols]
    for p in pairs:
        kn_ref[:, cols[p]] = kn[p]
        kwin_ref[ATT_REACH:ATT_WIN, cols[p]] = kn[p].astype(BF16)
    v_new = v_ref[...]
    vk_ref[...] = v_new
    vwin_ref[ATT_REACH:ATT_WIN] = v_new.astype(BF16)

    def attend(mask_start):
        qn = [_pair_rms(q_ref[:, c], qg_ref[...]) * (ATT_SCALE * LOG2E) for c in cols]
        kb = [kwin_ref[:, c] for c in cols]
        vb = [vwin_ref[:, c] for c in cols]
        first = _iota((ATT_QROWS, PAIR), 1) < HEAD_DIM
        first_w = _iota((ATT_WIN, PAIR), 1) < HEAD_DIM
        qh = [jnp.where(first, qn[p], 0.0) if h == 0 else jnp.where(first, 0.0, qn[p]) for p, h in chains]
        s = [_dot(qh[i], kb[p], NT) + bias_ref[i] for i, (p, h) in enumerate(chains)]
        if mask_start:
            started = _iota(shape, 1) >= ATT_REACH - qb * ATT_QROWS
            s = [jnp.where(started, x, -jnp.inf) for x in s]
        m = [jnp.max(x, axis=-1, keepdims=True) for x in s]
        pr = [jnp.exp2(x - mm) for x, mm in zip(s, m)]
        one = jnp.ones((), BF16)
        v_aug = [jnp.where(first_w, vb[p], one) if h == 0 else jnp.where(first_w, one, vb[p]) for p, h in chains]
        o = [_dot(pr[i], v_aug[i]) for i in range(len(chains))]
        o = [x / pltpu.roll(x, HEAD_DIM, 1) for x in o]
        for p in pairs:
            o_ref[:, cols[p]] = jnp.where(first, o[2 * p], o[2 * p + 1]).astype(BF16)

    full_window_from = ATT_REACH // ATT_QROWS
    pl.when(qb < full_window_from)(lambda: attend(True))
    pl.when(qb >= full_window_from)(lambda: attend(False))


def _attn_prompt(z3, q_gain, k_gain, u):
    nb, t, _ = z3.shape
    npairs = N_ATT_HEADS // 2
    npp = ATT_PAIRS
    width = npp * PAIR
    ngroups = npairs // npp
    nq = t // ATT_QROWS
    kcol = D_ATT // width
    vcol = 2 * D_ATT // width
    keep_blocks = ATT_REACH // ATT_QROWS
    blk = (None, ATT_QROWS, width)

    keep_spec = pl.BlockSpec(blk, lambda g, b, q: (b, jnp.maximum(q - (nq - keep_blocks), 0), g))
    att, kn, vk = pl.pallas_call(
        _attn_prompt_kernel,
        grid=(ngroups, nb, nq),
        in_specs=[
            pl.BlockSpec(blk, lambda g, b, q: (b, q, g)),
            pl.BlockSpec(blk, lambda g, b, q: (b, q, kcol + g)),
            pl.BlockSpec(blk, lambda g, b, q: (b, q, vcol + g)),
            pl.BlockSpec((1, PAIR), lambda g, b, q: (0, 0)),
            pl.BlockSpec((1, PAIR), lambda g, b, q: (0, 0)),
            pl.BlockSpec((npp, 2, BIAS_LEN), lambda g, b, q: (g, 0, 0)),
        ],
        out_specs=[
            pl.BlockSpec(blk, lambda g, b, q: (b, q, g)),
            keep_spec, keep_spec,
        ],
        out_shape=[
            jax.ShapeDtypeStruct((nb, t, D_ATT), BF16),
            jax.ShapeDtypeStruct((nb, ATT_REACH, D_ATT), F32),
            jax.ShapeDtypeStruct((nb, ATT_REACH, D_ATT), F32),
        ],
        scratch_shapes=[pltpu.VMEM((2 * npp, ATT_QROWS, ATT_WIN), F32),
                        pltpu.VMEM((ATT_WIN, width), BF16), pltpu.VMEM((ATT_WIN, width), BF16)],
        compiler_params=_cparams(("arbitrary", "arbitrary", "arbitrary")),
        name="attn_prompt",
    )(z3, z3, z3, jnp.tile(q_gain, 2).reshape(1, PAIR),
      jnp.tile(k_gain, 2).reshape(1, PAIR), u.reshape(npairs, 2, BIAS_LEN))
    return att, kn, vk


def _attn_sample_kernel(q_ref, k_ref, v_ref, kp_ref, vp_ref, qg_ref, kg_ref, u_ref, o_ref, kn_ref):
    t = q_ref.shape[0]
    reach = kp_ref.shape[0]
    first = _iota((t, PAIR), 1) < HEAD_DIM
    pairs = range(N_ATT_HEADS // 2)
    cols = [slice(p * PAIR, (p + 1) * PAIR) for p in pairs]
    chains = [(p, h) for p in pairs for h in range(2)]
    qn = [_pair_rms(q_ref[:, c], qg_ref[...]) * ATT_SCALE for c in cols]
    kn = [_pair_rms(k_ref[:, c], kg_ref[...]) for c in cols]
    for p in pairs:
        kn_ref[:, cols[p]] = kn[p]
    kpast = [kp_ref[:, c].astype(BF16) for c in cols]
    vpast = [vp_ref[:, c].astype(BF16) for c in cols]
    vnew = [v_ref[:, c].astype(BF16) for c in cols]
    qh = [jnp.where(first, qn[p], 0.0) if h == 0 else jnp.where(first, 0.0, qn[p]) for p, h in chains]
    bias = [_toeplitz(u_ref[p, h:h + 1, :], t) for p, h in chains]
    s_past = [_dot(qh[i], kpast[p], NT) + bias[i][:, :reach] for i, (p, h) in enumerate(chains)]
    s_new = [_dot(qh[i], kn[p], NT) + bias[i][:, reach:reach + t] for i, (p, h) in enumerate(chains)]
    m = [jnp.maximum(jnp.max(a, axis=-1, keepdims=True), jnp.max(b, axis=-1, keepdims=True))
         for a, b in zip(s_past, s_new)]
    p_past = [jnp.exp(a - mm) for a, mm in zip(s_past, m)]
    p_new = [jnp.exp(b - mm) for b, mm in zip(s_new, m)]
    l = [jnp.sum(a, axis=-1, keepdims=True) + jnp.sum(b, axis=-1, keepdims=True)
         for a, b in zip(p_past, p_new)]
    o = [(_dot(p_past[i], vpast[p]) + _dot(p_new[i], vnew[p])) / l[i] for i, (p, h) in enumerate(chains)]
    for p in pairs:
        o_ref[:, cols[p]] = jnp.where(first, o[2 * p], o[2 * p + 1]).astype(BF16)


def _attn_sample(z3, k_past, v_past, q_gain, k_gain, u):
    nb, t, _ = z3.shape
    reach = k_past.shape[1]
    npairs = N_ATT_HEADS // 2
    att, kn = pl.pallas_call(
        _attn_sample_kernel,
        grid=(nb,),
        in_specs=[
            pl.BlockSpec((None, t, D_ATT), lambda b: (b, 0, 0)),
            pl.BlockSpec((None, t, D_ATT), lambda b: (b, 0, 1)),
            pl.BlockSpec((None, t, D_ATT), lambda b: (b, 0, 2)),
            pl.BlockSpec((None, reach, D_ATT), lambda b: (b, 0, 0)),
            pl.BlockSpec((None, reach, D_ATT), lambda b: (b, 0, 0)),
            pl.BlockSpec((1, PAIR), lambda b: (0, 0)),
            pl.BlockSpec((1, PAIR), lambda b: (0, 0)),
            pl.BlockSpec((npairs, 2, BIAS_LEN), lambda b: (0, 0, 0)),
        ],
        out_specs=[
            pl.BlockSpec((None, t, D_ATT), lambda b: (b, 0, 0)),
            pl.BlockSpec((None, t, D_ATT), lambda b: (b, 0, 0)),
        ],
        out_shape=[
            jax.ShapeDtypeStruct((nb, t, D_ATT), BF16),
            jax.ShapeDtypeStruct((nb, t, D_ATT), F32),
        ],
        compiler_params=_cparams(("arbitrary",)),
        name="attn_sample",
    )(z3, z3, z3, k_past.reshape(nb, reach, D_ATT), v_past.reshape(nb, reach, D_ATT),
      jnp.tile(q_gain, 2).reshape(1, PAIR), jnp.tile(k_gain, 2).reshape(1, PAIR),
      u.reshape(npairs, 2, BIAS_LEN))
    return att, kn


def _tri_inverse(l_mats, c):
    n = l_mats[0].shape[0]
    nb = n // c
    row, col = _iota((n, n), 0), _iota((n, n), 1)
    eye = jnp.where(row == col, 1.0, 0.0).astype(F32)
    same_block = _blk(row, c) == _blk(col, c)

    def side_by_side(m):
        parts = [m[i * c:(i + 1) * c] for i in range(nb)]
        return functools.reduce(lambda x, y: x + y, parts)

    def block_diag(m):
        return m if nb == 1 else jnp.where(same_block, jnp.tile(m, (nb, 1)), 0.0)

    eye_blocks = side_by_side(eye)
    a_s = [side_by_side(eye + l).astype(BF16) for l in l_mats]
    t_s = [side_by_side(eye - l) for l in l_mats]
    t_bd = [eye - l for l in l_mats]
    for _ in range(c.bit_length() - 2):
        r_s = [eye_blocks - _dot(a, t) for a, t in zip(a_s, t_bd)]
        t_s = [t + _dot(t, block_diag(r)) for t, r in zip(t_s, r_s)]
        t_bd = [block_diag(t) for t in t_s]
    return t_bd


def _rwkv_kernel(c, r_ref, k_ref, v_ref, lo_ref, sr_ref, sk_ref, sv_ref, slo_ref, s0_ref,
                 mur_ref, muk_ref, muv_ref, mulo_ref, w0_ref, a0_ref, kkg_ref, ka_ref, rk_ref,
                 lnw_ref, lnb_ref, w2_ref, a2_ref, g2_ref,
                 o_ref, sT_ref, s_ref, cr_ref, ck_ref, cv_ref, clo_ref):
    tb = pl.program_id(2)
    rows, width = r_ref.shape
    npp = width // PAIR
    nchunks = rows // c
    h0 = _iota((rows, PAIR), 1) < HEAD_DIM
    bd = _blk(_iota((PAIR, PAIR), 0), HEAD_DIM) == _blk(_iota((PAIR, PAIR), 1), HEAD_DIM)

    @pl.when(tb == 0)
    def _():
        s_ref[...] = jnp.zeros(s_ref.shape, F32)
        for pp in range(npp):
            s_ref[pp, 0:HEAD_DIM, 0:HEAD_DIM] = s0_ref[2 * pp]
            s_ref[pp, HEAD_DIM:PAIR, HEAD_DIM:PAIR] = s0_ref[2 * pp + 1]
        cr_ref[...] = sr_ref[...]
        ck_ref[...] = sk_ref[...]
        cv_ref[...] = sv_ref[...]
        clo_ref[...] = slo_ref[...]

    def shifted(x_ref, carry_ref, mu_ref):
        x = x_ref[...]
        prev = jnp.where(_iota(x.shape, 0) == 0, carry_ref[...], pltpu.roll(x, 1, 0))
        carry_ref[...] = x[rows - 1:rows]
        return x + (prev - x) * mu_ref[...]

    r = shifted(r_ref, cr_ref, mur_ref)
    k = shifted(k_ref, ck_ref, muk_ref)
    v = shifted(v_ref, cv_ref, muv_ref)
    lo = shifted(lo_ref, clo_ref, mulo_ref)

    zeros_w = jnp.zeros((RANK_W, width), F32)
    w2p = jnp.concatenate([w2_ref[...], zeros_w], axis=0)
    a2p = jnp.concatenate([zeros_w, a2_ref[...]], axis=0)
    lo_wa = lo[:, 0:RANK_W + RANK_A]
    u = w0_ref[...] + _dot(jnp.tanh(lo_wa), w2p)
    lw = -DECAY_SCALE * _sigmoid(u)
    a = _sigmoid(a0_ref[...] + _dot(lo_wa, a2p))
    g = _dot(_sigmoid(lo[:, RANK_W + RANK_A:]), g2_ref[...])

    kk = k * kkg_ref[...]
    kk = kk * lax.rsqrt(jnp.maximum(_head_sums(kk * kk), KK_EPS * KK_EPS))
    k = k * (1.0 + (a - 1.0) * ka_ref[...])
    b = kk * a
    bonus = _head_sums(r * k * rk_ref[...]) * v

    tr = _iota((rows, rows), 0)
    tc = _iota((rows, rows), 1)
    same_chunk = _blk(tr, c) == _blk(tc, c)
    strict = same_chunk & (tr > tc)
    incl = same_chunk & (tr >= tc)
    lw_hi, lw_lo = _split2(lw)
    tril_ones = jnp.where(incl, 1.0, 0.0).astype(BF16)
    lp = jnp.dot(tril_ones, lw_hi, preferred_element_type=F32) + \
        jnp.dot(tril_ones, lw_lo, preferred_element_type=F32)
    decay_end = [jnp.exp(lp[(ci + 1) * c - 1:(ci + 1) * c]) for ci in range(nchunks)]

    alpha_w = kk * jnp.exp(lp - lw)
    inv_p = jnp.exp(-lp)
    beta_w = b * inv_p
    kappa_w = k * inv_p
    rho_w = r * jnp.exp(lp)
    to_end = [inv_p[ci * c:(ci + 1) * c] * decay_end[ci] for ci in range(nchunks)]
    to_end = to_end[0] if nchunks == 1 else jnp.concatenate(to_end, axis=0)
    beta_ew = b * to_end
    kappa_ew = k * to_end

    wide = (rows, nchunks * PAIR)
    col_chunk = _blk(_iota(wide, 1), PAIR) == _blk(_iota(wide, 0), c)
    spread = lambda m: jnp.where(col_chunk, jnp.tile(m, (1, nchunks)), 0.0)
    eye_p = _iota((PAIR, PAIR), 0) == _iota((PAIR, PAIR), 1)

    pairs = range(npp)
    lanes = [slice(pp * PAIR, (pp + 1) * PAIR) for pp in pairs]
    alpha = [alpha_w[:, l] for l in lanes]
    rho = [rho_w[:, l] for l in lanes]
    vv = [v[:, l] for l in lanes]
    head_mask = [h0, jnp.logical_not(h0)]
    beta = [beta_w[:, l] for l in lanes]
    kappa = [kappa_w[:, l] for l in lanes]
    a_b, a_k, r_b, r_k = [], [], [], []
    if rows == 2 * PAIR:
        halves = [slice(0, PAIR), slice(PAIR, rows)]
        swap = lambda m: jnp.concatenate([m[:, PAIR:], m[:, :PAIR]], axis=1)
        half_h0 = _iota((PAIR, PAIR), 1) < HEAD_DIM
        for pp in pairs:
            keys = [jnp.concatenate([beta[pp][halves[0]], kappa[pp][halves[0]]], axis=0).astype(BF16),
                    jnp.concatenate([kappa[pp][halves[1]], beta[pp][halves[1]]], axis=0).astype(BF16)]
            for hm in (half_h0, jnp.logical_not(half_h0)):
                out = [_dot(jnp.concatenate([jnp.where(hm, alpha[pp][sl], 0.0),
                                             jnp.where(hm, rho[pp][sl], 0.0)], axis=0), keys[g], NT)
                       for g, sl in enumerate(halves)]
                a_rows = jnp.concatenate([out[0][:PAIR], out[1][:PAIR]], axis=0)
                r_rows = jnp.concatenate([out[0][PAIR:], out[1][PAIR:]], axis=0)
                a_b.append(a_rows)
                a_k.append(swap(a_rows))
                r_b.append(r_rows)
                r_k.append(swap(r_rows))
    else:
        for pp in pairs:
            keys = jnp.concatenate([beta[pp], kappa[pp]], axis=0).astype(BF16)
            for hm in head_mask:
                out = _dot(jnp.concatenate([jnp.where(hm, alpha[pp], 0.0), jnp.where(hm, rho[pp], 0.0)], axis=0),
                           keys, NT)
                a_b.append(out[:rows, :rows])
                a_k.append(out[:rows, rows:])
                r_b.append(out[rows:, :rows])
                r_k.append(out[rows:, rows:])
    t_inv = _tri_inverse([jnp.where(strict, m, 0.0) for m in a_b], c)
    x = [[_dot(jnp.where(strict, a_k[2 * pp + h], 0.0), vv[pp]) for h in range(2)] for pp in pairs]
    ws = [[_dot(t_inv[2 * pp + h], jnp.concatenate([alpha[pp], x[pp][h]], axis=1)) for h in range(2)]
          for pp in pairs]
    w12 = [jnp.concatenate([jnp.where(h0, ws[pp][0][:, :PAIR], ws[pp][1][:, :PAIR]),
                            jnp.where(h0, ws[pp][0][:, PAIR:], ws[pp][1][:, PAIR:])], axis=1) for pp in pairs]
    q = [[_dot(jnp.where(incl, r_b[2 * pp + h], 0.0), w12[pp]) for h in range(2)] for pp in pairs]
    qk = [[_dot(jnp.where(incl, r_k[2 * pp + h], 0.0), vv[pp]) for h in range(2)] for pp in pairs]
    rp = [rho[pp] - jnp.where(h0, q[pp][0][:, :PAIR], q[pp][1][:, :PAIR]) for pp in pairs]
    y0 = [jnp.where(h0, qk[pp][0] - q[pp][0][:, PAIR:], qk[pp][1] - q[pp][1][:, PAIR:]) for pp in pairs]
    wtb = [_dot(w12[pp], spread(beta_ew[:, lanes[pp]]), TN) for pp in pairs]
    vtk = [_dot(vv[pp], spread(kappa_ew[:, lanes[pp]]), TN) for pp in pairs]

    s_cur = [s_ref[pp] for pp in pairs]
    ys = [[] for _ in pairs]
    for ci in range(nchunks):
        sl = slice(ci * c, (ci + 1) * c)
        cols = slice(ci * PAIR, (ci + 1) * PAIR)
        for pp in pairs:
            p_end = decay_end[ci][:, lanes[pp]]
            gmat = jnp.where(eye_p, jnp.broadcast_to(p_end, (PAIR, PAIR)), 0.0) \
                - jnp.where(bd, wtb[pp][:PAIR, cols], 0.0)
            hmat = jnp.where(bd, vtk[pp][:, cols] - wtb[pp][PAIR:, cols], 0.0)
            ys[pp].append(_dot(rp[pp][sl], s_cur[pp], NT) + y0[pp][sl])
            s_cur[pp] = _dot(s_cur[pp], gmat) + hmat
    for pp in pairs:
        s_ref[pp] = s_cur[pp]
    y_pairs = [ys[pp][0] if nchunks == 1 else jnp.concatenate(ys[pp], axis=0) for pp in pairs]

    @pl.when(tb == pl.num_programs(2) - 1)
    def _():
        for pp in range(npp):
            sT_ref[2 * pp] = s_ref[pp, 0:HEAD_DIM, 0:HEAD_DIM]
            sT_ref[2 * pp + 1] = s_ref[pp, HEAD_DIM:PAIR, HEAD_DIM:PAIR]

    y = y_pairs[0] if npp == 1 else jnp.concatenate(y_pairs, axis=1)
    mu = _head_sums(y) * (1.0 / HEAD_DIM)
    d = y - mu
    var = _head_sums(d * d) * (1.0 / HEAD_DIM)
    yn = d * lax.rsqrt(var + GN_EPS) * lnw_ref[...] + lnb_ref[...]
    o_ref[...] = ((yn + bonus) * g).astype(BF16)


def _rwkv(z3, zlo3, shift_prev, s0, p, rows, c, npp):
    nb, t, _ = z3.shape
    width = npp * PAIR
    ngroups = D_RWKV // width
    col0 = 3 * D_ATT // width
    sp = shift_prev.reshape(nb, 1, D_SHIFT)

    def zspec(off):
        return pl.BlockSpec((None, rows, width), lambda b, q, s: (b, s, col0 + off * ngroups + q))

    def sspec(off):
        return pl.BlockSpec((None, 1, width), lambda b, q, s: (b, 0, off * ngroups + q))

    def vec(off=0):
        return pl.BlockSpec((1, width), lambda b, q, s: (0, off * ngroups + q))

    def row2(x):
        return x.reshape(1, -1)

    out, s_fin = pl.pallas_call(
        functools.partial(_rwkv_kernel, c),
        grid=(nb, ngroups, t // rows),
        in_specs=[
            zspec(0), zspec(1), zspec(2),
            pl.BlockSpec((None, rows, D_LORA), lambda b, q, s: (b, s, 0)),
            sspec(0), sspec(1), sspec(2),
            pl.BlockSpec((None, 1, D_LORA), lambda b, q, s: (b, 0, 3 * D_RWKV // D_LORA)),
            pl.BlockSpec((None, 2 * npp, HEAD_DIM, HEAD_DIM), lambda b, q, s: (b, q, 0, 0)),
            vec(0), vec(1), vec(2),
            pl.BlockSpec((1, D_LORA), lambda b, q, s: (0, 3 * D_RWKV // D_LORA)),
            vec(), vec(), vec(), vec(), vec(), vec(), vec(),
            pl.BlockSpec((RANK_W, width), lambda b, q, s: (0, q)),
            pl.BlockSpec((RANK_A, width), lambda b, q, s: (0, q)),
            pl.BlockSpec((RANK_G, width), lambda b, q, s: (0, q)),
        ],
        out_specs=[
            pl.BlockSpec((None, rows, width), lambda b, q, s: (b, s, q)),
            pl.BlockSpec((None, 2 * npp, HEAD_DIM, HEAD_DIM), lambda b, q, s: (b, q, 0, 0)),
        ],
        out_shape=[
            jax.ShapeDtypeStruct((nb, t, D_RWKV), BF16),
            jax.ShapeDtypeStruct((nb, N_RWKV_HEADS, HEAD_DIM, HEAD_DIM), F32),
        ],
        scratch_shapes=[
            pltpu.VMEM((npp, PAIR, PAIR), F32),
            pltpu.VMEM((1, width), F32), pltpu.VMEM((1, width), F32), pltpu.VMEM((1, width), F32),
            pltpu.VMEM((1, D_LORA), F32),
        ],
        compiler_params=_cparams(("arbitrary", "arbitrary", "arbitrary")),
        name="rwkv7_mix",
    )(z3, z3, z3, zlo3, sp, sp, sp, sp, s0,
      row2(p['mu_shift']), row2(p['mu_shift']), row2(p['mu_shift']), row2(p['mu_shift']),
      row2(p['w0']), row2(p['a0']), row2(p['k_k']), row2(p['k_a']), row2(p['r_k']),
      row2(p['ln_x_w']), row2(p['ln_x_b']), p['w2'], p['a2'], p['g2'])
    return out, s_fin


def _layer(xp3, xs3, mod_p, mod_s, p, u, k_past, v_past, s0_p, s0_s, shift_p, shift_s, conv_p, conv_s):
    bp, tp, d = xp3.shape
    bs, ts, _ = xs3.shape
    mp, msr = bp * tp, bs * ts
    xp = xp3.reshape(mp, d)
    xs = xs3.reshape(msr, d)
    zp, zp_lo, zs, zs_lo = _norm_proj(xp, xs, p['norm_att_g'], mod_p, mod_s, 1, 0, p['w_in'], IN_COLS,
                                      "in_proj", min(IN_ROW_TILE, tp))
    assert zp.shape[1] == 3 * D_ATT + 3 * D_RWKV and zp_lo.shape[1] == D_LORA
    zp3, zp_lo3 = zp.reshape(bp, tp, -1), zp_lo.reshape(bp, tp, -1)
    zs3, zs_lo3 = zs.reshape(bs, ts, -1), zs_lo.reshape(bs, ts, -1)
    att_p, k_keep_p, v_keep_p = _attn_prompt(zp3, p['q_norm_g'], p['k_norm_g'], u)
    rw_p, s_fin_p = _rwkv(zp3, zp_lo3, shift_p, s0_p, p, RWKV_ROWS, CHUNK, RWKV_PAIRS_PROMPT)
    att_s, k_keep_s = _attn_sample(zs3, k_past, v_past, p['q_norm_g'], p['k_norm_g'], u)
    v_keep_s = zs3[:, :, 2 * D_ATT:3 * D_ATT]
    rw_s, s_fin_s = _rwkv(zs3, zs_lo3, shift_s, s0_s, p, ts, ts, RWKV_PAIRS_SAMPLE)
    shift_last_p = jnp.concatenate([zp3[:, tp - 1, 3 * D_ATT:], zp_lo3[:, tp - 1]], axis=-1)
    shift_last_s = jnp.concatenate([zs3[:, ts - 1, 3 * D_ATT:], zs_lo3[:, ts - 1]], axis=-1)
    x1p, h2p = _out_proj(att_p.reshape(mp, D_ATT), rw_p.reshape(mp, D_RWKV), p['w_out'], xp, mod_p,
                         p['norm_ffn_g'], OUT_ROW_TILE)
    x1s, h2s = _out_proj(att_s.reshape(msr, D_ATT), rw_s.reshape(msr, D_RWKV), p['w_out'], xs, mod_s,
                         p['norm_ffn_g'], OUT_ROW_TILE)
    act_p, conv_last_p, gate_s, val_s, w_down = _ffn_up_fused(h2p, h2s, p['w_up'], p['w_down'], conv_p,
                                                              p['dw_conv'], p['dw_bias'], tp)
    act_s = _act_sample(gate_s, val_s, conv_s, p['dw_conv'], p['dw_bias'], bs, ts)
    conv_last_s = gate_s.reshape(bs, ts, -1)[:, ts - (CONV_W - 1):]
    x2p, x2s = _proj_resid(act_p, act_s, w_down, x1p, x1s, mod_p, mod_s, 5, DOWN_COLS, "ffn_down")
    heads = lambda a: a.reshape(a.shape[0], a.shape[1], N_ATT_HEADS, HEAD_DIM)
    out_p = (x2p.reshape(bp, tp, d), heads(k_keep_p), heads(v_keep_p), s_fin_p, shift_last_p, conv_last_p)
    out_s = (x2s.reshape(bs, ts, d), heads(k_keep_s), heads(v_keep_s), s_fin_s, shift_last_s, conv_last_s)
    return out_p, out_s


def kernel(x_prompt, x_sample, c_prompt, c_sample, cache_att_k, cache_att_v, state_rwkv, state_shift, state_ffn_conv, norm_att_g, norm_ffn_g, w_ada, b_ada, w_in, q_norm_g, k_norm_g, rel_bias, mu_shift, w0, w2, a0, a2, g2, k_k, k_a, r_k, ln_x_w, ln_x_b, w_out, w_up, dw_conv, dw_bias, w_down):
    depth = w_in.shape[0]
    bp, tp, d = x_prompt.shape
    bs, ts, _ = x_sample.shape
    d_ff = w_down.shape[1]
    hp, hs = x_prompt, x_sample
    outs_p = [[] for _ in range(5)]
    outs_s = [[] for _ in range(5)]
    for l in range(depth):
        p = dict(norm_att_g=norm_att_g[l], norm_ffn_g=norm_ffn_g[l], w_in=w_in[l], q_norm_g=q_norm_g[l],
                 k_norm_g=k_norm_g[l], mu_shift=mu_shift[l], w0=w0[l], w2=w2[l], a0=a0[l], a2=a2[l],
                 g2=g2[l], k_k=k_k[l], k_a=k_a[l], r_k=r_k[l], ln_x_w=ln_x_w[l], ln_x_b=ln_x_b[l],
                 w_out=w_out[l], w_up=w_up[l], dw_conv=dw_conv[l], dw_bias=dw_bias[l], w_down=w_down[l])
        n_c = bp + bs
        pad = (-n_c) % 8
        c_all = jnp.concatenate([c_prompt, c_sample, jnp.zeros((pad, d), F32)], axis=0)
        mod = _ada(c_all, w_ada[l], b_ada[l])
        mod_p = _Mod(mod.reshape(n_c + pad, 6, 1, d), False, rows_per_batch=tp)
        mod_s = _Mod(jnp.repeat(mod[bp:bp + bs], ts, axis=0), True)
        u = _bias_rows(rel_bias[l])

        res_p, res_s = _layer(hp, hs, mod_p, mod_s, p, u, cache_att_k[l], cache_att_v[l],
                              jnp.zeros((bp, N_RWKV_HEADS, HEAD_DIM, HEAD_DIM), F32), state_rwkv[l],
                              jnp.zeros((bp, D_SHIFT), F32), state_shift[l],
                              jnp.zeros((bp, CONV_W - 1, d_ff), F32), state_ffn_conv[l])
        hp, hs = res_p[0], res_s[0]
        for lst, val in zip(outs_p, res_p[1:]):
            lst.append(val)
        for lst, val in zip(outs_s, res_s[1:]):
            lst.append(val)
    st = lambda lst: jnp.stack(lst)
    return (hp, hs, *[st(x) for x in outs_p], *[st(x) for x in outs_s])
```
